```python
import math
import jax, jax.numpy as jnp
from jax import lax
import numpy as np

D_MODEL = 1024
BATCH = 4
SEQ = 8192
DEPTH = 4
DEC_BATCH = 16
DEC_SEQ = 64
PAST_LEN = 2048

CHUNK = 64
HEAD_DIM = 64
GDN_HEADS = (3 * D_MODEL // 8) // HEAD_DIM
GDN_W = GDN_HEADS * HEAD_DIM
GDN_CONV = 4
SWA_Q_HEADS = (3 * D_MODEL // 8) // HEAD_DIM
SWA_KV_HEADS = 2
SWA_GROUP = SWA_Q_HEADS // SWA_KV_HEADS
SWA_W = SWA_Q_HEADS * HEAD_DIM
SWA_KV_W = SWA_KV_HEADS * HEAD_DIM
WINDOW = 128
WINDOW_CHUNKS = WINDOW // CHUNK
ROPE_THETA = 10000.0
SC_W = D_MODEL - GDN_W - SWA_W
SC_CONV = 3
MIX_W = GDN_W + SWA_W + SC_W
OFF_GDN_QKV = 0
OFF_GDN_Z = OFF_GDN_QKV + 3 * GDN_W
OFF_GDN_B = OFF_GDN_Z + GDN_W
OFF_GDN_A = OFF_GDN_B + GDN_HEADS
OFF_SWA_Q = OFF_GDN_A + GDN_HEADS
OFF_SWA_K = OFF_SWA_Q + SWA_W
OFF_SWA_V = OFF_SWA_K + SWA_KV_W
OFF_SC = OFF_SWA_V + SWA_KV_W
P_IN = OFF_SC + 3 * SC_W
N_MEM = 256
MEM_HEADS = 4
MEM_W = MEM_HEADS * HEAD_DIM
N_EXPERTS = 32
TOP_K = 4
D_FF = D_MODEL
SWIGLU_LIMIT = 7.0
SWIGLU_ALPHA = 1.702
MOE_BLOCK = 256
DN_ALPHA = (2 * DEPTH) ** 0.25
DN_BETA = (8 * DEPTH) ** -0.25
LN_EPS = 1e-5
RMS_EPS = 1e-6

kernel_name = "hymba_gdn_swa_sconv_moe_stream_step"


def layer_norm(x, g, b):
    xf = x.astype(jnp.float32)
    mu = jnp.mean(xf, -1, keepdims=True)
    var = jnp.mean(jnp.square(xf - mu), -1, keepdims=True)
    return ((xf - mu) * lax.rsqrt(var + LN_EPS) * g.astype(jnp.float32) + b.astype(jnp.float32)).astype(x.dtype)


def l2norm(x):
    return x * lax.rsqrt(jnp.sum(x * x, -1, keepdims=True) + RMS_EPS)


def rope(x, pos):
    half = HEAD_DIM // 2
    inv = ROPE_THETA ** (-jnp.arange(half, dtype=jnp.float32) / half)
    ang = pos.astype(jnp.float32)[:, None] * inv[None, :]
    cos = jnp.cos(ang)[None, :, None, :]
    sin = jnp.sin(ang)[None, :, None, :]
    xf = x.astype(jnp.float32)
    x1, x2 = xf[..., :half], xf[..., half:]
    return jnp.concatenate([x1 * cos - x2 * sin, x2 * cos + x1 * sin], -1).astype(x.dtype)


def causal_conv(u, buf, w):
    width, ch = w.shape
    full = jnp.concatenate([buf.astype(u.dtype), u], axis=1)
    out = lax.conv_general_dilated(full, w.astype(u.dtype)[:, None, :], (1,), 'VALID',
                                   dimension_numbers=('NWC', 'WIO', 'NWC'), feature_group_count=ch)
    return out, full[:, full.shape[1] - (width - 1):]


def gdn_chunk(S, q, k, v, g, beta):
    L = q.shape[2]
    gam = jnp.cumsum(g, axis=-1)
    incl = jnp.tril(jnp.ones((L, L), bool))
    strict = jnp.tril(jnp.ones((L, L), bool), -1)
    decay = jnp.exp(jnp.where(incl, gam[..., :, None] - gam[..., None, :], -jnp.inf))
    kk = jnp.einsum('bhtd,bhid->bhti', k, k)
    a_mat = jnp.where(strict, beta[..., :, None] * decay * kk, 0.0)
    g_exp = jnp.exp(gam)[..., None]
    rhs = beta[..., None] * (v - g_exp * jnp.einsum('bhtd,bhde->bhte', k, S))
    u = lax.linalg.triangular_solve(a_mat + jnp.eye(L, dtype=a_mat.dtype), rhs,
                                    left_side=True, lower=True, unit_diagonal=True)
    qk = jnp.einsum('bhtd,bhid->bhti', q, k) * decay
    o = g_exp * jnp.einsum('bhtd,bhde->bhte', q, S) + jnp.einsum('bhti,bhie->bhte', qk, u)
    g_last = gam[..., -1:]
    S_new = jnp.exp(g_last)[..., None] * S + jnp.einsum('bhid,bhie->bhde', k * jnp.exp(g_last - gam)[..., None], u)
    return S_new, o


def gdn_recurrence(S0, q, k, v, g, beta):
    L = q.shape[2]
    if L <= CHUNK:
        return gdn_chunk(S0, q, k, v, g, beta)
    n_chunks = L // CHUNK

    def to_chunks(a):
        return jnp.moveaxis(a.reshape(a.shape[:2] + (n_chunks, CHUNK) + a.shape[3:]), 2, 0)

    S_fin, o = lax.scan(lambda S, xs: gdn_chunk(S, *xs), S0,
                        tuple(to_chunks(a) for a in (q, k, v, g, beta)))
    o = jnp.moveaxis(o, 0, 2).reshape(q.shape[:3] + (HEAD_DIM,))
    return S_fin, o


def gated_deltanet(h, conv_buf, S0, conv_w, a_log, dt_bias, norm_g):
    B, L = h.shape[:2]
    f32 = jnp.float32
    qkv, conv_buf_new = causal_conv(h[..., OFF_GDN_QKV:OFF_GDN_Z], conv_buf, conv_w)
    qkv = jax.nn.silu(qkv.astype(f32))
    q, k, v = (qkv[..., i * GDN_W:(i + 1) * GDN_W].reshape(B, L, GDN_HEADS, HEAD_DIM) for i in range(3))
    q = l2norm(q) * HEAD_DIM ** -0.5
    k = l2norm(k)
    z = h[..., OFF_GDN_Z:OFF_GDN_B].astype(f32).reshape(B, L, GDN_HEADS, HEAD_DIM)
    beta = jax.nn.sigmoid(h[..., OFF_GDN_B:OFF_GDN_A].astype(f32))
    g = -jnp.exp(a_log.astype(f32)) * jax.nn.softplus(h[..., OFF_GDN_A:OFF_SWA_Q].astype(f32)
                                                      + dt_bias.astype(f32))
    t = lambda a: jnp.swapaxes(a, 1, 2)
    S_new, o = gdn_recurrence(S0.astype(f32), t(q), t(k), t(v), t(g), t(beta))
    o = jnp.swapaxes(o, 1, 2)
    o = o * lax.rsqrt(jnp.mean(o * o, -1, keepdims=True) + RMS_EPS) * norm_g.astype(f32) * jax.nn.silu(z)
    return o.reshape(B, L, GDN_W).astype(h.dtype), conv_buf_new, S_new.astype(S0.dtype)


def sink_attention(q, k, v, sinks, valid):
    s = jnp.einsum('...tjgd,...sjd->...jgts', q, k).astype(jnp.float32) * HEAD_DIM ** -0.5
    if valid is not None:
        s = jnp.where(valid, s, -jnp.inf)
    sink = jnp.broadcast_to(sinks.astype(jnp.float32).reshape(SWA_KV_HEADS, SWA_GROUP, 1, 1), s.shape[:-1] + (1,))
    p = jax.nn.softmax(jnp.concatenate([s, sink], axis=-1), axis=-1)[..., :-1]
    return jnp.einsum('...jgts,...sjd->...tjgd', p.astype(v.dtype), v)


def swa_banded(q, k, v, sinks):
    B, L = q.shape[:2]
    n_chunks = L // CHUNK
    qc = q.reshape(B, n_chunks, CHUNK, SWA_KV_HEADS, SWA_GROUP, HEAD_DIM)

    def band(a):
        ac = a.reshape(B, n_chunks, CHUNK, SWA_KV_HEADS, HEAD_DIM)
        ap = jnp.pad(ac, ((0, 0), (WINDOW_CHUNKS, 0), (0, 0), (0, 0), (0, 0)))
        return jnp.concatenate([ap[:, j:j + n_chunks] for j in range(WINDOW_CHUNKS + 1)], axis=2)

    src = jnp.arange(n_chunks)[:, None] + jnp.arange(WINDOW_CHUNKS + 1)[None, :] - WINDOW_CHUNKS
    valid = jnp.repeat(src >= 0, CHUNK, axis=1)[None, :, None, None, None, :]
    o = sink_attention(qc, band(k), band(v), sinks, valid)
    return o.reshape(B, L, SWA_W)


def swa_with_cache(q, k, v, cache_k, cache_v, sinks):
    B, L = q.shape[:2]
    kf = jnp.concatenate([cache_k.astype(k.dtype), k], axis=1)
    vf = jnp.concatenate([cache_v.astype(v.dtype), v], axis=1)
    o = sink_attention(q.reshape(B, L, SWA_KV_HEADS, SWA_GROUP, HEAD_DIM), kf, vf, sinks, None)
    return o.reshape(B, L, SWA_W)


def short_conv_mixer(h, buf, conv_w):
    bg = h[..., OFF_SC:OFF_SC + SC_W]
    cg = h[..., OFF_SC + SC_W:OFF_SC + 2 * SC_W]
    hx = h[..., OFF_SC + 2 * SC_W:OFF_SC + 3 * SC_W]
    y, new_buf = causal_conv(cg * hx, buf, conv_w)
    return bg * y, new_buf


def memory_attention(x, mem_k, mem_v, w_cq, w_co):
    B, L = x.shape[:2]
    q = (x @ w_cq).reshape(B, L, MEM_HEADS, HEAD_DIM)
    s = jnp.einsum('blhd,bmhd->bhlm', q, mem_k.astype(q.dtype)).astype(jnp.float32) * HEAD_DIM ** -0.5
    p = jax.nn.softmax(s, axis=-1)
    o = jnp.einsum('bhlm,bmhd->blhd', p.astype(x.dtype), mem_v.astype(x.dtype)).reshape(B, L, MEM_W)
    return o @ w_co


def moe(x, w_r, b_r, w1, b1, w2, b2):
    shp = x.shape
    xt = x.reshape(-1, D_MODEL)
    T = xt.shape[0]
    logits = (xt @ w_r).astype(jnp.float32) + b_r.astype(jnp.float32)
    top_v, top_i = lax.top_k(logits, TOP_K)
    gates = jax.nn.softmax(top_v, axis=-1)
    TK = T * TOP_K
    flat_e = top_i.reshape(-1).astype(jnp.int32)
    order = jnp.argsort(flat_e)
    sorted_e = flat_e[order]
    counts = jnp.bincount(flat_e, length=N_EXPERTS).astype(jnp.int32)
    padded = (counts + MOE_BLOCK - 1) // MOE_BLOCK * MOE_BLOCK
    pad_end = jnp.cumsum(padded)
    pad_start = pad_end - padded
    grp_start = jnp.cumsum(counts) - counts
    dest_sorted = pad_start[sorted_e] + jnp.arange(TK, dtype=jnp.int32) - grp_start[sorted_e]
    dest = jnp.zeros((TK,), jnp.int32).at[order].set(dest_sorted.astype(jnp.int32))
    n_blocks = -(-TK // MOE_BLOCK) + N_EXPERTS
    n_rows = n_blocks * MOE_BLOCK
    row_tok = jnp.full((n_rows,), T, jnp.int32).at[dest].set(jnp.arange(TK, dtype=jnp.int32) // TOP_K)
    x_pad = jnp.concatenate([xt, jnp.zeros((1, D_MODEL), xt.dtype)], axis=0)
    xb = x_pad[row_tok].reshape(n_blocks, MOE_BLOCK, D_MODEL)
    blk_e = jnp.minimum(jnp.searchsorted(pad_end, jnp.arange(n_blocks, dtype=jnp.int32) * MOE_BLOCK,
                                         side='right'), N_EXPERTS - 1)

    def expert_block(args):
        xs, e = args
        hh = xs @ w1[e] + b1[e]
        glu = jnp.minimum(hh[:, :D_FF], SWIGLU_LIMIT)
        lin = jnp.clip(hh[:, D_FF:], -SWIGLU_LIMIT, SWIGLU_LIMIT)
        act = glu * jax.nn.sigmoid(SWIGLU_ALPHA * glu) * (lin + 1.0)
        return act @ w2[e] + b2[e]

    yb = lax.map(expert_block, (xb, blk_e))
    y = yb.reshape(n_rows, D_MODEL)[dest].reshape(T, TOP_K, D_MODEL)
    out = jnp.einsum('tk,tkd->td', gates.astype(y.dtype), y)
    return out.reshape(shp).astype(x.dtype)


def setup_inputs(seed: int = 0) -> dict:
    key = jax.random.key(seed)
    ks = iter(jax.random.split(key, 48))
    f32 = jnp.float32

    def nrm(shape, scale):
        return jax.random.normal(next(ks), shape, f32) * scale

    swa_rows = min(WINDOW, PAST_LEN)
    dt = jnp.exp(jax.random.uniform(next(ks), (DEPTH, GDN_HEADS), f32, math.log(1e-3), math.log(1e-1)))
    return {
        "x_prompt": nrm((BATCH, SEQ, D_MODEL), 1.0),
        "x_sample": nrm((DEC_BATCH, DEC_SEQ, D_MODEL), 1.0),
        "state_gdn_S": nrm((DEPTH, DEC_BATCH, GDN_HEADS, HEAD_DIM, HEAD_DIM), 0.1),
        "state_gdn_conv": nrm((DEPTH, DEC_BATCH, GDN_CONV - 1, 3 * GDN_W), 1.0),
        "cache_swa_k": nrm((DEPTH, DEC_BATCH, swa_rows, SWA_KV_HEADS, HEAD_DIM), 1.0),
        "cache_swa_v": nrm((DEPTH, DEC_BATCH, swa_rows, SWA_KV_HEADS, HEAD_DIM), 1.0),
        "state_sconv": nrm((DEPTH, DEC_BATCH, SC_CONV - 1, SC_W), 1.0),
        "cache_mem_k": nrm((DEPTH, DEC_BATCH, N_MEM, MEM_HEADS, HEAD_DIM), 1.0),
        "cache_mem_v": nrm((DEPTH, DEC_BATCH, N_MEM, MEM_HEADS, HEAD_DIM), 1.0),
        "mem_prompt": nrm((BATCH, N_MEM, D_MODEL), 1.0),
        "ln_in_g": 1.0 + nrm((D_MODEL,), 0.02),
        "ln_in_b": nrm((D_MODEL,), 0.02),
        "w_in": nrm((DEPTH, D_MODEL, P_IN), D_MODEL ** -0.5),
        "gdn_conv_w": nrm((DEPTH, GDN_CONV, 3 * GDN_W), GDN_CONV ** -0.5),
        "gdn_a_log": jnp.log(jax.random.uniform(next(ks), (DEPTH, GDN_HEADS), f32, 1.0, 16.0)),
        "gdn_dt_bias": dt + jnp.log(-jnp.expm1(-dt)),
        "gdn_norm_g": 1.0 + nrm((DEPTH, HEAD_DIM), 0.02),
        "swa_sinks": nrm((DEPTH, SWA_Q_HEADS), 0.5),
        "sc_conv_w": nrm((DEPTH, SC_CONV, SC_W), SC_CONV ** -0.5),
        "w_o": nrm((DEPTH, MIX_W, D_MODEL), MIX_W ** -0.5 * DN_BETA),
        "ln1_g": 1.0 + nrm((DEPTH, D_MODEL), 0.02),
        "ln1_b": nrm((DEPTH, D_MODEL), 0.02),
        "w_cq": nrm((DEPTH, D_MODEL, MEM_W), D_MODEL ** -0.5),
        "w_mk": nrm((DEPTH, D_MODEL, MEM_W), D_MODEL ** -0.5),
        "w_mv": nrm((DEPTH, D_MODEL, MEM_W), D_MODEL ** -0.5),
        "w_co": nrm((DEPTH, MEM_W, D_MODEL), MEM_W ** -0.5 * DN_BETA),
        "ln2_g": 1.0 + nrm((DEPTH, D_MODEL), 0.02),
        "ln2_b": nrm((DEPTH, D_MODEL), 0.02),
        "w_router": nrm((DEPTH, D_MODEL, N_EXPERTS), D_MODEL ** -0.5),
        "b_router": nrm((DEPTH, N_EXPERTS), 0.01),
        "w1": nrm((DEPTH, N_EXPERTS, D_MODEL, 2 * D_FF), D_MODEL ** -0.5),
        "b1": nrm((DEPTH, N_EXPERTS, 2 * D_FF), 0.01),
        "w2": nrm((DEPTH, N_EXPERTS, D_FF, D_MODEL), D_FF ** -0.5 * DN_BETA),
        "b2": nrm((DEPTH, N_EXPERTS, D_MODEL), 0.01),
        "ln3_g": 1.0 + nrm((DEPTH, D_MODEL), 0.02),
        "ln3_b": nrm((DEPTH, D_MODEL), 0.02),
    }


def reference(x_prompt, x_sample, state_gdn_S, state_gdn_conv, cache_swa_k, cache_swa_v, state_sconv,
              cache_mem_k, cache_mem_v, mem_prompt, ln_in_g, ln_in_b, w_in, gdn_conv_w, gdn_a_log,
              gdn_dt_bias, gdn_norm_g, swa_sinks, sc_conv_w, w_o, ln1_g, ln1_b, w_cq, w_mk, w_mv, w_co,
              ln2_g, ln2_b, w_router, b_router, w1, b1, w2, b2, ln3_g, ln3_b):

    def layer(l, x, pos, gdn_S, gdn_buf, swa_hist, sc_buf, mem_k, mem_v):
        B, L = x.shape[:2]
        h = x @ w_in[l]
        a_out, gdn_buf, gdn_S = gated_deltanet(h, gdn_buf, gdn_S, gdn_conv_w[l], gdn_a_log[l],
                                               gdn_dt_bias[l], gdn_norm_g[l])
        q = rope(h[..., OFF_SWA_Q:OFF_SWA_K].reshape(B, L, SWA_Q_HEADS, HEAD_DIM), pos)
        k = rope(h[..., OFF_SWA_K:OFF_SWA_V].reshape(B, L, SWA_KV_HEADS, HEAD_DIM), pos)
        v = h[..., OFF_SWA_V:OFF_SC].reshape(B, L, SWA_KV_HEADS, HEAD_DIM)
        if swa_hist is None:
            b_out = swa_banded(q, k, v, swa_sinks[l])
            keep = min(WINDOW, L)
            swa_k, swa_v = k[:, L - keep:], v[:, L - keep:]
        else:
            b_out = swa_with_cache(q, k, v, swa_hist[0], swa_hist[1], swa_sinks[l])
            swa_k, swa_v = k, v
        c_out, sc_buf = short_conv_mixer(h, sc_buf, sc_conv_w[l])
        mix = jnp.concatenate([a_out, b_out, c_out], axis=-1) @ w_o[l]
        x = layer_norm(DN_ALPHA * x + mix, ln1_g[l], ln1_b[l])
        x = layer_norm(DN_ALPHA * x + memory_attention(x, mem_k, mem_v, w_cq[l], w_co[l]), ln2_g[l], ln2_b[l])
        x = layer_norm(DN_ALPHA * x + moe(x, w_router[l], b_router[l], w1[l], b1[l], w2[l], b2[l]),
                       ln3_g[l], ln3_b[l])
        return x, gdn_S, gdn_buf, swa_k, swa_v, sc_buf

    xp = layer_norm(x_prompt, ln_in_g, ln_in_b)
    Bp, Lp = xp.shape[:2]
    n_mem = mem_prompt.shape[1]
    pos_p = jnp.arange(Lp)
    p_S, p_conv, p_k, p_v, p_sc, p_mk, p_mv = [], [], [], [], [], [], []
    for l in range(DEPTH):
        mk = (mem_prompt @ w_mk[l]).reshape(Bp, n_mem, MEM_HEADS, HEAD_DIM)
        mv = (mem_prompt @ w_mv[l]).reshape(Bp, n_mem, MEM_HEADS, HEAD_DIM)
        xp, S_l, cb_l, k_l, v_l, sb_l = layer(
            l, xp, pos_p,
            jnp.zeros((Bp, GDN_HEADS, HEAD_DIM, HEAD_DIM), xp.dtype),
            jnp.zeros((Bp, GDN_CONV - 1, 3 * GDN_W), xp.dtype),
            None,
            jnp.zeros((Bp, SC_CONV - 1, SC_W), xp.dtype),
            mk, mv)
        p_S.append(S_l); p_conv.append(cb_l); p_k.append(k_l); p_v.append(v_l)
        p_sc.append(sb_l); p_mk.append(mk); p_mv.append(mv)

    xs = layer_norm(x_sample, ln_in_g, ln_in_b)
    pos_s = PAST_LEN + jnp.arange(xs.shape[1])
    s_S, s_conv, s_k, s_v, s_sc = [], [], [], [], []
    for l in range(DEPTH):
        xs, S_l, cb_l, k_l, v_l, sb_l = layer(
            l, xs, pos_s, state_gdn_S[l], state_gdn_conv[l], (cache_swa_k[l], cache_swa_v[l]),
            state_sconv[l], cache_mem_k[l], cache_mem_v[l])
        s_S.append(S_l); s_conv.append(cb_l); s_k.append(k_l); s_v.append(v_l); s_sc.append(sb_l)

    return (xp, xs,
            jnp.stack(p_S), jnp.stack(p_conv), jnp.stack(p_k), jnp.stack(p_v), jnp.stack(p_sc),
            jnp.stack(p_mk), jnp.stack(p_mv),
            jnp.stack(s_S), jnp.stack(s_conv), jnp.stack(s_k), jnp.stack(s_v), jnp.stack(s_sc))
```

```python
import functools

import jax
import jax.numpy as jnp
from jax import lax
from jax.experimental import pallas as pl
from jax.experimental.pallas import tpu as pltpu

F32 = jnp.float32
BF16 = jnp.bfloat16

LANE = 128
SUBLANE = 8
VMEM_LIMIT_BYTES = 56 * 1024 * 1024

CHUNK = 64
CHUNK_SHIFT = 6
HEAD_DIM = 64
GDN_HEADS = 6
GDN_W = GDN_HEADS * HEAD_DIM
GDN_CONV = 4
SWA_Q_HEADS = 6
SWA_KV_HEADS = 2
SWA_GROUP = SWA_Q_HEADS // SWA_KV_HEADS
SWA_W = SWA_Q_HEADS * HEAD_DIM
SWA_KV_W = SWA_KV_HEADS * HEAD_DIM
WINDOW = 128
ROPE_THETA = 10000.0
SC_W = 256
SC_CONV = 3
MEM_HEADS = 4
MEM_W = MEM_HEADS * HEAD_DIM
TOP_K = 4
SWIGLU_LIMIT = 7.0
SWIGLU_ALPHA = 1.702
PAST_LEN = 2048
LN_EPS = 1e-5
RMS_EPS = 1e-6
NEG_BIG = -1e30

COL_QKV = 0
COL_Z = 1152
COL_SC = 1536
COL_SWA = 2304
COL_SWA_V = COL_SWA + SWA_W + SWA_KV_W
COL_BA = COL_SWA + SWA_W + 2 * SWA_KV_W
P_PAD = 3072
BA_B = 0
BA_A = GDN_HEADS
MIX_W = 1024

MOE_BM = 512


def _cparams(*sem):
    return pltpu.CompilerParams(dimension_semantics=sem, vmem_limit_bytes=VMEM_LIMIT_BYTES)


def _pick_tile(n, candidates):
    for c in candidates:
        if n % c == 0:
            return c
    raise ValueError(f"no tile in {candidates} divides {n}")


def _ln(x, g, b):
    mu = jnp.mean(x, -1, keepdims=True)
    xc = x - mu
    var = jnp.mean(xc * xc, -1, keepdims=True)
    return xc * lax.rsqrt(var + LN_EPS) * g + b


def _dot(a, b):
    return jnp.dot(a.astype(BF16), b.astype(BF16), preferred_element_type=F32)


def _dot_nt(a, b):
    return lax.dot_general(a.astype(BF16), b.astype(BF16), (((1,), (1,)), ((), ())),
                           preferred_element_type=F32)


def _silu(x):
    return x * jax.nn.sigmoid(x)


def _ln_in_kernel(n_p_tiles, xp_ref, xs_ref, g_ref, b_ref, o_ref):
    i = pl.program_id(0)

    @pl.when(i < n_p_tiles)
    def _():
        o_ref[...] = _ln(xp_ref[...], g_ref[...], b_ref[...])

    @pl.when(i >= n_p_tiles)
    def _():
        o_ref[...] = _ln(xs_ref[...], g_ref[...], b_ref[...])


def _ln_in(xp, xs, g, b):
    tp, d = xp.shape
    ts = xs.shape[0]
    tm = _pick_tile(ts, (512, 256, 128, 64))
    assert tp % tm == 0
    npt, nst = tp // tm, ts // tm
    return pl.pallas_call(
        functools.partial(_ln_in_kernel, npt),
        grid=(npt + nst,),
        in_specs=[
            pl.BlockSpec((tm, d), lambda i: (jnp.minimum(i, npt - 1), 0)),
            pl.BlockSpec((tm, d), lambda i: (jnp.maximum(i - npt, 0), 0)),
            pl.BlockSpec((1, d), lambda i: (0, 0)),
            pl.BlockSpec((1, d), lambda i: (0, 0)),
        ],
        out_specs=pl.BlockSpec((tm, d), lambda i: (i, 0)),
        out_shape=jax.ShapeDtypeStruct((tp + ts, d), F32),
        compiler_params=_cparams("arbitrary"),
        name="ln_in",
    )(xp, xs, g.reshape(1, d), b.reshape(1, d))


def _proj_kernel(x_ref, w_ref, o_ref):
    o_ref[...] = _dot(x_ref[...], w_ref[...])


def _proj(x, w, name):
    t, k = x.shape
    n = w.shape[1]
    tm = _pick_tile(t, (512, 256, 128, 64))
    return pl.pallas_call(
        _proj_kernel,
        grid=(t // tm,),
        in_specs=[pl.BlockSpec((tm, k), lambda i: (i, 0)),
                  pl.BlockSpec((k, n), lambda i: (0, 0))],
        out_specs=pl.BlockSpec((tm, n), lambda i: (i, 0)),
        out_shape=jax.ShapeDtypeStruct((t, n), F32),
        compiler_params=_cparams("arbitrary"),
        name=name,
    )(x, w)


def _split2(x):
    hi = x.astype(BF16)
    lo = (x - hi.astype(F32)).astype(BF16)
    return hi, lo


def _split3(x):
    hi = x.astype(BF16)
    r1 = x - hi.astype(F32)
    mid = r1.astype(BF16)
    lo = (r1 - mid.astype(F32)).astype(BF16)
    return hi, mid, lo


def _sel_dot2(x, sel2):
    return jnp.dot(jnp.concatenate(_split2(x), axis=1), sel2, preferred_element_type=F32)


def _gdn_kernel(lt, h_ref, cbuf_ref, s0_ref, sbuf_ref, cw_ref, scw_ref, gp_ref, ng_ref,
                ho_ref, cnew_ref, snew_ref, scnew_ref, ext_scr, uext_scr, s_scr, o_scr):
    i = pl.program_id(1)
    n_chunks = lt // CHUNK
    pad = SUBLANE

    @pl.when(i == 0)
    def _():
        ext_scr[0:pad, :] = jnp.concatenate(
            [jnp.zeros((pad - (GDN_CONV - 1), 3 * GDN_W), F32), cbuf_ref[0]], axis=0)
        uext_scr[0:pad, :] = jnp.concatenate(
            [jnp.zeros((pad - (SC_CONV - 1), SC_W), F32), sbuf_ref[0]], axis=0)
        s_scr[...] = s0_ref[0]

    ext_scr[pad:pad + lt, :] = h_ref[:, COL_QKV:COL_QKV + 3 * GDN_W]
    conv = cw_ref[0:1, :] * ext_scr[pl.ds(pad - 3, lt), :]
    for j in range(1, GDN_CONV):
        conv = conv + cw_ref[j:j + 1, :] * ext_scr[pl.ds(pad - 3 + j, lt), :]
    cnew_ref[0] = ext_scr[pl.ds(lt + pad - 3, 3), :]
    ext_scr[0:pad, :] = ext_scr[lt:lt + pad, :]
    qkv = _silu(conv)
    q = qkv[:, 0:GDN_W]
    k = qkv[:, GDN_W:2 * GDN_W]
    v = qkv[:, 2 * GDN_W:3 * GDN_W]

    uext_scr[pad:pad + lt, :] = (h_ref[:, COL_SC + SC_W:COL_SC + 2 * SC_W]
                                 * h_ref[:, COL_SC + 2 * SC_W:COL_SC + 3 * SC_W])
    y = scw_ref[0:1, :] * uext_scr[pl.ds(pad - 2, lt), :]
    for j in range(1, SC_CONV):
        y = y + scw_ref[j:j + 1, :] * uext_scr[pl.ds(pad - 2 + j, lt), :]
    scnew_ref[0] = uext_scr[pl.ds(lt + pad - 2, 2), :]
    uext_scr[0:pad, :] = uext_scr[lt:lt + pad, :]
    ho_ref[:, 2 * GDN_W:MIX_W] = h_ref[:, COL_SC:COL_SC + SC_W] * y
    ho_ref[:, GDN_W:2 * GDN_W] = h_ref[:, GDN_W:2 * GDN_W]

    r384 = lax.broadcasted_iota(jnp.int32, (GDN_W, GDN_W), 0)
    c384 = lax.broadcasted_iota(jnp.int32, (GDN_W, GDN_W), 1)
    head_ones = jnp.where((r384 >> CHUNK_SHIFT) == (c384 >> CHUNK_SHIFT), 1.0, 0.0).astype(BF16)
    head_ones2 = jnp.concatenate([head_ones, head_ones], axis=0)
    r128 = lax.broadcasted_iota(jnp.int32, (LANE, GDN_W), 0)
    c128 = lax.broadcasted_iota(jnp.int32, (LANE, GDN_W), 1) >> CHUNK_SHIFT
    exp_a = jnp.where(r128 == c128 + BA_A, 1.0, 0.0).astype(BF16)
    exp_b = jnp.where(r128 == c128 + BA_B, 1.0, 0.0).astype(BF16)
    exp_a2 = jnp.concatenate([exp_a, exp_a], axis=0)
    exp_b2 = jnp.concatenate([exp_b, exp_b], axis=0)

    qn = q * lax.rsqrt(_sel_dot2(q * q, head_ones2) + RMS_EPS) * HEAD_DIM ** -0.5
    kn = k * lax.rsqrt(_sel_dot2(k * k, head_ones2) + RMS_EPS)

    ba = h_ref[:, COL_BA:COL_BA + LANE]
    lane_ba = lax.broadcasted_iota(jnp.int32, ba.shape, 1)
    is_a = (lane_ba >= BA_A) & (lane_ba < BA_A + GDN_HEADS)
    beta_blk = jax.nn.sigmoid(ba)
    sp_in = ba + gp_ref[1:2, :]
    softplus = jnp.maximum(sp_in, 0.0) + jnp.log(1.0 + jnp.exp(-jnp.abs(sp_in)))
    g_blk = jnp.where(is_a, -jnp.exp(gp_ref[0:1, :]) * softplus, 0.0)
    row = lax.broadcasted_iota(jnp.int32, (lt, lt), 0)
    col = lax.broadcasted_iota(jnp.int32, (lt, lt), 1)
    same_chunk = (row >> CHUNK_SHIFT) == (col >> CHUNK_SHIFT)
    m_incl = same_chunk & (col <= row)
    m_strict = same_chunk & (col < row)
    ltri = jnp.where(m_incl, 1.0, 0.0).astype(BF16)
    gam3 = jnp.dot(ltri, jnp.concatenate(_split3(g_blk), axis=1), preferred_element_type=F32)
    gam_blk = gam3[:, 0:LANE] + gam3[:, LANE:2 * LANE] + gam3[:, 2 * LANE:3 * LANE]
    gam_t = gam_blk.T
    glast_blk = jnp.concatenate(
        [jnp.broadcast_to(gam_blk[c * CHUNK + CHUNK - 1:(c + 1) * CHUNK, :], (CHUNK, LANE))
         for c in range(n_chunks)], axis=0)
    g_exp = _sel_dot2(jnp.exp(gam_blk), exp_a2)
    k_fac = _sel_dot2(jnp.exp(glast_blk - gam_blk), exp_a2)
    beta = _sel_dot2(beta_blk, exp_b2)
    qg = qn * g_exp
    kbeta = kn * beta
    kb = kbeta * g_exp
    vb = v * beta
    kt_t = (kn * k_fac).T

    eye = jnp.where(row == col, 1.0, 0.0)
    m_base = ((row >> 1) == (col >> 1)) & (col < row)
    lvl_masks = [((row >> (sh + 1)) == (col >> (sh + 1))) & ((row >> sh) != (col >> sh)) & (col < row)
                 for sh in range(1, CHUNK_SHIFT)]

    prep = []
    for h in range(GDN_HEADS):
        sl = slice(h * HEAD_DIM, (h + 1) * HEAD_DIM)
        gc = gam_blk[:, BA_A + h:BA_A + h + 1]
        gr = gam_t[BA_A + h:BA_A + h + 1, :]
        dec = jnp.exp(jnp.where(m_incl, gc - gr, NEG_BIG))
        kq = _dot_nt(jnp.concatenate([qn[:, sl], kbeta[:, sl]], axis=0), kn[:, sl])
        qd = kq[0:lt] * dec
        a_mat = jnp.where(m_strict, dec * kq[lt:2 * lt], 0.0)
        t_inv = eye - jnp.where(m_base, a_mat, 0.0)
        for lm in lvl_masks:
            t_inv = t_inv - _dot(t_inv, _dot(jnp.where(lm, a_mat, 0.0), t_inv))
        x_sol = _dot(t_inv, jnp.concatenate([vb[:, sl], kb[:, sl]], axis=1))
        prep.append((x_sol[:, 0:HEAD_DIM], x_sol[:, HEAD_DIM:2 * HEAD_DIM], qd))

    ng = ng_ref[...]
    for c in range(n_chunks):
        rs = slice(c * CHUNK, (c + 1) * CHUNK)
        for h in range(GDN_HEADS):
            sl = slice(h * HEAD_DIM, (h + 1) * HEAD_DIM)
            u0, w_mat, qd = prep[h]
            s_old = s_scr[h]
            wq = _dot(jnp.concatenate([w_mat[rs], qg[rs, sl]], axis=0), s_old)
            u = u0[rs] - wq[0:CHUNK]
            o = wq[CHUNK:2 * CHUNK] + _dot(qd[rs, rs], u)
            gl = gam_blk[(c + 1) * CHUNK - 1:(c + 1) * CHUNK, BA_A + h:BA_A + h + 1]
            s_scr[h] = jnp.exp(gl) * s_old + _dot(kt_t[sl, rs], u)
            o_scr[rs, sl] = o * lax.rsqrt(jnp.mean(o * o, -1, keepdims=True) + RMS_EPS) * ng

    ho_ref[:, 0:GDN_W] = o_scr[...] * _silu(h_ref[:, COL_Z:COL_Z + GDN_W])
    snew_ref[0] = s_scr[...]


def _gdn_sconv(h_all, row_off, nb, seq, cbuf, s0, sbuf, cw, scw, gp, ng):
    lt = _pick_tile(seq, (256, 128, 64))
    nt = seq // lt
    off = row_off // lt
    assert row_off % lt == 0
    tok = lambda b, i: (off + b * nt + i, 0)
    return pl.pallas_call(
        functools.partial(_gdn_kernel, lt),
        grid=(nb, nt),
        in_specs=[
            pl.BlockSpec((lt, P_PAD), tok),
            pl.BlockSpec((1, GDN_CONV - 1, 3 * GDN_W), lambda b, i: (b, 0, 0)),
            pl.BlockSpec((1, GDN_HEADS, HEAD_DIM, HEAD_DIM), lambda b, i: (b, 0, 0, 0)),
            pl.BlockSpec((1, SC_CONV - 1, SC_W), lambda b, i: (b, 0, 0)),
            pl.BlockSpec((GDN_CONV, 3 * GDN_W), lambda b, i: (0, 0)),
            pl.BlockSpec((SC_CONV, SC_W), lambda b, i: (0, 0)),
            pl.BlockSpec((SUBLANE, LANE), lambda b, i: (0, 0)),
            pl.BlockSpec((1, HEAD_DIM), lambda b, i: (0, 0)),
        ],
        out_specs=[
            pl.BlockSpec((lt, MIX_W), tok),
            pl.BlockSpec((1, GDN_CONV - 1, 3 * GDN_W), lambda b, i: (b, 0, 0)),
            pl.BlockSpec((1, GDN_HEADS, HEAD_DIM, HEAD_DIM), lambda b, i: (b, 0, 0, 0)),
            pl.BlockSpec((1, SC_CONV - 1, SC_W), lambda b, i: (b, 0, 0)),
        ],
        out_shape=[
            jax.ShapeDtypeStruct(h_all.shape, F32),
            jax.ShapeDtypeStruct((nb, GDN_CONV - 1, 3 * GDN_W), F32),
            jax.ShapeDtypeStruct((nb, GDN_HEADS, HEAD_DIM, HEAD_DIM), F32),
            jax.ShapeDtypeStruct((nb, SC_CONV - 1, SC_W), F32),
        ],
        scratch_shapes=[
            pltpu.VMEM((lt + SUBLANE, 3 * GDN_W), F32),
            pltpu.VMEM((lt + SUBLANE, SC_W), F32),
            pltpu.VMEM((GDN_HEADS, HEAD_DIM, HEAD_DIM), F32),
            pltpu.VMEM((lt, GDN_W), F32),
        ],
        input_output_aliases={0: 0},
        compiler_params=_cparams("arbitrary", "arbitrary"),
        name="gdn_sconv",
    )(h_all, cbuf, s0, sbuf, cw, scw, gp, ng)


def _rope(x, cos, sin_signed):
    w = x.shape[1]
    nrep = w // LANE
    if nrep > 1:
        cos = jnp.concatenate([cos] * nrep, axis=1)
        sin_signed = jnp.concatenate([sin_signed] * nrep, axis=1)
    lane = lax.broadcasted_iota(jnp.int32, x.shape, 1) & (HEAD_DIM - 1)
    half = HEAD_DIM // 2
    swapped = jnp.where(lane < half, pltpu.roll(x, w - half, 1), pltpu.roll(x, half, 1))
    return x * cos + swapped * sin_signed


def _swa_kernel(lt, hist_valid, h_ref, hk_ref, hv_ref, cos_ref, sin_ref, sink_ref,
                o_ref, kr_ref, kall_scr, vall_scr):
    i = pl.program_id(1)

    @pl.when(i == 0)
    def _():
        kall_scr[0:WINDOW, :] = hk_ref[0]
        vall_scr[0:WINDOW, :] = hv_ref[0]

    cos = cos_ref[...]
    sin = sin_ref[...]
    q = _rope(h_ref[:, 0:SWA_W], cos, sin)
    k = _rope(h_ref[:, SWA_W:SWA_W + SWA_KV_W], cos, sin)
    kr_ref[...] = k
    kall_scr[WINDOW:WINDOW + lt, :] = k
    vall_scr[WINDOW:WINDOW + lt, :] = h_ref[:, SWA_W + SWA_KV_W:SWA_W + 2 * SWA_KV_W]
    kall = kall_scr[...]
    vall = vall_scr[...]

    nk = lt + WINDOW
    qc = lax.broadcasted_iota(jnp.int32, (lt, nk), 0) >> CHUNK_SHIFT
    kc = lax.broadcasted_iota(jnp.int32, (lt, nk), 1) >> CHUNK_SHIFT
    band = (kc >= qc) & (kc <= qc + WINDOW // CHUNK)
    if not hist_valid:
        band = band & ((kc >= WINDOW // CHUNK) | (i > 0))

    for j in range(SWA_KV_HEADS):
        kj = kall[:, j * HEAD_DIM:(j + 1) * HEAD_DIM]
        vj = vall[:, j * HEAD_DIM:(j + 1) * HEAD_DIM]
        for g in range(SWA_GROUP):
            hq = j * SWA_GROUP + g
            s = _dot_nt(q[:, hq * HEAD_DIM:(hq + 1) * HEAD_DIM], kj) * HEAD_DIM ** -0.5
            s = jnp.where(band, s, NEG_BIG)
            sink = sink_ref[0:1, hq:hq + 1]
            m = jnp.maximum(jnp.max(s, -1, keepdims=True), sink)
            e = jnp.exp(s - m)
            den = jnp.sum(e, -1, keepdims=True) + jnp.exp(sink - m)
            o_ref[:, hq * HEAD_DIM:(hq + 1) * HEAD_DIM] = _dot(e / den, vj)

    if lt >= WINDOW:
        kall_scr[0:WINDOW, :] = kall_scr[lt:lt + WINDOW, :]
        vall_scr[0:WINDOW, :] = vall_scr[lt:lt + WINDOW, :]


def _swa(h_all, row_off, nb, seq, hist_k, hist_v, hist_valid, cos, sin, sinks):
    lt = _pick_tile(seq, (256, 128, 64))
    nt = seq // lt
    assert nt == 1 or lt >= WINDOW
    off = row_off // lt
    assert row_off % lt == 0
    swa_blk_w = P_PAD - COL_SWA
    return pl.pallas_call(
        functools.partial(_swa_kernel, lt, hist_valid),
        grid=(nb, nt),
        in_specs=[
            pl.BlockSpec((lt, swa_blk_w), lambda b, i: (off + b * nt + i, COL_SWA // swa_blk_w)),
            pl.BlockSpec((1, WINDOW, SWA_KV_W), lambda b, i: (b, 0, 0)),
            pl.BlockSpec((1, WINDOW, SWA_KV_W), lambda b, i: (b, 0, 0)),
            pl.BlockSpec((lt, LANE), lambda b, i: (i, 0)),
            pl.BlockSpec((lt, LANE), lambda b, i: (i, 0)),
            pl.BlockSpec((1, LANE), lambda b, i: (0, 0)),
        ],
        out_specs=[
            pl.BlockSpec((lt, SWA_W), lambda b, i: (off + b * nt + i, 1)),
            pl.BlockSpec((lt, SWA_KV_W), lambda b, i: (b * nt + i, 0)),
        ],
        out_shape=[
            jax.ShapeDtypeStruct(h_all.shape, F32),
            jax.ShapeDtypeStruct((nb * seq, SWA_KV_W), F32),
        ],
        scratch_shapes=[
            pltpu.VMEM((lt + WINDOW, SWA_KV_W), F32),
            pltpu.VMEM((lt + WINDOW, SWA_KV_W), F32),
        ],
        input_output_aliases={0: 0},
        compiler_params=_cparams("arbitrary", "arbitrary"),
        name="swa",
    )(h_all, hist_k, hist_v, cos, sin, sinks)


def _oproj_kernel(alpha, mix_ref, x_ref, w_ref, g_ref, b_ref, o_ref):
    mix = _dot(mix_ref[...], w_ref[...])
    o_ref[...] = _ln(alpha * x_ref[...] + mix, g_ref[...], b_ref[...])


def _oproj(h_all, x_all, w_o, g, b, alpha):
    t, d = x_all.shape
    tm = _pick_tile(t, (512, 256, 128, 64))
    return pl.pallas_call(
        functools.partial(_oproj_kernel, alpha),
        grid=(t // tm,),
        in_specs=[
            pl.BlockSpec((tm, MIX_W), lambda i: (i, 0)),
            pl.BlockSpec((tm, d), lambda i: (i, 0)),
            pl.BlockSpec((MIX_W, d), lambda i: (0, 0)),
            pl.BlockSpec((1, d), lambda i: (0, 0)),
            pl.BlockSpec((1, d), lambda i: (0, 0)),
        ],
        out_specs=pl.BlockSpec((tm, d), lambda i: (i, 0)),
        out_shape=jax.ShapeDtypeStruct((t, d), F32),
        input_output_aliases={1: 0},
        compiler_params=_cparams("arbitrary"),
        name="oproj_ln1",
    )(h_all, x_all, w_o, g.reshape(1, d), b.reshape(1, d))


def _memattn_kernel(alpha, x_ref, xb_in_ref, gate_in_ref, ids_in_ref, mk_ref, mv_ref, wcq_ref, wco_ref,
                    g_ref, b_ref, wr_ref, br_ref, x_out_ref, xb_ref, gate_ref, ids_ref):
    del xb_in_ref, gate_in_ref, ids_in_ref
    x = x_ref[...]
    q = _dot(x, wcq_ref[...])
    mk = mk_ref[0]
    mv = mv_ref[0]
    outs = []
    for h in range(MEM_HEADS):
        sl = slice(h * HEAD_DIM, (h + 1) * HEAD_DIM)
        s = _dot_nt(q[:, sl], mk[:, sl]) * HEAD_DIM ** -0.5
        e = jnp.exp(s - jnp.max(s, -1, keepdims=True))
        p = e / jnp.sum(e, -1, keepdims=True)
        outs.append(_dot(p, mv[:, sl]))
    o = _dot(jnp.concatenate(outs, axis=1), wco_ref[...])
    x2 = _ln(alpha * x + o, g_ref[...], b_ref[...])
    x_out_ref[...] = x2
    xb_ref[...] = x2.astype(BF16)

    logits = _dot(x2, wr_ref[...]) + br_ref[...]
    lane = lax.broadcasted_iota(jnp.int32, logits.shape, 1).astype(F32)
    vals, idxs = [], []
    for _ in range(TOP_K):
        m = jnp.max(logits, -1, keepdims=True)
        idx = jnp.min(jnp.where(logits == m, lane, float(LANE)), -1, keepdims=True)
        vals.append(m)
        idxs.append(idx)
        logits = jnp.where(lane == idx, 2 * NEG_BIG, logits)
    es = [jnp.exp(val - vals[0]) for val in vals]
    den = es[0] + es[1] + es[2] + es[3]
    gate = jnp.zeros(logits.shape, F32)
    ids = jnp.zeros(logits.shape, F32)
    for kk in range(TOP_K):
        gate = jnp.where(lane == kk, es[kk] / den, gate)
        ids = jnp.where(lane == kk, idxs[kk], ids)
    gate_ref[...] = gate
    ids_ref[...] = ids.astype(jnp.int32)


def _memattn_router(x_all, xb_all, gate_all, ids_all, row_off, nb, seq, mk, mv, wcq, wco, g, b, wr, br, alpha):
    t, d = x_all.shape
    tl = _pick_tile(seq, (512, 256, 128, 64))
    nt = seq // tl
    off = row_off // tl
    assert row_off % tl == 0
    n_mem = mk.shape[1]
    tok = lambda bb, i: (off + bb * nt + i, 0)
    const = lambda bb, i: (0, 0)
    return pl.pallas_call(
        functools.partial(_memattn_kernel, alpha),
        grid=(nb, nt),
        in_specs=[
            pl.BlockSpec((tl, d), tok),
            pl.BlockSpec(memory_space=pl.ANY),
            pl.BlockSpec(memory_space=pl.ANY),
            pl.BlockSpec(memory_space=pl.ANY),
            pl.BlockSpec((1, n_mem, MEM_W), lambda bb, i: (bb, 0, 0)),
            pl.BlockSpec((1, n_mem, MEM_W), lambda bb, i: (bb, 0, 0)),
            pl.BlockSpec((d, MEM_W), const),
            pl.BlockSpec((MEM_W, d), const),
            pl.BlockSpec((1, d), const),
            pl.BlockSpec((1, d), const),
            pl.BlockSpec((d, LANE), const),
            pl.BlockSpec((1, LANE), const),
        ],
        out_specs=[
            pl.BlockSpec((tl, d), tok),
            pl.BlockSpec((tl, d), tok),
            pl.BlockSpec((tl, LANE), tok),
            pl.BlockSpec((tl, LANE), tok),
        ],
        out_shape=[
            jax.ShapeDtypeStruct((t, d), F32),
            jax.ShapeDtypeStruct((t, d), BF16),
            jax.ShapeDtypeStruct((t, LANE), F32),
            jax.ShapeDtypeStruct((t, LANE), jnp.int32),
        ],
        input_output_aliases={0: 0, 1: 1, 2: 2, 3: 3},
        compiler_params=_cparams("arbitrary", "arbitrary"),
        name="memattn_router",
    )(x_all, xb_all, gate_all, ids_all, mk, mv, wcq, wco, g.reshape(1, d), b.reshape(1, d), wr, br)


def _expert_kernel(d_ff, be_ref, nu_ref, xs_ref, w1_ref, b1_ref, w2_ref, b2_ref, ys_ref, w1b_scr, w2b_scr):
    i = pl.program_id(0)
    prev = be_ref[jnp.maximum(i - 1, 0)]
    new_expert = (i == 0) | (be_ref[i] != prev)

    @pl.when(new_expert & (i < nu_ref[0]))
    def _():
        w1b_scr[...] = w1_ref[0].astype(BF16)
        w2b_scr[...] = w2_ref[0].astype(BF16)

    @pl.when(i < nu_ref[0])
    def _():
        hh = jnp.dot(xs_ref[...], w1b_scr[...], preferred_element_type=F32) + b1_ref[0]
        glu = jnp.minimum(hh[:, 0:d_ff], SWIGLU_LIMIT)
        lin = jnp.clip(hh[:, d_ff:2 * d_ff], -SWIGLU_LIMIT, SWIGLU_LIMIT)
        act = glu * jax.nn.sigmoid(SWIGLU_ALPHA * glu) * (lin + 1.0)
        ys_ref[...] = jnp.dot(act.astype(BF16), w2b_scr[...], preferred_element_type=F32) + b2_ref[0]


def _experts(xs, blk_e, n_used, w1, b1, w2, b2):
    n_rows, d = xs.shape
    n_e, _, two_ff = w1.shape
    d_ff = two_ff // 2
    n_blocks = n_rows // MOE_BM
    row_blk = lambda i, be, nu: (jnp.minimum(i, nu[0] - 1), 0)
    exp_blk = lambda i, be, nu: (be[jnp.minimum(i, nu[0] - 1)], 0, 0)
    return pl.pallas_call(
        functools.partial(_expert_kernel, d_ff),
        grid_spec=pltpu.PrefetchScalarGridSpec(
            num_scalar_prefetch=2,
            grid=(n_blocks,),
            in_specs=[
                pl.BlockSpec((MOE_BM, d), row_blk),
                pl.BlockSpec((1, d, two_ff), exp_blk),
                pl.BlockSpec((1, 1, two_ff), exp_blk),
                pl.BlockSpec((1, d_ff, d), exp_blk),
                pl.BlockSpec((1, 1, d), exp_blk),
            ],
            out_specs=pl.BlockSpec((MOE_BM, d), row_blk),
            scratch_shapes=[pltpu.VMEM((d, two_ff), BF16), pltpu.VMEM((d_ff, d), BF16)],
        ),
        out_shape=jax.ShapeDtypeStruct((n_rows, d), F32),
        compiler_params=_cparams("arbitrary"),
        name="moe_experts",
    )(blk_e, n_used, xs, w1, b1.reshape(n_e, 1, two_ff), w2, b2.reshape(n_e, 1, d))


def _combine_kernel(alpha, d, yg_ref, gate_ref, x_ref, g_ref, b_ref, o_ref):
    gate = gate_ref[...]
    acc = alpha * x_ref[...]
    for k in range(TOP_K):
        acc = acc + gate[:, k:k + 1] * yg_ref[:, k * d:(k + 1) * d]
    o_ref[...] = _ln(acc, g_ref[...], b_ref[...])


def _combine(yg, gate_all, x_all, g, b, alpha):
    t, d = x_all.shape
    tm = _pick_tile(t, (256, 128, 64))
    return pl.pallas_call(
        functools.partial(_combine_kernel, alpha, d),
        grid=(t // tm,),
        in_specs=[
            pl.BlockSpec((tm, TOP_K * d), lambda i: (i, 0)),
            pl.BlockSpec((tm, LANE), lambda i: (i, 0)),
            pl.BlockSpec((tm, d), lambda i: (i, 0)),
            pl.BlockSpec((1, d), lambda i: (0, 0)),
            pl.BlockSpec((1, d), lambda i: (0, 0)),
        ],
        out_specs=pl.BlockSpec((tm, d), lambda i: (i, 0)),
        out_shape=jax.ShapeDtypeStruct((t, d), F32),
        input_output_aliases={2: 0},
        compiler_params=_cparams("arbitrary"),
        name="moe_combine_ln3",
    )(yg, gate_all, x_all, g.reshape(1, d), b.reshape(1, d))


def _route(ids, n_experts, t):
    tk = t * TOP_K
    n_blocks = -(-tk // MOE_BM) + n_experts
    n_rows = n_blocks * MOE_BM
    multi_hot = jnp.sum(jax.nn.one_hot(ids, n_experts, dtype=jnp.int32), axis=1)
    before = jnp.cumsum(multi_hot, axis=0) - multi_hot
    counts = jnp.sum(multi_hot, axis=0)
    padded = (counts + MOE_BM - 1) // MOE_BM * MOE_BM
    pad_end = jnp.cumsum(padded)
    pad_start = pad_end - padded
    dest = pad_start[ids] + jnp.take_along_axis(before, ids, axis=1)
    row_tok = jnp.zeros((n_rows,), jnp.int32).at[dest.reshape(-1)].set(
        jnp.arange(tk, dtype=jnp.int32) // TOP_K)
    blk_e = jnp.minimum(jnp.searchsorted(pad_end, jnp.arange(n_blocks, dtype=jnp.int32) * MOE_BM, side='right'),
                        n_experts - 1).astype(jnp.int32)
    n_used = (pad_end[-1] // MOE_BM).astype(jnp.int32).reshape(1)
    return dest.astype(jnp.int32), row_tok, blk_e, n_used


def _rope_tables(pos):
    half = HEAD_DIM // 2
    inv = ROPE_THETA ** (-jnp.arange(half, dtype=F32) / half)
    ang = pos.astype(F32)[:, None] * inv[None, :]
    cos = jnp.cos(ang)
    sin = jnp.sin(ang)
    cos = jnp.concatenate([cos, cos, cos, cos], axis=1)
    sin = jnp.concatenate([-sin, sin, -sin, sin], axis=1)
    return cos, sin


def _permute_w_in(w):
    d = w.shape[0]
    o_b = 4 * GDN_W
    o_q = o_b + 2 * GDN_HEADS
    o_sc = o_q + SWA_W + 2 * SWA_KV_W
    ba = jnp.concatenate([w[:, o_b:o_q], jnp.zeros((d, LANE - 2 * GDN_HEADS), w.dtype)], axis=1)
    return jnp.concatenate([w[:, 0:o_b], w[:, o_sc:], w[:, o_q:o_sc], ba], axis=1)


def kernel(x_prompt, x_sample, state_gdn_S, state_gdn_conv, cache_swa_k, cache_swa_v, state_sconv, cache_mem_k, cache_mem_v, mem_prompt, ln_in_g, ln_in_b, w_in, gdn_conv_w, gdn_a_log, gdn_dt_bias, gdn_norm_g, swa_sinks, sc_conv_w, w_o, ln1_g, ln1_b, w_cq, w_mk, w_mv, w_co, ln2_g, ln2_b, w_router, b_router, w1, b1, w2, b2, ln3_g, ln3_b):
    bp, lp, d = x_prompt.shape
    bs, ls, _ = x_sample.shape
    depth = w_in.shape[0]
    n_experts = w_router.shape[2]
    n_mem = mem_prompt.shape[1]
    tp, ts = bp * lp, bs * ls
    t_all = tp + ts
    alpha = (2 * depth) ** 0.25
    assert cache_swa_k.shape[2] == WINDOW and d == MIX_W

    x_all = _ln_in(x_prompt.reshape(tp, d), x_sample.reshape(ts, d), ln_in_g, ln_in_b)
    xb_all = jnp.zeros((t_all, d), BF16)
    gate_all = jnp.zeros((t_all, LANE), F32)
    ids_all = jnp.zeros((t_all, LANE), jnp.int32)

    cos_p, sin_p = _rope_tables(jnp.arange(lp))
    cos_s, sin_s = _rope_tables(PAST_LEN + jnp.arange(ls))
    mem_flat = mem_prompt.reshape(bp * n_mem, d)
    zeros_cbuf = jnp.zeros((bp, GDN_CONV - 1, 3 * GDN_W), F32)
    zeros_s = jnp.zeros((bp, GDN_HEADS, HEAD_DIM, HEAD_DIM), F32)
    zeros_sbuf = jnp.zeros((bp, SC_CONV - 1, SC_W), F32)
    zeros_hist = jnp.zeros((bp, WINDOW, SWA_KV_W), F32)

    outs = {k: [] for k in ("p_S", "p_conv", "p_k", "p_v", "p_sc", "p_mk", "p_mv",
                            "s_S", "s_conv", "s_k", "s_v", "s_sc")}
    for l in range(depth):
        w_in_l = _permute_w_in(w_in[l]).astype(BF16)
        gp = jnp.zeros((SUBLANE, LANE), F32)
        gp = gp.at[0, BA_A:BA_A + GDN_HEADS].set(gdn_a_log[l]).at[1, BA_A:BA_A + GDN_HEADS].set(gdn_dt_bias[l])
        ng = gdn_norm_g[l].reshape(1, HEAD_DIM)
        sinks = jnp.zeros((1, LANE), F32).at[0, 0:SWA_Q_HEADS].set(swa_sinks[l])
        wr = jnp.concatenate([w_router[l], jnp.zeros((d, LANE - n_experts), F32)], axis=1).astype(BF16)
        br = jnp.concatenate([b_router[l], jnp.full((LANE - n_experts,), NEG_BIG, F32)]).reshape(1, LANE)

        mkv = _proj(mem_flat, jnp.concatenate([w_mk[l], w_mv[l]], axis=1).astype(BF16), "mem_kv")
        mk_p = mkv[:, 0:MEM_W].reshape(bp, n_mem, MEM_W)
        mv_p = mkv[:, MEM_W:2 * MEM_W].reshape(bp, n_mem, MEM_W)
        outs["p_mk"].append(mk_p.reshape(bp, n_mem, MEM_HEADS, HEAD_DIM))
        outs["p_mv"].append(mv_p.reshape(bp, n_mem, MEM_HEADS, HEAD_DIM))

        h_all = _proj(x_all, w_in_l, "w_in")

        v_p = h_all[0:tp, COL_SWA_V:COL_SWA_V + SWA_KV_W].reshape(bp, lp, SWA_KV_W)[:, lp - WINDOW:]
        v_s = h_all[tp:, COL_SWA_V:COL_SWA_V + SWA_KV_W]
        h_all, c_p, s_p, sc_p = _gdn_sconv(h_all, 0, bp, lp, zeros_cbuf, zeros_s, zeros_sbuf,
                                           gdn_conv_w[l], sc_conv_w[l], gp, ng)
        h_all, c_s, s_s, sc_s = _gdn_sconv(h_all, tp, bs, ls, state_gdn_conv[l], state_gdn_S[l], state_sconv[l],
                                           gdn_conv_w[l], sc_conv_w[l], gp, ng)
        h_all, kr_p = _swa(h_all, 0, bp, lp, zeros_hist, zeros_hist, False, cos_p, sin_p, sinks)
        h_all, kr_s = _swa(h_all, tp, bs, ls, cache_swa_k[l].reshape(bs, WINDOW, SWA_KV_W),
                           cache_swa_v[l].reshape(bs, WINDOW, SWA_KV_W), True, cos_s, sin_s, sinks)
        outs["p_S"].append(s_p); outs["p_conv"].append(c_p); outs["p_sc"].append(sc_p)
        outs["s_S"].append(s_s); outs["s_conv"].append(c_s); outs["s_sc"].append(sc_s)
        outs["p_k"].append(kr_p.reshape(bp, lp, SWA_KV_W)[:, lp - WINDOW:].reshape(bp, WINDOW, SWA_KV_HEADS, HEAD_DIM))
        outs["p_v"].append(v_p.reshape(bp, WINDOW, SWA_KV_HEADS, HEAD_DIM))
        outs["s_k"].append(kr_s.reshape(bs, ls, SWA_KV_HEADS, HEAD_DIM))
        outs["s_v"].append(v_s.reshape(bs, ls, SWA_KV_HEADS, HEAD_DIM))

        x_all = _oproj(h_all, x_all, w_o[l].astype(BF16), ln1_g[l], ln1_b[l], alpha)

        wcq = w_cq[l].astype(BF16)
        wco = w_co[l].astype(BF16)
        x_all, xb_all, gate_all, ids_all = _memattn_router(
            x_all, xb_all, gate_all, ids_all, 0, bp, lp, mk_p, mv_p, wcq, wco, ln2_g[l], ln2_b[l], wr, br, alpha)
        x_all, xb_all, gate_all, ids_all = _memattn_router(
            x_all, xb_all, gate_all, ids_all, tp, bs, ls,
            cache_mem_k[l].reshape(bs, n_mem, MEM_W), cache_mem_v[l].reshape(bs, n_mem, MEM_W),
            wcq, wco, ln2_g[l], ln2_b[l], wr, br, alpha)

        dest, row_tok, blk_e, n_used = _route(ids_all[:, 0:TOP_K], n_experts, t_all)
        xs = jnp.take(xb_all, row_tok, axis=0)
        ys = _experts(xs, blk_e, n_used, w1[l], b1[l], w2[l], b2[l])
        yg = jnp.take(ys, dest.reshape(-1), axis=0).reshape(t_all, TOP_K * d)
        x_all = _combine(yg, gate_all, x_all, ln3_g[l], ln3_b[l], alpha)

    st = lambda k: jnp.stack(outs[k])
    return (x_all[0:tp].reshape(bp, lp, d), x_all[tp:].reshape(bs, ls, d),
            st("p_S"), st("p_conv"), st("p_k"), st("p_v"), st("p_sc"), st("p_mk"), st("p_mv"),
            st("s_S"), st("s_conv"), st("s_k"), st("s_v"), st("s_sc"))
```

```python
import functools

import jax
import jax.numpy as jnp
from jax import lax
from jax.experimental import pallas as pl
from jax.experimental.pallas import tpu as pltpu

F32 = jnp.float32
BF16 = jnp.bfloat16

LANE = 128
SUBLANE = 8
VMEM_LIMIT_BYTES = 56 * 1024 * 1024

CHUNK = 64
CHUNK_SHIFT = 6
HEAD_DIM = 64
GDN_HEADS = 6
GDN_W = GDN_HEADS * HEAD_DIM
GDN_CONV = 4
SWA_Q_HEADS = 6
SWA_KV_HEADS = 2
SWA_GROUP = SWA_Q_HEADS // SWA_KV_HEADS
SWA_W = SWA_Q_HEADS * HEAD_DIM
SWA_KV_W = SWA_KV_HEADS * HEAD_DIM
WINDOW = 128
ROPE_THETA = 10000.0
SC_W = 256
SC_CONV = 3
MEM_HEADS = 4
MEM_W = MEM_HEADS * HEAD_DIM
TOP_K = 4
SWIGLU_LIMIT = 7.0
SWIGLU_ALPHA = 1.702
PAST_LEN = 2048
LN_EPS = 1e-5
RMS_EPS = 1e-6
NEG_BIG = -1e30

COL_QKV = 0
COL_Z = 1152
COL_SC = 1536
COL_SWA = 2304
COL_SWA_V = COL_SWA + SWA_W + SWA_KV_W
COL_BA = COL_SWA + SWA_W + 2 * SWA_KV_W
P_PAD = 3072
BA_B = 0
BA_A = GDN_HEADS
MIX_W = 1024

MOE_BM = 512


def _cparams(*sem):
    return pltpu.CompilerParams(dimension_semantics=sem, vmem_limit_bytes=VMEM_LIMIT_BYTES)


def _pick_tile(n, candidates):
    for c in candidates:
        if n % c == 0:
            return c
    raise ValueError(f"no tile in {candidates} divides {n}")


def _ln(x, g, b):
    mu = jnp.mean(x, -1, keepdims=True)
    xc = x - mu
    var = jnp.mean(xc * xc, -1, keepdims=True)
    return xc * lax.rsqrt(var + LN_EPS) * g + b


def _dot(a, b):
    return jnp.dot(a.astype(BF16), b.astype(BF16), preferred_element_type=F32)


def _dot_nt(a, b):
    return lax.dot_general(a.astype(BF16), b.astype(BF16), (((1,), (1,)), ((), ())),
                           preferred_element_type=F32)


def _silu(x):
    return x * jax.nn.sigmoid(x)


def _ln_in_kernel(n_p_tiles, xp_ref, xs_ref, g_ref, b_ref, o_ref):
    i = pl.program_id(0)

    @pl.when(i < n_p_tiles)
    def _():
        o_ref[...] = _ln(xp_ref[...], g_ref[...], b_ref[...])

    @pl.when(i >= n_p_tiles)
    def _():
        o_ref[...] = _ln(xs_ref[...], g_ref[...], b_ref[...])


def _ln_in(xp, xs, g, b):
    tp, d = xp.shape
    ts = xs.shape[0]
    tm = _pick_tile(ts, (512, 256, 128, 64))
    assert tp % tm == 0
    npt, nst = tp // tm, ts // tm
    return pl.pallas_call(
        functools.partial(_ln_in_kernel, npt),
        grid=(npt + nst,),
        in_specs=[
            pl.BlockSpec((tm, d), lambda i: (jnp.minimum(i, npt - 1), 0)),
            pl.BlockSpec((tm, d), lambda i: (jnp.maximum(i - npt, 0), 0)),
            pl.BlockSpec((1, d), lambda i: (0, 0)),
            pl.BlockSpec((1, d), lambda i: (0, 0)),
        ],
        out_specs=pl.BlockSpec((tm, d), lambda i: (i, 0)),
        out_shape=jax.ShapeDtypeStruct((tp + ts, d), F32),
        compiler_params=_cparams("arbitrary"),
        name="ln_in",
    )(xp, xs, g.reshape(1, d), b.reshape(1, d))


def _proj_kernel(x_ref, w_ref, o_ref):
    o_ref[...] = _dot(x_ref[...], w_ref[...])


def _proj(x, w, name):
    t, k = x.shape
    n = w.shape[1]
    tm = _pick_tile(t, (512, 256, 128, 64))
    return pl.pallas_call(
        _proj_kernel,
        grid=(t // tm,),
        in_specs=[pl.BlockSpec((tm, k), lambda i: (i, 0)),
                  pl.BlockSpec((k, n), lambda i: (0, 0))],
        out_specs=pl.BlockSpec((tm, n), lambda i: (i, 0)),
        out_shape=jax.ShapeDtypeStruct((t, n), F32),
        compiler_params=_cparams("arbitrary"),
        name=name,
    )(x, w)


def _split2(x):
    hi = x.astype(BF16)
    lo = (x - hi.astype(F32)).astype(BF16)
    return hi, lo


def _split3(x):
    hi = x.astype(BF16)
    r1 = x - hi.astype(F32)
    mid = r1.astype(BF16)
    lo = (r1 - mid.astype(F32)).astype(BF16)
    return hi, mid, lo


def _sel_dot2(x, sel2):
    return jnp.dot(jnp.concatenate(_split2(x), axis=1), sel2, preferred_element_type=F32)


def _gdn_kernel(lt, h_ref, cbuf_ref, s0_ref, sbuf_ref, cw_ref, scw_ref, gp_ref, ng_ref,
                ho_ref, cnew_ref, snew_ref, scnew_ref, ext_scr, uext_scr, s_scr, o_scr):
    i = pl.program_id(1)
    n_chunks = lt // CHUNK
    pad = SUBLANE

    @pl.when(i == 0)
    def _():
        ext_scr[0:pad, :] = jnp.concatenate(
            [jnp.zeros((pad - (GDN_CONV - 1), 3 * GDN_W), F32), cbuf_ref[0]], axis=0)
        uext_scr[0:pad, :] = jnp.concatenate(
            [jnp.zeros((pad - (SC_CONV - 1), SC_W), F32), sbuf_ref[0]], axis=0)
        zero_blk = jnp.zeros((HEAD_DIM, HEAD_DIM), F32)
        for p in range(GDN_HEADS // 2):
            s_scr[p] = jnp.concatenate(
                [jnp.concatenate([s0_ref[0, 2 * p], zero_blk], axis=1),
                 jnp.concatenate([zero_blk, s0_ref[0, 2 * p + 1]], axis=1)], axis=0)

    ext_scr[pad:pad + lt, :] = h_ref[:, COL_QKV:COL_QKV + 3 * GDN_W]
    conv = cw_ref[0:1, :] * ext_scr[pl.ds(pad - 3, lt), :]
    for j in range(1, GDN_CONV):
        conv = conv + cw_ref[j:j + 1, :] * ext_scr[pl.ds(pad - 3 + j, lt), :]
    cnew_ref[0] = ext_scr[pl.ds(lt + pad - 3, 3), :]
    ext_scr[0:pad, :] = ext_scr[lt:lt + pad, :]
    qkv = _silu(conv)
    q = qkv[:, 0:GDN_W]
    k = qkv[:, GDN_W:2 * GDN_W]
    v = qkv[:, 2 * GDN_W:3 * GDN_W]

    uext_scr[pad:pad + lt, :] = (h_ref[:, COL_SC + SC_W:COL_SC + 2 * SC_W]
                                 * h_ref[:, COL_SC + 2 * SC_W:COL_SC + 3 * SC_W])
    y = scw_ref[0:1, :] * uext_scr[pl.ds(pad - 2, lt), :]
    for j in range(1, SC_CONV):
        y = y + scw_ref[j:j + 1, :] * uext_scr[pl.ds(pad - 2 + j, lt), :]
    scnew_ref[0] = uext_scr[pl.ds(lt + pad - 2, 2), :]
    uext_scr[0:pad, :] = uext_scr[lt:lt + pad, :]
    ho_ref[:, 2 * GDN_W:MIX_W] = h_ref[:, COL_SC:COL_SC + SC_W] * y
    ho_ref[:, GDN_W:2 * GDN_W] = h_ref[:, GDN_W:2 * GDN_W]

    r384 = lax.broadcasted_iota(jnp.int32, (GDN_W, GDN_W), 0)
    c384 = lax.broadcasted_iota(jnp.int32, (GDN_W, GDN_W), 1)
    head_ones = jnp.where((r384 >> CHUNK_SHIFT) == (c384 >> CHUNK_SHIFT), 1.0, 0.0).astype(BF16)
    head_ones2 = jnp.concatenate([head_ones, head_ones], axis=0)
    r128 = lax.broadcasted_iota(jnp.int32, (LANE, GDN_W), 0)
    c128 = lax.broadcasted_iota(jnp.int32, (LANE, GDN_W), 1) >> CHUNK_SHIFT
    exp_a = jnp.where(r128 == c128 + BA_A, 1.0, 0.0).astype(BF16)
    exp_b = jnp.where(r128 == c128 + BA_B, 1.0, 0.0).astype(BF16)
    exp_a3 = jnp.concatenate([exp_a, exp_a, exp_a], axis=0)
    exp_b2 = jnp.concatenate([exp_b, exp_b], axis=0)

    qn = q * lax.rsqrt(_sel_dot2(q * q, head_ones2) + RMS_EPS) * HEAD_DIM ** -0.5
    kn = k * lax.rsqrt(_sel_dot2(k * k, head_ones2) + RMS_EPS)

    ba = h_ref[:, COL_BA:COL_BA + LANE]
    lane_ba = lax.broadcasted_iota(jnp.int32, ba.shape, 1)
    is_a = (lane_ba >= BA_A) & (lane_ba < BA_A + GDN_HEADS)
    beta_blk = jax.nn.sigmoid(ba)
    sp_in = ba + gp_ref[1:2, :]
    softplus = jnp.maximum(sp_in, 0.0) + jnp.log(1.0 + jnp.exp(-jnp.abs(sp_in)))
    g_blk = jnp.where(is_a, -jnp.exp(gp_ref[0:1, :]) * softplus, 0.0)
    row_t = lax.broadcasted_iota(jnp.int32, (lt, lt), 0)
    col_t = lax.broadcasted_iota(jnp.int32, (lt, lt), 1)
    ltri = jnp.where(((row_t >> CHUNK_SHIFT) == (col_t >> CHUNK_SHIFT)) & (col_t <= row_t), 1.0, 0.0).astype(BF16)
    gam3 = jnp.dot(ltri, jnp.concatenate(_split3(g_blk), axis=1), preferred_element_type=F32)
    gam_blk = gam3[:, 0:LANE] + gam3[:, LANE:2 * LANE] + gam3[:, 2 * LANE:3 * LANE]
    gam = jnp.dot(jnp.concatenate(_split3(gam_blk), axis=1), exp_a3, preferred_element_type=F32)
    glast = jnp.concatenate(
        [jnp.broadcast_to(gam[(c + 1) * CHUNK - 1:(c + 1) * CHUNK, :], (CHUNK, GDN_W)) for c in range(n_chunks)],
        axis=0)
    g_exp = jnp.exp(gam)
    beta = _sel_dot2(beta_blk, exp_b2)
    kbeta = kn * beta
    qn_b = qn.astype(BF16)
    kn_b = kn.astype(BF16)
    kbeta_b = kbeta.astype(BF16)
    qg_b = (qn * g_exp).astype(BF16)
    kb_b = (kbeta * g_exp).astype(BF16)
    vb_b = (v * beta).astype(BF16)
    kt_b = (kn * jnp.exp(glast - gam)).astype(BF16)

    row = lax.broadcasted_iota(jnp.int32, (CHUNK, LANE), 0)
    lane = lax.broadcasted_iota(jnp.int32, (CHUNK, LANE), 1)
    col = lane & (HEAD_DIM - 1)
    left = lane < HEAD_DIM
    m_incl = col <= row
    m_strict = col < row
    m_diag = col == row
    eye = jnp.where(m_diag, 1.0, 0.0)
    m_base = ((row >> 1) == (col >> 1)) & m_strict
    lvl_masks = [((row >> (sh + 1)) == (col >> (sh + 1))) & ((row >> sh) != (col >> sh)) & m_strict
                 for sh in range(1, CHUNK_SHIFT)]
    r2 = lax.broadcasted_iota(jnp.int32, (LANE, LANE), 0)
    c2 = lax.broadcasted_iota(jnp.int32, (LANE, LANE), 1)
    m_bd = (r2 >> CHUNK_SHIFT) == (c2 >> CHUNK_SHIFT)

    def bd(x):
        zero = jnp.zeros_like(x)
        return jnp.concatenate([jnp.where(left, x, zero), jnp.where(left, zero, x)], axis=0)

    units = [(c, p) for c in range(n_chunks) for p in range(GDN_HEADS // 2)]
    rsl = lambda c: slice(c * CHUNK, (c + 1) * CHUNK)
    psl = lambda p: slice(p * LANE, (p + 1) * LANE)

    a_b, t_inv, qd_b = {}, {}, {}
    for c, p in units:
        gc = gam[rsl(c), psl(p)]
        gr = jnp.sum(jnp.where(m_diag, gc, 0.0), axis=0, keepdims=True)
        dec = jnp.exp(jnp.where(m_incl, gc - gr, NEG_BIG))
        kq = lax.dot_general(jnp.concatenate([qn_b[rsl(c), psl(p)], kbeta_b[rsl(c), psl(p)]], axis=0),
                             bd(kn_b[rsl(c), psl(p)]), (((1,), (1,)), ((), ())), preferred_element_type=F32)
        qd_b[c, p] = (kq[0:CHUNK] * dec).astype(BF16)
        a_mat = jnp.where(m_strict, dec * kq[CHUNK:2 * CHUNK], 0.0)
        a_b[c, p] = a_mat.astype(BF16)
        t_inv[c, p] = eye - jnp.where(m_base, a_mat, 0.0)
    zero_b = jnp.zeros((CHUNK, LANE), BF16)
    for lm in lvl_masks:
        for u_ in units:
            w_lvl = jnp.dot(jnp.where(lm, a_b[u_], zero_b), bd(t_inv[u_].astype(BF16)), preferred_element_type=F32)
            t_inv[u_] = t_inv[u_] - jnp.dot(t_inv[u_].astype(BF16), bd(w_lvl.astype(BF16)),
                                            preferred_element_type=F32)
    u0, w_b, kw_b, q_add = {}, {}, {}, {}
    for c, p in units:
        rhs = jnp.concatenate([bd(vb_b[rsl(c), psl(p)]), bd(kb_b[rsl(c), psl(p)])], axis=1)
        x_sol = jnp.dot(t_inv[c, p].astype(BF16), rhs, preferred_element_type=F32)
        u0[c, p] = x_sol[:, 0:LANE]
        w_b[c, p] = x_sol[:, LANE:2 * LANE].astype(BF16)
        ktx = lax.dot_general(kt_b[rsl(c), psl(p)], x_sol.astype(BF16), (((0,), (0,)), ((), ())),
                              preferred_element_type=F32)
        q_add[c, p] = jnp.where(m_bd, ktx[:, 0:LANE], 0.0)
        kw_b[c, p] = jnp.where(m_bd, ktx[:, LANE:2 * LANE], 0.0).astype(BF16)

    s_start = {}
    for p in range(GDN_HEADS // 2):
        s_cur = s_scr[p]
        for c in range(n_chunks):
            s_bf = s_cur.astype(BF16)
            s_start[c, p] = s_bf
            e_last = jnp.exp(gam[(c + 1) * CHUNK - 1:(c + 1) * CHUNK, psl(p)])
            s_cur = s_cur * e_last - jnp.dot(kw_b[c, p], s_bf, preferred_element_type=F32) + q_add[c, p]
        s_scr[p] = s_cur

    for c, p in units:
        wq = jnp.dot(jnp.concatenate([w_b[c, p], qg_b[rsl(c), psl(p)]], axis=0), s_start[c, p],
                     preferred_element_type=F32)
        u = u0[c, p] - wq[0:CHUNK]
        o_scr[rsl(c), psl(p)] = wq[CHUNK:2 * CHUNK] + jnp.dot(qd_b[c, p], bd(u.astype(BF16)),
                                                               preferred_element_type=F32)

    o = o_scr[...]
    o_ms = _sel_dot2(o * o, head_ones2) * (1.0 / HEAD_DIM)
    ho_ref[:, 0:GDN_W] = o * lax.rsqrt(o_ms + RMS_EPS) * ng_ref[...] * _silu(h_ref[:, COL_Z:COL_Z + GDN_W])

    @pl.when(i == pl.num_programs(1) - 1)
    def _():
        for p in range(GDN_HEADS // 2):
            snew_ref[0, 2 * p] = s_scr[p, 0:HEAD_DIM, 0:HEAD_DIM]
            snew_ref[0, 2 * p + 1] = s_scr[p, HEAD_DIM:LANE, HEAD_DIM:LANE]


def _gdn_sconv(h_all, row_off, nb, seq, cbuf, s0, sbuf, cw, scw, gp, ng):
    lt = _pick_tile(seq, (256, 128, 64))
    nt = seq // lt
    off = row_off // lt
    assert row_off % lt == 0
    tok = lambda b, i: (off + b * nt + i, 0)
    return pl.pallas_call(
        functools.partial(_gdn_kernel, lt),
        grid=(nb, nt),
        in_specs=[
            pl.BlockSpec((lt, P_PAD), tok),
            pl.BlockSpec((1, GDN_CONV - 1, 3 * GDN_W), lambda b, i: (b, 0, 0)),
            pl.BlockSpec((1, GDN_HEADS, HEAD_DIM, HEAD_DIM), lambda b, i: (b, 0, 0, 0)),
            pl.BlockSpec((1, SC_CONV - 1, SC_W), lambda b, i: (b, 0, 0)),
            pl.BlockSpec((GDN_CONV, 3 * GDN_W), lambda b, i: (0, 0)),
            pl.BlockSpec((SC_CONV, SC_W), lambda b, i: (0, 0)),
            pl.BlockSpec((SUBLANE, LANE), lambda b, i: (0, 0)),
            pl.BlockSpec((1, GDN_W), lambda b, i: (0, 0)),
        ],
        out_specs=[
            pl.BlockSpec((lt, MIX_W), tok),
            pl.BlockSpec((1, GDN_CONV - 1, 3 * GDN_W), lambda b, i: (b, 0, 0)),
            pl.BlockSpec((1, GDN_HEADS, HEAD_DIM, HEAD_DIM), lambda b, i: (b, 0, 0, 0)),
            pl.BlockSpec((1, SC_CONV - 1, SC_W), lambda b, i: (b, 0, 0)),
        ],
        out_shape=[
            jax.ShapeDtypeStruct(h_all.shape, F32),
            jax.ShapeDtypeStruct((nb, GDN_CONV - 1, 3 * GDN_W), F32),
            jax.ShapeDtypeStruct((nb, GDN_HEADS, HEAD_DIM, HEAD_DIM), F32),
            jax.ShapeDtypeStruct((nb, SC_CONV - 1, SC_W), F32),
        ],
        scratch_shapes=[
            pltpu.VMEM((lt + SUBLANE, 3 * GDN_W), F32),
            pltpu.VMEM((lt + SUBLANE, SC_W), F32),
            pltpu.VMEM((GDN_HEADS // 2, LANE, LANE), F32),
            pltpu.VMEM((lt, GDN_W), F32),
        ],
        input_output_aliases={0: 0},
        compiler_params=_cparams("arbitrary", "arbitrary"),
        name="gdn_sconv",
    )(h_all, cbuf, s0, sbuf, cw, scw, gp, ng)


def _rope(x, cos, sin_signed):
    w = x.shape[1]
    nrep = w // LANE
    if nrep > 1:
        cos = jnp.concatenate([cos] * nrep, axis=1)
        sin_signed = jnp.concatenate([sin_signed] * nrep, axis=1)
    lane = lax.broadcasted_iota(jnp.int32, x.shape, 1) & (HEAD_DIM - 1)
    half = HEAD_DIM // 2
    swapped = jnp.where(lane < half, pltpu.roll(x, w - half, 1), pltpu.roll(x, half, 1))
    return x * cos + swapped * sin_signed


def _swa_kernel(lt, hist_valid, h_ref, hk_ref, hv_ref, cos_ref, sin_ref, sink_ref,
                o_ref, kr_ref, kall_scr, vall_scr):
    i = pl.program_id(1)

    @pl.when(i == 0)
    def _():
        kall_scr[0:WINDOW, :] = hk_ref[0]
        vall_scr[0:WINDOW, :] = hv_ref[0]

    cos = cos_ref[...]
    sin = sin_ref[...]
    q = _rope(h_ref[:, 0:SWA_W], cos, sin)
    k = _rope(h_ref[:, SWA_W:SWA_W + SWA_KV_W], cos, sin)
    kr_ref[...] = k
    kall_scr[WINDOW:WINDOW + lt, :] = k
    vall_scr[WINDOW:WINDOW + lt, :] = h_ref[:, SWA_W + SWA_KV_W:SWA_W + 2 * SWA_KV_W]
    kall = kall_scr[...]
    vall = vall_scr[...]

    nk = lt + WINDOW
    qc = lax.broadcasted_iota(jnp.int32, (lt, nk), 0) >> CHUNK_SHIFT
    kc = lax.broadcasted_iota(jnp.int32, (lt, nk), 1) >> CHUNK_SHIFT
    band = (kc >= qc) & (kc <= qc + WINDOW // CHUNK)
    if not hist_valid:
        band = band & ((kc >= WINDOW // CHUNK) | (i > 0))

    for j in range(SWA_KV_HEADS):
        kj = kall[:, j * HEAD_DIM:(j + 1) * HEAD_DIM]
        vj = vall[:, j * HEAD_DIM:(j + 1) * HEAD_DIM]
        for g in range(SWA_GROUP):
            hq = j * SWA_GROUP + g
            s = _dot_nt(q[:, hq * HEAD_DIM:(hq + 1) * HEAD_DIM], kj) * HEAD_DIM ** -0.5
            s = jnp.where(band, s, NEG_BIG)
            sink = sink_ref[0:1, hq:hq + 1]
            m = jnp.maximum(jnp.max(s, -1, keepdims=True), sink)
            e = jnp.exp(s - m)
            den = jnp.sum(e, -1, keepdims=True) + jnp.exp(sink - m)
            o_ref[:, hq * HEAD_DIM:(hq + 1) * HEAD_DIM] = _dot(e / den, vj)

    if lt >= WINDOW:
        kall_scr[0:WINDOW, :] = kall_scr[lt:lt + WINDOW, :]
        vall_scr[0:WINDOW, :] = vall_scr[lt:lt + WINDOW, :]


def _swa(h_all, row_off, nb, seq, hist_k, hist_v, hist_valid, cos, sin, sinks):
    lt = _pick_tile(seq, (256, 128, 64))
    nt = seq // lt
    assert nt == 1 or lt >= WINDOW
    off = row_off // lt
    assert row_off % lt == 0
    swa_blk_w = P_PAD - COL_SWA
    return pl.pallas_call(
        functools.partial(_swa_kernel, lt, hist_valid),
        grid=(nb, nt),
        in_specs=[
            pl.BlockSpec((lt, swa_blk_w), lambda b, i: (off + b * nt + i, COL_SWA // swa_blk_w)),
            pl.BlockSpec((1, WINDOW, SWA_KV_W), lambda b, i: (b, 0, 0)),
            pl.BlockSpec((1, WINDOW, SWA_KV_W), lambda b, i: (b, 0, 0)),
            pl.BlockSpec((lt, LANE), lambda b, i: (i, 0)),
            pl.BlockSpec((lt, LANE), lambda b, i: (i, 0)),
            pl.BlockSpec((1, LANE), lambda b, i: (0, 0)),
        ],
        out_specs=[
            pl.BlockSpec((lt, SWA_W), lambda b, i: (off + b * nt + i, 1)),
            pl.BlockSpec((lt, SWA_KV_W), lambda b, i: (b * nt + i, 0)),
        ],
        out_shape=[
            jax.ShapeDtypeStruct(h_all.shape, F32),
            jax.ShapeDtypeStruct((nb * seq, SWA_KV_W), F32),
        ],
        scratch_shapes=[
            pltpu.VMEM((lt + WINDOW, SWA_KV_W), F32),
            pltpu.VMEM((lt + WINDOW, SWA_KV_W), F32),
        ],
        input_output_aliases={0: 0},
        compiler_params=_cparams("arbitrary", "arbitrary"),
        name="swa",
    )(h_all, hist_k, hist_v, cos, sin, sinks)


def _oproj_kernel(alpha, mix_ref, x_ref, w_ref, g_ref, b_ref, o_ref):
    mix = _dot(mix_ref[...], w_ref[...])
    o_ref[...] = _ln(alpha * x_ref[...] + mix, g_ref[...], b_ref[...])


def _oproj(h_all, x_all, w_o, g, b, alpha):
    t, d = x_all.shape
    tm = _pick_tile(t, (512, 256, 128, 64))
    return pl.pallas_call(
        functools.partial(_oproj_kernel, alpha),
        grid=(t // tm,),
        in_specs=[
            pl.BlockSpec((tm, MIX_W), lambda i: (i, 0)),
            pl.BlockSpec((tm, d), lambda i: (i, 0)),
            pl.BlockSpec((MIX_W, d), lambda i: (0, 0)),
            pl.BlockSpec((1, d), lambda i: (0, 0)),
            pl.BlockSpec((1, d), lambda i: (0, 0)),
        ],
        out_specs=pl.BlockSpec((tm, d), lambda i: (i, 0)),
        out_shape=jax.ShapeDtypeStruct((t, d), F32),
        input_output_aliases={1: 0},
        compiler_params=_cparams("arbitrary"),
        name="oproj_ln1",
    )(h_all, x_all, w_o, g.reshape(1, d), b.reshape(1, d))


def _memattn_kernel(alpha, x_ref, xb_in_ref, gate_in_ref, ids_in_ref, mk_ref, mv_ref, wcq_ref, wco_ref,
                    g_ref, b_ref, wr_ref, br_ref, x_out_ref, xb_ref, gate_ref, ids_ref):
    del xb_in_ref, gate_in_ref, ids_in_ref
    x = x_ref[...]
    q = _dot(x, wcq_ref[...])
    mk = mk_ref[0]
    mv = mv_ref[0]
    outs = []
    for h in range(MEM_HEADS):
        sl = slice(h * HEAD_DIM, (h + 1) * HEAD_DIM)
        s = _dot_nt(q[:, sl], mk[:, sl]) * HEAD_DIM ** -0.5
        e = jnp.exp(s - jnp.max(s, -1, keepdims=True))
        p = e / jnp.sum(e, -1, keepdims=True)
        outs.append(_dot(p, mv[:, sl]))
    o = _dot(jnp.concatenate(outs, axis=1), wco_ref[...])
    x2 = _ln(alpha * x + o, g_ref[...], b_ref[...])
    x_out_ref[...] = x2
    xb_ref[...] = x2.astype(BF16)

    logits = _dot(x2, wr_ref[...]) + br_ref[...]
    lane = lax.broadcasted_iota(jnp.int32, logits.shape, 1).astype(F32)
    vals, idxs = [], []
    for _ in range(TOP_K):
        m = jnp.max(logits, -1, keepdims=True)
        idx = jnp.min(jnp.where(logits == m, lane, float(LANE)), -1, keepdims=True)
        vals.append(m)
        idxs.append(idx)
        logits = jnp.where(lane == idx, 2 * NEG_BIG, logits)
    es = [jnp.exp(val - vals[0]) for val in vals]
    den = es[0] + es[1] + es[2] + es[3]
    gate = jnp.zeros(logits.shape, F32)
    ids = jnp.zeros(logits.shape, F32)
    for kk in range(TOP_K):
        gate = jnp.where(lane == kk, es[kk] / den, gate)
        ids = jnp.where(lane == kk, idxs[kk], ids)
    gate_ref[...] = gate
    ids_ref[...] = ids.astype(jnp.int32)


def _memattn_router(x_all, xb_all, gate_all, ids_all, row_off, nb, seq, mk, mv, wcq, wco, g, b, wr, br, alpha):
    t, d = x_all.shape
    tl = _pick_tile(seq, (512, 256, 128, 64))
    nt = seq // tl
    off = row_off // tl
    assert row_off % tl == 0
    n_mem = mk.shape[1]
    tok = lambda bb, i: (off + bb * nt + i, 0)
    const = lambda bb, i: (0, 0)
    return pl.pallas_call(
        functools.partial(_memattn_kernel, alpha),
        grid=(nb, nt),
        in_specs=[
            pl.BlockSpec((tl, d), tok),
            pl.BlockSpec(memory_space=pl.ANY),
            pl.BlockSpec(memory_space=pl.ANY),
            pl.BlockSpec(memory_space=pl.ANY),
            pl.BlockSpec((1, n_mem, MEM_W), lambda bb, i: (bb, 0, 0)),
            pl.BlockSpec((1, n_mem, MEM_W), lambda bb, i: (bb, 0, 0)),
            pl.BlockSpec((d, MEM_W), const),
            pl.BlockSpec((MEM_W, d), const),
            pl.BlockSpec((1, d), const),
            pl.BlockSpec((1, d), const),
            pl.BlockSpec((d, LANE), const),
            pl.BlockSpec((1, LANE), const),
        ],
        out_specs=[
            pl.BlockSpec((tl, d), tok),
            pl.BlockSpec((tl, d), tok),
            pl.BlockSpec((tl, LANE), tok),
            pl.BlockSpec((tl, LANE), tok),
        ],
        out_shape=[
            jax.ShapeDtypeStruct((t, d), F32),
            jax.ShapeDtypeStruct((t, d), BF16),
            jax.ShapeDtypeStruct((t, LANE), F32),
            jax.ShapeDtypeStruct((t, LANE), jnp.int32),
        ],
        input_output_aliases={0: 0, 1: 1, 2: 2, 3: 3},
        compiler_params=_cparams("arbitrary", "arbitrary"),
        name="memattn_router",
    )(x_all, xb_all, gate_all, ids_all, mk, mv, wcq, wco, g.reshape(1, d), b.reshape(1, d), wr, br)


def _expert_kernel(d_ff, be_ref, nu_ref, xs_ref, w1_ref, b1_ref, w2_ref, b2_ref, ys_ref, w1b_scr, w2b_scr):
    i = pl.program_id(0)
    prev = be_ref[jnp.maximum(i - 1, 0)]
    new_expert = (i == 0) | (be_ref[i] != prev)

    @pl.when(new_expert & (i < nu_ref[0]))
    def _():
        w1b_scr[...] = w1_ref[0, 0].astype(BF16)
        w2b_scr[...] = w2_ref[0, 0].astype(BF16)

    @pl.when(i < nu_ref[0])
    def _():
        hh = jnp.dot(xs_ref[...], w1b_scr[...], preferred_element_type=F32) + b1_ref[0, 0]
        glu = jnp.minimum(hh[:, 0:d_ff], SWIGLU_LIMIT)
        lin = jnp.clip(hh[:, d_ff:2 * d_ff], -SWIGLU_LIMIT, SWIGLU_LIMIT)
        act = glu * jax.nn.sigmoid(SWIGLU_ALPHA * glu) * (lin + 1.0)
        ys_ref[...] = jnp.dot(act.astype(BF16), w2b_scr[...], preferred_element_type=F32) + b2_ref[0, 0]


def _experts(xs, blk_e, n_used, layer, w1, b1, w2, b2):
    n_rows, d = xs.shape
    n_l, n_e, _, two_ff = w1.shape
    d_ff = two_ff // 2
    n_blocks = n_rows // MOE_BM
    row_blk = lambda i, be, nu: (jnp.minimum(i, nu[0] - 1), 0)
    exp_blk = lambda i, be, nu: (layer, be[jnp.minimum(i, nu[0] - 1)], 0, 0)
    return pl.pallas_call(
        functools.partial(_expert_kernel, d_ff),
        grid_spec=pltpu.PrefetchScalarGridSpec(
            num_scalar_prefetch=2,
            grid=(n_blocks,),
            in_specs=[
                pl.BlockSpec((MOE_BM, d), row_blk),
                pl.BlockSpec((1, 1, d, two_ff), exp_blk),
                pl.BlockSpec((1, 1, 1, two_ff), exp_blk),
                pl.BlockSpec((1, 1, d_ff, d), exp_blk),
                pl.BlockSpec((1, 1, 1, d), exp_blk),
            ],
            out_specs=pl.BlockSpec((MOE_BM, d), row_blk),
            scratch_shapes=[pltpu.VMEM((d, two_ff), BF16), pltpu.VMEM((d_ff, d), BF16)],
        ),
        out_shape=jax.ShapeDtypeStruct((n_rows, d), F32),
        compiler_params=_cparams("arbitrary"),
        name="moe_experts",
    )(blk_e, n_used, xs, w1, b1.reshape(n_l, n_e, 1, two_ff), w2, b2.reshape(n_l, n_e, 1, d))


def _combine_kernel(alpha, y0_ref, y1_ref, y2_ref, y3_ref, gate_ref, x_ref, g_ref, b_ref, o_ref):
    gate = gate_ref[...]
    acc = alpha * x_ref[...]
    for k, y_ref in enumerate((y0_ref, y1_ref, y2_ref, y3_ref)):
        acc = acc + gate[:, k:k + 1] * y_ref[...]
    o_ref[...] = _ln(acc, g_ref[...], b_ref[...])


def _combine(yg, gate_all, x_all, g, b, alpha):
    t, d = x_all.shape
    tm = _pick_tile(t, (512, 256, 128, 64))
    nt = t // tm
    y_spec = lambda k: pl.BlockSpec((tm, d), lambda i: (k * nt + i, 0))
    return pl.pallas_call(
        functools.partial(_combine_kernel, alpha),
        grid=(nt,),
        in_specs=[
            y_spec(0), y_spec(1), y_spec(2), y_spec(3),
            pl.BlockSpec((tm, LANE), lambda i: (i, 0)),
            pl.BlockSpec((tm, d), lambda i: (i, 0)),
            pl.BlockSpec((1, d), lambda i: (0, 0)),
            pl.BlockSpec((1, d), lambda i: (0, 0)),
        ],
        out_specs=pl.BlockSpec((tm, d), lambda i: (i, 0)),
        out_shape=jax.ShapeDtypeStruct((t, d), F32),
        input_output_aliases={5: 0},
        compiler_params=_cparams("arbitrary"),
        name="moe_combine_ln3",
    )(yg, yg, yg, yg, gate_all, x_all, g.reshape(1, d), b.reshape(1, d))


def _rank_kernel(ids_ref, rank_ref, cnt_ref, run_scr):
    i = pl.program_id(0)

    @pl.when(i == 0)
    def _():
        run_scr[...] = jnp.zeros_like(run_scr)

    ids = ids_ref[...].astype(F32)
    tm = ids.shape[0]
    lane = lax.broadcasted_iota(jnp.int32, ids.shape, 1).astype(F32)
    one_hots = [jnp.where(lane == ids[:, k:k + 1], 1.0, 0.0) for k in range(TOP_K)]
    multi_hot = one_hots[0] + one_hots[1] + one_hots[2] + one_hots[3]
    r = lax.broadcasted_iota(jnp.int32, (tm, tm), 0)
    c = lax.broadcasted_iota(jnp.int32, (tm, tm), 1)
    before = jnp.dot(jnp.where(c < r, 1.0, 0.0).astype(BF16), multi_hot.astype(BF16),
                     preferred_element_type=F32) + run_scr[0:1, :]
    rank = jnp.zeros(ids.shape, F32)
    for k in range(TOP_K):
        rank = jnp.where(lane == k, jnp.sum(one_hots[k] * before, -1, keepdims=True), rank)
    rank_ref[...] = rank.astype(jnp.int32)
    run_scr[0:1, :] = run_scr[0:1, :] + jnp.sum(multi_hot, 0, keepdims=True)
    cnt_ref[...] = run_scr[...]


def _dest_kernel(ids_ref, rank_ref, start_ref, dest_ref):
    ids = ids_ref[...].astype(F32)
    lane = lax.broadcasted_iota(jnp.int32, ids.shape, 1).astype(F32)
    start = start_ref[...]
    dest = rank_ref[...].astype(F32)
    for k in range(TOP_K):
        seg = jnp.sum(jnp.where(lane == ids[:, k:k + 1], start, 0.0), -1, keepdims=True)
        dest = jnp.where(lane == k, dest + seg, dest)
    dest_ref[...] = dest.astype(jnp.int32)


def _route(ids_all, n_experts):
    t = ids_all.shape[0]
    tk = t * TOP_K
    n_blocks = -(-tk // MOE_BM) + n_experts
    n_rows = n_blocks * MOE_BM
    assert n_rows < 2 ** 24
    tm = _pick_tile(t, (512, 256, 128, 64))
    tok = pl.BlockSpec((tm, LANE), lambda i: (i, 0))
    rank, cnt = pl.pallas_call(
        _rank_kernel,
        grid=(t // tm,),
        in_specs=[tok],
        out_specs=[tok, pl.BlockSpec((SUBLANE, LANE), lambda i: (0, 0))],
        out_shape=[jax.ShapeDtypeStruct((t, LANE), jnp.int32), jax.ShapeDtypeStruct((SUBLANE, LANE), F32)],
        scratch_shapes=[pltpu.VMEM((SUBLANE, LANE), F32)],
        compiler_params=_cparams("arbitrary"),
        name="moe_rank",
    )(ids_all)
    counts = cnt[0, 0:n_experts].astype(jnp.int32)
    padded = (counts + MOE_BM - 1) // MOE_BM * MOE_BM
    pad_end = jnp.cumsum(padded)
    pad_start = pad_end - padded
    start_row = jnp.zeros((1, LANE), F32).at[0, 0:n_experts].set(pad_start.astype(F32))
    dest = pl.pallas_call(
        _dest_kernel,
        grid=(t // tm,),
        in_specs=[tok, tok, pl.BlockSpec((1, LANE), lambda i: (0, 0))],
        out_specs=tok,
        out_shape=jax.ShapeDtypeStruct((t, LANE), jnp.int32),
        compiler_params=_cparams("arbitrary"),
        name="moe_dest",
    )(ids_all, rank, start_row)
    dest_km = dest[:, 0:TOP_K].T.reshape(-1)
    row_tok = jnp.zeros((n_rows,), jnp.int32).at[dest_km].set(
        jnp.tile(jnp.arange(t, dtype=jnp.int32), TOP_K))
    blk_e = jnp.minimum(jnp.searchsorted(pad_end, jnp.arange(n_blocks, dtype=jnp.int32) * MOE_BM, side='right'),
                        n_experts - 1).astype(jnp.int32)
    n_used = (pad_end[-1] // MOE_BM).astype(jnp.int32).reshape(1)
    return dest_km, row_tok, blk_e, n_used


def _rope_tables(pos):
    half = HEAD_DIM // 2
    inv = ROPE_THETA ** (-jnp.arange(half, dtype=F32) / half)
    ang = pos.astype(F32)[:, None] * inv[None, :]
    cos = jnp.cos(ang)
    sin = jnp.sin(ang)
    cos = jnp.concatenate([cos, cos, cos, cos], axis=1)
    sin = jnp.concatenate([-sin, sin, -sin, sin], axis=1)
    return cos, sin


def _permute_w_in(w):
    d = w.shape[0]
    o_b = 4 * GDN_W
    o_q = o_b + 2 * GDN_HEADS
    o_sc = o_q + SWA_W + 2 * SWA_KV_W
    ba = jnp.concatenate([w[:, o_b:o_q], jnp.zeros((d, LANE - 2 * GDN_HEADS), w.dtype)], axis=1)
    return jnp.concatenate([w[:, 0:o_b], w[:, o_sc:], w[:, o_q:o_sc], ba], axis=1)


def kernel(x_prompt, x_sample, state_gdn_S, state_gdn_conv, cache_swa_k, cache_swa_v, state_sconv, cache_mem_k, cache_mem_v, mem_prompt, ln_in_g, ln_in_b, w_in, gdn_conv_w, gdn_a_log, gdn_dt_bias, gdn_norm_g, swa_sinks, sc_conv_w, w_o, ln1_g, ln1_b, w_cq, w_mk, w_mv, w_co, ln2_g, ln2_b, w_router, b_router, w1, b1, w2, b2, ln3_g, ln3_b):
    bp, lp, d = x_prompt.shape
    bs, ls, _ = x_sample.shape
    depth = w_in.shape[0]
    n_experts = w_router.shape[2]
    n_mem = mem_prompt.shape[1]
    tp, ts = bp * lp, bs * ls
    t_all = tp + ts
    alpha = (2 * depth) ** 0.25
    assert cache_swa_k.shape[2] == WINDOW and d == MIX_W

    x_all = _ln_in(x_prompt.reshape(tp, d), x_sample.reshape(ts, d), ln_in_g, ln_in_b)
    xb_all = jnp.zeros((t_all, d), BF16)
    gate_all = jnp.zeros((t_all, LANE), F32)
    ids_all = jnp.zeros((t_all, LANE), jnp.int32)

    cos_p, sin_p = _rope_tables(jnp.arange(lp))
    cos_s, sin_s = _rope_tables(PAST_LEN + jnp.arange(ls))
    mem_flat = mem_prompt.reshape(bp * n_mem, d)
    zeros_cbuf = jnp.zeros((bp, GDN_CONV - 1, 3 * GDN_W), F32)
    zeros_s = jnp.zeros((bp, GDN_HEADS, HEAD_DIM, HEAD_DIM), F32)
    zeros_sbuf = jnp.zeros((bp, SC_CONV - 1, SC_W), F32)
    zeros_hist = jnp.zeros((bp, WINDOW, SWA_KV_W), F32)

    outs = {k: [] for k in ("p_S", "p_conv", "p_k", "p_v", "p_sc", "p_mk", "p_mv",
                            "s_S", "s_conv", "s_k", "s_v", "s_sc")}
    for l in range(depth):
        w_in_l = _permute_w_in(w_in[l]).astype(BF16)
        gp = jnp.zeros((SUBLANE, LANE), F32)
        gp = gp.at[0, BA_A:BA_A + GDN_HEADS].set(gdn_a_log[l]).at[1, BA_A:BA_A + GDN_HEADS].set(gdn_dt_bias[l])
        ng = jnp.tile(gdn_norm_g[l].reshape(1, HEAD_DIM), (1, GDN_HEADS))
        sinks = jnp.zeros((1, LANE), F32).at[0, 0:SWA_Q_HEADS].set(swa_sinks[l])
        wr = jnp.concatenate([w_router[l], jnp.zeros((d, LANE - n_experts), F32)], axis=1).astype(BF16)
        br = jnp.concatenate([b_router[l], jnp.full((LANE - n_experts,), NEG_BIG, F32)]).reshape(1, LANE)

        mkv = _proj(mem_flat, jnp.concatenate([w_mk[l], w_mv[l]], axis=1).astype(BF16), "mem_kv")
        mk_p = mkv[:, 0:MEM_W].reshape(bp, n_mem, MEM_W)
        mv_p = mkv[:, MEM_W:2 * MEM_W].reshape(bp, n_mem, MEM_W)
        outs["p_mk"].append(mk_p.reshape(bp, n_mem, MEM_HEADS, HEAD_DIM))
        outs["p_mv"].append(mv_p.reshape(bp, n_mem, MEM_HEADS, HEAD_DIM))

        h_all = _proj(x_all, w_in_l, "w_in")

        v_p = h_all[0:tp, COL_SWA_V:COL_SWA_V + SWA_KV_W].reshape(bp, lp, SWA_KV_W)[:, lp - WINDOW:]
        v_s = h_all[tp:, COL_SWA_V:COL_SWA_V + SWA_KV_W]
        h_all, c_p, s_p, sc_p = _gdn_sconv(h_all, 0, bp, lp, zeros_cbuf, zeros_s, zeros_sbuf,
                                           gdn_conv_w[l], sc_conv_w[l], gp, ng)
        h_all, c_s, s_s, sc_s = _gdn_sconv(h_all, tp, bs, ls, state_gdn_conv[l], state_gdn_S[l], state_sconv[l],
                                           gdn_conv_w[l], sc_conv_w[l], gp, ng)
        h_all, kr_p = _swa(h_all, 0, bp, lp, zeros_hist, zeros_hist, False, cos_p, sin_p, sinks)
        h_all, kr_s = _swa(h_all, tp, bs, ls, cache_swa_k[l].reshape(bs, WINDOW, SWA_KV_W),
                           cache_swa_v[l].reshape(bs, WINDOW, SWA_KV_W), True, cos_s, sin_s, sinks)
        outs["p_S"].append(s_p); outs["p_conv"].append(c_p); outs["p_sc"].append(sc_p)
        outs["s_S"].append(s_s); outs["s_conv"].append(c_s); outs["s_sc"].append(sc_s)
        outs["p_k"].append(kr_p.reshape(bp, lp, SWA_KV_W)[:, lp - WINDOW:].reshape(bp, WINDOW, SWA_KV_HEADS, HEAD_DIM))
        outs["p_v"].append(v_p.reshape(bp, WINDOW, SWA_KV_HEADS, HEAD_DIM))
        outs["s_k"].append(kr_s.reshape(bs, ls, SWA_KV_HEADS, HEAD_DIM))
        outs["s_v"].append(v_s.reshape(bs, ls, SWA_KV_HEADS, HEAD_DIM))

        x_all = _oproj(h_all, x_all, w_o[l].astype(BF16), ln1_g[l], ln1_b[l], alpha)

        wcq = w_cq[l].astype(BF16)
        wco = w_co[l].astype(BF16)
        x_all, xb_all, gate_all, ids_all = _memattn_router(
            x_all, xb_all, gate_all, ids_all, 0, bp, lp, mk_p, mv_p, wcq, wco, ln2_g[l], ln2_b[l], wr, br, alpha)
        x_all, xb_all, gate_all, ids_all = _memattn_router(
            x_all, xb_all, gate_all, ids_all, tp, bs, ls,
            cache_mem_k[l].reshape(bs, n_mem, MEM_W), cache_mem_v[l].reshape(bs, n_mem, MEM_W),
            wcq, wco, ln2_g[l], ln2_b[l], wr, br, alpha)

        dest_km, row_tok, blk_e, n_used = _route(ids_all, n_experts)
        xs = jnp.take(xb_all, row_tok, axis=0)
        ys = _experts(xs, blk_e, n_used, l, w1, b1, w2, b2)
        yg = jnp.take(ys, dest_km, axis=0)
        x_all = _combine(yg, gate_all, x_all, ln3_g[l], ln3_b[l], alpha)

    st = lambda k: jnp.stack(outs[k])
    return (x_all[0:tp].reshape(bp, lp, d), x_all[tp:].reshape(bs, ls, d),
            st("p_S"), st("p_conv"), st("p_k"), st("p_v"), st("p_sc"), st("p_mk"), st("p_mv"),
            st("s_S"), st("s_conv"), st("s_k"), st("s_v"), st("s_sc"))
```

```python
import functools

import jax
import jax.numpy as jnp
from jax import lax
from jax.experimental import pallas as pl
from jax.experimental.pallas import tpu as pltpu
from jax.experimental.pallas import tpu_sc as plsc

F32 = jnp.float32
BF16 = jnp.bfloat16

LANE = 128
SUBLANE = 8
VMEM_LIMIT_BYTES = 56 * 1024 * 1024

CHUNK = 64
CHUNK_SHIFT = 6
HEAD_DIM = 64
GDN_HEADS = 6
GDN_W = GDN_HEADS * HEAD_DIM
GDN_CONV = 4
SWA_Q_HEADS = 6
SWA_KV_HEADS = 2
SWA_GROUP = SWA_Q_HEADS // SWA_KV_HEADS
SWA_W = SWA_Q_HEADS * HEAD_DIM
SWA_KV_W = SWA_KV_HEADS * HEAD_DIM
WINDOW = 128
ROPE_THETA = 10000.0
SC_W = 256
SC_CONV = 3
MEM_HEADS = 4
MEM_W = MEM_HEADS * HEAD_DIM
TOP_K = 4
SWIGLU_LIMIT = 7.0
SWIGLU_ALPHA = 1.702
PAST_LEN = 2048
LN_EPS = 1e-5
RMS_EPS = 1e-6
NEG_BIG = -1e30

COL_QKV = 0
COL_Z = 1152
COL_SC = 1536
COL_SWA = 2304
COL_SWA_V = COL_SWA + SWA_W + SWA_KV_W
COL_BA = COL_SWA + SWA_W + 2 * SWA_KV_W
P_PAD = 3072
BA_B = 0
BA_A = GDN_HEADS
MIX_W = 1024

MOE_BM = 512
SC_WINDOW = 128


def _cparams(*sem):
    return pltpu.CompilerParams(dimension_semantics=sem, vmem_limit_bytes=VMEM_LIMIT_BYTES)


def _pick_tile(n, candidates):
    for c in candidates:
        if n % c == 0:
            return c
    raise ValueError(f"no tile in {candidates} divides {n}")


def _ln(x, g, b):
    mu = jnp.mean(x, -1, keepdims=True)
    xc = x - mu
    var = jnp.mean(xc * xc, -1, keepdims=True)
    return xc * lax.rsqrt(var + LN_EPS) * g + b


def _dot(a, b):
    return jnp.dot(a.astype(BF16), b.astype(BF16), preferred_element_type=F32)


def _dot_nt(a, b):
    return lax.dot_general(a.astype(BF16), b.astype(BF16), (((1,), (1,)), ((), ())),
                           preferred_element_type=F32)


def _silu(x):
    return x * jax.nn.sigmoid(x)


def _pack_halves(x):
    w = x.shape[1] // 2
    lo = lax.bitcast_convert_type(x[:, 0:w].astype(BF16).astype(F32), jnp.uint32)
    hi = lax.bitcast_convert_type(x[:, w:2 * w].astype(BF16).astype(F32), jnp.uint32)
    return (hi & jnp.uint32(0xFFFF0000)) | (lo >> 16)


def _unpack_halves(p):
    lo = lax.bitcast_convert_type(p << 16, F32)
    hi = lax.bitcast_convert_type(p & jnp.uint32(0xFFFF0000), F32)
    return jnp.concatenate([lo, hi], axis=1)


def _ln_in_kernel(n_p_tiles, xp_ref, xs_ref, g_ref, b_ref, o_ref):
    i = pl.program_id(0)

    @pl.when(i < n_p_tiles)
    def _():
        o_ref[...] = _ln(xp_ref[...], g_ref[...], b_ref[...])

    @pl.when(i >= n_p_tiles)
    def _():
        o_ref[...] = _ln(xs_ref[...], g_ref[...], b_ref[...])


def _ln_in(xp, xs, g, b):
    tp, d = xp.shape
    ts = xs.shape[0]
    tm = _pick_tile(ts, (512, 256, 128, 64))
    assert tp % tm == 0
    npt, nst = tp // tm, ts // tm
    return pl.pallas_call(
        functools.partial(_ln_in_kernel, npt),
        grid=(npt + nst,),
        in_specs=[
            pl.BlockSpec((tm, d), lambda i: (jnp.minimum(i, npt - 1), 0)),
            pl.BlockSpec((tm, d), lambda i: (jnp.maximum(i - npt, 0), 0)),
            pl.BlockSpec((1, d), lambda i: (0, 0)),
            pl.BlockSpec((1, d), lambda i: (0, 0)),
        ],
        out_specs=pl.BlockSpec((tm, d), lambda i: (i, 0)),
        out_shape=jax.ShapeDtypeStruct((tp + ts, d), F32),
        compiler_params=_cparams("arbitrary"),
        name="ln_in",
    )(xp, xs, g.reshape(1, d), b.reshape(1, d))


def _proj_kernel(x_ref, w_ref, o_ref):
    o_ref[...] = _dot(x_ref[...], w_ref[...])


def _proj(x, w, name):
    t, k = x.shape
    n = w.shape[1]
    tm = _pick_tile(t, (512, 256, 128, 64))
    return pl.pallas_call(
        _proj_kernel,
        grid=(t // tm,),
        in_specs=[pl.BlockSpec((tm, k), lambda i: (i, 0)),
                  pl.BlockSpec((k, n), lambda i: (0, 0))],
        out_specs=pl.BlockSpec((tm, n), lambda i: (i, 0)),
        out_shape=jax.ShapeDtypeStruct((t, n), F32),
        compiler_params=_cparams("arbitrary"),
        name=name,
    )(x, w)


def _split2(x):
    hi = x.astype(BF16)
    lo = (x - hi.astype(F32)).astype(BF16)
    return hi, lo


def _split3(x):
    hi = x.astype(BF16)
    r1 = x - hi.astype(F32)
    mid = r1.astype(BF16)
    lo = (r1 - mid.astype(F32)).astype(BF16)
    return hi, mid, lo


def _sel_dot2(x, sel2):
    return jnp.dot(jnp.concatenate(_split2(x), axis=1), sel2, preferred_element_type=F32)


def _gdn_kernel(lt, h_ref, cbuf_ref, s0_ref, sbuf_ref, cw_ref, scw_ref, gp_ref, ng_ref,
                ho_ref, cnew_ref, snew_ref, scnew_ref, ext_scr, uext_scr, s_scr, o_scr):
    i = pl.program_id(1)
    n_chunks = lt // CHUNK
    pad = SUBLANE

    @pl.when(i == 0)
    def _():
        ext_scr[0:pad, :] = jnp.concatenate(
            [jnp.zeros((pad - (GDN_CONV - 1), 3 * GDN_W), F32), cbuf_ref[0]], axis=0)
        uext_scr[0:pad, :] = jnp.concatenate(
            [jnp.zeros((pad - (SC_CONV - 1), SC_W), F32), sbuf_ref[0]], axis=0)
        zero_blk = jnp.zeros((HEAD_DIM, HEAD_DIM), F32)
        for p in range(GDN_HEADS // 2):
            s_scr[p] = jnp.concatenate(
                [jnp.concatenate([s0_ref[0, 2 * p], zero_blk], axis=1),
                 jnp.concatenate([zero_blk, s0_ref[0, 2 * p + 1]], axis=1)], axis=0)

    ext_scr[pad:pad + lt, :] = h_ref[:, COL_QKV:COL_QKV + 3 * GDN_W]
    conv = cw_ref[0:1, :] * ext_scr[pl.ds(pad - 3, lt), :]
    for j in range(1, GDN_CONV):
        conv = conv + cw_ref[j:j + 1, :] * ext_scr[pl.ds(pad - 3 + j, lt), :]
    cnew_ref[0] = ext_scr[pl.ds(lt + pad - 3, 3), :]
    ext_scr[0:pad, :] = ext_scr[lt:lt + pad, :]
    qkv = _silu(conv)
    q = qkv[:, 0:GDN_W]
    k = qkv[:, GDN_W:2 * GDN_W]
    v = qkv[:, 2 * GDN_W:3 * GDN_W]

    uext_scr[pad:pad + lt, :] = (h_ref[:, COL_SC + SC_W:COL_SC + 2 * SC_W]
                                 * h_ref[:, COL_SC + 2 * SC_W:COL_SC + 3 * SC_W])
    y = scw_ref[0:1, :] * uext_scr[pl.ds(pad - 2, lt), :]
    for j in range(1, SC_CONV):
        y = y + scw_ref[j:j + 1, :] * uext_scr[pl.ds(pad - 2 + j, lt), :]
    scnew_ref[0] = uext_scr[pl.ds(lt + pad - 2, 2), :]
    uext_scr[0:pad, :] = uext_scr[lt:lt + pad, :]
    ho_ref[:, 2 * GDN_W:MIX_W] = h_ref[:, COL_SC:COL_SC + SC_W] * y
    ho_ref[:, GDN_W:2 * GDN_W] = h_ref[:, GDN_W:2 * GDN_W]

    r384 = lax.broadcasted_iota(jnp.int32, (GDN_W, GDN_W), 0)
    c384 = lax.broadcasted_iota(jnp.int32, (GDN_W, GDN_W), 1)
    head_ones = jnp.where((r384 >> CHUNK_SHIFT) == (c384 >> CHUNK_SHIFT), 1.0, 0.0).astype(BF16)
    head_ones2 = jnp.concatenate([head_ones, head_ones], axis=0)
    r128 = lax.broadcasted_iota(jnp.int32, (LANE, GDN_W), 0)
    c128 = lax.broadcasted_iota(jnp.int32, (LANE, GDN_W), 1) >> CHUNK_SHIFT
    exp_a = jnp.where(r128 == c128 + BA_A, 1.0, 0.0).astype(BF16)
    exp_b = jnp.where(r128 == c128 + BA_B, 1.0, 0.0).astype(BF16)
    exp_a3 = jnp.concatenate([exp_a, exp_a, exp_a], axis=0)
    exp_b2 = jnp.concatenate([exp_b, exp_b], axis=0)

    qn = q * lax.rsqrt(_sel_dot2(q * q, head_ones2) + RMS_EPS) * HEAD_DIM ** -0.5
    kn = k * lax.rsqrt(_sel_dot2(k * k, head_ones2) + RMS_EPS)

    ba = h_ref[:, COL_BA:COL_BA + LANE]
    lane_ba = lax.broadcasted_iota(jnp.int32, ba.shape, 1)
    is_a = (lane_ba >= BA_A) & (lane_ba < BA_A + GDN_HEADS)
    beta_blk = jax.nn.sigmoid(ba)
    sp_in = ba + gp_ref[1:2, :]
    softplus = jnp.maximum(sp_in, 0.0) + jnp.log(1.0 + jnp.exp(-jnp.abs(sp_in)))
    g_blk = jnp.where(is_a, -jnp.exp(gp_ref[0:1, :]) * softplus, 0.0)
    row_t = lax.broadcasted_iota(jnp.int32, (lt, lt), 0)
    col_t = lax.broadcasted_iota(jnp.int32, (lt, lt), 1)
    ltri = jnp.where(((row_t >> CHUNK_SHIFT) == (col_t >> CHUNK_SHIFT)) & (col_t <= row_t), 1.0, 0.0).astype(BF16)
    gam3 = jnp.dot(ltri, jnp.concatenate(_split3(g_blk), axis=1), preferred_element_type=F32)
    gam_blk = gam3[:, 0:LANE] + gam3[:, LANE:2 * LANE] + gam3[:, 2 * LANE:3 * LANE]
    gam = jnp.dot(jnp.concatenate(_split3(gam_blk), axis=1), exp_a3, preferred_element_type=F32)
    glast = jnp.concatenate(
        [jnp.broadcast_to(gam[(c + 1) * CHUNK - 1:(c + 1) * CHUNK, :], (CHUNK, GDN_W)) for c in range(n_chunks)],
        axis=0)
    g_exp = jnp.exp(gam)
    beta = _sel_dot2(beta_blk, exp_b2)
    kbeta = kn * beta
    qn_b = qn.astype(BF16)
    kn_b = kn.astype(BF16)
    kbeta_b = kbeta.astype(BF16)
    qg_b = (qn * g_exp).astype(BF16)
    kb_b = (kbeta * g_exp).astype(BF16)
    vb_b = (v * beta).astype(BF16)
    kt_b = (kn * jnp.exp(glast - gam)).astype(BF16)

    row = lax.broadcasted_iota(jnp.int32, (CHUNK, LANE), 0)
    lane = lax.broadcasted_iota(jnp.int32, (CHUNK, LANE), 1)
    col = lane & (HEAD_DIM - 1)
    left = lane < HEAD_DIM
    m_incl = col <= row
    m_strict = col < row
    m_diag = col == row
    eye = jnp.where(m_diag, 1.0, 0.0)
    m_base = ((row >> 1) == (col >> 1)) & m_strict
    lvl_masks = [((row >> (sh + 1)) == (col >> (sh + 1))) & ((row >> sh) != (col >> sh)) & m_strict
                 for sh in range(1, CHUNK_SHIFT)]
    r2 = lax.broadcasted_iota(jnp.int32, (LANE, LANE), 0)
    c2 = lax.broadcasted_iota(jnp.int32, (LANE, LANE), 1)
    m_bd = (r2 >> CHUNK_SHIFT) == (c2 >> CHUNK_SHIFT)

    def bd(x):
        zero = jnp.zeros_like(x)
        return jnp.concatenate([jnp.where(left, x, zero), jnp.where(left, zero, x)], axis=0)

    units = [(c, p) for c in range(n_chunks) for p in range(GDN_HEADS // 2)]
    rsl = lambda c: slice(c * CHUNK, (c + 1) * CHUNK)
    psl = lambda p: slice(p * LANE, (p + 1) * LANE)

    a_b, t_inv, qd_b = {}, {}, {}
    for c, p in units:
        gc = gam[rsl(c), psl(p)]
        gr = jnp.sum(jnp.where(m_diag, gc, 0.0), axis=0, keepdims=True)
        dec = jnp.exp(jnp.where(m_incl, gc - gr, NEG_BIG))
        kq = lax.dot_general(jnp.concatenate([qn_b[rsl(c), psl(p)], kbeta_b[rsl(c), psl(p)]], axis=0),
                             bd(kn_b[rsl(c), psl(p)]), (((1,), (1,)), ((), ())), preferred_element_type=F32)
        qd_b[c, p] = (kq[0:CHUNK] * dec).astype(BF16)
        a_mat = jnp.where(m_strict, dec * kq[CHUNK:2 * CHUNK], 0.0)
        a_b[c, p] = a_mat.astype(BF16)
        t_inv[c, p] = eye - jnp.where(m_base, a_mat, 0.0)
    zero_b = jnp.zeros((CHUNK, LANE), BF16)
    for lm in lvl_masks:
        for u_ in units:
            w_lvl = jnp.dot(jnp.where(lm, a_b[u_], zero_b), bd(t_inv[u_].astype(BF16)), preferred_element_type=F32)
            t_inv[u_] = t_inv[u_] - jnp.dot(t_inv[u_].astype(BF16), bd(w_lvl.astype(BF16)),
                                            preferred_element_type=F32)
    u0, w_b, kw_b, q_add = {}, {}, {}, {}
    for c, p in units:
        rhs = jnp.concatenate([bd(vb_b[rsl(c), psl(p)]), bd(kb_b[rsl(c), psl(p)])], axis=1)
        x_sol = jnp.dot(t_inv[c, p].astype(BF16), rhs, preferred_element_type=F32)
        u0[c, p] = x_sol[:, 0:LANE]
        w_b[c, p] = x_sol[:, LANE:2 * LANE].astype(BF16)
        ktx = lax.dot_general(kt_b[rsl(c), psl(p)], x_sol.astype(BF16), (((0,), (0,)), ((), ())),
                              preferred_element_type=F32)
        q_add[c, p] = jnp.where(m_bd, ktx[:, 0:LANE], 0.0)
        kw_b[c, p] = jnp.where(m_bd, ktx[:, LANE:2 * LANE], 0.0).astype(BF16)

    s_start = {}
    for p in range(GDN_HEADS // 2):
        s_cur = s_scr[p]
        for c in range(n_chunks):
            s_bf = s_cur.astype(BF16)
            s_start[c, p] = s_bf
            e_last = jnp.exp(gam[(c + 1) * CHUNK - 1:(c + 1) * CHUNK, psl(p)])
            s_cur = s_cur * e_last - jnp.dot(kw_b[c, p], s_bf, preferred_element_type=F32) + q_add[c, p]
        s_scr[p] = s_cur

    for c, p in units:
        wq = jnp.dot(jnp.concatenate([w_b[c, p], qg_b[rsl(c), psl(p)]], axis=0), s_start[c, p],
                     preferred_element_type=F32)
        u = u0[c, p] - wq[0:CHUNK]
        o_scr[rsl(c), psl(p)] = wq[CHUNK:2 * CHUNK] + jnp.dot(qd_b[c, p], bd(u.astype(BF16)),
                                                               preferred_element_type=F32)

    o = o_scr[...]
    o_ms = _sel_dot2(o * o, head_ones2) * (1.0 / HEAD_DIM)
    ho_ref[:, 0:GDN_W] = o * lax.rsqrt(o_ms + RMS_EPS) * ng_ref[...] * _silu(h_ref[:, COL_Z:COL_Z + GDN_W])

    @pl.when(i == pl.num_programs(1) - 1)
    def _():
        for p in range(GDN_HEADS // 2):
            snew_ref[0, 2 * p] = s_scr[p, 0:HEAD_DIM, 0:HEAD_DIM]
            snew_ref[0, 2 * p + 1] = s_scr[p, HEAD_DIM:LANE, HEAD_DIM:LANE]


def _gdn_sconv(h_all, row_off, nb, seq, cbuf, s0, sbuf, cw, scw, gp, ng):
    lt = _pick_tile(seq, (256, 128, 64))
    nt = seq // lt
    off = row_off // lt
    assert row_off % lt == 0
    tok = lambda b, i: (off + b * nt + i, 0)
    return pl.pallas_call(
        functools.partial(_gdn_kernel, lt),
        grid=(nb, nt),
        in_specs=[
            pl.BlockSpec((lt, P_PAD), tok),
            pl.BlockSpec((1, GDN_CONV - 1, 3 * GDN_W), lambda b, i: (b, 0, 0)),
            pl.BlockSpec((1, GDN_HEADS, HEAD_DIM, HEAD_DIM), lambda b, i: (b, 0, 0, 0)),
            pl.BlockSpec((1, SC_CONV - 1, SC_W), lambda b, i: (b, 0, 0)),
            pl.BlockSpec((GDN_CONV, 3 * GDN_W), lambda b, i: (0, 0)),
            pl.BlockSpec((SC_CONV, SC_W), lambda b, i: (0, 0)),
            pl.BlockSpec((SUBLANE, LANE), lambda b, i: (0, 0)),
            pl.BlockSpec((1, GDN_W), lambda b, i: (0, 0)),
        ],
        out_specs=[
            pl.BlockSpec((lt, MIX_W), tok),
            pl.BlockSpec((1, GDN_CONV - 1, 3 * GDN_W), lambda b, i: (b, 0, 0)),
            pl.BlockSpec((1, GDN_HEADS, HEAD_DIM, HEAD_DIM), lambda b, i: (b, 0, 0, 0)),
            pl.BlockSpec((1, SC_CONV - 1, SC_W), lambda b, i: (b, 0, 0)),
        ],
        out_shape=[
            jax.ShapeDtypeStruct(h_all.shape, F32),
            jax.ShapeDtypeStruct((nb, GDN_CONV - 1, 3 * GDN_W), F32),
            jax.ShapeDtypeStruct((nb, GDN_HEADS, HEAD_DIM, HEAD_DIM), F32),
            jax.ShapeDtypeStruct((nb, SC_CONV - 1, SC_W), F32),
        ],
        scratch_shapes=[
            pltpu.VMEM((lt + SUBLANE, 3 * GDN_W), F32),
            pltpu.VMEM((lt + SUBLANE, SC_W), F32),
            pltpu.VMEM((GDN_HEADS // 2, LANE, LANE), F32),
            pltpu.VMEM((lt, GDN_W), F32),
        ],
        input_output_aliases={0: 0},
        compiler_params=_cparams("arbitrary", "arbitrary"),
        name="gdn_sconv",
    )(h_all, cbuf, s0, sbuf, cw, scw, gp, ng)


def _rope(x, cos, sin_signed):
    w = x.shape[1]
    nrep = w // LANE
    if nrep > 1:
        cos = jnp.concatenate([cos] * nrep, axis=1)
        sin_signed = jnp.concatenate([sin_signed] * nrep, axis=1)
    lane = lax.broadcasted_iota(jnp.int32, x.shape, 1) & (HEAD_DIM - 1)
    half = HEAD_DIM // 2
    swapped = jnp.where(lane < half, pltpu.roll(x, w - half, 1), pltpu.roll(x, half, 1))
    return x * cos + swapped * sin_signed


def _swa_kernel(lt, hist_valid, h_ref, hk_ref, hv_ref, cos_ref, sin_ref, sink_ref,
                o_ref, kr_ref, kall_scr, vall_scr):
    i = pl.program_id(1)

    @pl.when(i == 0)
    def _():
        kall_scr[0:WINDOW, :] = hk_ref[0]
        vall_scr[0:WINDOW, :] = hv_ref[0]

    cos = cos_ref[...]
    sin = sin_ref[...]
    q = _rope(h_ref[:, 0:SWA_W], cos, sin)
    k = _rope(h_ref[:, SWA_W:SWA_W + SWA_KV_W], cos, sin)
    kr_ref[...] = k
    kall_scr[WINDOW:WINDOW + lt, :] = k
    vall_scr[WINDOW:WINDOW + lt, :] = h_ref[:, SWA_W + SWA_KV_W:SWA_W + 2 * SWA_KV_W]
    kall = kall_scr[...]
    vall = vall_scr[...]

    nk = lt + WINDOW
    qc = lax.broadcasted_iota(jnp.int32, (lt, nk), 0) >> CHUNK_SHIFT
    kc = lax.broadcasted_iota(jnp.int32, (lt, nk), 1) >> CHUNK_SHIFT
    band = (kc >= qc) & (kc <= qc + WINDOW // CHUNK)
    if not hist_valid:
        band = band & ((kc >= WINDOW // CHUNK) | (i > 0))

    for j in range(SWA_KV_HEADS):
        kj = kall[:, j * HEAD_DIM:(j + 1) * HEAD_DIM]
        vj = vall[:, j * HEAD_DIM:(j + 1) * HEAD_DIM]
        for g in range(SWA_GROUP):
            hq = j * SWA_GROUP + g
            s = _dot_nt(q[:, hq * HEAD_DIM:(hq + 1) * HEAD_DIM], kj) * HEAD_DIM ** -0.5
            s = jnp.where(band, s, NEG_BIG)
            sink = sink_ref[0:1, hq:hq + 1]
            m = jnp.maximum(jnp.max(s, -1, keepdims=True), sink)
            e = jnp.exp(s - m)
            den = jnp.sum(e, -1, keepdims=True) + jnp.exp(sink - m)
            o_ref[:, hq * HEAD_DIM:(hq + 1) * HEAD_DIM] = _dot(e / den, vj)

    if lt >= WINDOW:
        kall_scr[0:WINDOW, :] = kall_scr[lt:lt + WINDOW, :]
        vall_scr[0:WINDOW, :] = vall_scr[lt:lt + WINDOW, :]


def _swa(h_all, row_off, nb, seq, hist_k, hist_v, hist_valid, cos, sin, sinks):
    lt = _pick_tile(seq, (256, 128, 64))
    nt = seq // lt
    assert nt == 1 or lt >= WINDOW
    off = row_off // lt
    assert row_off % lt == 0
    swa_blk_w = P_PAD - COL_SWA
    return pl.pallas_call(
        functools.partial(_swa_kernel, lt, hist_valid),
        grid=(nb, nt),
        in_specs=[
            pl.BlockSpec((lt, swa_blk_w), lambda b, i: (off + b * nt + i, COL_SWA // swa_blk_w)),
            pl.BlockSpec((1, WINDOW, SWA_KV_W), lambda b, i: (b, 0, 0)),
            pl.BlockSpec((1, WINDOW, SWA_KV_W), lambda b, i: (b, 0, 0)),
            pl.BlockSpec((lt, LANE), lambda b, i: (i, 0)),
            pl.BlockSpec((lt, LANE), lambda b, i: (i, 0)),
            pl.BlockSpec((1, LANE), lambda b, i: (0, 0)),
        ],
        out_specs=[
            pl.BlockSpec((lt, SWA_W), lambda b, i: (off + b * nt + i, 1)),
            pl.BlockSpec((lt, SWA_KV_W), lambda b, i: (b * nt + i, 0)),
        ],
        out_shape=[
            jax.ShapeDtypeStruct(h_all.shape, F32),
            jax.ShapeDtypeStruct((nb * seq, SWA_KV_W), F32),
        ],
        scratch_shapes=[
            pltpu.VMEM((lt + WINDOW, SWA_KV_W), F32),
            pltpu.VMEM((lt + WINDOW, SWA_KV_W), F32),
        ],
        input_output_aliases={0: 0},
        compiler_params=_cparams("arbitrary", "arbitrary"),
        name="swa",
    )(h_all, hist_k, hist_v, cos, sin, sinks)


def _oproj_kernel(alpha, mix_ref, x_ref, w_ref, g_ref, b_ref, o_ref):
    mix = _dot(mix_ref[...], w_ref[...])
    o_ref[...] = _ln(alpha * x_ref[...] + mix, g_ref[...], b_ref[...])


def _oproj(h_all, x_all, w_o, g, b, alpha):
    t, d = x_all.shape
    tm = _pick_tile(t, (512, 256, 128, 64))
    return pl.pallas_call(
        functools.partial(_oproj_kernel, alpha),
        grid=(t // tm,),
        in_specs=[
            pl.BlockSpec((tm, MIX_W), lambda i: (i, 0)),
            pl.BlockSpec((tm, d), lambda i: (i, 0)),
            pl.BlockSpec((MIX_W, d), lambda i: (0, 0)),
            pl.BlockSpec((1, d), lambda i: (0, 0)),
            pl.BlockSpec((1, d), lambda i: (0, 0)),
        ],
        out_specs=pl.BlockSpec((tm, d), lambda i: (i, 0)),
        out_shape=jax.ShapeDtypeStruct((t, d), F32),
        input_output_aliases={1: 0},
        compiler_params=_cparams("arbitrary"),
        name="oproj_ln1",
    )(h_all, x_all, w_o, g.reshape(1, d), b.reshape(1, d))


def _memattn_kernel(alpha, x_ref, xb_in_ref, gate_in_ref, ids_in_ref, mk_ref, mv_ref, wcq_ref, wco_ref,
                    g_ref, b_ref, wr_ref, br_ref, x_out_ref, xb_ref, gate_ref, ids_ref):
    del xb_in_ref, gate_in_ref, ids_in_ref
    x = x_ref[...]
    q = _dot(x, wcq_ref[...])
    mk = mk_ref[0]
    mv = mv_ref[0]
    outs = []
    for h in range(MEM_HEADS):
        sl = slice(h * HEAD_DIM, (h + 1) * HEAD_DIM)
        s = _dot_nt(q[:, sl], mk[:, sl]) * HEAD_DIM ** -0.5
        e = jnp.exp(s - jnp.max(s, -1, keepdims=True))
        p = e / jnp.sum(e, -1, keepdims=True)
        outs.append(_dot(p, mv[:, sl]))
    o = _dot(jnp.concatenate(outs, axis=1), wco_ref[...])
    x2 = _ln(alpha * x + o, g_ref[...], b_ref[...])
    x_out_ref[...] = x2
    xb_ref[...] = _pack_halves(x2)

    logits = _dot(x2, wr_ref[...]) + br_ref[...]
    lane = lax.broadcasted_iota(jnp.int32, logits.shape, 1).astype(F32)
    vals, idxs = [], []
    for _ in range(TOP_K):
        m = jnp.max(logits, -1, keepdims=True)
        idx = jnp.min(jnp.where(logits == m, lane, float(LANE)), -1, keepdims=True)
        vals.append(m)
        idxs.append(idx)
        logits = jnp.where(lane == idx, 2 * NEG_BIG, logits)
    es = [jnp.exp(val - vals[0]) for val in vals]
    den = es[0] + es[1] + es[2] + es[3]
    gate = jnp.zeros(logits.shape, F32)
    ids = jnp.zeros(logits.shape, F32)
    for kk in range(TOP_K):
        gate = jnp.where(lane == kk, es[kk] / den, gate)
        ids = jnp.where(lane == kk, idxs[kk], ids)
    gate_ref[...] = gate
    ids_ref[...] = ids.astype(jnp.int32)


def _memattn_router(x_all, xb_all, gate_all, ids_all, row_off, nb, seq, mk, mv, wcq, wco, g, b, wr, br, alpha):
    t, d = x_all.shape
    tl = _pick_tile(seq, (512, 256, 128, 64))
    nt = seq // tl
    off = row_off // tl
    assert row_off % tl == 0
    n_mem = mk.shape[1]
    tok = lambda bb, i: (off + bb * nt + i, 0)
    const = lambda bb, i: (0, 0)
    return pl.pallas_call(
        functools.partial(_memattn_kernel, alpha),
        grid=(nb, nt),
        in_specs=[
            pl.BlockSpec((tl, d), tok),
            pl.BlockSpec(memory_space=pl.ANY),
            pl.BlockSpec(memory_space=pl.ANY),
            pl.BlockSpec(memory_space=pl.ANY),
            pl.BlockSpec((1, n_mem, MEM_W), lambda bb, i: (bb, 0, 0)),
            pl.BlockSpec((1, n_mem, MEM_W), lambda bb, i: (bb, 0, 0)),
            pl.BlockSpec((d, MEM_W), const),
            pl.BlockSpec((MEM_W, d), const),
            pl.BlockSpec((1, d), const),
            pl.BlockSpec((1, d), const),
            pl.BlockSpec((d, LANE), const),
            pl.BlockSpec((1, LANE), const),
        ],
        out_specs=[
            pl.BlockSpec((tl, d), tok),
            pl.BlockSpec((tl, d // 2), tok),
            pl.BlockSpec((tl, LANE), tok),
            pl.BlockSpec((tl, LANE), tok),
        ],
        out_shape=[
            jax.ShapeDtypeStruct((t, d), F32),
            jax.ShapeDtypeStruct((t, d // 2), jnp.uint32),
            jax.ShapeDtypeStruct((t, LANE), F32),
            jax.ShapeDtypeStruct((t, LANE), jnp.int32),
        ],
        input_output_aliases={0: 0, 1: 1, 2: 2, 3: 3},
        compiler_params=_cparams("arbitrary", "arbitrary"),
        name="memattn_router",
    )(x_all, xb_all, gate_all, ids_all, mk, mv, wcq, wco, g.reshape(1, d), b.reshape(1, d), wr, br)


def _expert_kernel(d_ff, be_ref, bv_ref, nu_ref, xs_ref, w1_ref, b1_ref, w2_ref, b2_ref, ys_ref, w1b_scr, w2b_scr):
    i = pl.program_id(0)
    prev = be_ref[jnp.maximum(i - 1, 0)]
    new_expert = (i == 0) | (be_ref[i] != prev)

    @pl.when(new_expert & (i < nu_ref[0]))
    def _():
        w1b_scr[...] = w1_ref[0, 0].astype(BF16)
        w2b_scr[...] = w2_ref[0, 0].astype(BF16)

    @pl.when(i < nu_ref[0])
    def _():
        row = lax.broadcasted_iota(jnp.int32, (xs_ref.shape[0], 1), 0)
        xs = _unpack_halves(jnp.where(row < bv_ref[i], xs_ref[...], jnp.zeros_like(xs_ref))).astype(BF16)
        hh = jnp.dot(xs, w1b_scr[...], preferred_element_type=F32) + b1_ref[0, 0]
        glu = jnp.minimum(hh[:, 0:d_ff], SWIGLU_LIMIT)
        lin = jnp.clip(hh[:, d_ff:2 * d_ff], -SWIGLU_LIMIT, SWIGLU_LIMIT)
        act = glu * jax.nn.sigmoid(SWIGLU_ALPHA * glu) * (lin + 1.0)
        y = jnp.dot(act.astype(BF16), w2b_scr[...], preferred_element_type=F32) + b2_ref[0, 0]
        ys_ref[...] = _pack_halves(y)


def _experts(xs, blk_e, blk_valid, n_used, layer, w1, b1, w2, b2):
    n_rows = xs.shape[0]
    n_l, n_e, d, two_ff = w1.shape
    d_ff = two_ff // 2
    n_blocks = n_rows // MOE_BM
    row_blk = lambda i, be, bv, nu: (jnp.minimum(i, nu[0] - 1), 0)
    exp_blk = lambda i, be, bv, nu: (layer, be[jnp.minimum(i, nu[0] - 1)], 0, 0)
    return pl.pallas_call(
        functools.partial(_expert_kernel, d_ff),
        grid_spec=pltpu.PrefetchScalarGridSpec(
            num_scalar_prefetch=3,
            grid=(n_blocks,),
            in_specs=[
                pl.BlockSpec((MOE_BM, d // 2), row_blk),
                pl.BlockSpec((1, 1, d, two_ff), exp_blk),
                pl.BlockSpec((1, 1, 1, two_ff), exp_blk),
                pl.BlockSpec((1, 1, d_ff, d), exp_blk),
                pl.BlockSpec((1, 1, 1, d), exp_blk),
            ],
            out_specs=pl.BlockSpec((MOE_BM, d // 2), row_blk),
            scratch_shapes=[pltpu.VMEM((d, two_ff), BF16), pltpu.VMEM((d_ff, d), BF16)],
        ),
        out_shape=jax.ShapeDtypeStruct((n_rows, d // 2), jnp.uint32),
        compiler_params=_cparams("arbitrary"),
        name="moe_experts",
    )(blk_e, blk_valid, n_used, xs, w1, b1.reshape(n_l, n_e, 1, two_ff), w2, b2.reshape(n_l, n_e, 1, d))


def _sc_mesh():
    return plsc.VectorSubcoreMesh(core_axis_name="c", subcore_axis_name="s")


def _piece_indices(idx, split):
    return (idx[..., None] * split + jnp.arange(split, dtype=jnp.int32)).reshape(idx.shape[:-1] + (-1,))


def _sc_dispatch(xw, dest_km, n_rows):
    t, dw = xw.shape
    split = dw // LANE
    nw = t * split // SC_WINDOW
    assert dw % LANE == 0 and (t * split) % SC_WINDOW == 0

    @functools.partial(pl.kernel, out_type=jax.ShapeDtypeStruct((n_rows * split, LANE), xw.dtype),
                       mesh=_sc_mesh(), scratch_types=[], name="moe_dispatch_sc")
    def scatter_rows(x_hbm, i0_hbm, i1_hbm, i2_hbm, i3_hbm, o_hbm):
        def body(x_vmem, i0, i1, i2, i3):
            for idx in (i0, i1, i2, i3):
                pltpu.sync_copy(x_vmem, o_hbm.at[idx.at[0]])

        pltpu.emit_pipeline(
            body,
            grid=(nw,),
            in_specs=[pl.BlockSpec((SC_WINDOW, LANE), lambda i: (i, 0))]
            + [pl.BlockSpec((1, SC_WINDOW), functools.partial(lambda k, i: (0, k * nw + i), k))
               for k in range(TOP_K)],
            out_specs=[],
            core_axis_name=("c", "s"),
            dimension_semantics=(pltpu.PARALLEL,),
        )(x_hbm, i0_hbm, i1_hbm, i2_hbm, i3_hbm)

    idx = _piece_indices(dest_km.reshape(TOP_K, t), split).reshape(1, TOP_K * t * split)
    return scatter_rows(xw.reshape(t * split, LANE), idx, idx, idx, idx).reshape(n_rows, dw)


def _sc_gather(yw, idx_flat):
    n_rows, dw = yw.shape
    n = idx_flat.shape[0]
    split = dw // LANE
    assert dw % LANE == 0 and (n * split) % SC_WINDOW == 0

    @functools.partial(pl.kernel, out_type=jax.ShapeDtypeStruct((n * split, LANE), yw.dtype),
                       mesh=_sc_mesh(), scratch_types=[], name="moe_gather_sc")
    def gather_rows(y_hbm, i_hbm, o_hbm):
        def body(i_vmem, o_vmem):
            pltpu.sync_copy(y_hbm.at[i_vmem.at[0]], o_vmem)

        pltpu.emit_pipeline(
            body,
            grid=(n * split // SC_WINDOW,),
            in_specs=[pl.BlockSpec((1, SC_WINDOW), lambda i: (0, i))],
            out_specs=[pl.BlockSpec((SC_WINDOW, LANE), lambda i: (i, 0))],
            core_axis_name=("c", "s"),
            dimension_semantics=(pltpu.PARALLEL,),
        )(i_hbm, o_hbm)

    idx = _piece_indices(idx_flat, split).reshape(1, n * split)
    return gather_rows(yw.reshape(n_rows * split, LANE), idx).reshape(n, dw)


def _combine_kernel(alpha, y0_ref, y1_ref, y2_ref, y3_ref, gate_ref, x_ref, g_ref, b_ref, o_ref):
    gate = gate_ref[...]
    acc = alpha * x_ref[...]
    for k, y_ref in enumerate((y0_ref, y1_ref, y2_ref, y3_ref)):
        acc = acc + gate[:, k:k + 1] * _unpack_halves(y_ref[...])
    o_ref[...] = _ln(acc, g_ref[...], b_ref[...])


def _combine(yg, gate_all, x_all, g, b, alpha):
    t, d = x_all.shape
    tm = _pick_tile(t, (512, 256, 128, 64))
    nt = t // tm
    y_spec = lambda k: pl.BlockSpec((tm, d // 2), lambda i: (k * nt + i, 0))
    return pl.pallas_call(
        functools.partial(_combine_kernel, alpha),
        grid=(nt,),
        in_specs=[
            y_spec(0), y_spec(1), y_spec(2), y_spec(3),
            pl.BlockSpec((tm, LANE), lambda i: (i, 0)),
            pl.BlockSpec((tm, d), lambda i: (i, 0)),
            pl.BlockSpec((1, d), lambda i: (0, 0)),
            pl.BlockSpec((1, d), lambda i: (0, 0)),
        ],
        out_specs=pl.BlockSpec((tm, d), lambda i: (i, 0)),
        out_shape=jax.ShapeDtypeStruct((t, d), F32),
        input_output_aliases={5: 0},
        compiler_params=_cparams("arbitrary"),
        name="moe_combine_ln3",
    )(yg, yg, yg, yg, gate_all, x_all, g.reshape(1, d), b.reshape(1, d))


def _rank_kernel(ids_ref, rank_ref, cnt_ref, run_scr):
    i = pl.program_id(0)

    @pl.when(i == 0)
    def _():
        run_scr[...] = jnp.zeros_like(run_scr)

    ids = ids_ref[...].astype(F32)
    tm = ids.shape[0]
    lane = lax.broadcasted_iota(jnp.int32, ids.shape, 1).astype(F32)
    one_hots = [jnp.where(lane == ids[:, k:k + 1], 1.0, 0.0) for k in range(TOP_K)]
    multi_hot = one_hots[0] + one_hots[1] + one_hots[2] + one_hots[3]
    r = lax.broadcasted_iota(jnp.int32, (tm, tm), 0)
    c = lax.broadcasted_iota(jnp.int32, (tm, tm), 1)
    before = jnp.dot(jnp.where(c < r, 1.0, 0.0).astype(BF16), multi_hot.astype(BF16),
                     preferred_element_type=F32) + run_scr[0:1, :]
    rank = jnp.zeros(ids.shape, F32)
    for k in range(TOP_K):
        rank = jnp.where(lane == k, jnp.sum(one_hots[k] * before, -1, keepdims=True), rank)
    rank_ref[...] = rank.astype(jnp.int32)
    run_scr[0:1, :] = run_scr[0:1, :] + jnp.sum(multi_hot, 0, keepdims=True)
    cnt_ref[...] = run_scr[...]


def _dest_kernel(ids_ref, rank_ref, start_ref, dest_ref):
    ids = ids_ref[...].astype(F32)
    lane = lax.broadcasted_iota(jnp.int32, ids.shape, 1).astype(F32)
    start = start_ref[...]
    dest = rank_ref[...].astype(F32)
    for k in range(TOP_K):
        seg = jnp.sum(jnp.where(lane == ids[:, k:k + 1], start, 0.0), -1, keepdims=True)
        dest = jnp.where(lane == k, dest + seg, dest)
    dest_ref[...] = dest.astype(jnp.int32)


def _route(ids_all, n_experts):
    t = ids_all.shape[0]
    tk = t * TOP_K
    n_blocks = -(-tk // MOE_BM) + n_experts
    n_rows = n_blocks * MOE_BM
    assert n_rows < 2 ** 24
    tm = _pick_tile(t, (512, 256, 128, 64))
    tok = pl.BlockSpec((tm, LANE), lambda i: (i, 0))
    rank, cnt = pl.pallas_call(
        _rank_kernel,
        grid=(t // tm,),
        in_specs=[tok],
        out_specs=[tok, pl.BlockSpec((SUBLANE, LANE), lambda i: (0, 0))],
        out_shape=[jax.ShapeDtypeStruct((t, LANE), jnp.int32), jax.ShapeDtypeStruct((SUBLANE, LANE), F32)],
        scratch_shapes=[pltpu.VMEM((SUBLANE, LANE), F32)],
        compiler_params=_cparams("arbitrary"),
        name="moe_rank",
    )(ids_all)
    counts = cnt[0, 0:n_experts].astype(jnp.int32)
    padded = (counts + MOE_BM - 1) // MOE_BM * MOE_BM
    pad_end = jnp.cumsum(padded)
    pad_start = pad_end - padded
    start_row = jnp.zeros((1, LANE), F32).at[0, 0:n_experts].set(pad_start.astype(F32))
    dest = pl.pallas_call(
        _dest_kernel,
        grid=(t // tm,),
        in_specs=[tok, tok, pl.BlockSpec((1, LANE), lambda i: (0, 0))],
        out_specs=tok,
        out_shape=jax.ShapeDtypeStruct((t, LANE), jnp.int32),
        compiler_params=_cparams("arbitrary"),
        name="moe_dest",
    )(ids_all, rank, start_row)
    dest_km = dest[:, 0:TOP_K].T.reshape(-1)
    blk = jnp.arange(n_blocks, dtype=jnp.int32)
    blk_e = jnp.minimum(jnp.searchsorted(pad_end, blk * MOE_BM, side='right'), n_experts - 1).astype(jnp.int32)
    blk_valid = jnp.clip(counts[blk_e] - (blk * MOE_BM - pad_start[blk_e]), 0, MOE_BM).astype(jnp.int32)
    n_used = (pad_end[-1] // MOE_BM).astype(jnp.int32).reshape(1)
    return dest_km, n_rows, blk_e, blk_valid, n_used


def _rope_tables(pos):
    half = HEAD_DIM // 2
    inv = ROPE_THETA ** (-jnp.arange(half, dtype=F32) / half)
    ang = pos.astype(F32)[:, None] * inv[None, :]
    cos = jnp.cos(ang)
    sin = jnp.sin(ang)
    cos = jnp.concatenate([cos, cos, cos, cos], axis=1)
    sin = jnp.concatenate([-sin, sin, -sin, sin], axis=1)
    return cos, sin


def _permute_w_in(w):
    d = w.shape[0]
    o_b = 4 * GDN_W
    o_q = o_b + 2 * GDN_HEADS
    o_sc = o_q + SWA_W + 2 * SWA_KV_W
    ba = jnp.concatenate([w[:, o_b:o_q], jnp.zeros((d, LANE - 2 * GDN_HEADS), w.dtype)], axis=1)
    return jnp.concatenate([w[:, 0:o_b], w[:, o_sc:], w[:, o_q:o_sc], ba], axis=1)


def kernel(x_prompt, x_sample, state_gdn_S, state_gdn_conv, cache_swa_k, cache_swa_v, state_sconv, cache_mem_k, cache_mem_v, mem_prompt, ln_in_g, ln_in_b, w_in, gdn_conv_w, gdn_a_log, gdn_dt_bias, gdn_norm_g, swa_sinks, sc_conv_w, w_o, ln1_g, ln1_b, w_cq, w_mk, w_mv, w_co, ln2_g, ln2_b, w_router, b_router, w1, b1, w2, b2, ln3_g, ln3_b):
    bp, lp, d = x_prompt.shape
    bs, ls, _ = x_sample.shape
    depth = w_in.shape[0]
    n_experts = w_router.shape[2]
    n_mem = mem_prompt.shape[1]
    tp, ts = bp * lp, bs * ls
    t_all = tp + ts
    alpha = (2 * depth) ** 0.25
    assert cache_swa_k.shape[2] == WINDOW and d == MIX_W

    x_all = _ln_in(x_prompt.reshape(tp, d), x_sample.reshape(ts, d), ln_in_g, ln_in_b)
    xb_all = jnp.zeros((t_all, d // 2), jnp.uint32)
    gate_all = jnp.zeros((t_all, LANE), F32)
    ids_all = jnp.zeros((t_all, LANE), jnp.int32)

    cos_p, sin_p = _rope_tables(jnp.arange(lp))
    cos_s, sin_s = _rope_tables(PAST_LEN + jnp.arange(ls))
    mem_flat = mem_prompt.reshape(bp * n_mem, d)
    zeros_cbuf = jnp.zeros((bp, GDN_CONV - 1, 3 * GDN_W), F32)
    zeros_s = jnp.zeros((bp, GDN_HEADS, HEAD_DIM, HEAD_DIM), F32)
    zeros_sbuf = jnp.zeros((bp, SC_CONV - 1, SC_W), F32)
    zeros_hist = jnp.zeros((bp, WINDOW, SWA_KV_W), F32)

    outs = {k: [] for k in ("p_S", "p_conv", "p_k", "p_v", "p_sc", "p_mk", "p_mv",
                            "s_S", "s_conv", "s_k", "s_v", "s_sc")}
    for l in range(depth):
        w_in_l = _permute_w_in(w_in[l]).astype(BF16)
        gp = jnp.zeros((SUBLANE, LANE), F32)
        gp = gp.at[0, BA_A:BA_A + GDN_HEADS].set(gdn_a_log[l]).at[1, BA_A:BA_A + GDN_HEADS].set(gdn_dt_bias[l])
        ng = jnp.tile(gdn_norm_g[l].reshape(1, HEAD_DIM), (1, GDN_HEADS))
        sinks = jnp.zeros((1, LANE), F32).at[0, 0:SWA_Q_HEADS].set(swa_sinks[l])
        wr = jnp.concatenate([w_router[l], jnp.zeros((d, LANE - n_experts), F32)], axis=1).astype(BF16)
        br = jnp.concatenate([b_router[l], jnp.full((LANE - n_experts,), NEG_BIG, F32)]).reshape(1, LANE)

        mkv = _proj(mem_flat, jnp.concatenate([w_mk[l], w_mv[l]], axis=1).astype(BF16), "mem_kv")
        mk_p = mkv[:, 0:MEM_W].reshape(bp, n_mem, MEM_W)
        mv_p = mkv[:, MEM_W:2 * MEM_W].reshape(bp, n_mem, MEM_W)
        outs["p_mk"].append(mk_p.reshape(bp, n_mem, MEM_HEADS, HEAD_DIM))
        outs["p_mv"].append(mv_p.reshape(bp, n_mem, MEM_HEADS, HEAD_DIM))

        h_all = _proj(x_all, w_in_l, "w_in")

        v_p = h_all[0:tp, COL_SWA_V:COL_SWA_V + SWA_KV_W].reshape(bp, lp, SWA_KV_W)[:, lp - WINDOW:]
        v_s = h_all[tp:, COL_SWA_V:COL_SWA_V + SWA_KV_W]
        h_all, c_p, s_p, sc_p = _gdn_sconv(h_all, 0, bp, lp, zeros_cbuf, zeros_s, zeros_sbuf,
                                           gdn_conv_w[l], sc_conv_w[l], gp, ng)
        h_all, c_s, s_s, sc_s = _gdn_sconv(h_all, tp, bs, ls, state_gdn_conv[l], state_gdn_S[l], state_sconv[l],
                                           gdn_conv_w[l], sc_conv_w[l], gp, ng)
        h_all, kr_p = _swa(h_all, 0, bp, lp, zeros_hist, zeros_hist, False, cos_p, sin_p, sinks)
        h_all, kr_s = _swa(h_all, tp, bs, ls, cache_swa_k[l].reshape(bs, WINDOW, SWA_KV_W),
                           cache_swa_v[l].reshape(bs, WINDOW, SWA_KV_W), True, cos_s, sin_s, sinks)
        outs["p_S"].append(s_p); outs["p_conv"].append(c_p); outs["p_sc"].append(sc_p)
        outs["s_S"].append(s_s); outs["s_conv"].append(c_s); outs["s_sc"].append(sc_s)
        outs["p_k"].append(kr_p.reshape(bp, lp, SWA_KV_W)[:, lp - WINDOW:].reshape(bp, WINDOW, SWA_KV_HEADS, HEAD_DIM))
        outs["p_v"].append(v_p.reshape(bp, WINDOW, SWA_KV_HEADS, HEAD_DIM))
        outs["s_k"].append(kr_s.reshape(bs, ls, SWA_KV_HEADS, HEAD_DIM))
        outs["s_v"].append(v_s.reshape(bs, ls, SWA_KV_HEADS, HEAD_DIM))

        x_all = _oproj(h_all, x_all, w_o[l].astype(BF16), ln1_g[l], ln1_b[l], alpha)

        wcq = w_cq[l].astype(BF16)
        wco = w_co[l].astype(BF16)
        x_all, xb_all, gate_all, ids_all = _memattn_router(
            x_all, xb_all, gate_all, ids_all, 0, bp, lp, mk_p, mv_p, wcq, wco, ln2_g[l], ln2_b[l], wr, br, alpha)
        x_all, xb_all, gate_all, ids_all = _memattn_router(
            x_all, xb_all, gate_all, ids_all, tp, bs, ls,
            cache_mem_k[l].reshape(bs, n_mem, MEM_W), cache_mem_v[l].reshape(bs, n_mem, MEM_W),
            wcq, wco, ln2_g[l], ln2_b[l], wr, br, alpha)

        dest_km, n_rows, blk_e, blk_valid, n_used = _route(ids_all, n_experts)
        xs = _sc_dispatch(xb_all, dest_km, n_rows)
        ys = _experts(xs, blk_e, blk_valid, n_used, l, w1, b1, w2, b2)
        yg = _sc_gather(ys, dest_km)
        x_all = _combine(yg, gate_all, x_all, ln3_g[l], ln3_b[l], alpha)

    st = lambda k: jnp.stack(outs[k])
    return (x_all[0:tp].reshape(bp, lp, d), x_all[tp:].reshape(bs, ls, d),
            st("p_S"), st("p_conv"), st("p_k"), st("p_v"), st("p_sc"), st("p_mk"), st("p_mv"),
            st("s_S"), st("s_conv"), st("s_k"), st("s_v"), st("s_sc"))
```

```python
import functools

import jax
import jax.numpy as jnp
from jax import lax
from jax.experimental import pallas as pl
from jax.experimental.pallas import tpu as pltpu
from jax.experimental.pallas import tpu_sc as plsc

F32 = jnp.float32
BF16 = jnp.bfloat16

LANE = 128
SUBLANE = 8
VMEM_LIMIT_BYTES = 56 * 1024 * 1024

CHUNK = 64
CHUNK_SHIFT = 6
HEAD_DIM = 64
GDN_HEADS = 6
GDN_W = GDN_HEADS * HEAD_DIM
GDN_CONV = 4
SWA_Q_HEADS = 6
SWA_KV_HEADS = 2
SWA_GROUP = SWA_Q_HEADS // SWA_KV_HEADS
SWA_W = SWA_Q_HEADS * HEAD_DIM
SWA_KV_W = SWA_KV_HEADS * HEAD_DIM
WINDOW = 128
ROPE_THETA = 10000.0
SC_W = 256
SC_CONV = 3
MEM_HEADS = 4
MEM_W = MEM_HEADS * HEAD_DIM
TOP_K = 4
SWIGLU_LIMIT = 7.0
SWIGLU_ALPHA = 1.702
PAST_LEN = 2048
LN_EPS = 1e-5
RMS_EPS = 1e-6
NEG_BIG = -1e30

COL_QKV = 0
COL_Z = 1152
COL_SC = 1536
COL_SWA = 2304
COL_SWA_V = COL_SWA + SWA_W + SWA_KV_W
COL_BA = COL_SWA + SWA_W + 2 * SWA_KV_W
P_PAD = 3072
BA_B = 0
BA_A = GDN_HEADS
MIX_W = 1024

MOE_BM = 512
SC_WINDOW = 128


def _cparams(*sem):
    return pltpu.CompilerParams(dimension_semantics=sem, vmem_limit_bytes=VMEM_LIMIT_BYTES)


def _pick_tile(n, candidates):
    for c in candidates:
        if n % c == 0:
            return c
    raise ValueError(f"no tile in {candidates} divides {n}")


def _ln(x, g, b):
    mu = jnp.mean(x, -1, keepdims=True)
    xc = x - mu
    var = jnp.mean(xc * xc, -1, keepdims=True)
    return xc * lax.rsqrt(var + LN_EPS) * g + b


def _dot(a, b):
    return jnp.dot(a.astype(BF16), b.astype(BF16), preferred_element_type=F32)


def _dot_nt(a, b):
    return lax.dot_general(a.astype(BF16), b.astype(BF16), (((1,), (1,)), ((), ())),
                           preferred_element_type=F32)


def _silu(x):
    return x * jax.nn.sigmoid(x)


def _pack_halves(x):
    w = x.shape[1] // 2
    lo = lax.bitcast_convert_type(x[:, 0:w].astype(BF16).astype(F32), jnp.uint32)
    hi = lax.bitcast_convert_type(x[:, w:2 * w].astype(BF16).astype(F32), jnp.uint32)
    return (hi & jnp.uint32(0xFFFF0000)) | (lo >> 16)


def _unpack_halves(p):
    lo = lax.bitcast_convert_type(p << 16, F32)
    hi = lax.bitcast_convert_type(p & jnp.uint32(0xFFFF0000), F32)
    return jnp.concatenate([lo, hi], axis=1)


def _ln_in_kernel(n_p_tiles, xp_ref, xs_ref, g_ref, b_ref, o_ref):
    i = pl.program_id(0)

    @pl.when(i < n_p_tiles)
    def _():
        o_ref[...] = _ln(xp_ref[...], g_ref[...], b_ref[...])

    @pl.when(i >= n_p_tiles)
    def _():
        o_ref[...] = _ln(xs_ref[...], g_ref[...], b_ref[...])


def _ln_in(xp, xs, g, b):
    tp, d = xp.shape
    ts = xs.shape[0]
    tm = _pick_tile(ts, (512, 256, 128, 64))
    assert tp % tm == 0
    npt, nst = tp // tm, ts // tm
    return pl.pallas_call(
        functools.partial(_ln_in_kernel, npt),
        grid=(npt + nst,),
        in_specs=[
            pl.BlockSpec((tm, d), lambda i: (jnp.minimum(i, npt - 1), 0)),
            pl.BlockSpec((tm, d), lambda i: (jnp.maximum(i - npt, 0), 0)),
            pl.BlockSpec((1, d), lambda i: (0, 0)),
            pl.BlockSpec((1, d), lambda i: (0, 0)),
        ],
        out_specs=pl.BlockSpec((tm, d), lambda i: (i, 0)),
        out_shape=jax.ShapeDtypeStruct((tp + ts, d), F32),
        compiler_params=_cparams("arbitrary"),
        name="ln_in",
    )(xp, xs, g.reshape(1, d), b.reshape(1, d))


def _proj_kernel(x_ref, w_ref, o_ref):
    o_ref[...] = _dot(x_ref[...], w_ref[...])


def _proj(x, w, name):
    t, k = x.shape
    n = w.shape[1]
    tm = _pick_tile(t, (512, 256, 128, 64))
    return pl.pallas_call(
        _proj_kernel,
        grid=(t // tm,),
        in_specs=[pl.BlockSpec((tm, k), lambda i: (i, 0)),
                  pl.BlockSpec((k, n), lambda i: (0, 0))],
        out_specs=pl.BlockSpec((tm, n), lambda i: (i, 0)),
        out_shape=jax.ShapeDtypeStruct((t, n), F32),
        compiler_params=_cparams("arbitrary"),
        name=name,
    )(x, w)


def _split2(x):
    hi = x.astype(BF16)
    lo = (x - hi.astype(F32)).astype(BF16)
    return hi, lo


def _split3(x):
    hi = x.astype(BF16)
    r1 = x - hi.astype(F32)
    mid = r1.astype(BF16)
    lo = (r1 - mid.astype(F32)).astype(BF16)
    return hi, mid, lo


def _sel_dot2(x, sel2):
    return jnp.dot(jnp.concatenate(_split2(x), axis=1), sel2, preferred_element_type=F32)


def _gdn_kernel(lt, h_ref, cbuf_ref, s0_ref, sbuf_ref, cw_ref, scw_ref, gp_ref, ng_ref,
                ho_ref, cnew_ref, snew_ref, scnew_ref, ext_scr, uext_scr, s_scr, o_scr):
    i = pl.program_id(1)
    n_chunks = lt // CHUNK
    pad = SUBLANE

    @pl.when(i == 0)
    def _():
        ext_scr[0:pad, :] = jnp.concatenate(
            [jnp.zeros((pad - (GDN_CONV - 1), 3 * GDN_W), F32), cbuf_ref[0]], axis=0)
        uext_scr[0:pad, :] = jnp.concatenate(
            [jnp.zeros((pad - (SC_CONV - 1), SC_W), F32), sbuf_ref[0]], axis=0)
        zero_blk = jnp.zeros((HEAD_DIM, HEAD_DIM), F32)
        for p in range(GDN_HEADS // 2):
            s_scr[p] = jnp.concatenate(
                [jnp.concatenate([s0_ref[0, 2 * p], zero_blk], axis=1),
                 jnp.concatenate([zero_blk, s0_ref[0, 2 * p + 1]], axis=1)], axis=0)

    ext_scr[pad:pad + lt, :] = h_ref[:, COL_QKV:COL_QKV + 3 * GDN_W]
    conv = cw_ref[0:1, :] * ext_scr[pl.ds(pad - 3, lt), :]
    for j in range(1, GDN_CONV):
        conv = conv + cw_ref[j:j + 1, :] * ext_scr[pl.ds(pad - 3 + j, lt), :]
    cnew_ref[0] = ext_scr[pl.ds(lt + pad - 3, 3), :]
    ext_scr[0:pad, :] = ext_scr[lt:lt + pad, :]
    qkv = _silu(conv)
    q = qkv[:, 0:GDN_W]
    k = qkv[:, GDN_W:2 * GDN_W]
    v = qkv[:, 2 * GDN_W:3 * GDN_W]

    uext_scr[pad:pad + lt, :] = (h_ref[:, COL_SC + SC_W:COL_SC + 2 * SC_W]
                                 * h_ref[:, COL_SC + 2 * SC_W:COL_SC + 3 * SC_W])
    y = scw_ref[0:1, :] * uext_scr[pl.ds(pad - 2, lt), :]
    for j in range(1, SC_CONV):
        y = y + scw_ref[j:j + 1, :] * uext_scr[pl.ds(pad - 2 + j, lt), :]
    scnew_ref[0] = uext_scr[pl.ds(lt + pad - 2, 2), :]
    uext_scr[0:pad, :] = uext_scr[lt:lt + pad, :]
    ho_ref[:, 2 * GDN_W:MIX_W] = h_ref[:, COL_SC:COL_SC + SC_W] * y
    ho_ref[:, GDN_W:2 * GDN_W] = h_ref[:, GDN_W:2 * GDN_W]

    r384 = lax.broadcasted_iota(jnp.int32, (GDN_W, GDN_W), 0)
    c384 = lax.broadcasted_iota(jnp.int32, (GDN_W, GDN_W), 1)
    head_ones = jnp.where((r384 >> CHUNK_SHIFT) == (c384 >> CHUNK_SHIFT), 1.0, 0.0).astype(BF16)
    head_ones2 = jnp.concatenate([head_ones, head_ones], axis=0)
    r128 = lax.broadcasted_iota(jnp.int32, (LANE, GDN_W), 0)
    c128 = lax.broadcasted_iota(jnp.int32, (LANE, GDN_W), 1) >> CHUNK_SHIFT
    exp_a = jnp.where(r128 == c128 + BA_A, 1.0, 0.0).astype(BF16)
    exp_b = jnp.where(r128 == c128 + BA_B, 1.0, 0.0).astype(BF16)
    exp_a3 = jnp.concatenate([exp_a, exp_a, exp_a], axis=0)
    exp_b2 = jnp.concatenate([exp_b, exp_b], axis=0)

    qn = q * lax.rsqrt(_sel_dot2(q * q, head_ones2) + RMS_EPS) * HEAD_DIM ** -0.5
    kn = k * lax.rsqrt(_sel_dot2(k * k, head_ones2) + RMS_EPS)

    ba = h_ref[:, COL_BA:COL_BA + LANE]
    lane_ba = lax.broadcasted_iota(jnp.int32, ba.shape, 1)
    is_a = (lane_ba >= BA_A) & (lane_ba < BA_A + GDN_HEADS)
    beta_blk = jax.nn.sigmoid(ba)
    sp_in = ba + gp_ref[1:2, :]
    softplus = jnp.maximum(sp_in, 0.0) + jnp.log(1.0 + jnp.exp(-jnp.abs(sp_in)))
    g_blk = jnp.where(is_a, -jnp.exp(gp_ref[0:1, :]) * softplus, 0.0)
    row_t = lax.broadcasted_iota(jnp.int32, (lt, lt), 0)
    col_t = lax.broadcasted_iota(jnp.int32, (lt, lt), 1)
    ltri = jnp.where(((row_t >> CHUNK_SHIFT) == (col_t >> CHUNK_SHIFT)) & (col_t <= row_t), 1.0, 0.0).astype(BF16)
    gam3 = jnp.dot(ltri, jnp.concatenate(_split3(g_blk), axis=1), preferred_element_type=F32)
    gam_blk = gam3[:, 0:LANE] + gam3[:, LANE:2 * LANE] + gam3[:, 2 * LANE:3 * LANE]
    gam = jnp.dot(jnp.concatenate(_split3(gam_blk), axis=1), exp_a3, preferred_element_type=F32)
    glast = jnp.concatenate(
        [jnp.broadcast_to(gam[(c + 1) * CHUNK - 1:(c + 1) * CHUNK, :], (CHUNK, GDN_W)) for c in range(n_chunks)],
        axis=0)
    g_exp = jnp.exp(gam)
    beta = _sel_dot2(beta_blk, exp_b2)
    kbeta = kn * beta
    qn_b = qn.astype(BF16)
    kn_b = kn.astype(BF16)
    kbeta_b = kbeta.astype(BF16)
    qg_b = (qn * g_exp).astype(BF16)
    kb_b = (kbeta * g_exp).astype(BF16)
    vb_b = (v * beta).astype(BF16)
    kt_b = (kn * jnp.exp(glast - gam)).astype(BF16)

    row = lax.broadcasted_iota(jnp.int32, (CHUNK, LANE), 0)
    lane = lax.broadcasted_iota(jnp.int32, (CHUNK, LANE), 1)
    col = lane & (HEAD_DIM - 1)
    left = lane < HEAD_DIM
    m_incl = col <= row
    m_strict = col < row
    m_diag = col == row
    eye = jnp.where(m_diag, 1.0, 0.0)
    m_base = ((row >> 1) == (col >> 1)) & m_strict
    lvl_masks = [((row >> (sh + 1)) == (col >> (sh + 1))) & ((row >> sh) != (col >> sh)) & m_strict
                 for sh in range(1, CHUNK_SHIFT)]
    r2 = lax.broadcasted_iota(jnp.int32, (LANE, LANE), 0)
    c2 = lax.broadcasted_iota(jnp.int32, (LANE, LANE), 1)
    m_bd = (r2 >> CHUNK_SHIFT) == (c2 >> CHUNK_SHIFT)

    def bd(x):
        zero = jnp.zeros_like(x)
        return jnp.concatenate([jnp.where(left, x, zero), jnp.where(left, zero, x)], axis=0)

    units = [(c, p) for c in range(n_chunks) for p in range(GDN_HEADS // 2)]
    rsl = lambda c: slice(c * CHUNK, (c + 1) * CHUNK)
    psl = lambda p: slice(p * LANE, (p + 1) * LANE)

    a_b, t_inv, qd_b = {}, {}, {}
    for c, p in units:
        gc = gam[rsl(c), psl(p)]
        gr = jnp.sum(jnp.where(m_diag, gc, 0.0), axis=0, keepdims=True)
        dec = jnp.exp(jnp.where(m_incl, gc - gr, NEG_BIG))
        kq = lax.dot_general(jnp.concatenate([qn_b[rsl(c), psl(p)], kbeta_b[rsl(c), psl(p)]], axis=0),
                             bd(kn_b[rsl(c), psl(p)]), (((1,), (1,)), ((), ())), preferred_element_type=F32)
        qd_b[c, p] = (kq[0:CHUNK] * dec).astype(BF16)
        a_mat = jnp.where(m_strict, dec * kq[CHUNK:2 * CHUNK], 0.0)
        a_b[c, p] = a_mat.astype(BF16)
        t_inv[c, p] = eye - jnp.where(m_base, a_mat, 0.0)
    zero_b = jnp.zeros((CHUNK, LANE), BF16)
    for lm in lvl_masks:
        for u_ in units:
            w_lvl = jnp.dot(jnp.where(lm, a_b[u_], zero_b), bd(t_inv[u_].astype(BF16)), preferred_element_type=F32)
            t_inv[u_] = t_inv[u_] - jnp.dot(t_inv[u_].astype(BF16), bd(w_lvl.astype(BF16)),
                                            preferred_element_type=F32)
    u0, w_b, kw_b, q_add = {}, {}, {}, {}
    for c, p in units:
        rhs = jnp.concatenate([bd(vb_b[rsl(c), psl(p)]), bd(kb_b[rsl(c), psl(p)])], axis=1)
        x_sol = jnp.dot(t_inv[c, p].astype(BF16), rhs, preferred_element_type=F32)
        u0[c, p] = x_sol[:, 0:LANE]
        w_b[c, p] = x_sol[:, LANE:2 * LANE].astype(BF16)
        ktx = lax.dot_general(kt_b[rsl(c), psl(p)], x_sol.astype(BF16), (((0,), (0,)), ((), ())),
                              preferred_element_type=F32)
        q_add[c, p] = jnp.where(m_bd, ktx[:, 0:LANE], 0.0)
        kw_b[c, p] = jnp.where(m_bd, ktx[:, LANE:2 * LANE], 0.0).astype(BF16)

    s_start = {}
    for p in range(GDN_HEADS // 2):
        s_cur = s_scr[p]
        for c in range(n_chunks):
            s_bf = s_cur.astype(BF16)
            s_start[c, p] = s_bf
            e_last = jnp.exp(gam[(c + 1) * CHUNK - 1:(c + 1) * CHUNK, psl(p)])
            s_cur = s_cur * e_last - jnp.dot(kw_b[c, p], s_bf, preferred_element_type=F32) + q_add[c, p]
        s_scr[p] = s_cur

    for c, p in units:
        wq = jnp.dot(jnp.concatenate([w_b[c, p], qg_b[rsl(c), psl(p)]], axis=0), s_start[c, p],
                     preferred_element_type=F32)
        u = u0[c, p] - wq[0:CHUNK]
        o_scr[rsl(c), psl(p)] = wq[CHUNK:2 * CHUNK] + jnp.dot(qd_b[c, p], bd(u.astype(BF16)),
                                                               preferred_element_type=F32)

    o = o_scr[...]
    o_ms = _sel_dot2(o * o, head_ones2) * (1.0 / HEAD_DIM)
    ho_ref[:, 0:GDN_W] = o * lax.rsqrt(o_ms + RMS_EPS) * ng_ref[...] * _silu(h_ref[:, COL_Z:COL_Z + GDN_W])

    @pl.when(i == pl.num_programs(1) - 1)
    def _():
        for p in range(GDN_HEADS // 2):
            snew_ref[0, 2 * p] = s_scr[p, 0:HEAD_DIM, 0:HEAD_DIM]
            snew_ref[0, 2 * p + 1] = s_scr[p, HEAD_DIM:LANE, HEAD_DIM:LANE]


def _gdn_sconv(h_all, row_off, nb, seq, cbuf, s0, sbuf, cw, scw, gp, ng):
    lt = _pick_tile(seq, (256, 128, 64))
    nt = seq // lt
    off = row_off // lt
    assert row_off % lt == 0
    tok = lambda b, i: (off + b * nt + i, 0)
    return pl.pallas_call(
        functools.partial(_gdn_kernel, lt),
        grid=(nb, nt),
        in_specs=[
            pl.BlockSpec((lt, P_PAD), tok),
            pl.BlockSpec((1, GDN_CONV - 1, 3 * GDN_W), lambda b, i: (b, 0, 0)),
            pl.BlockSpec((1, GDN_HEADS, HEAD_DIM, HEAD_DIM), lambda b, i: (b, 0, 0, 0)),
            pl.BlockSpec((1, SC_CONV - 1, SC_W), lambda b, i: (b, 0, 0)),
            pl.BlockSpec((GDN_CONV, 3 * GDN_W), lambda b, i: (0, 0)),
            pl.BlockSpec((SC_CONV, SC_W), lambda b, i: (0, 0)),
            pl.BlockSpec((SUBLANE, LANE), lambda b, i: (0, 0)),
            pl.BlockSpec((1, GDN_W), lambda b, i: (0, 0)),
        ],
        out_specs=[
            pl.BlockSpec((lt, MIX_W), tok),
            pl.BlockSpec((1, GDN_CONV - 1, 3 * GDN_W), lambda b, i: (b, 0, 0)),
            pl.BlockSpec((1, GDN_HEADS, HEAD_DIM, HEAD_DIM), lambda b, i: (b, 0, 0, 0)),
            pl.BlockSpec((1, SC_CONV - 1, SC_W), lambda b, i: (b, 0, 0)),
        ],
        out_shape=[
            jax.ShapeDtypeStruct(h_all.shape, F32),
            jax.ShapeDtypeStruct((nb, GDN_CONV - 1, 3 * GDN_W), F32),
            jax.ShapeDtypeStruct((nb, GDN_HEADS, HEAD_DIM, HEAD_DIM), F32),
            jax.ShapeDtypeStruct((nb, SC_CONV - 1, SC_W), F32),
        ],
        scratch_shapes=[
            pltpu.VMEM((lt + SUBLANE, 3 * GDN_W), F32),
            pltpu.VMEM((lt + SUBLANE, SC_W), F32),
            pltpu.VMEM((GDN_HEADS // 2, LANE, LANE), F32),
            pltpu.VMEM((lt, GDN_W), F32),
        ],
        input_output_aliases={0: 0},
        compiler_params=_cparams("arbitrary", "arbitrary"),
        name="gdn_sconv",
    )(h_all, cbuf, s0, sbuf, cw, scw, gp, ng)


def _rope(x, cos, sin_signed):
    w = x.shape[1]
    nrep = w // LANE
    if nrep > 1:
        cos = jnp.concatenate([cos] * nrep, axis=1)
        sin_signed = jnp.concatenate([sin_signed] * nrep, axis=1)
    lane = lax.broadcasted_iota(jnp.int32, x.shape, 1) & (HEAD_DIM - 1)
    half = HEAD_DIM // 2
    swapped = jnp.where(lane < half, pltpu.roll(x, w - half, 1), pltpu.roll(x, half, 1))
    return x * cos + swapped * sin_signed


def _swa_kernel(lt, hist_valid, h_ref, hk_ref, hv_ref, cos_ref, sin_ref, sink_ref,
                o_ref, kr_ref, kall_scr, vall_scr):
    i = pl.program_id(1)

    @pl.when(i == 0)
    def _():
        kall_scr[0:WINDOW, :] = hk_ref[0]
        vall_scr[0:WINDOW, :] = hv_ref[0]

    cos = cos_ref[...]
    sin = sin_ref[...]
    q = _rope(h_ref[:, 0:SWA_W], cos, sin)
    k = _rope(h_ref[:, SWA_W:SWA_W + SWA_KV_W], cos, sin)
    kr_ref[...] = k
    kall_scr[WINDOW:WINDOW + lt, :] = k
    vall_scr[WINDOW:WINDOW + lt, :] = h_ref[:, SWA_W + SWA_KV_W:SWA_W + 2 * SWA_KV_W]
    q_b = q.astype(BF16)
    kall = kall_scr[...].astype(BF16)
    vall = vall_scr[...].astype(BF16)
    kall_sw = pltpu.roll(kall_scr[...], HEAD_DIM, 1).astype(BF16)
    vall_sw = pltpu.roll(vall_scr[...], HEAD_DIM, 1).astype(BF16)

    band = WINDOW + CHUNK
    half = 2 * LANE
    n_chunks = lt // CHUNK
    n_pairs = SWA_Q_HEADS // 2
    left = lax.broadcasted_iota(jnp.int32, (band, LANE), 1) < HEAD_DIM
    zero_band = jnp.zeros((band, LANE), BF16)
    zero_pad = jnp.zeros((half - band, LANE), BF16)
    k_src = ((kall, kall_sw), (kall, kall), (kall_sw, kall))
    v_src = ((vall, vall_sw), (vall, vall), (vall_sw, vall))

    def pair_bd(top, bot, ks):
        return jnp.concatenate([jnp.where(left, top[ks], zero_band), zero_pad,
                                jnp.where(left, zero_band, bot[ks]), zero_pad], axis=0)

    kb = lax.broadcasted_iota(jnp.int32, (CHUNK, half), 1)
    hist_ok = jnp.logical_or(hist_valid, i > 0)
    units = [(c, p) for c in range(n_chunks) for p in range(n_pairs)]
    scores = {}
    for c, p in units:
        ks = slice(c * CHUNK, c * CHUNK + band)
        kbd = pair_bd(k_src[p][0], k_src[p][1], ks)
        scores[c, p] = lax.dot_general(q_b[c * CHUNK:(c + 1) * CHUNK, p * LANE:(p + 1) * LANE], kbd,
                                       (((1,), (1,)), ((), ())), preferred_element_type=F32)
    probs = {}
    for c, p in units:
        valid = (kb < band) & (hist_ok | ((kb >> CHUNK_SHIFT) + c >= WINDOW // CHUNK))
        halves = []
        for e_ in range(2):
            s = jnp.where(valid, scores[c, p][:, e_ * half:(e_ + 1) * half] * HEAD_DIM ** -0.5, NEG_BIG)
            sink = sink_ref[0:1, 2 * p + e_:2 * p + e_ + 1]
            m = jnp.maximum(jnp.max(s, -1, keepdims=True), sink)
            e = jnp.exp(s - m)
            den = jnp.sum(e, -1, keepdims=True) + jnp.exp(sink - m)
            halves.append((e * (1.0 / den)).astype(BF16))
        probs[c, p] = jnp.concatenate(halves, axis=1)
    for c, p in units:
        ks = slice(c * CHUNK, c * CHUNK + band)
        vbd = pair_bd(v_src[p][0], v_src[p][1], ks)
        o_ref[c * CHUNK:(c + 1) * CHUNK, p * LANE:(p + 1) * LANE] = jnp.dot(
            probs[c, p], vbd, preferred_element_type=F32)

    if lt >= WINDOW:
        kall_scr[0:WINDOW, :] = kall_scr[lt:lt + WINDOW, :]
        vall_scr[0:WINDOW, :] = vall_scr[lt:lt + WINDOW, :]


def _swa(h_all, row_off, nb, seq, hist_k, hist_v, hist_valid, cos, sin, sinks):
    lt = _pick_tile(seq, (256, 128, 64))
    nt = seq // lt
    assert nt == 1 or lt >= WINDOW
    off = row_off // lt
    assert row_off % lt == 0
    swa_blk_w = P_PAD - COL_SWA
    return pl.pallas_call(
        functools.partial(_swa_kernel, lt, hist_valid),
        grid=(nb, nt),
        in_specs=[
            pl.BlockSpec((lt, swa_blk_w), lambda b, i: (off + b * nt + i, COL_SWA // swa_blk_w)),
            pl.BlockSpec((1, WINDOW, SWA_KV_W), lambda b, i: (b, 0, 0)),
            pl.BlockSpec((1, WINDOW, SWA_KV_W), lambda b, i: (b, 0, 0)),
            pl.BlockSpec((lt, LANE), lambda b, i: (i, 0)),
            pl.BlockSpec((lt, LANE), lambda b, i: (i, 0)),
            pl.BlockSpec((1, LANE), lambda b, i: (0, 0)),
        ],
        out_specs=[
            pl.BlockSpec((lt, SWA_W), lambda b, i: (off + b * nt + i, 1)),
            pl.BlockSpec((lt, SWA_KV_W), lambda b, i: (b * nt + i, 0)),
        ],
        out_shape=[
            jax.ShapeDtypeStruct(h_all.shape, F32),
            jax.ShapeDtypeStruct((nb * seq, SWA_KV_W), F32),
        ],
        scratch_shapes=[
            pltpu.VMEM((lt + WINDOW, SWA_KV_W), F32),
            pltpu.VMEM((lt + WINDOW, SWA_KV_W), F32),
        ],
        input_output_aliases={0: 0},
        compiler_params=_cparams("arbitrary", "arbitrary"),
        name="swa",
    )(h_all, hist_k, hist_v, cos, sin, sinks)


def _oproj_kernel(alpha, mix_ref, x_ref, w_ref, g_ref, b_ref, o_ref):
    mix = _dot(mix_ref[...], w_ref[...])
    o_ref[...] = _ln(alpha * x_ref[...] + mix, g_ref[...], b_ref[...])


def _oproj(h_all, x_all, w_o, g, b, alpha):
    t, d = x_all.shape
    tm = _pick_tile(t, (512, 256, 128, 64))
    return pl.pallas_call(
        functools.partial(_oproj_kernel, alpha),
        grid=(t // tm,),
        in_specs=[
            pl.BlockSpec((tm, MIX_W), lambda i: (i, 0)),
            pl.BlockSpec((tm, d), lambda i: (i, 0)),
            pl.BlockSpec((MIX_W, d), lambda i: (0, 0)),
            pl.BlockSpec((1, d), lambda i: (0, 0)),
            pl.BlockSpec((1, d), lambda i: (0, 0)),
        ],
        out_specs=pl.BlockSpec((tm, d), lambda i: (i, 0)),
        out_shape=jax.ShapeDtypeStruct((t, d), F32),
        input_output_aliases={1: 0},
        compiler_params=_cparams("arbitrary"),
        name="oproj_ln1",
    )(h_all, x_all, w_o, g.reshape(1, d), b.reshape(1, d))


def _memattn_kernel(alpha, x_ref, xb_in_ref, gate_in_ref, ids_in_ref, mk_ref, mv_ref, wcq_ref, wco_ref,
                    g_ref, b_ref, wr_ref, br_ref, x_out_ref, xb_ref, gate_ref, ids_ref):
    del xb_in_ref, gate_in_ref, ids_in_ref
    x = x_ref[...]
    q = _dot(x, wcq_ref[...])
    mk = mk_ref[0].astype(BF16)
    mv = mv_ref[0].astype(BF16)
    n_mem = mk.shape[0]
    q_b = q.astype(BF16)
    left = lax.broadcasted_iota(jnp.int32, (n_mem, LANE), 1) < HEAD_DIM
    zero_kv = jnp.zeros((n_mem, LANE), BF16)

    def pair_bd(x):
        return jnp.concatenate([jnp.where(left, x, zero_kv), jnp.where(left, zero_kv, x)], axis=0)

    pairs = range(MEM_HEADS // 2)
    scores = [lax.dot_general(q_b[:, p * LANE:(p + 1) * LANE], pair_bd(mk[:, p * LANE:(p + 1) * LANE]),
                              (((1,), (1,)), ((), ())), preferred_element_type=F32) for p in pairs]
    probs = []
    for p in pairs:
        halves = []
        for e_ in range(2):
            s = scores[p][:, e_ * n_mem:(e_ + 1) * n_mem] * HEAD_DIM ** -0.5
            e = jnp.exp(s - jnp.max(s, -1, keepdims=True))
            halves.append((e * (1.0 / jnp.sum(e, -1, keepdims=True))).astype(BF16))
        probs.append(jnp.concatenate(halves, axis=1))
    outs = [jnp.dot(probs[p], pair_bd(mv[:, p * LANE:(p + 1) * LANE]), preferred_element_type=F32) for p in pairs]
    o = _dot(jnp.concatenate(outs, axis=1), wco_ref[...])
    x2 = _ln(alpha * x + o, g_ref[...], b_ref[...])
    x_out_ref[...] = x2
    packed = _pack_halves(x2)
    for p in range(xb_ref.shape[0]):
        xb_ref[p] = packed[:, p * LANE:(p + 1) * LANE]

    logits = _dot(x2, wr_ref[...]) + br_ref[...]
    lane = lax.broadcasted_iota(jnp.int32, logits.shape, 1).astype(F32)
    vals, idxs = [], []
    for _ in range(TOP_K):
        m = jnp.max(logits, -1, keepdims=True)
        idx = jnp.min(jnp.where(logits == m, lane, float(LANE)), -1, keepdims=True)
        vals.append(m)
        idxs.append(idx)
        logits = jnp.where(lane == idx, 2 * NEG_BIG, logits)
    es = [jnp.exp(val - vals[0]) for val in vals]
    den = es[0] + es[1] + es[2] + es[3]
    gate = jnp.zeros(logits.shape, F32)
    ids = jnp.zeros(logits.shape, F32)
    for kk in range(TOP_K):
        gate = jnp.where(lane == kk, es[kk] / den, gate)
        ids = jnp.where(lane == kk, idxs[kk], ids)
    gate_ref[...] = gate
    ids_ref[...] = ids.astype(jnp.int32)


def _memattn_router(x_all, xb_all, gate_all, ids_all, row_off, nb, seq, mk, mv, wcq, wco, g, b, wr, br, alpha):
    t, d = x_all.shape
    tl = _pick_tile(seq, (512, 256, 128, 64))
    nt = seq // tl
    off = row_off // tl
    assert row_off % tl == 0
    n_mem = mk.shape[1]
    tok = lambda bb, i: (off + bb * nt + i, 0)
    const = lambda bb, i: (0, 0)
    return pl.pallas_call(
        functools.partial(_memattn_kernel, alpha),
        grid=(nb, nt),
        in_specs=[
            pl.BlockSpec((tl, d), tok),
            pl.BlockSpec(memory_space=pl.ANY),
            pl.BlockSpec(memory_space=pl.ANY),
            pl.BlockSpec(memory_space=pl.ANY),
            pl.BlockSpec((1, n_mem, MEM_W), lambda bb, i: (bb, 0, 0)),
            pl.BlockSpec((1, n_mem, MEM_W), lambda bb, i: (bb, 0, 0)),
            pl.BlockSpec((d, MEM_W), const),
            pl.BlockSpec((MEM_W, d), const),
            pl.BlockSpec((1, d), const),
            pl.BlockSpec((1, d), const),
            pl.BlockSpec((d, LANE), const),
            pl.BlockSpec((1, LANE), const),
        ],
        out_specs=[
            pl.BlockSpec((tl, d), tok),
            pl.BlockSpec((d // (2 * LANE), tl, LANE), lambda bb, i: (0, off + bb * nt + i, 0)),
            pl.BlockSpec((tl, LANE), tok),
            pl.BlockSpec((tl, LANE), tok),
        ],
        out_shape=[
            jax.ShapeDtypeStruct((t, d), F32),
            jax.ShapeDtypeStruct((d // (2 * LANE), t, LANE), jnp.uint32),
            jax.ShapeDtypeStruct((t, LANE), F32),
            jax.ShapeDtypeStruct((t, LANE), jnp.int32),
        ],
        input_output_aliases={0: 0, 1: 1, 2: 2, 3: 3},
        compiler_params=_cparams("arbitrary", "arbitrary"),
        name="memattn_router",
    )(x_all, xb_all, gate_all, ids_all, mk, mv, wcq, wco, g.reshape(1, d), b.reshape(1, d), wr, br)


def _expert_kernel(d_ff, be_ref, bv_ref, nu_ref, xs_ref, w1_ref, b1_ref, w2_ref, b2_ref, ys_ref, w1b_scr, w2b_scr):
    i = pl.program_id(0)
    prev = be_ref[jnp.maximum(i - 1, 0)]
    new_expert = (i == 0) | (be_ref[i] != prev)

    @pl.when(new_expert & (i < nu_ref[0]))
    def _():
        w1b_scr[...] = w1_ref[0, 0].astype(BF16)
        w2b_scr[...] = w2_ref[0, 0].astype(BF16)

    @pl.when(i < nu_ref[0])
    def _():
        row = lax.broadcasted_iota(jnp.int32, (xs_ref.shape[1], 1), 0)
        words = jnp.concatenate([xs_ref[p] for p in range(xs_ref.shape[0])], axis=1)
        xs = _unpack_halves(jnp.where(row < bv_ref[i], words, jnp.zeros_like(words))).astype(BF16)
        hh = jnp.dot(xs, w1b_scr[...], preferred_element_type=F32) + b1_ref[0, 0]
        glu = jnp.minimum(hh[:, 0:d_ff], SWIGLU_LIMIT)
        lin = jnp.clip(hh[:, d_ff:2 * d_ff], -SWIGLU_LIMIT, SWIGLU_LIMIT)
        act = glu * jax.nn.sigmoid(SWIGLU_ALPHA * glu) * (lin + 1.0)
        y = jnp.dot(act.astype(BF16), w2b_scr[...], preferred_element_type=F32) + b2_ref[0, 0]
        packed = _pack_halves(y)
        for p in range(ys_ref.shape[0]):
            ys_ref[p] = packed[:, p * LANE:(p + 1) * LANE]


def _experts(xs, blk_e, blk_valid, n_used, layer, w1, b1, w2, b2):
    n_planes, n_rows, _ = xs.shape
    n_l, n_e, d, two_ff = w1.shape
    d_ff = two_ff // 2
    n_blocks = n_rows // MOE_BM
    row_blk = lambda i, be, bv, nu: (0, jnp.minimum(i, nu[0] - 1), 0)
    exp_blk = lambda i, be, bv, nu: (layer, be[jnp.minimum(i, nu[0] - 1)], 0, 0)
    return pl.pallas_call(
        functools.partial(_expert_kernel, d_ff),
        grid_spec=pltpu.PrefetchScalarGridSpec(
            num_scalar_prefetch=3,
            grid=(n_blocks,),
            in_specs=[
                pl.BlockSpec((n_planes, MOE_BM, LANE), row_blk),
                pl.BlockSpec((1, 1, d, two_ff), exp_blk),
                pl.BlockSpec((1, 1, 1, two_ff), exp_blk),
                pl.BlockSpec((1, 1, d_ff, d), exp_blk),
                pl.BlockSpec((1, 1, 1, d), exp_blk),
            ],
            out_specs=pl.BlockSpec((n_planes, MOE_BM, LANE), row_blk),
            scratch_shapes=[pltpu.VMEM((d, two_ff), BF16), pltpu.VMEM((d_ff, d), BF16)],
        ),
        out_shape=jax.ShapeDtypeStruct((n_planes, n_rows, LANE), jnp.uint32),
        compiler_params=_cparams("arbitrary"),
        name="moe_experts",
    )(blk_e, blk_valid, n_used, xs, w1, b1.reshape(n_l, n_e, 1, two_ff), w2, b2.reshape(n_l, n_e, 1, d))


def _sc_mesh():
    return plsc.VectorSubcoreMesh(core_axis_name="c", subcore_axis_name="s")


def _sc_dispatch(xw, dest_km, n_rows):
    n_planes, t, _ = xw.shape
    n_src = n_planes * t
    nw = n_src // SC_WINDOW
    assert t % SC_WINDOW == 0

    @functools.partial(pl.kernel, out_type=jax.ShapeDtypeStruct((n_planes * n_rows, LANE), xw.dtype),
                       mesh=_sc_mesh(), scratch_types=[], name="moe_dispatch_sc")
    def scatter_rows(x_hbm, i0_hbm, i1_hbm, i2_hbm, i3_hbm, o_hbm):
        def body(x_vmem, i0, i1, i2, i3):
            for idx in (i0, i1, i2, i3):
                pltpu.sync_copy(x_vmem, o_hbm.at[idx.at[0]])

        pltpu.emit_pipeline(
            body,
            grid=(nw,),
            in_specs=[pl.BlockSpec((SC_WINDOW, LANE), lambda i: (i, 0))]
            + [pl.BlockSpec((1, SC_WINDOW), functools.partial(lambda k, i: (0, k * nw + i), k))
               for k in range(TOP_K)],
            out_specs=[],
            core_axis_name=("c", "s"),
            dimension_semantics=(pltpu.PARALLEL,),
        )(x_hbm, i0_hbm, i1_hbm, i2_hbm, i3_hbm)

    plane_off = (jnp.arange(n_planes, dtype=jnp.int32) * n_rows)[None, :, None]
    idx = (dest_km.reshape(TOP_K, 1, t) + plane_off).reshape(1, TOP_K * n_src)
    return scatter_rows(xw.reshape(n_src, LANE), idx, idx, idx, idx).reshape(n_planes, n_rows, LANE)


def _sc_gather(yw, idx_flat):
    n_planes, n_rows, _ = yw.shape
    n = idx_flat.shape[0]
    assert n % SC_WINDOW == 0

    @functools.partial(pl.kernel, out_type=jax.ShapeDtypeStruct((n_planes * n, LANE), yw.dtype),
                       mesh=_sc_mesh(), scratch_types=[], name="moe_gather_sc")
    def gather_rows(y_hbm, i_hbm, o_hbm):
        def body(i_vmem, o_vmem):
            pltpu.sync_copy(y_hbm.at[i_vmem.at[0]], o_vmem)

        pltpu.emit_pipeline(
            body,
            grid=(n_planes * n // SC_WINDOW,),
            in_specs=[pl.BlockSpec((1, SC_WINDOW), lambda i: (0, i))],
            out_specs=[pl.BlockSpec((SC_WINDOW, LANE), lambda i: (i, 0))],
            core_axis_name=("c", "s"),
            dimension_semantics=(pltpu.PARALLEL,),
        )(i_hbm, o_hbm)

    plane_off = (jnp.arange(n_planes, dtype=jnp.int32) * n_rows)[:, None]
    idx = (idx_flat[None, :] + plane_off).reshape(1, n_planes * n)
    return gather_rows(yw.reshape(n_planes * n_rows, LANE), idx).reshape(n_planes, n, LANE)


def _combine_kernel(alpha, y0_ref, y1_ref, y2_ref, y3_ref, gate_ref, x_ref, g_ref, b_ref, o_ref):
    gate = gate_ref[...]
    acc = alpha * x_ref[...]
    for k, y_ref in enumerate((y0_ref, y1_ref, y2_ref, y3_ref)):
        words = jnp.concatenate([y_ref[p] for p in range(y_ref.shape[0])], axis=1)
        acc = acc + gate[:, k:k + 1] * _unpack_halves(words)
    o_ref[...] = _ln(acc, g_ref[...], b_ref[...])


def _combine(yg, gate_all, x_all, g, b, alpha):
    t, d = x_all.shape
    tm = _pick_tile(t, (512, 256, 128, 64))
    nt = t // tm
    y_spec = lambda k: pl.BlockSpec((yg.shape[0], tm, LANE), lambda i: (0, k * nt + i, 0))
    return pl.pallas_call(
        functools.partial(_combine_kernel, alpha),
        grid=(nt,),
        in_specs=[
            y_spec(0), y_spec(1), y_spec(2), y_spec(3),
            pl.BlockSpec((tm, LANE), lambda i: (i, 0)),
            pl.BlockSpec((tm, d), lambda i: (i, 0)),
            pl.BlockSpec((1, d), lambda i: (0, 0)),
            pl.BlockSpec((1, d), lambda i: (0, 0)),
        ],
        out_specs=pl.BlockSpec((tm, d), lambda i: (i, 0)),
        out_shape=jax.ShapeDtypeStruct((t, d), F32),
        input_output_aliases={5: 0},
        compiler_params=_cparams("arbitrary"),
        name="moe_combine_ln3",
    )(yg, yg, yg, yg, gate_all, x_all, g.reshape(1, d), b.reshape(1, d))


def _rank_kernel(ids_ref, rank_ref, cnt_ref, run_scr):
    i = pl.program_id(0)

    @pl.when(i == 0)
    def _():
        run_scr[...] = jnp.zeros_like(run_scr)

    ids = ids_ref[...].astype(F32)
    tm = ids.shape[0]
    lane = lax.broadcasted_iota(jnp.int32, ids.shape, 1).astype(F32)
    one_hots = [jnp.where(lane == ids[:, k:k + 1], 1.0, 0.0) for k in range(TOP_K)]
    multi_hot = one_hots[0] + one_hots[1] + one_hots[2] + one_hots[3]
    r = lax.broadcasted_iota(jnp.int32, (tm, tm), 0)
    c = lax.broadcasted_iota(jnp.int32, (tm, tm), 1)
    before = jnp.dot(jnp.where(c < r, 1.0, 0.0).astype(BF16), multi_hot.astype(BF16),
                     preferred_element_type=F32) + run_scr[0:1, :]
    rank = jnp.zeros(ids.shape, F32)
    for k in range(TOP_K):
        rank = jnp.where(lane == k, jnp.sum(one_hots[k] * before, -1, keepdims=True), rank)
    rank_ref[...] = rank.astype(jnp.int32)
    run_scr[0:1, :] = run_scr[0:1, :] + jnp.sum(multi_hot, 0, keepdims=True)
    cnt_ref[...] = run_scr[...]


def _dest_kernel(ids_ref, rank_ref, start_ref, dest_ref):
    ids = ids_ref[...].astype(F32)
    lane = lax.broadcasted_iota(jnp.int32, ids.shape, 1).astype(F32)
    start = start_ref[...]
    dest = rank_ref[...].astype(F32)
    for k in range(TOP_K):
        seg = jnp.sum(jnp.where(lane == ids[:, k:k + 1], start, 0.0), -1, keepdims=True)
        dest = jnp.where(lane == k, dest + seg, dest)
    dest_ref[...] = dest.astype(jnp.int32)


def _route(ids_all, n_experts):
    t = ids_all.shape[0]
    tk = t * TOP_K
    n_blocks = -(-tk // MOE_BM) + n_experts
    n_rows = n_blocks * MOE_BM
    assert n_rows < 2 ** 24
    tm = _pick_tile(t, (512, 256, 128, 64))
    tok = pl.BlockSpec((tm, LANE), lambda i: (i, 0))
    rank, cnt = pl.pallas_call(
        _rank_kernel,
        grid=(t // tm,),
        in_specs=[tok],
        out_specs=[tok, pl.BlockSpec((SUBLANE, LANE), lambda i: (0, 0))],
        out_shape=[jax.ShapeDtypeStruct((t, LANE), jnp.int32), jax.ShapeDtypeStruct((SUBLANE, LANE), F32)],
        scratch_shapes=[pltpu.VMEM((SUBLANE, LANE), F32)],
        compiler_params=_cparams("arbitrary"),
        name="moe_rank",
    )(ids_all)
    counts = cnt[0, 0:n_experts].astype(jnp.int32)
    padded = (counts + MOE_BM - 1) // MOE_BM * MOE_BM
    pad_end = jnp.cumsum(padded)
    pad_start = pad_end - padded
    start_row = jnp.zeros((1, LANE), F32).at[0, 0:n_experts].set(pad_start.astype(F32))
    dest = pl.pallas_call(
        _dest_kernel,
        grid=(t // tm,),
        in_specs=[tok, tok, pl.BlockSpec((1, LANE), lambda i: (0, 0))],
        out_specs=tok,
        out_shape=jax.ShapeDtypeStruct((t, LANE), jnp.int32),
        compiler_params=_cparams("arbitrary"),
        name="moe_dest",
    )(ids_all, rank, start_row)
    dest_km = dest[:, 0:TOP_K].T.reshape(-1)
    blk = jnp.arange(n_blocks, dtype=jnp.int32)
    blk_e = jnp.minimum(jnp.sum((blk[:, None] * MOE_BM >= pad_end[None, :]).astype(jnp.int32), axis=1),
                        n_experts - 1)
    blk_valid = jnp.clip(counts[blk_e] - (blk * MOE_BM - pad_start[blk_e]), 0, MOE_BM).astype(jnp.int32)
    n_used = (pad_end[-1] // MOE_BM).astype(jnp.int32).reshape(1)
    return dest_km, n_rows, blk_e, blk_valid, n_used


def _rope_tables(pos):
    half = HEAD_DIM // 2
    inv = ROPE_THETA ** (-jnp.arange(half, dtype=F32) / half)
    ang = pos.astype(F32)[:, None] * inv[None, :]
    cos = jnp.cos(ang)
    sin = jnp.sin(ang)
    cos = jnp.concatenate([cos, cos, cos, cos], axis=1)
    sin = jnp.concatenate([-sin, sin, -sin, sin], axis=1)
    return cos, sin


def _permute_w_in(w):
    d = w.shape[0]
    o_b = 4 * GDN_W
    o_q = o_b + 2 * GDN_HEADS
    o_sc = o_q + SWA_W + 2 * SWA_KV_W
    ba = jnp.concatenate([w[:, o_b:o_q], jnp.zeros((d, LANE - 2 * GDN_HEADS), w.dtype)], axis=1)
    return jnp.concatenate([w[:, 0:o_b], w[:, o_sc:], w[:, o_q:o_sc], ba], axis=1)


def kernel(x_prompt, x_sample, state_gdn_S, state_gdn_conv, cache_swa_k, cache_swa_v, state_sconv, cache_mem_k, cache_mem_v, mem_prompt, ln_in_g, ln_in_b, w_in, gdn_conv_w, gdn_a_log, gdn_dt_bias, gdn_norm_g, swa_sinks, sc_conv_w, w_o, ln1_g, ln1_b, w_cq, w_mk, w_mv, w_co, ln2_g, ln2_b, w_router, b_router, w1, b1, w2, b2, ln3_g, ln3_b):
    bp, lp, d = x_prompt.shape
    bs, ls, _ = x_sample.shape
    depth = w_in.shape[0]
    n_experts = w_router.shape[2]
    n_mem = mem_prompt.shape[1]
    tp, ts = bp * lp, bs * ls
    t_all = tp + ts
    alpha = (2 * depth) ** 0.25
    assert cache_swa_k.shape[2] == WINDOW and d == MIX_W

    x_all = _ln_in(x_prompt.reshape(tp, d), x_sample.reshape(ts, d), ln_in_g, ln_in_b)
    xb_all = jnp.zeros((d // (2 * LANE), t_all, LANE), jnp.uint32)
    gate_all = jnp.zeros((t_all, LANE), F32)
    ids_all = jnp.zeros((t_all, LANE), jnp.int32)

    cos_p, sin_p = _rope_tables(jnp.arange(lp))
    cos_s, sin_s = _rope_tables(PAST_LEN + jnp.arange(ls))
    mem_flat = mem_prompt.reshape(bp * n_mem, d)
    zeros_cbuf = jnp.zeros((bp, GDN_CONV - 1, 3 * GDN_W), F32)
    zeros_s = jnp.zeros((bp, GDN_HEADS, HEAD_DIM, HEAD_DIM), F32)
    zeros_sbuf = jnp.zeros((bp, SC_CONV - 1, SC_W), F32)
    zeros_hist = jnp.zeros((bp, WINDOW, SWA_KV_W), F32)

    outs = {k: [] for k in ("p_S", "p_conv", "p_k", "p_v", "p_sc", "p_mk", "p_mv",
                            "s_S", "s_conv", "s_k", "s_v", "s_sc")}
    for l in range(depth):
        w_in_l = _permute_w_in(w_in[l]).astype(BF16)
        gp = jnp.zeros((SUBLANE, LANE), F32)
        gp = gp.at[0, BA_A:BA_A + GDN_HEADS].set(gdn_a_log[l]).at[1, BA_A:BA_A + GDN_HEADS].set(gdn_dt_bias[l])
        ng = jnp.tile(gdn_norm_g[l].reshape(1, HEAD_DIM), (1, GDN_HEADS))
        sinks = jnp.zeros((1, LANE), F32).at[0, 0:SWA_Q_HEADS].set(swa_sinks[l])
        wr = jnp.concatenate([w_router[l], jnp.zeros((d, LANE - n_experts), F32)], axis=1).astype(BF16)
        br = jnp.concatenate([b_router[l], jnp.full((LANE - n_experts,), NEG_BIG, F32)]).reshape(1, LANE)

        mkv = _proj(mem_flat, jnp.concatenate([w_mk[l], w_mv[l]], axis=1).astype(BF16), "mem_kv")
        mk_p = mkv[:, 0:MEM_W].reshape(bp, n_mem, MEM_W)
        mv_p = mkv[:, MEM_W:2 * MEM_W].reshape(bp, n_mem, MEM_W)
        outs["p_mk"].append(mk_p.reshape(bp, n_mem, MEM_HEADS, HEAD_DIM))
        outs["p_mv"].append(mv_p.reshape(bp, n_mem, MEM_HEADS, HEAD_DIM))

        h_all = _proj(x_all, w_in_l, "w_in")

        v_p = h_all[0:tp, COL_SWA_V:COL_SWA_V + SWA_KV_W].reshape(bp, lp, SWA_KV_W)[:, lp - WINDOW:]
        v_s = h_all[tp:, COL_SWA_V:COL_SWA_V + SWA_KV_W]
        h_all, c_p, s_p, sc_p = _gdn_sconv(h_all, 0, bp, lp, zeros_cbuf, zeros_s, zeros_sbuf,
                                           gdn_conv_w[l], sc_conv_w[l], gp, ng)
        h_all, c_s, s_s, sc_s = _gdn_sconv(h_all, tp, bs, ls, state_gdn_conv[l], state_gdn_S[l], state_sconv[l],
                                           gdn_conv_w[l], sc_conv_w[l], gp, ng)
        h_all, kr_p = _swa(h_all, 0, bp, lp, zeros_hist, zeros_hist, False, cos_p, sin_p, sinks)
        h_all, kr_s = _swa(h_all, tp, bs, ls, cache_swa_k[l].reshape(bs, WINDOW, SWA_KV_W),
                           cache_swa_v[l].reshape(bs, WINDOW, SWA_KV_W), True, cos_s, sin_s, sinks)
        outs["p_S"].append(s_p); outs["p_conv"].append(c_p); outs["p_sc"].append(sc_p)
        outs["s_S"].append(s_s); outs["s_conv"].append(c_s); outs["s_sc"].append(sc_s)
        outs["p_k"].append(kr_p.reshape(bp, lp, SWA_KV_W)[:, lp - WINDOW:].reshape(bp, WINDOW, SWA_KV_HEADS, HEAD_DIM))
        outs["p_v"].append(v_p.reshape(bp, WINDOW, SWA_KV_HEADS, HEAD_DIM))
        outs["s_k"].append(kr_s.reshape(bs, ls, SWA_KV_HEADS, HEAD_DIM))
        outs["s_v"].append(v_s.reshape(bs, ls, SWA_KV_HEADS, HEAD_DIM))

        x_all = _oproj(h_all, x_all, w_o[l].astype(BF16), ln1_g[l], ln1_b[l], alpha)

        wcq = w_cq[l].astype(BF16)
        wco = w_co[l].astype(BF16)
        x_all, xb_all, gate_all, ids_all = _memattn_router(
            x_all, xb_all, gate_all, ids_all, 0, bp, lp, mk_p, mv_p, wcq, wco, ln2_g[l], ln2_b[l], wr, br, alpha)
        x_all, xb_all, gate_all, ids_all = _memattn_router(
            x_all, xb_all, gate_all, ids_all, tp, bs, ls,
            cache_mem_k[l].reshape(bs, n_mem, MEM_W), cache_mem_v[l].reshape(bs, n_mem, MEM_W),
            wcq, wco, ln2_g[l], ln2_b[l], wr, br, alpha)

        dest_km, n_rows, blk_e, blk_valid, n_used = _route(ids_all, n_experts)
        xs = _sc_dispatch(xb_all, dest_km, n_rows)
        ys = _experts(xs, blk_e, blk_valid, n_used, l, w1, b1, w2, b2)
        yg = _sc_gather(ys, dest_km)
        x_all = _combine(yg, gate_all, x_all, ln3_g[l], ln3_b[l], alpha)

    st = lambda k: jnp.stack(outs[k])
    return (x_all[0:tp].reshape(bp, lp, d), x_all[tp:].reshape(bs, ls, d),
            st("p_S"), st("p_conv"), st("p_k"), st("p_v"), st("p_sc"), st("p_mk"), st("p_mv"),
            st("s_S"), st("s_conv"), st("s_k"), st("s_v"), st("s_sc"))
```

```python
import functools

import jax
import jax.numpy as jnp
from jax import lax
from jax.experimental import pallas as pl
from jax.experimental.pallas import tpu as pltpu
from jax.experimental.pallas import tpu_sc as plsc

F32 = jnp.float32
BF16 = jnp.bfloat16

LANE = 128
SUBLANE = 8
VMEM_LIMIT_BYTES = 56 * 1024 * 1024

CHUNK = 64
CHUNK_SHIFT = 6
HEAD_DIM = 64
GDN_HEADS = 6
GDN_W = GDN_HEADS * HEAD_DIM
GDN_CONV = 4
SWA_Q_HEADS = 6
SWA_KV_HEADS = 2
SWA_GROUP = SWA_Q_HEADS // SWA_KV_HEADS
SWA_W = SWA_Q_HEADS * HEAD_DIM
SWA_KV_W = SWA_KV_HEADS * HEAD_DIM
WINDOW = 128
ROPE_THETA = 10000.0
SC_W = 256
SC_CONV = 3
MEM_HEADS = 4
MEM_W = MEM_HEADS * HEAD_DIM
TOP_K = 4
SWIGLU_LIMIT = 7.0
SWIGLU_ALPHA = 1.702
PAST_LEN = 2048
LN_EPS = 1e-5
RMS_EPS = 1e-6
NEG_BIG = -1e30

COL_QKV = 0
COL_Z = 1152
COL_SC = 1536
COL_SWA = 2304
COL_SWA_V = COL_SWA + SWA_W + SWA_KV_W
COL_BA = COL_SWA + SWA_W + 2 * SWA_KV_W
P_PAD = 3072
BA_B = 0
BA_A = GDN_HEADS
MIX_W = 1024

MOE_BM = 512
SC_WINDOW = 128


def _cparams(*sem):
    return pltpu.CompilerParams(dimension_semantics=sem, vmem_limit_bytes=VMEM_LIMIT_BYTES)


def _pick_tile(n, candidates):
    for c in candidates:
        if n % c == 0:
            return c
    raise ValueError(f"no tile in {candidates} divides {n}")


def _ln(x, g, b):
    mu = jnp.mean(x, -1, keepdims=True)
    xc = x - mu
    var = jnp.mean(xc * xc, -1, keepdims=True)
    return xc * lax.rsqrt(var + LN_EPS) * g + b


def _dot(a, b):
    return jnp.dot(a.astype(BF16), b.astype(BF16), preferred_element_type=F32)


def _dot_nt(a, b):
    return lax.dot_general(a.astype(BF16), b.astype(BF16), (((1,), (1,)), ((), ())),
                           preferred_element_type=F32)


def _silu(x):
    return x * jax.nn.sigmoid(x)


def _pack_halves(x):
    w = x.shape[1] // 2
    lo = lax.bitcast_convert_type(x[:, 0:w].astype(BF16).astype(F32), jnp.uint32)
    hi = lax.bitcast_convert_type(x[:, w:2 * w].astype(BF16).astype(F32), jnp.uint32)
    return (hi & jnp.uint32(0xFFFF0000)) | (lo >> 16)


def _unpack_halves(p):
    lo = lax.bitcast_convert_type(p << 16, F32)
    hi = lax.bitcast_convert_type(p & jnp.uint32(0xFFFF0000), F32)
    return jnp.concatenate([lo, hi], axis=1)


def _ln_in_kernel(n_p_tiles, xp_ref, xs_ref, g_ref, b_ref, o_ref):
    i = pl.program_id(0)

    @pl.when(i < n_p_tiles)
    def _():
        o_ref[...] = _ln(xp_ref[...], g_ref[...], b_ref[...])

    @pl.when(i >= n_p_tiles)
    def _():
        o_ref[...] = _ln(xs_ref[...], g_ref[...], b_ref[...])


def _ln_in(xp, xs, g, b):
    tp, d = xp.shape
    ts = xs.shape[0]
    tm = _pick_tile(ts, (512, 256, 128, 64))
    assert tp % tm == 0
    npt, nst = tp // tm, ts // tm
    return pl.pallas_call(
        functools.partial(_ln_in_kernel, npt),
        grid=(npt + nst,),
        in_specs=[
            pl.BlockSpec((tm, d), lambda i: (jnp.minimum(i, npt - 1), 0)),
            pl.BlockSpec((tm, d), lambda i: (jnp.maximum(i - npt, 0), 0)),
            pl.BlockSpec((1, d), lambda i: (0, 0)),
            pl.BlockSpec((1, d), lambda i: (0, 0)),
        ],
        out_specs=pl.BlockSpec((tm, d), lambda i: (i, 0)),
        out_shape=jax.ShapeDtypeStruct((tp + ts, d), F32),
        compiler_params=_cparams("arbitrary"),
        name="ln_in",
    )(xp, xs, g.reshape(1, d), b.reshape(1, d))


def _proj_kernel(x_ref, w_ref, o_ref):
    o_ref[...] = _dot(x_ref[...], w_ref[...])


def _proj(x, w, name):
    t, k = x.shape
    n = w.shape[1]
    tm = _pick_tile(t, (512, 256, 128, 64))
    return pl.pallas_call(
        _proj_kernel,
        grid=(t // tm,),
        in_specs=[pl.BlockSpec((tm, k), lambda i: (i, 0)),
                  pl.BlockSpec((k, n), lambda i: (0, 0))],
        out_specs=pl.BlockSpec((tm, n), lambda i: (i, 0)),
        out_shape=jax.ShapeDtypeStruct((t, n), F32),
        compiler_params=_cparams("arbitrary"),
        name=name,
    )(x, w)


def _split2(x):
    hi = x.astype(BF16)
    lo = (x - hi.astype(F32)).astype(BF16)
    return hi, lo


def _split3(x):
    hi = x.astype(BF16)
    r1 = x - hi.astype(F32)
    mid = r1.astype(BF16)
    lo = (r1 - mid.astype(F32)).astype(BF16)
    return hi, mid, lo


def _sel_dot2(x, sel2):
    return jnp.dot(jnp.concatenate(_split2(x), axis=1), sel2, preferred_element_type=F32)


def _gdn_kernel(lt, h_ref, cbuf_ref, s0_ref, sbuf_ref, cw_ref, scw_ref, gp_ref, ng_ref,
                ho_ref, cnew_ref, snew_ref, scnew_ref, ext_scr, uext_scr, s_scr, o_scr):
    i = pl.program_id(1)
    n_chunks = lt // CHUNK
    pad = SUBLANE

    @pl.when(i == 0)
    def _():
        ext_scr[0:pad, :] = jnp.concatenate(
            [jnp.zeros((pad - (GDN_CONV - 1), 3 * GDN_W), F32), cbuf_ref[0]], axis=0)
        uext_scr[0:pad, :] = jnp.concatenate(
            [jnp.zeros((pad - (SC_CONV - 1), SC_W), F32), sbuf_ref[0]], axis=0)
        zero_blk = jnp.zeros((HEAD_DIM, HEAD_DIM), F32)
        for p in range(GDN_HEADS // 2):
            s_scr[p] = jnp.concatenate(
                [jnp.concatenate([s0_ref[0, 2 * p], zero_blk], axis=1),
                 jnp.concatenate([zero_blk, s0_ref[0, 2 * p + 1]], axis=1)], axis=0)

    ext_scr[pad:pad + lt, :] = h_ref[:, COL_QKV:COL_QKV + 3 * GDN_W]
    conv = cw_ref[0:1, :] * ext_scr[pl.ds(pad - 3, lt), :]
    for j in range(1, GDN_CONV):
        conv = conv + cw_ref[j:j + 1, :] * ext_scr[pl.ds(pad - 3 + j, lt), :]
    cnew_ref[0] = ext_scr[pl.ds(lt + pad - 3, 3), :]
    ext_scr[0:pad, :] = ext_scr[lt:lt + pad, :]
    qkv = _silu(conv)
    q = qkv[:, 0:GDN_W]
    k = qkv[:, GDN_W:2 * GDN_W]
    v = qkv[:, 2 * GDN_W:3 * GDN_W]

    uext_scr[pad:pad + lt, :] = (h_ref[:, COL_SC + SC_W:COL_SC + 2 * SC_W]
                                 * h_ref[:, COL_SC + 2 * SC_W:COL_SC + 3 * SC_W])
    y = scw_ref[0:1, :] * uext_scr[pl.ds(pad - 2, lt), :]
    for j in range(1, SC_CONV):
        y = y + scw_ref[j:j + 1, :] * uext_scr[pl.ds(pad - 2 + j, lt), :]
    scnew_ref[0] = uext_scr[pl.ds(lt + pad - 2, 2), :]
    uext_scr[0:pad, :] = uext_scr[lt:lt + pad, :]
    ho_ref[:, 2 * GDN_W:MIX_W] = h_ref[:, COL_SC:COL_SC + SC_W] * y
    ho_ref[:, GDN_W:2 * GDN_W] = h_ref[:, GDN_W:2 * GDN_W]

    r384 = lax.broadcasted_iota(jnp.int32, (GDN_W, GDN_W), 0)
    c384 = lax.broadcasted_iota(jnp.int32, (GDN_W, GDN_W), 1)
    head_ones = jnp.where((r384 >> CHUNK_SHIFT) == (c384 >> CHUNK_SHIFT), 1.0, 0.0).astype(BF16)
    r128 = lax.broadcasted_iota(jnp.int32, (LANE, GDN_W), 0)
    c128 = lax.broadcasted_iota(jnp.int32, (LANE, GDN_W), 1) >> CHUNK_SHIFT
    exp_a = jnp.where(r128 == c128 + BA_A, 1.0, 0.0).astype(BF16)
    exp_b = jnp.where(r128 == c128 + BA_B, 1.0, 0.0).astype(BF16)
    exp_a3 = jnp.concatenate([exp_a, exp_a, exp_a], axis=0)
    exp_b2 = jnp.concatenate([exp_b, exp_b], axis=0)

    head_sum = lambda x: jnp.dot(x.astype(BF16), head_ones, preferred_element_type=F32)
    qn = q * lax.rsqrt(head_sum(q * q) + RMS_EPS) * HEAD_DIM ** -0.5
    kn = k * lax.rsqrt(head_sum(k * k) + RMS_EPS)

    ba = h_ref[:, COL_BA:COL_BA + LANE]
    lane_ba = lax.broadcasted_iota(jnp.int32, ba.shape, 1)
    is_a = (lane_ba >= BA_A) & (lane_ba < BA_A + GDN_HEADS)
    beta_blk = jax.nn.sigmoid(ba)
    sp_in = ba + gp_ref[1:2, :]
    softplus = jnp.maximum(sp_in, 0.0) + jnp.log(1.0 + jnp.exp(-jnp.abs(sp_in)))
    g_blk = jnp.where(is_a, -jnp.exp(gp_ref[0:1, :]) * softplus, 0.0)
    row_t = lax.broadcasted_iota(jnp.int32, (lt, lt), 0)
    col_t = lax.broadcasted_iota(jnp.int32, (lt, lt), 1)
    ltri = jnp.where(((row_t >> CHUNK_SHIFT) == (col_t >> CHUNK_SHIFT)) & (col_t <= row_t), 1.0, 0.0).astype(BF16)
    gam3 = jnp.dot(ltri, jnp.concatenate(_split3(g_blk), axis=1), preferred_element_type=F32)
    gam_blk = gam3[:, 0:LANE] + gam3[:, LANE:2 * LANE] + gam3[:, 2 * LANE:3 * LANE]
    gam = jnp.dot(jnp.concatenate(_split3(gam_blk), axis=1), exp_a3, preferred_element_type=F32)
    glast = jnp.concatenate(
        [jnp.broadcast_to(gam[(c + 1) * CHUNK - 1:(c + 1) * CHUNK, :], (CHUNK, GDN_W)) for c in range(n_chunks)],
        axis=0)
    g_exp = jnp.exp(gam)
    beta = _sel_dot2(beta_blk, exp_b2)
    kbeta = kn * beta
    qn_b = qn.astype(BF16)
    kn_b = kn.astype(BF16)
    kbeta_b = kbeta.astype(BF16)
    qg_b = (qn * g_exp).astype(BF16)
    kb_b = (kbeta * g_exp).astype(BF16)
    vb_b = (v * beta).astype(BF16)
    kt_b = (kn * jnp.exp(glast - gam)).astype(BF16)

    row = lax.broadcasted_iota(jnp.int32, (CHUNK, LANE), 0)
    lane = lax.broadcasted_iota(jnp.int32, (CHUNK, LANE), 1)
    col = lane & (HEAD_DIM - 1)
    left = lane < HEAD_DIM
    m_incl = col <= row
    m_strict = col < row
    m_diag = col == row
    eye = jnp.where(m_diag, 1.0, 0.0)
    m_base = ((row >> 1) == (col >> 1)) & m_strict
    lvl_masks = [((row >> (sh + 1)) == (col >> (sh + 1))) & ((row >> sh) != (col >> sh)) & m_strict
                 for sh in range(1, CHUNK_SHIFT)]
    r2 = lax.broadcasted_iota(jnp.int32, (LANE, LANE), 0)
    c2 = lax.broadcasted_iota(jnp.int32, (LANE, LANE), 1)
    m_bd = (r2 >> CHUNK_SHIFT) == (c2 >> CHUNK_SHIFT)

    def bd(x):
        zero = jnp.zeros_like(x)
        return jnp.concatenate([jnp.where(left, x, zero), jnp.where(left, zero, x)], axis=0)

    units = [(c, p) for c in range(n_chunks) for p in range(GDN_HEADS // 2)]
    rsl = lambda c: slice(c * CHUNK, (c + 1) * CHUNK)
    psl = lambda p: slice(p * LANE, (p + 1) * LANE)

    a_b, t_inv, qd_b = {}, {}, {}
    for c, p in units:
        gc = gam[rsl(c), psl(p)]
        gr = jnp.sum(jnp.where(m_diag, gc, 0.0), axis=0, keepdims=True)
        dec = jnp.exp(jnp.where(m_incl, gc - gr, NEG_BIG))
        kq = lax.dot_general(jnp.concatenate([qn_b[rsl(c), psl(p)], kbeta_b[rsl(c), psl(p)]], axis=0),
                             bd(kn_b[rsl(c), psl(p)]), (((1,), (1,)), ((), ())), preferred_element_type=F32)
        qd_b[c, p] = (kq[0:CHUNK] * dec).astype(BF16)
        a_mat = jnp.where(m_strict, dec * kq[CHUNK:2 * CHUNK], 0.0)
        a_b[c, p] = a_mat.astype(BF16)
        t_inv[c, p] = eye - jnp.where(m_base, a_mat, 0.0)
    zero_b = jnp.zeros((CHUNK, LANE), BF16)
    for lm in lvl_masks:
        for u_ in units:
            w_lvl = jnp.dot(jnp.where(lm, a_b[u_], zero_b), bd(t_inv[u_].astype(BF16)), preferred_element_type=F32)
            t_inv[u_] = t_inv[u_] - jnp.dot(t_inv[u_].astype(BF16), bd(w_lvl.astype(BF16)),
                                            preferred_element_type=F32)
    u0, w_b, kw_b, q_add = {}, {}, {}, {}
    for c, p in units:
        rhs = jnp.concatenate([bd(vb_b[rsl(c), psl(p)]), bd(kb_b[rsl(c), psl(p)])], axis=1)
        x_sol = jnp.dot(t_inv[c, p].astype(BF16), rhs, preferred_element_type=F32)
        u0[c, p] = x_sol[:, 0:LANE]
        w_b[c, p] = x_sol[:, LANE:2 * LANE].astype(BF16)
        ktx = lax.dot_general(kt_b[rsl(c), psl(p)], x_sol.astype(BF16), (((0,), (0,)), ((), ())),
                              preferred_element_type=F32)
        q_add[c, p] = jnp.where(m_bd, ktx[:, 0:LANE], 0.0)
        kw_b[c, p] = jnp.where(m_bd, ktx[:, LANE:2 * LANE], 0.0).astype(BF16)

    s_start = {}
    for p in range(GDN_HEADS // 2):
        s_cur = s_scr[p]
        for c in range(n_chunks):
            s_bf = s_cur.astype(BF16)
            s_start[c, p] = s_bf
            e_last = jnp.exp(gam[(c + 1) * CHUNK - 1:(c + 1) * CHUNK, psl(p)])
            s_cur = s_cur * e_last - jnp.dot(kw_b[c, p], s_bf, preferred_element_type=F32) + q_add[c, p]
        s_scr[p] = s_cur

    for c, p in units:
        wq = jnp.dot(jnp.concatenate([w_b[c, p], qg_b[rsl(c), psl(p)]], axis=0), s_start[c, p],
                     preferred_element_type=F32)
        u = u0[c, p] - wq[0:CHUNK]
        o_scr[rsl(c), psl(p)] = wq[CHUNK:2 * CHUNK] + jnp.dot(qd_b[c, p], bd(u.astype(BF16)),
                                                               preferred_element_type=F32)

    o = o_scr[...]
    o_ms = head_sum(o * o) * (1.0 / HEAD_DIM)
    ho_ref[:, 0:GDN_W] = o * lax.rsqrt(o_ms + RMS_EPS) * ng_ref[...] * _silu(h_ref[:, COL_Z:COL_Z + GDN_W])

    @pl.when(i == pl.num_programs(1) - 1)
    def _():
        for p in range(GDN_HEADS // 2):
            snew_ref[0, 2 * p] = s_scr[p, 0:HEAD_DIM, 0:HEAD_DIM]
            snew_ref[0, 2 * p + 1] = s_scr[p, HEAD_DIM:LANE, HEAD_DIM:LANE]


def _gdn_sconv(h_all, row_off, nb, seq, cbuf, s0, sbuf, cw, scw, gp, ng):
    lt = _pick_tile(seq, (256, 128, 64))
    nt = seq // lt
    off = row_off // lt
    assert row_off % lt == 0
    tok = lambda b, i: (off + b * nt + i, 0)
    return pl.pallas_call(
        functools.partial(_gdn_kernel, lt),
        grid=(nb, nt),
        in_specs=[
            pl.BlockSpec((lt, P_PAD), tok),
            pl.BlockSpec((1, GDN_CONV - 1, 3 * GDN_W), lambda b, i: (b, 0, 0)),
            pl.BlockSpec((1, GDN_HEADS, HEAD_DIM, HEAD_DIM), lambda b, i: (b, 0, 0, 0)),
            pl.BlockSpec((1, SC_CONV - 1, SC_W), lambda b, i: (b, 0, 0)),
            pl.BlockSpec((GDN_CONV, 3 * GDN_W), lambda b, i: (0, 0)),
            pl.BlockSpec((SC_CONV, SC_W), lambda b, i: (0, 0)),
            pl.BlockSpec((SUBLANE, LANE), lambda b, i: (0, 0)),
            pl.BlockSpec((1, GDN_W), lambda b, i: (0, 0)),
        ],
        out_specs=[
            pl.BlockSpec((lt, MIX_W), tok),
            pl.BlockSpec((1, GDN_CONV - 1, 3 * GDN_W), lambda b, i: (b, 0, 0)),
            pl.BlockSpec((1, GDN_HEADS, HEAD_DIM, HEAD_DIM), lambda b, i: (b, 0, 0, 0)),
            pl.BlockSpec((1, SC_CONV - 1, SC_W), lambda b, i: (b, 0, 0)),
        ],
        out_shape=[
            jax.ShapeDtypeStruct(h_all.shape, F32),
            jax.ShapeDtypeStruct((nb, GDN_CONV - 1, 3 * GDN_W), F32),
            jax.ShapeDtypeStruct((nb, GDN_HEADS, HEAD_DIM, HEAD_DIM), F32),
            jax.ShapeDtypeStruct((nb, SC_CONV - 1, SC_W), F32),
        ],
        scratch_shapes=[
            pltpu.VMEM((lt + SUBLANE, 3 * GDN_W), F32),
            pltpu.VMEM((lt + SUBLANE, SC_W), F32),
            pltpu.VMEM((GDN_HEADS // 2, LANE, LANE), F32),
            pltpu.VMEM((lt, GDN_W), F32),
        ],
        input_output_aliases={0: 0},
        compiler_params=_cparams("arbitrary", "arbitrary"),
        name="gdn_sconv",
    )(h_all, cbuf, s0, sbuf, cw, scw, gp, ng)


def _rope(x, cos, sin_signed):
    w = x.shape[1]
    nrep = w // LANE
    if nrep > 1:
        cos = jnp.concatenate([cos] * nrep, axis=1)
        sin_signed = jnp.concatenate([sin_signed] * nrep, axis=1)
    lane = lax.broadcasted_iota(jnp.int32, x.shape, 1) & (HEAD_DIM - 1)
    half = HEAD_DIM // 2
    swapped = jnp.where(lane < half, pltpu.roll(x, w - half, 1), pltpu.roll(x, half, 1))
    return x * cos + swapped * sin_signed


def _swa_kernel(lt, hist_valid, h_ref, hk_ref, hv_ref, cos_ref, sin_ref, sink_ref,
                o_ref, kr_ref, kall_scr, vall_scr):
    i = pl.program_id(1)

    @pl.when(i == 0)
    def _():
        kall_scr[0:WINDOW, :] = hk_ref[0]
        vall_scr[0:WINDOW, :] = hv_ref[0]

    cos = cos_ref[...]
    sin = sin_ref[...]
    q = _rope(h_ref[:, 0:SWA_W], cos, sin)
    k = _rope(h_ref[:, SWA_W:SWA_W + SWA_KV_W], cos, sin)
    kr_ref[...] = k
    kall_scr[WINDOW:WINDOW + lt, :] = k
    vall_scr[WINDOW:WINDOW + lt, :] = h_ref[:, SWA_W + SWA_KV_W:SWA_W + 2 * SWA_KV_W]
    q_b = q.astype(BF16)
    kall = kall_scr[...].astype(BF16)
    vall = vall_scr[...].astype(BF16)
    kall_sw = pltpu.roll(kall_scr[...], HEAD_DIM, 1).astype(BF16)
    vall_sw = pltpu.roll(vall_scr[...], HEAD_DIM, 1).astype(BF16)

    band = WINDOW + CHUNK
    half = 2 * LANE
    n_chunks = lt // CHUNK
    n_pairs = SWA_Q_HEADS // 2
    left = lax.broadcasted_iota(jnp.int32, (band, LANE), 1) < HEAD_DIM
    zero_band = jnp.zeros((band, LANE), BF16)
    zero_pad = jnp.zeros((half - band, LANE), BF16)
    k_src = ((kall, kall_sw), (kall, kall), (kall_sw, kall))
    v_src = ((vall, vall_sw), (vall, vall), (vall_sw, vall))

    def pair_bd(top, bot, ks):
        return jnp.concatenate([jnp.where(left, top[ks], zero_band), zero_pad,
                                jnp.where(left, zero_band, bot[ks]), zero_pad], axis=0)

    kb = lax.broadcasted_iota(jnp.int32, (1, half), 1)
    hist_ok = jnp.logical_or(hist_valid, i > 0)
    fill = [jnp.where(kb == band, sink_ref[0:1, hq:hq + 1], NEG_BIG) for hq in range(SWA_Q_HEADS)]
    r512 = lax.broadcasted_iota(jnp.int32, (2 * half, LANE), 0)
    l512 = lax.broadcasted_iota(jnp.int32, (2 * half, LANE), 1)
    ones_bd = jnp.where((r512 < half) == (l512 < HEAD_DIM), 1.0, 0.0).astype(BF16)
    units = [(c, p) for c in range(n_chunks) for p in range(n_pairs)]
    scores = {}
    for c, p in units:
        ks = slice(c * CHUNK, c * CHUNK + band)
        kbd = pair_bd(k_src[p][0], k_src[p][1], ks)
        scores[c, p] = lax.dot_general(q_b[c * CHUNK:(c + 1) * CHUNK, p * LANE:(p + 1) * LANE], kbd,
                                       (((1,), (1,)), ((), ())), preferred_element_type=F32)
    probs = {}
    for c, p in units:
        valid = (kb < band) & (hist_ok | ((kb >> CHUNK_SHIFT) + c >= WINDOW // CHUNK))
        halves = []
        for e_ in range(2):
            s = jnp.where(valid, scores[c, p][:, e_ * half:(e_ + 1) * half] * HEAD_DIM ** -0.5, fill[2 * p + e_])
            halves.append(jnp.exp(s - jnp.max(s, -1, keepdims=True)).astype(BF16))
        probs[c, p] = jnp.concatenate(halves, axis=1)
    for c, p in units:
        ks = slice(c * CHUNK, c * CHUNK + band)
        rhs = jnp.concatenate([pair_bd(v_src[p][0], v_src[p][1], ks), ones_bd], axis=1)
        od = jnp.dot(probs[c, p], rhs, preferred_element_type=F32)
        o_ref[c * CHUNK:(c + 1) * CHUNK, p * LANE:(p + 1) * LANE] = od[:, 0:LANE] / od[:, LANE:2 * LANE]

    if lt >= WINDOW:
        kall_scr[0:WINDOW, :] = kall_scr[lt:lt + WINDOW, :]
        vall_scr[0:WINDOW, :] = vall_scr[lt:lt + WINDOW, :]


def _swa(h_all, row_off, nb, seq, hist_k, hist_v, hist_valid, cos, sin, sinks):
    lt = _pick_tile(seq, (256, 128, 64))
    nt = seq // lt
    assert nt == 1 or lt >= WINDOW
    off = row_off // lt
    assert row_off % lt == 0
    swa_blk_w = P_PAD - COL_SWA
    return pl.pallas_call(
        functools.partial(_swa_kernel, lt, hist_valid),
        grid=(nb, nt),
        in_specs=[
            pl.BlockSpec((lt, swa_blk_w), lambda b, i: (off + b * nt + i, COL_SWA // swa_blk_w)),
            pl.BlockSpec((1, WINDOW, SWA_KV_W), lambda b, i: (b, 0, 0)),
            pl.BlockSpec((1, WINDOW, SWA_KV_W), lambda b, i: (b, 0, 0)),
            pl.BlockSpec((lt, LANE), lambda b, i: (i, 0)),
            pl.BlockSpec((lt, LANE), lambda b, i: (i, 0)),
            pl.BlockSpec((1, LANE), lambda b, i: (0, 0)),
        ],
        out_specs=[
            pl.BlockSpec((lt, SWA_W), lambda b, i: (off + b * nt + i, 1)),
            pl.BlockSpec((lt, SWA_KV_W), lambda b, i: (b * nt + i, 0)),
        ],
        out_shape=[
            jax.ShapeDtypeStruct(h_all.shape, F32),
            jax.ShapeDtypeStruct((nb * seq, SWA_KV_W), F32),
        ],
        scratch_shapes=[
            pltpu.VMEM((lt + WINDOW, SWA_KV_W), F32),
            pltpu.VMEM((lt + WINDOW, SWA_KV_W), F32),
        ],
        input_output_aliases={0: 0},
        compiler_params=_cparams("arbitrary", "arbitrary"),
        name="swa",
    )(h_all, hist_k, hist_v, cos, sin, sinks)


def _oproj_kernel(alpha, mix_ref, x_ref, w_ref, g_ref, b_ref, o_ref):
    mix = _dot(mix_ref[...], w_ref[...])
    o_ref[...] = _ln(alpha * x_ref[...] + mix, g_ref[...], b_ref[...])


def _oproj(h_all, x_all, w_o, g, b, alpha):
    t, d = x_all.shape
    tm = _pick_tile(t, (512, 256, 128, 64))
    return pl.pallas_call(
        functools.partial(_oproj_kernel, alpha),
        grid=(t // tm,),
        in_specs=[
            pl.BlockSpec((tm, MIX_W), lambda i: (i, 0)),
            pl.BlockSpec((tm, d), lambda i: (i, 0)),
            pl.BlockSpec((MIX_W, d), lambda i: (0, 0)),
            pl.BlockSpec((1, d), lambda i: (0, 0)),
            pl.BlockSpec((1, d), lambda i: (0, 0)),
        ],
        out_specs=pl.BlockSpec((tm, d), lambda i: (i, 0)),
        out_shape=jax.ShapeDtypeStruct((t, d), F32),
        input_output_aliases={1: 0},
        compiler_params=_cparams("arbitrary"),
        name="oproj_ln1",
    )(h_all, x_all, w_o, g.reshape(1, d), b.reshape(1, d))


def _memattn_kernel(alpha, x_ref, xb_in_ref, gate_in_ref, ids_in_ref, mk_ref, mv_ref, wcq_ref, wco_ref,
                    g_ref, b_ref, wr_ref, br_ref, x_out_ref, xb_ref, gate_ref, ids_ref):
    del xb_in_ref, gate_in_ref, ids_in_ref
    x = x_ref[...]
    q = _dot(x, wcq_ref[...])
    mk = mk_ref[0].astype(BF16)
    mv = mv_ref[0].astype(BF16)
    n_mem = mk.shape[0]
    q_b = q.astype(BF16)
    left = lax.broadcasted_iota(jnp.int32, (n_mem, LANE), 1) < HEAD_DIM
    zero_kv = jnp.zeros((n_mem, LANE), BF16)

    def pair_bd(x):
        return jnp.concatenate([jnp.where(left, x, zero_kv), jnp.where(left, zero_kv, x)], axis=0)

    pairs = range(MEM_HEADS // 2)
    scores = [lax.dot_general(q_b[:, p * LANE:(p + 1) * LANE], pair_bd(mk[:, p * LANE:(p + 1) * LANE]),
                              (((1,), (1,)), ((), ())), preferred_element_type=F32) for p in pairs]
    probs = []
    for p in pairs:
        halves = []
        for e_ in range(2):
            s = scores[p][:, e_ * n_mem:(e_ + 1) * n_mem] * HEAD_DIM ** -0.5
            halves.append(jnp.exp(s - jnp.max(s, -1, keepdims=True)).astype(BF16))
        probs.append(jnp.concatenate(halves, axis=1))
    r2 = lax.broadcasted_iota(jnp.int32, (2 * n_mem, LANE), 0)
    l2 = lax.broadcasted_iota(jnp.int32, (2 * n_mem, LANE), 1)
    ones_bd = jnp.where((r2 < n_mem) == (l2 < HEAD_DIM), 1.0, 0.0).astype(BF16)
    outs = []
    for p in pairs:
        rhs = jnp.concatenate([pair_bd(mv[:, p * LANE:(p + 1) * LANE]), ones_bd], axis=1)
        od = jnp.dot(probs[p], rhs, preferred_element_type=F32)
        outs.append(od[:, 0:LANE] / od[:, LANE:2 * LANE])
    o = _dot(jnp.concatenate(outs, axis=1), wco_ref[...])
    x2 = _ln(alpha * x + o, g_ref[...], b_ref[...])
    x_out_ref[...] = x2
    packed = _pack_halves(x2)
    for p in range(xb_ref.shape[0]):
        xb_ref[p] = packed[:, p * LANE:(p + 1) * LANE]

    logits = _dot(x2, wr_ref[...]) + br_ref[...]
    lane = lax.broadcasted_iota(jnp.int32, logits.shape, 1).astype(F32)
    vals, idxs = [], []
    for _ in range(TOP_K):
        m = jnp.max(logits, -1, keepdims=True)
        idx = jnp.min(jnp.where(logits == m, lane, float(LANE)), -1, keepdims=True)
        vals.append(m)
        idxs.append(idx)
        logits = jnp.where(lane == idx, 2 * NEG_BIG, logits)
    es = [jnp.exp(val - vals[0]) for val in vals]
    den = es[0] + es[1] + es[2] + es[3]
    gate = jnp.zeros(logits.shape, F32)
    ids = jnp.zeros(logits.shape, F32)
    for kk in range(TOP_K):
        gate = jnp.where(lane == kk, es[kk] / den, gate)
        ids = jnp.where(lane == kk, idxs[kk], ids)
    gate_ref[...] = gate
    ids_ref[...] = ids.astype(jnp.int32)


def _memattn_router(x_all, xb_all, gate_all, ids_all, row_off, nb, seq, mk, mv, wcq, wco, g, b, wr, br, alpha):
    t, d = x_all.shape
    tl = _pick_tile(seq, (512, 256, 128, 64))
    nt = seq // tl
    off = row_off // tl
    assert row_off % tl == 0
    n_mem = mk.shape[1]
    tok = lambda bb, i: (off + bb * nt + i, 0)
    const = lambda bb, i: (0, 0)
    return pl.pallas_call(
        functools.partial(_memattn_kernel, alpha),
        grid=(nb, nt),
        in_specs=[
            pl.BlockSpec((tl, d), tok),
            pl.BlockSpec(memory_space=pl.ANY),
            pl.BlockSpec(memory_space=pl.ANY),
            pl.BlockSpec(memory_space=pl.ANY),
            pl.BlockSpec((1, n_mem, MEM_W), lambda bb, i: (bb, 0, 0)),
            pl.BlockSpec((1, n_mem, MEM_W), lambda bb, i: (bb, 0, 0)),
            pl.BlockSpec((d, MEM_W), const),
            pl.BlockSpec((MEM_W, d), const),
            pl.BlockSpec((1, d), const),
            pl.BlockSpec((1, d), const),
            pl.BlockSpec((d, LANE), const),
            pl.BlockSpec((1, LANE), const),
        ],
        out_specs=[
            pl.BlockSpec((tl, d), tok),
            pl.BlockSpec((d // (2 * LANE), tl, LANE), lambda bb, i: (0, off + bb * nt + i, 0)),
            pl.BlockSpec((tl, LANE), tok),
            pl.BlockSpec((tl, LANE), tok),
        ],
        out_shape=[
            jax.ShapeDtypeStruct((t, d), F32),
            jax.ShapeDtypeStruct((d // (2 * LANE), t, LANE), jnp.uint32),
            jax.ShapeDtypeStruct((t, LANE), F32),
            jax.ShapeDtypeStruct((t, LANE), jnp.int32),
        ],
        input_output_aliases={0: 0, 1: 1, 2: 2, 3: 3},
        compiler_params=_cparams("arbitrary", "arbitrary"),
        name="memattn_router",
    )(x_all, xb_all, gate_all, ids_all, mk, mv, wcq, wco, g.reshape(1, d), b.reshape(1, d), wr, br)


def _expert_kernel(d_ff, be_ref, bv_ref, nu_ref, xs_ref, w1_ref, b1_ref, w2_ref, b2_ref, ys_ref, w1b_scr, w2b_scr):
    i = pl.program_id(0)
    prev = be_ref[jnp.maximum(i - 1, 0)]
    new_expert = (i == 0) | (be_ref[i] != prev)

    @pl.when(new_expert & (i < nu_ref[0]))
    def _():
        w1b_scr[...] = w1_ref[0, 0].astype(BF16)
        w2b_scr[...] = w2_ref[0, 0].astype(BF16)

    @pl.when(i < nu_ref[0])
    def _():
        row = lax.broadcasted_iota(jnp.int32, (xs_ref.shape[1], 1), 0)
        words = jnp.concatenate([xs_ref[p] for p in range(xs_ref.shape[0])], axis=1)
        xs = _unpack_halves(jnp.where(row < bv_ref[i], words, jnp.zeros_like(words))).astype(BF16)
        hh = jnp.dot(xs, w1b_scr[...], preferred_element_type=F32) + b1_ref[0, 0]
        glu = jnp.minimum(hh[:, 0:d_ff], SWIGLU_LIMIT)
        lin = jnp.clip(hh[:, d_ff:2 * d_ff], -SWIGLU_LIMIT, SWIGLU_LIMIT)
        act = glu * jax.nn.sigmoid(SWIGLU_ALPHA * glu) * (lin + 1.0)
        y = jnp.dot(act.astype(BF16), w2b_scr[...], preferred_element_type=F32) + b2_ref[0, 0]
        packed = _pack_halves(y)
        for p in range(ys_ref.shape[0]):
            ys_ref[p] = packed[:, p * LANE:(p + 1) * LANE]


def _experts(xs, blk_e, blk_valid, n_used, layer, w1, b1, w2, b2):
    n_planes, n_rows, _ = xs.shape
    n_l, n_e, d, two_ff = w1.shape
    d_ff = two_ff // 2
    n_blocks = n_rows // MOE_BM
    row_blk = lambda i, be, bv, nu: (0, jnp.minimum(i, nu[0] - 1), 0)
    exp_blk = lambda i, be, bv, nu: (layer, be[jnp.minimum(i, nu[0] - 1)], 0, 0)
    return pl.pallas_call(
        functools.partial(_expert_kernel, d_ff),
        grid_spec=pltpu.PrefetchScalarGridSpec(
            num_scalar_prefetch=3,
            grid=(n_blocks,),
            in_specs=[
                pl.BlockSpec((n_planes, MOE_BM, LANE), row_blk),
                pl.BlockSpec((1, 1, d, two_ff), exp_blk),
                pl.BlockSpec((1, 1, 1, two_ff), exp_blk),
                pl.BlockSpec((1, 1, d_ff, d), exp_blk),
                pl.BlockSpec((1, 1, 1, d), exp_blk),
            ],
            out_specs=pl.BlockSpec((n_planes, MOE_BM, LANE), row_blk),
            scratch_shapes=[pltpu.VMEM((d, two_ff), BF16), pltpu.VMEM((d_ff, d), BF16)],
        ),
        out_shape=jax.ShapeDtypeStruct((n_planes, n_rows, LANE), jnp.uint32),
        compiler_params=_cparams("arbitrary"),
        name="moe_experts",
    )(blk_e, blk_valid, n_used, xs, w1, b1.reshape(n_l, n_e, 1, two_ff), w2, b2.reshape(n_l, n_e, 1, d))


def _sc_mesh():
    return plsc.VectorSubcoreMesh(core_axis_name="c", subcore_axis_name="s")


def _sc_dispatch(xw, dest_km, n_rows):
    n_planes, t, _ = xw.shape
    n_src = n_planes * t
    nw = n_src // SC_WINDOW
    assert t % SC_WINDOW == 0

    @functools.partial(pl.kernel, out_type=jax.ShapeDtypeStruct((n_planes * n_rows, LANE), xw.dtype),
                       mesh=_sc_mesh(), scratch_types=[], name="moe_dispatch_sc")
    def scatter_rows(x_hbm, i0_hbm, i1_hbm, i2_hbm, i3_hbm, o_hbm):
        def body(x_vmem, i0, i1, i2, i3):
            for idx in (i0, i1, i2, i3):
                pltpu.sync_copy(x_vmem, o_hbm.at[idx.at[0]])

        pltpu.emit_pipeline(
            body,
            grid=(nw,),
            in_specs=[pl.BlockSpec((SC_WINDOW, LANE), lambda i: (i, 0))]
            + [pl.BlockSpec((1, SC_WINDOW), functools.partial(lambda k, i: (0, k * nw + i), k))
               for k in range(TOP_K)],
            out_specs=[],
            core_axis_name=("c", "s"),
            dimension_semantics=(pltpu.PARALLEL,),
        )(x_hbm, i0_hbm, i1_hbm, i2_hbm, i3_hbm)

    plane_off = (jnp.arange(n_planes, dtype=jnp.int32) * n_rows)[None, :, None]
    idx = (dest_km.reshape(TOP_K, 1, t) + plane_off).reshape(1, TOP_K * n_src)
    return scatter_rows(xw.reshape(n_src, LANE), idx, idx, idx, idx).reshape(n_planes, n_rows, LANE)


def _sc_gather(yw, idx_flat):
    n_planes, n_rows, _ = yw.shape
    n = idx_flat.shape[0]
    assert n % SC_WINDOW == 0

    @functools.partial(pl.kernel, out_type=jax.ShapeDtypeStruct((n_planes * n, LANE), yw.dtype),
                       mesh=_sc_mesh(), scratch_types=[], name="moe_gather_sc")
    def gather_rows(y_hbm, i_hbm, o_hbm):
        def body(i_vmem, o_vmem):
            pltpu.sync_copy(y_hbm.at[i_vmem.at[0]], o_vmem)

        pltpu.emit_pipeline(
            body,
            grid=(n_planes * n // SC_WINDOW,),
            in_specs=[pl.BlockSpec((1, SC_WINDOW), lambda i: (0, i))],
            out_specs=[pl.BlockSpec((SC_WINDOW, LANE), lambda i: (i, 0))],
            core_axis_name=("c", "s"),
            dimension_semantics=(pltpu.PARALLEL,),
        )(i_hbm, o_hbm)

    plane_off = (jnp.arange(n_planes, dtype=jnp.int32) * n_rows)[:, None]
    idx = (idx_flat[None, :] + plane_off).reshape(1, n_planes * n)
    return gather_rows(yw.reshape(n_planes * n_rows, LANE), idx).reshape(n_planes, n, LANE)


def _combine_kernel(alpha, y0_ref, y1_ref, y2_ref, y3_ref, gate_ref, x_ref, g_ref, b_ref, o_ref):
    gate = gate_ref[...]
    acc = alpha * x_ref[...]
    for k, y_ref in enumerate((y0_ref, y1_ref, y2_ref, y3_ref)):
        words = jnp.concatenate([y_ref[p] for p in range(y_ref.shape[0])], axis=1)
        acc = acc + gate[:, k:k + 1] * _unpack_halves(words)
    o_ref[...] = _ln(acc, g_ref[...], b_ref[...])


def _combine(yg, gate_all, x_all, g, b, alpha):
    t, d = x_all.shape
    tm = _pick_tile(t, (512, 256, 128, 64))
    nt = t // tm
    y_spec = lambda k: pl.BlockSpec((yg.shape[0], tm, LANE), lambda i: (0, k * nt + i, 0))
    return pl.pallas_call(
        functools.partial(_combine_kernel, alpha),
        grid=(nt,),
        in_specs=[
            y_spec(0), y_spec(1), y_spec(2), y_spec(3),
            pl.BlockSpec((tm, LANE), lambda i: (i, 0)),
            pl.BlockSpec((tm, d), lambda i: (i, 0)),
            pl.BlockSpec((1, d), lambda i: (0, 0)),
            pl.BlockSpec((1, d), lambda i: (0, 0)),
        ],
        out_specs=pl.BlockSpec((tm, d), lambda i: (i, 0)),
        out_shape=jax.ShapeDtypeStruct((t, d), F32),
        input_output_aliases={5: 0},
        compiler_params=_cparams("arbitrary"),
        name="moe_combine_ln3",
    )(yg, yg, yg, yg, gate_all, x_all, g.reshape(1, d), b.reshape(1, d))


def _rank_kernel(ids_ref, rank_ref, cnt_ref, run_scr):
    i = pl.program_id(0)

    @pl.when(i == 0)
    def _():
        run_scr[...] = jnp.zeros_like(run_scr)

    ids = ids_ref[...].astype(F32)
    tm = ids.shape[0]
    lane = lax.broadcasted_iota(jnp.int32, ids.shape, 1).astype(F32)
    one_hots = [jnp.where(lane == ids[:, k:k + 1], 1.0, 0.0) for k in range(TOP_K)]
    multi_hot = one_hots[0] + one_hots[1] + one_hots[2] + one_hots[3]
    r = lax.broadcasted_iota(jnp.int32, (tm, tm), 0)
    c = lax.broadcasted_iota(jnp.int32, (tm, tm), 1)
    before = jnp.dot(jnp.where(c < r, 1.0, 0.0).astype(BF16), multi_hot.astype(BF16),
                     preferred_element_type=F32) + run_scr[0:1, :]
    rank = jnp.zeros(ids.shape, F32)
    for k in range(TOP_K):
        rank = jnp.where(lane == k, jnp.sum(one_hots[k] * before, -1, keepdims=True), rank)
    rank_ref[...] = rank.astype(jnp.int32)
    run_scr[0:1, :] = run_scr[0:1, :] + jnp.sum(multi_hot, 0, keepdims=True)
    cnt_ref[...] = run_scr[...]


def _dest_kernel(ids_ref, rank_ref, start_ref, dest_ref):
    ids = ids_ref[...].astype(F32)
    lane = lax.broadcasted_iota(jnp.int32, ids.shape, 1).astype(F32)
    start = start_ref[...]
    dest = rank_ref[...].astype(F32)
    for k in range(TOP_K):
        seg = jnp.sum(jnp.where(lane == ids[:, k:k + 1], start, 0.0), -1, keepdims=True)
        dest = jnp.where(lane == k, dest + seg, dest)
    dest_ref[...] = dest.astype(jnp.int32)


def _route(ids_all, n_experts):
    t = ids_all.shape[0]
    tk = t * TOP_K
    n_blocks = -(-tk // MOE_BM) + n_experts
    n_rows = n_blocks * MOE_BM
    assert n_rows < 2 ** 24
    tm = _pick_tile(t, (512, 256, 128, 64))
    tok = pl.BlockSpec((tm, LANE), lambda i: (i, 0))
    rank, cnt = pl.pallas_call(
        _rank_kernel,
        grid=(t // tm,),
        in_specs=[tok],
        out_specs=[tok, pl.BlockSpec((SUBLANE, LANE), lambda i: (0, 0))],
        out_shape=[jax.ShapeDtypeStruct((t, LANE), jnp.int32), jax.ShapeDtypeStruct((SUBLANE, LANE), F32)],
        scratch_shapes=[pltpu.VMEM((SUBLANE, LANE), F32)],
        compiler_params=_cparams("arbitrary"),
        name="moe_rank",
    )(ids_all)
    counts = cnt[0, 0:n_experts].astype(jnp.int32)
    padded = (counts + MOE_BM - 1) // MOE_BM * MOE_BM
    pad_end = jnp.cumsum(padded)
    pad_start = pad_end - padded
    start_row = jnp.zeros((1, LANE), F32).at[0, 0:n_experts].set(pad_start.astype(F32))
    dest = pl.pallas_call(
        _dest_kernel,
        grid=(t // tm,),
        in_specs=[tok, tok, pl.BlockSpec((1, LANE), lambda i: (0, 0))],
        out_specs=tok,
        out_shape=jax.ShapeDtypeStruct((t, LANE), jnp.int32),
        compiler_params=_cparams("arbitrary"),
        name="moe_dest",
    )(ids_all, rank, start_row)
    dest_km = dest[:, 0:TOP_K].T.reshape(-1)
    blk = jnp.arange(n_blocks, dtype=jnp.int32)
    blk_e = jnp.minimum(jnp.sum((blk[:, None] * MOE_BM >= pad_end[None, :]).astype(jnp.int32), axis=1),
                        n_experts - 1)
    blk_valid = jnp.clip(counts[blk_e] - (blk * MOE_BM - pad_start[blk_e]), 0, MOE_BM).astype(jnp.int32)
    n_used = (pad_end[-1] // MOE_BM).astype(jnp.int32).reshape(1)
    return dest_km, n_rows, blk_e, blk_valid, n_used


def _rope_tables(pos):
    half = HEAD_DIM // 2
    inv = ROPE_THETA ** (-jnp.arange(half, dtype=F32) / half)
    ang = pos.astype(F32)[:, None] * inv[None, :]
    cos = jnp.cos(ang)
    sin = jnp.sin(ang)
    cos = jnp.concatenate([cos, cos, cos, cos], axis=1)
    sin = jnp.concatenate([-sin, sin, -sin, sin], axis=1)
    return cos, sin


def _permute_w_in(w):
    d = w.shape[0]
    o_b = 4 * GDN_W
    o_q = o_b + 2 * GDN_HEADS
    o_sc = o_q + SWA_W + 2 * SWA_KV_W
    ba = jnp.concatenate([w[:, o_b:o_q], jnp.zeros((d, LANE - 2 * GDN_HEADS), w.dtype)], axis=1)
    return jnp.concatenate([w[:, 0:o_b], w[:, o_sc:], w[:, o_q:o_sc], ba], axis=1)


def kernel(x_prompt, x_sample, state_gdn_S, state_gdn_conv, cache_swa_k, cache_swa_v, state_sconv, cache_mem_k, cache_mem_v, mem_prompt, ln_in_g, ln_in_b, w_in, gdn_conv_w, gdn_a_log, gdn_dt_bias, gdn_norm_g, swa_sinks, sc_conv_w, w_o, ln1_g, ln1_b, w_cq, w_mk, w_mv, w_co, ln2_g, ln2_b, w_router, b_router, w1, b1, w2, b2, ln3_g, ln3_b):
    bp, lp, d = x_prompt.shape
    bs, ls, _ = x_sample.shape
    depth = w_in.shape[0]
    n_experts = w_router.shape[2]
    n_mem = mem_prompt.shape[1]
    tp, ts = bp * lp, bs * ls
    t_all = tp + ts
    alpha = (2 * depth) ** 0.25
    assert cache_swa_k.shape[2] == WINDOW and d == MIX_W

    x_all = _ln_in(x_prompt.reshape(tp, d), x_sample.reshape(ts, d), ln_in_g, ln_in_b)
    xb_all = jnp.zeros((d // (2 * LANE), t_all, LANE), jnp.uint32)
    gate_all = jnp.zeros((t_all, LANE), F32)
    ids_all = jnp.zeros((t_all, LANE), jnp.int32)

    cos_p, sin_p = _rope_tables(jnp.arange(lp))
    cos_s, sin_s = _rope_tables(PAST_LEN + jnp.arange(ls))
    mem_flat = mem_prompt.reshape(bp * n_mem, d)
    zeros_cbuf = jnp.zeros((bp, GDN_CONV - 1, 3 * GDN_W), F32)
    zeros_s = jnp.zeros((bp, GDN_HEADS, HEAD_DIM, HEAD_DIM), F32)
    zeros_sbuf = jnp.zeros((bp, SC_CONV - 1, SC_W), F32)
    zeros_hist = jnp.zeros((bp, WINDOW, SWA_KV_W), F32)

    outs = {k: [] for k in ("p_S", "p_conv", "p_k", "p_v", "p_sc", "p_mk", "p_mv",
                            "s_S", "s_conv", "s_k", "s_v", "s_sc")}
    for l in range(depth):
        w_in_l = _permute_w_in(w_in[l]).astype(BF16)
        gp = jnp.zeros((SUBLANE, LANE), F32)
        gp = gp.at[0, BA_A:BA_A + GDN_HEADS].set(gdn_a_log[l]).at[1, BA_A:BA_A + GDN_HEADS].set(gdn_dt_bias[l])
        ng = jnp.tile(gdn_norm_g[l].reshape(1, HEAD_DIM), (1, GDN_HEADS))
        sinks = jnp.zeros((1, LANE), F32).at[0, 0:SWA_Q_HEADS].set(swa_sinks[l])
        wr = jnp.concatenate([w_router[l], jnp.zeros((d, LANE - n_experts), F32)], axis=1).astype(BF16)
        br = jnp.concatenate([b_router[l], jnp.full((LANE - n_experts,), NEG_BIG, F32)]).reshape(1, LANE)

        mkv = _proj(mem_flat, jnp.concatenate([w_mk[l], w_mv[l]], axis=1).astype(BF16), "mem_kv")
        mk_p = mkv[:, 0:MEM_W].reshape(bp, n_mem, MEM_W)
        mv_p = mkv[:, MEM_W:2 * MEM_W].reshape(bp, n_mem, MEM_W)
        outs["p_mk"].append(mk_p.reshape(bp, n_mem, MEM_HEADS, HEAD_DIM))
        outs["p_mv"].append(mv_p.reshape(bp, n_mem, MEM_HEADS, HEAD_DIM))

        h_all = _proj(x_all, w_in_l, "w_in")

        v_p = h_all[0:tp, COL_SWA_V:COL_SWA_V + SWA_KV_W].reshape(bp, lp, SWA_KV_W)[:, lp - WINDOW:]
        v_s = h_all[tp:, COL_SWA_V:COL_SWA_V + SWA_KV_W]
        h_all, c_p, s_p, sc_p = _gdn_sconv(h_all, 0, bp, lp, zeros_cbuf, zeros_s, zeros_sbuf,
                                           gdn_conv_w[l], sc_conv_w[l], gp, ng)
        h_all, c_s, s_s, sc_s = _gdn_sconv(h_all, tp, bs, ls, state_gdn_conv[l], state_gdn_S[l], state_sconv[l],
                                           gdn_conv_w[l], sc_conv_w[l], gp, ng)
        h_all, kr_p = _swa(h_all, 0, bp, lp, zeros_hist, zeros_hist, False, cos_p, sin_p, sinks)
        h_all, kr_s = _swa(h_all, tp, bs, ls, cache_swa_k[l].reshape(bs, WINDOW, SWA_KV_W),
                           cache_swa_v[l].reshape(bs, WINDOW, SWA_KV_W), True, cos_s, sin_s, sinks)
        outs["p_S"].append(s_p); outs["p_conv"].append(c_p); outs["p_sc"].append(sc_p)
        outs["s_S"].append(s_s); outs["s_conv"].append(c_s); outs["s_sc"].append(sc_s)
        outs["p_k"].append(kr_p.reshape(bp, lp, SWA_KV_W)[:, lp - WINDOW:].reshape(bp, WINDOW, SWA_KV_HEADS, HEAD_DIM))
        outs["p_v"].append(v_p.reshape(bp, WINDOW, SWA_KV_HEADS, HEAD_DIM))
        outs["s_k"].append(kr_s.reshape(bs, ls, SWA_KV_HEADS, HEAD_DIM))
        outs["s_v"].append(v_s.reshape(bs, ls, SWA_KV_HEADS, HEAD_DIM))

        x_all = _oproj(h_all, x_all, w_o[l].astype(BF16), ln1_g[l], ln1_b[l], alpha)

        wcq = w_cq[l].astype(BF16)
        wco = w_co[l].astype(BF16)
        x_all, xb_all, gate_all, ids_all = _memattn_router(
            x_all, xb_all, gate_all, ids_all, 0, bp, lp, mk_p, mv_p, wcq, wco, ln2_g[l], ln2_b[l], wr, br, alpha)
        x_all, xb_all, gate_all, ids_all = _memattn_router(
            x_all, xb_all, gate_all, ids_all, tp, bs, ls,
            cache_mem_k[l].reshape(bs, n_mem, MEM_W), cache_mem_v[l].reshape(bs, n_mem, MEM_W),
            wcq, wco, ln2_g[l], ln2_b[l], wr, br, alpha)

        dest_km, n_rows, blk_e, blk_valid, n_used = _route(ids_all, n_experts)
        xs = _sc_dispatch(xb_all, dest_km, n_rows)
        ys = _experts(xs, blk_e, blk_valid, n_used, l, w1, b1, w2, b2)
        yg = _sc_gather(ys, dest_km)
        x_all = _combine(yg, gate_all, x_all, ln3_g[l], ln3_b[l], alpha)

    st = lambda k: jnp.stack(outs[k])
    return (x_all[0:tp].reshape(bp, lp, d), x_all[tp:].reshape(bs, ls, d),
            st("p_S"), st("p_conv"), st("p_k"), st("p_v"), st("p_sc"), st("p_mk"), st("p_mv"),
            st("s_S"), st("s_conv"), st("s_k"), st("s_v"), st("s_sc"))
```

```python
import functools

import jax
import jax.numpy as jnp
from jax import lax
from jax.experimental import pallas as pl
from jax.experimental.pallas import tpu as pltpu
from jax.experimental.pallas import tpu_sc as plsc

F32 = jnp.float32
BF16 = jnp.bfloat16

LANE = 128
SUBLANE = 8
VMEM_LIMIT_BYTES = 56 * 1024 * 1024

CHUNK = 64
CHUNK_SHIFT = 6
HEAD_DIM = 64
GDN_HEADS = 6
GDN_W = GDN_HEADS * HEAD_DIM
GDN_CONV = 4
SWA_Q_HEADS = 6
SWA_KV_HEADS = 2
SWA_GROUP = SWA_Q_HEADS // SWA_KV_HEADS
SWA_W = SWA_Q_HEADS * HEAD_DIM
SWA_KV_W = SWA_KV_HEADS * HEAD_DIM
WINDOW = 128
ROPE_THETA = 10000.0
SC_W = 256
SC_CONV = 3
MEM_HEADS = 4
MEM_W = MEM_HEADS * HEAD_DIM
TOP_K = 4
SWIGLU_LIMIT = 7.0
SWIGLU_ALPHA = 1.702
PAST_LEN = 2048
LN_EPS = 1e-5
RMS_EPS = 1e-6
NEG_BIG = -1e30

COL_QKV = 0
COL_Z = 1152
COL_SC = 1536
COL_SWA = 2304
COL_SWA_V = COL_SWA + SWA_W + SWA_KV_W
COL_BA = COL_SWA + SWA_W + 2 * SWA_KV_W
P_PAD = 3072
BA_B = 0
BA_A = GDN_HEADS
MIX_W = 1024

MOE_BM = 512
SC_WINDOW = 128


def _cparams(*sem):
    return pltpu.CompilerParams(dimension_semantics=sem, vmem_limit_bytes=VMEM_LIMIT_BYTES)


def _pick_tile(n, candidates):
    for c in candidates:
        if n % c == 0:
            return c
    raise ValueError(f"no tile in {candidates} divides {n}")


def _ln(x, g, b):
    mu = jnp.mean(x, -1, keepdims=True)
    xc = x - mu
    var = jnp.mean(xc * xc, -1, keepdims=True)
    return xc * lax.rsqrt(var + LN_EPS) * g + b


def _dot(a, b):
    return jnp.dot(a.astype(BF16), b.astype(BF16), preferred_element_type=F32)


def _dot_nt(a, b):
    return lax.dot_general(a.astype(BF16), b.astype(BF16), (((1,), (1,)), ((), ())),
                           preferred_element_type=F32)


def _silu(x):
    return x * jax.nn.sigmoid(x)


def _pack_halves(x):
    w = x.shape[1] // 2
    lo = lax.bitcast_convert_type(x[:, 0:w].astype(BF16).astype(F32), jnp.uint32)
    hi = lax.bitcast_convert_type(x[:, w:2 * w].astype(BF16).astype(F32), jnp.uint32)
    return (hi & jnp.uint32(0xFFFF0000)) | (lo >> 16)


def _unpack_halves(p):
    lo = lax.bitcast_convert_type(p << 16, F32)
    hi = lax.bitcast_convert_type(p & jnp.uint32(0xFFFF0000), F32)
    return jnp.concatenate([lo, hi], axis=1)


def _ln_in_kernel(n_p_tiles, xp_ref, xs_ref, g_ref, b_ref, o_ref):
    i = pl.program_id(0)

    @pl.when(i < n_p_tiles)
    def _():
        o_ref[...] = _ln(xp_ref[...], g_ref[...], b_ref[...])

    @pl.when(i >= n_p_tiles)
    def _():
        o_ref[...] = _ln(xs_ref[...], g_ref[...], b_ref[...])


def _ln_in(xp, xs, g, b):
    tp, d = xp.shape
    ts = xs.shape[0]
    tm = _pick_tile(ts, (512, 256, 128, 64))
    assert tp % tm == 0
    npt, nst = tp // tm, ts // tm
    return pl.pallas_call(
        functools.partial(_ln_in_kernel, npt),
        grid=(npt + nst,),
        in_specs=[
            pl.BlockSpec((tm, d), lambda i: (jnp.minimum(i, npt - 1), 0)),
            pl.BlockSpec((tm, d), lambda i: (jnp.maximum(i - npt, 0), 0)),
            pl.BlockSpec((1, d), lambda i: (0, 0)),
            pl.BlockSpec((1, d), lambda i: (0, 0)),
        ],
        out_specs=pl.BlockSpec((tm, d), lambda i: (i, 0)),
        out_shape=jax.ShapeDtypeStruct((tp + ts, d), F32),
        compiler_params=_cparams("arbitrary"),
        name="ln_in",
    )(xp, xs, g.reshape(1, d), b.reshape(1, d))


def _proj_kernel(x_ref, w_ref, o_ref):
    o_ref[...] = _dot(x_ref[...], w_ref[...])


def _proj(x, w, name):
    t, k = x.shape
    n = w.shape[1]
    tm = _pick_tile(t, (512, 256, 128, 64))
    return pl.pallas_call(
        _proj_kernel,
        grid=(t // tm,),
        in_specs=[pl.BlockSpec((tm, k), lambda i: (i, 0)),
                  pl.BlockSpec((k, n), lambda i: (0, 0))],
        out_specs=pl.BlockSpec((tm, n), lambda i: (i, 0)),
        out_shape=jax.ShapeDtypeStruct((t, n), F32),
        compiler_params=_cparams("arbitrary"),
        name=name,
    )(x, w)


def _split2(x):
    hi = x.astype(BF16)
    lo = (x - hi.astype(F32)).astype(BF16)
    return hi, lo


def _split3(x):
    hi = x.astype(BF16)
    r1 = x - hi.astype(F32)
    mid = r1.astype(BF16)
    lo = (r1 - mid.astype(F32)).astype(BF16)
    return hi, mid, lo


def _sel_dot2(x, sel2):
    return jnp.dot(jnp.concatenate(_split2(x), axis=1), sel2, preferred_element_type=F32)


def _gdn_pipe_kernel(lt, h_ref, cbuf_ref, s0_ref, sbuf_ref, cw_ref, scw_ref, gp_ref, ng_ref,
                     ho_ref, cnew_ref, snew_ref, scnew_ref, ext_scr, uext_scr, s_scr, o_scr, pb_scr, pf_scr, pc_scr):
    i = pl.program_id(1)
    n_chunks = lt // CHUNK
    n_pairs = GDN_HEADS // 2
    pad = SUBLANE

    @pl.when(i == 0)
    def _():
        pb_scr[...] = jnp.zeros_like(pb_scr)
        pf_scr[...] = jnp.zeros_like(pf_scr)
        pc_scr[...] = jnp.zeros_like(pc_scr)
        ext_scr[0:pad, :] = jnp.concatenate(
            [jnp.zeros((pad - (GDN_CONV - 1), 3 * GDN_W), F32), cbuf_ref[0]], axis=0)
        uext_scr[0:pad, :] = jnp.concatenate(
            [jnp.zeros((pad - (SC_CONV - 1), SC_W), F32), sbuf_ref[0]], axis=0)
        zero_blk = jnp.zeros((HEAD_DIM, HEAD_DIM), F32)
        for p in range(n_pairs):
            s_scr[p] = jnp.concatenate(
                [jnp.concatenate([s0_ref[0, 2 * p], zero_blk], axis=1),
                 jnp.concatenate([zero_blk, s0_ref[0, 2 * p + 1]], axis=1)], axis=0)

    r384 = lax.broadcasted_iota(jnp.int32, (GDN_W, GDN_W), 0)
    c384 = lax.broadcasted_iota(jnp.int32, (GDN_W, GDN_W), 1)
    head_ones = jnp.where((r384 >> CHUNK_SHIFT) == (c384 >> CHUNK_SHIFT), 1.0, 0.0).astype(BF16)
    r128 = lax.broadcasted_iota(jnp.int32, (LANE, GDN_W), 0)
    c128 = lax.broadcasted_iota(jnp.int32, (LANE, GDN_W), 1) >> CHUNK_SHIFT
    exp_a = jnp.where(r128 == c128 + BA_A, 1.0, 0.0).astype(BF16)
    exp_b = jnp.where(r128 == c128 + BA_B, 1.0, 0.0).astype(BF16)
    exp_a3 = jnp.concatenate([exp_a, exp_a, exp_a], axis=0)
    exp_b2 = jnp.concatenate([exp_b, exp_b], axis=0)
    head_sum = lambda x: jnp.dot(x.astype(BF16), head_ones, preferred_element_type=F32)
    rc = lax.broadcasted_iota(jnp.int32, (CHUNK, CHUNK), 0)
    cc = lax.broadcasted_iota(jnp.int32, (CHUNK, CHUNK), 1)
    ltri = jnp.where(cc <= rc, 1.0, 0.0).astype(BF16)
    row = lax.broadcasted_iota(jnp.int32, (CHUNK, LANE), 0)
    lane = lax.broadcasted_iota(jnp.int32, (CHUNK, LANE), 1)
    col = lane & (HEAD_DIM - 1)
    left = lane < HEAD_DIM
    m_incl = col <= row
    m_strict = col < row
    m_diag = col == row
    eye = jnp.where(m_diag, 1.0, 0.0)
    m_base = ((row >> 1) == (col >> 1)) & m_strict
    lvl_masks = [((row >> (sh + 1)) == (col >> (sh + 1))) & ((row >> sh) != (col >> sh)) & m_strict
                 for sh in range(1, CHUNK_SHIFT)]
    r2 = lax.broadcasted_iota(jnp.int32, (LANE, LANE), 0)
    c2 = lax.broadcasted_iota(jnp.int32, (LANE, LANE), 1)
    m_bd = (r2 >> CHUNK_SHIFT) == (c2 >> CHUNK_SHIFT)
    lane_ba = lax.broadcasted_iota(jnp.int32, (CHUNK, LANE), 1)
    is_a = (lane_ba >= BA_A) & (lane_ba < BA_A + GDN_HEADS)

    def bd(x):
        zero = jnp.zeros_like(x)
        return jnp.concatenate([jnp.where(left, x, zero), jnp.where(left, zero, x)], axis=0)

    units = [(c, p) for c in range(n_chunks) for p in range(n_pairs)]
    rsl = lambda c: slice(c * CHUNK, (c + 1) * CHUNK)
    psl = lambda p: slice(p * LANE, (p + 1) * LANE)

    wr = i & 1
    rd = 1 - wr
    qn_b, kn_b, kbeta_b, qg_b, kb_b, vb_b, kt_b = (pb_scr[rd, j] for j in range(7))
    gam = pf_scr[rd, 0]
    z_gate = pf_scr[rd, 1]
    ho_ref[:, GDN_W:MIX_W] = pc_scr[rd]

    ext_scr[pad:pad + lt, :] = h_ref[:, COL_QKV:COL_QKV + 3 * GDN_W]
    uext_scr[pad:pad + lt, :] = (h_ref[:, COL_SC + SC_W:COL_SC + 2 * SC_W]
                                 * h_ref[:, COL_SC + 2 * SC_W:COL_SC + 3 * SC_W])
    qkv_chunks = {}

    def front_conv(c):
        r0 = c * CHUNK
        conv = cw_ref[0:1, :] * ext_scr[pl.ds(r0 + pad - 3, CHUNK), :]
        for j in range(1, GDN_CONV):
            conv = conv + cw_ref[j:j + 1, :] * ext_scr[pl.ds(r0 + pad - 3 + j, CHUNK), :]
        qkv_chunks[c] = _silu(conv)
        y = scw_ref[0:1, :] * uext_scr[pl.ds(r0 + pad - 2, CHUNK), :]
        for j in range(1, SC_CONV):
            y = y + scw_ref[j:j + 1, :] * uext_scr[pl.ds(r0 + pad - 2 + j, CHUNK), :]
        pc_scr[wr, rsl(c), :] = jnp.concatenate(
            [h_ref[rsl(c), GDN_W:2 * GDN_W], h_ref[rsl(c), COL_SC:COL_SC + SC_W] * y], axis=1)
        pf_scr[wr, 1, rsl(c), :] = _silu(h_ref[rsl(c), COL_Z:COL_Z + GDN_W])

    def front_gates(c):
        qkv = qkv_chunks.pop(c)
        q = qkv[:, 0:GDN_W]
        k = qkv[:, GDN_W:2 * GDN_W]
        v = qkv[:, 2 * GDN_W:3 * GDN_W]
        qn = q * lax.rsqrt(head_sum(q * q) + RMS_EPS) * HEAD_DIM ** -0.5
        kn = k * lax.rsqrt(head_sum(k * k) + RMS_EPS)
        ba = h_ref[rsl(c), COL_BA:COL_BA + LANE]
        beta_blk = jax.nn.sigmoid(ba)
        sp_in = ba + gp_ref[1:2, :]
        softplus = jnp.maximum(sp_in, 0.0) + jnp.log(1.0 + jnp.exp(-jnp.abs(sp_in)))
        g_blk = jnp.where(is_a, -jnp.exp(gp_ref[0:1, :]) * softplus, 0.0)
        gam3 = jnp.dot(ltri, jnp.concatenate(_split3(g_blk), axis=1), preferred_element_type=F32)
        gam_blk = gam3[:, 0:LANE] + gam3[:, LANE:2 * LANE] + gam3[:, 2 * LANE:3 * LANE]
        gam_c = jnp.dot(jnp.concatenate(_split3(gam_blk), axis=1), exp_a3, preferred_element_type=F32)
        g_exp = jnp.exp(gam_c)
        k_fac = jnp.exp(gam_c[CHUNK - 1:CHUNK, :] - gam_c)
        beta = _sel_dot2(beta_blk, exp_b2)
        kbeta = kn * beta
        for j, val in enumerate((qn, kn, kbeta, qn * g_exp, kbeta * g_exp, v * beta, kn * k_fac)):
            pb_scr[wr, j, rsl(c), :] = val.astype(BF16)
        pf_scr[wr, 0, rsl(c), :] = gam_c

    pieces = [functools.partial(f, c) for c in range(n_chunks) for f in (front_conv, front_gates)]

    def emit(n=1):
        for _ in range(n):
            if pieces:
                pieces.pop(0)()

    a_b, t_inv, qd_b = {}, {}, {}
    for c, p in units:
        gc = gam[rsl(c), psl(p)]
        gr = jnp.sum(jnp.where(m_diag, gc, 0.0), axis=0, keepdims=True)
        dec = jnp.exp(jnp.where(m_incl, gc - gr, NEG_BIG))
        kq = lax.dot_general(jnp.concatenate([qn_b[rsl(c), psl(p)], kbeta_b[rsl(c), psl(p)]], axis=0),
                             bd(kn_b[rsl(c), psl(p)]), (((1,), (1,)), ((), ())), preferred_element_type=F32)
        qd_b[c, p] = (kq[0:CHUNK] * dec).astype(BF16)
        a_mat = jnp.where(m_strict, dec * kq[CHUNK:2 * CHUNK], 0.0)
        a_b[c, p] = a_mat.astype(BF16)
        t_inv[c, p] = eye - jnp.where(m_base, a_mat, 0.0)
    emit()
    zero_b = jnp.zeros((CHUNK, LANE), BF16)
    for lm in lvl_masks:
        w_lvl = {u_: jnp.dot(jnp.where(lm, a_b[u_], zero_b), bd(t_inv[u_].astype(BF16)),
                             preferred_element_type=F32) for u_ in units}
        for u_ in units:
            t_inv[u_] = t_inv[u_] - jnp.dot(t_inv[u_].astype(BF16), bd(w_lvl[u_].astype(BF16)),
                                            preferred_element_type=F32)
        emit()
    u0, w_b, kw_b, q_add, x_sols = {}, {}, {}, {}, {}
    for c, p in units:
        rhs = jnp.concatenate([bd(vb_b[rsl(c), psl(p)]), bd(kb_b[rsl(c), psl(p)])], axis=1)
        x_sols[c, p] = jnp.dot(t_inv[c, p].astype(BF16), rhs, preferred_element_type=F32)
    emit()
    for c, p in units:
        x_sol = x_sols[c, p]
        u0[c, p] = x_sol[:, 0:LANE]
        w_b[c, p] = x_sol[:, LANE:2 * LANE].astype(BF16)
        ktx = lax.dot_general(kt_b[rsl(c), psl(p)], x_sol.astype(BF16), (((0,), (0,)), ((), ())),
                              preferred_element_type=F32)
        q_add[c, p] = jnp.where(m_bd, ktx[:, 0:LANE], 0.0)
        kw_b[c, p] = jnp.where(m_bd, ktx[:, LANE:2 * LANE], 0.0).astype(BF16)
    emit()

    s_start = {}
    s_cur = [s_scr[p] for p in range(n_pairs)]
    for c in range(n_chunks):
        for p in range(n_pairs):
            s_bf = s_cur[p].astype(BF16)
            s_start[c, p] = s_bf
            e_last = jnp.exp(gam[(c + 1) * CHUNK - 1:(c + 1) * CHUNK, psl(p)])
            s_cur[p] = s_cur[p] * e_last - jnp.dot(kw_b[c, p], s_bf, preferred_element_type=F32) + q_add[c, p]
    for p in range(n_pairs):
        s_scr[p] = s_cur[p]

    wqs = {u_: jnp.dot(jnp.concatenate([w_b[u_], qg_b[rsl(u_[0]), psl(u_[1])]], axis=0), s_start[u_],
                       preferred_element_type=F32) for u_ in units}
    emit(len(pieces))
    for c, p in units:
        u = u0[c, p] - wqs[c, p][0:CHUNK]
        o_scr[rsl(c), psl(p)] = wqs[c, p][CHUNK:2 * CHUNK] + jnp.dot(qd_b[c, p], bd(u.astype(BF16)),
                                                                      preferred_element_type=F32)
    o = o_scr[...]
    o_ms = head_sum(o * o) * (1.0 / HEAD_DIM)
    ho_ref[:, 0:GDN_W] = o * lax.rsqrt(o_ms + RMS_EPS) * ng_ref[...] * z_gate

    cnew_ref[0] = ext_scr[pl.ds(lt + pad - 3, 3), :]
    ext_scr[0:pad, :] = ext_scr[lt:lt + pad, :]
    scnew_ref[0] = uext_scr[pl.ds(lt + pad - 2, 2), :]
    uext_scr[0:pad, :] = uext_scr[lt:lt + pad, :]

    @pl.when(i == pl.num_programs(1) - 1)
    def _():
        for p in range(n_pairs):
            snew_ref[0, 2 * p] = s_scr[p, 0:HEAD_DIM, 0:HEAD_DIM]
            snew_ref[0, 2 * p + 1] = s_scr[p, HEAD_DIM:LANE, HEAD_DIM:LANE]


def _gdn_sconv(h_all, row_off, nb, seq, cbuf, s0, sbuf, cw, scw, gp, ng):
    lt = _pick_tile(seq, (256, 128, 64))
    nt = seq // lt
    off = row_off // lt
    assert row_off % lt == 0
    tok_in = lambda b, i: (off + b * nt + jnp.minimum(i, nt - 1), 0)
    tok_out = lambda b, i: (off + b * nt + jnp.maximum(i - 1, 0), 0)
    return pl.pallas_call(
        functools.partial(_gdn_pipe_kernel, lt),
        grid=(nb, nt + 1),
        in_specs=[
            pl.BlockSpec((lt, P_PAD), tok_in),
            pl.BlockSpec((1, GDN_CONV - 1, 3 * GDN_W), lambda b, i: (b, 0, 0)),
            pl.BlockSpec((1, GDN_HEADS, HEAD_DIM, HEAD_DIM), lambda b, i: (b, 0, 0, 0)),
            pl.BlockSpec((1, SC_CONV - 1, SC_W), lambda b, i: (b, 0, 0)),
            pl.BlockSpec((GDN_CONV, 3 * GDN_W), lambda b, i: (0, 0)),
            pl.BlockSpec((SC_CONV, SC_W), lambda b, i: (0, 0)),
            pl.BlockSpec((SUBLANE, LANE), lambda b, i: (0, 0)),
            pl.BlockSpec((1, GDN_W), lambda b, i: (0, 0)),
        ],
        out_specs=[
            pl.BlockSpec((lt, MIX_W), tok_out),
            pl.BlockSpec((1, GDN_CONV - 1, 3 * GDN_W), lambda b, i: (b, 0, 0)),
            pl.BlockSpec((1, GDN_HEADS, HEAD_DIM, HEAD_DIM), lambda b, i: (b, 0, 0, 0)),
            pl.BlockSpec((1, SC_CONV - 1, SC_W), lambda b, i: (b, 0, 0)),
        ],
        out_shape=[
            jax.ShapeDtypeStruct(h_all.shape, F32),
            jax.ShapeDtypeStruct((nb, GDN_CONV - 1, 3 * GDN_W), F32),
            jax.ShapeDtypeStruct((nb, GDN_HEADS, HEAD_DIM, HEAD_DIM), F32),
            jax.ShapeDtypeStruct((nb, SC_CONV - 1, SC_W), F32),
        ],
        scratch_shapes=[
            pltpu.VMEM((lt + SUBLANE, 3 * GDN_W), F32),
            pltpu.VMEM((lt + SUBLANE, SC_W), F32),
            pltpu.VMEM((GDN_HEADS // 2, LANE, LANE), F32),
            pltpu.VMEM((lt, GDN_W), F32),
            pltpu.VMEM((2, 7, lt, GDN_W), BF16),
            pltpu.VMEM((2, 2, lt, GDN_W), F32),
            pltpu.VMEM((2, lt, MIX_W - GDN_W), F32),
        ],
        input_output_aliases={0: 0},
        compiler_params=_cparams("arbitrary", "arbitrary"),
        name="gdn_sconv",
    )(h_all, cbuf, s0, sbuf, cw, scw, gp, ng)


def _rope(x, cos, sin_signed):
    w = x.shape[1]
    nrep = w // LANE
    if nrep > 1:
        cos = jnp.concatenate([cos] * nrep, axis=1)
        sin_signed = jnp.concatenate([sin_signed] * nrep, axis=1)
    lane = lax.broadcasted_iota(jnp.int32, x.shape, 1) & (HEAD_DIM - 1)
    half = HEAD_DIM // 2
    swapped = jnp.where(lane < half, pltpu.roll(x, w - half, 1), pltpu.roll(x, half, 1))
    return x * cos + swapped * sin_signed


def _swa_kernel(lt, hist_valid, h_ref, hk_ref, hv_ref, cos_ref, sin_ref, sink_ref,
                o_ref, kr_ref, kall_scr, vall_scr):
    i = pl.program_id(1)

    @pl.when(i == 0)
    def _():
        kall_scr[0:WINDOW, :] = hk_ref[0]
        vall_scr[0:WINDOW, :] = hv_ref[0]

    cos = cos_ref[...]
    sin = sin_ref[...]
    q = _rope(h_ref[:, 0:SWA_W], cos, sin)
    k = _rope(h_ref[:, SWA_W:SWA_W + SWA_KV_W], cos, sin)
    kr_ref[...] = k
    kall_scr[WINDOW:WINDOW + lt, :] = k
    vall_scr[WINDOW:WINDOW + lt, :] = h_ref[:, SWA_W + SWA_KV_W:SWA_W + 2 * SWA_KV_W]
    q_b = q.astype(BF16)
    kall = kall_scr[...].astype(BF16)
    vall = vall_scr[...].astype(BF16)
    kall_sw = pltpu.roll(kall_scr[...], HEAD_DIM, 1).astype(BF16)
    vall_sw = pltpu.roll(vall_scr[...], HEAD_DIM, 1).astype(BF16)

    band = WINDOW + CHUNK
    half = 2 * LANE
    n_chunks = lt // CHUNK
    n_pairs = SWA_Q_HEADS // 2
    left = lax.broadcasted_iota(jnp.int32, (band, LANE), 1) < HEAD_DIM
    zero_band = jnp.zeros((band, LANE), BF16)
    zero_pad = jnp.zeros((half - band, LANE), BF16)
    k_src = ((kall, kall_sw), (kall, kall), (kall_sw, kall))
    v_src = ((vall, vall_sw), (vall, vall), (vall_sw, vall))

    def pair_bd(top, bot, ks):
        return jnp.concatenate([jnp.where(left, top[ks], zero_band), zero_pad,
                                jnp.where(left, zero_band, bot[ks]), zero_pad], axis=0)

    kb = lax.broadcasted_iota(jnp.int32, (1, half), 1)
    hist_ok = jnp.logical_or(hist_valid, i > 0)
    fill = [jnp.where(kb == band, sink_ref[0:1, hq:hq + 1], NEG_BIG) for hq in range(SWA_Q_HEADS)]
    r512 = lax.broadcasted_iota(jnp.int32, (2 * half, LANE), 0)
    l512 = lax.broadcasted_iota(jnp.int32, (2 * half, LANE), 1)
    ones_bd = jnp.where((r512 < half) == (l512 < HEAD_DIM), 1.0, 0.0).astype(BF16)
    units = [(c, p) for c in range(n_chunks) for p in range(n_pairs)]
    scores = {}
    for c, p in units:
        ks = slice(c * CHUNK, c * CHUNK + band)
        kbd = pair_bd(k_src[p][0], k_src[p][1], ks)
        scores[c, p] = lax.dot_general(q_b[c * CHUNK:(c + 1) * CHUNK, p * LANE:(p + 1) * LANE], kbd,
                                       (((1,), (1,)), ((), ())), preferred_element_type=F32)
    probs = {}
    for c, p in units:
        valid = (kb < band) & (hist_ok | ((kb >> CHUNK_SHIFT) + c >= WINDOW // CHUNK))
        halves = []
        for e_ in range(2):
            s = jnp.where(valid, scores[c, p][:, e_ * half:(e_ + 1) * half] * HEAD_DIM ** -0.5, fill[2 * p + e_])
            halves.append(jnp.exp(s - jnp.max(s, -1, keepdims=True)).astype(BF16))
        probs[c, p] = jnp.concatenate(halves, axis=1)
    for c, p in units:
        ks = slice(c * CHUNK, c * CHUNK + band)
        rhs = jnp.concatenate([pair_bd(v_src[p][0], v_src[p][1], ks), ones_bd], axis=1)
        od = jnp.dot(probs[c, p], rhs, preferred_element_type=F32)
        o_ref[c * CHUNK:(c + 1) * CHUNK, p * LANE:(p + 1) * LANE] = od[:, 0:LANE] / od[:, LANE:2 * LANE]

    if lt >= WINDOW:
        kall_scr[0:WINDOW, :] = kall_scr[lt:lt + WINDOW, :]
        vall_scr[0:WINDOW, :] = vall_scr[lt:lt + WINDOW, :]


def _swa(h_all, row_off, nb, seq, hist_k, hist_v, hist_valid, cos, sin, sinks):
    lt = _pick_tile(seq, (256, 128, 64))
    nt = seq // lt
    assert nt == 1 or lt >= WINDOW
    off = row_off // lt
    assert row_off % lt == 0
    swa_blk_w = P_PAD - COL_SWA
    return pl.pallas_call(
        functools.partial(_swa_kernel, lt, hist_valid),
        grid=(nb, nt),
        in_specs=[
            pl.BlockSpec((lt, swa_blk_w), lambda b, i: (off + b * nt + i, COL_SWA // swa_blk_w)),
            pl.BlockSpec((1, WINDOW, SWA_KV_W), lambda b, i: (b, 0, 0)),
            pl.BlockSpec((1, WINDOW, SWA_KV_W), lambda b, i: (b, 0, 0)),
            pl.BlockSpec((lt, LANE), lambda b, i: (i, 0)),
            pl.BlockSpec((lt, LANE), lambda b, i: (i, 0)),
            pl.BlockSpec((1, LANE), lambda b, i: (0, 0)),
        ],
        out_specs=[
            pl.BlockSpec((lt, SWA_W), lambda b, i: (off + b * nt + i, 1)),
            pl.BlockSpec((lt, SWA_KV_W), lambda b, i: (b * nt + i, 0)),
        ],
        out_shape=[
            jax.ShapeDtypeStruct(h_all.shape, F32),
            jax.ShapeDtypeStruct((nb * seq, SWA_KV_W), F32),
        ],
        scratch_shapes=[
            pltpu.VMEM((lt + WINDOW, SWA_KV_W), F32),
            pltpu.VMEM((lt + WINDOW, SWA_KV_W), F32),
        ],
        input_output_aliases={0: 0},
        compiler_params=_cparams("arbitrary", "arbitrary"),
        name="swa",
    )(h_all, hist_k, hist_v, cos, sin, sinks)


def _oproj_kernel(alpha, mix_ref, x_ref, w_ref, g_ref, b_ref, o_ref):
    mix = _dot(mix_ref[...], w_ref[...])
    o_ref[...] = _ln(alpha * x_ref[...] + mix, g_ref[...], b_ref[...])


def _oproj(h_all, x_all, w_o, g, b, alpha):
    t, d = x_all.shape
    tm = _pick_tile(t, (512, 256, 128, 64))
    return pl.pallas_call(
        functools.partial(_oproj_kernel, alpha),
        grid=(t // tm,),
        in_specs=[
            pl.BlockSpec((tm, MIX_W), lambda i: (i, 0)),
            pl.BlockSpec((tm, d), lambda i: (i, 0)),
            pl.BlockSpec((MIX_W, d), lambda i: (0, 0)),
            pl.BlockSpec((1, d), lambda i: (0, 0)),
            pl.BlockSpec((1, d), lambda i: (0, 0)),
        ],
        out_specs=pl.BlockSpec((tm, d), lambda i: (i, 0)),
        out_shape=jax.ShapeDtypeStruct((t, d), F32),
        input_output_aliases={1: 0},
        compiler_params=_cparams("arbitrary"),
        name="oproj_ln1",
    )(h_all, x_all, w_o, g.reshape(1, d), b.reshape(1, d))


def _memattn_kernel(alpha, x_ref, xb_in_ref, gate_in_ref, ids_in_ref, mk_ref, mv_ref, wcq_ref, wco_ref,
                    g_ref, b_ref, wr_ref, br_ref, x_out_ref, xb_ref, gate_ref, ids_ref):
    del xb_in_ref, gate_in_ref, ids_in_ref
    x = x_ref[...]
    q = _dot(x, wcq_ref[...])
    mk = mk_ref[0].astype(BF16)
    mv = mv_ref[0].astype(BF16)
    n_mem = mk.shape[0]
    q_b = q.astype(BF16)
    left = lax.broadcasted_iota(jnp.int32, (n_mem, LANE), 1) < HEAD_DIM
    zero_kv = jnp.zeros((n_mem, LANE), BF16)

    def pair_bd(x):
        return jnp.concatenate([jnp.where(left, x, zero_kv), jnp.where(left, zero_kv, x)], axis=0)

    pairs = range(MEM_HEADS // 2)
    scores = [lax.dot_general(q_b[:, p * LANE:(p + 1) * LANE], pair_bd(mk[:, p * LANE:(p + 1) * LANE]),
                              (((1,), (1,)), ((), ())), preferred_element_type=F32) for p in pairs]
    probs = []
    for p in pairs:
        halves = []
        for e_ in range(2):
            s = scores[p][:, e_ * n_mem:(e_ + 1) * n_mem] * HEAD_DIM ** -0.5
            halves.append(jnp.exp(s - jnp.max(s, -1, keepdims=True)).astype(BF16))
        probs.append(jnp.concatenate(halves, axis=1))
    r2 = lax.broadcasted_iota(jnp.int32, (2 * n_mem, LANE), 0)
    l2 = lax.broadcasted_iota(jnp.int32, (2 * n_mem, LANE), 1)
    ones_bd = jnp.where((r2 < n_mem) == (l2 < HEAD_DIM), 1.0, 0.0).astype(BF16)
    outs = []
    for p in pairs:
        rhs = jnp.concatenate([pair_bd(mv[:, p * LANE:(p + 1) * LANE]), ones_bd], axis=1)
        od = jnp.dot(probs[p], rhs, preferred_element_type=F32)
        outs.append(od[:, 0:LANE] / od[:, LANE:2 * LANE])
    o = _dot(jnp.concatenate(outs, axis=1), wco_ref[...])
    x2 = _ln(alpha * x + o, g_ref[...], b_ref[...])
    x_out_ref[...] = x2
    packed = _pack_halves(x2)
    for p in range(xb_ref.shape[0]):
        xb_ref[p] = packed[:, p * LANE:(p + 1) * LANE]

    logits = _dot(x2, wr_ref[...]) + br_ref[...]
    lane = lax.broadcasted_iota(jnp.int32, logits.shape, 1).astype(F32)
    vals, idxs = [], []
    for _ in range(TOP_K):
        m = jnp.max(logits, -1, keepdims=True)
        idx = jnp.min(jnp.where(logits == m, lane, float(LANE)), -1, keepdims=True)
        vals.append(m)
        idxs.append(idx)
        logits = jnp.where(lane == idx, 2 * NEG_BIG, logits)
    es = [jnp.exp(val - vals[0]) for val in vals]
    den = es[0] + es[1] + es[2] + es[3]
    gate = jnp.zeros(logits.shape, F32)
    ids = jnp.zeros(logits.shape, F32)
    for kk in range(TOP_K):
        gate = jnp.where(lane == kk, es[kk] / den, gate)
        ids = jnp.where(lane == kk, idxs[kk], ids)
    gate_ref[...] = gate
    ids_ref[...] = ids.astype(jnp.int32)


def _memattn_router(x_all, xb_all, gate_all, ids_all, row_off, nb, seq, mk, mv, wcq, wco, g, b, wr, br, alpha):
    t, d = x_all.shape
    tl = _pick_tile(seq, (512, 256, 128, 64))
    nt = seq // tl
    off = row_off // tl
    assert row_off % tl == 0
    n_mem = mk.shape[1]
    tok = lambda bb, i: (off + bb * nt + i, 0)
    const = lambda bb, i: (0, 0)
    return pl.pallas_call(
        functools.partial(_memattn_kernel, alpha),
        grid=(nb, nt),
        in_specs=[
            pl.BlockSpec((tl, d), tok),
            pl.BlockSpec(memory_space=pl.ANY),
            pl.BlockSpec(memory_space=pl.ANY),
            pl.BlockSpec(memory_space=pl.ANY),
            pl.BlockSpec((1, n_mem, MEM_W), lambda bb, i: (bb, 0, 0)),
            pl.BlockSpec((1, n_mem, MEM_W), lambda bb, i: (bb, 0, 0)),
            pl.BlockSpec((d, MEM_W), const),
            pl.BlockSpec((MEM_W, d), const),
            pl.BlockSpec((1, d), const),
            pl.BlockSpec((1, d), const),
            pl.BlockSpec((d, LANE), const),
            pl.BlockSpec((1, LANE), const),
        ],
        out_specs=[
            pl.BlockSpec((tl, d), tok),
            pl.BlockSpec((d // (2 * LANE), tl, LANE), lambda bb, i: (0, off + bb * nt + i, 0)),
            pl.BlockSpec((tl, LANE), tok),
            pl.BlockSpec((tl, LANE), tok),
        ],
        out_shape=[
            jax.ShapeDtypeStruct((t, d), F32),
            jax.ShapeDtypeStruct((d // (2 * LANE), t, LANE), jnp.uint32),
            jax.ShapeDtypeStruct((t, LANE), F32),
            jax.ShapeDtypeStruct((t, LANE), jnp.int32),
        ],
        input_output_aliases={0: 0, 1: 1, 2: 2, 3: 3},
        compiler_params=_cparams("arbitrary", "arbitrary"),
        name="memattn_router",
    )(x_all, xb_all, gate_all, ids_all, mk, mv, wcq, wco, g.reshape(1, d), b.reshape(1, d), wr, br)


def _expert_kernel(d_ff, be_ref, bv_ref, nu_ref, xs_ref, w1_ref, b1_ref, w2_ref, b2_ref, ys_ref, w1b_scr, w2b_scr):
    i = pl.program_id(0)
    prev = be_ref[jnp.maximum(i - 1, 0)]
    new_expert = (i == 0) | (be_ref[i] != prev)

    @pl.when(new_expert & (i < nu_ref[0]))
    def _():
        w1b_scr[...] = w1_ref[0, 0].astype(BF16)
        w2b_scr[...] = w2_ref[0, 0].astype(BF16)

    @pl.when(i < nu_ref[0])
    def _():
        row = lax.broadcasted_iota(jnp.int32, (xs_ref.shape[1], 1), 0)
        words = jnp.concatenate([xs_ref[p] for p in range(xs_ref.shape[0])], axis=1)
        xs = _unpack_halves(jnp.where(row < bv_ref[i], words, jnp.zeros_like(words))).astype(BF16)
        hh = jnp.dot(xs, w1b_scr[...], preferred_element_type=F32) + b1_ref[0, 0]
        glu = jnp.minimum(hh[:, 0:d_ff], SWIGLU_LIMIT)
        lin = jnp.clip(hh[:, d_ff:2 * d_ff], -SWIGLU_LIMIT, SWIGLU_LIMIT)
        act = glu * jax.nn.sigmoid(SWIGLU_ALPHA * glu) * (lin + 1.0)
        y = jnp.dot(act.astype(BF16), w2b_scr[...], preferred_element_type=F32) + b2_ref[0, 0]
        packed = _pack_halves(y)
        for p in range(ys_ref.shape[0]):
            ys_ref[p] = packed[:, p * LANE:(p + 1) * LANE]


def _experts(xs, blk_e, blk_valid, n_used, layer, w1, b1, w2, b2):
    n_planes, n_rows, _ = xs.shape
    n_l, n_e, d, two_ff = w1.shape
    d_ff = two_ff // 2
    n_blocks = n_rows // MOE_BM
    row_blk = lambda i, be, bv, nu: (0, jnp.minimum(i, nu[0] - 1), 0)
    exp_blk = lambda i, be, bv, nu: (layer, be[jnp.minimum(i, nu[0] - 1)], 0, 0)
    return pl.pallas_call(
        functools.partial(_expert_kernel, d_ff),
        grid_spec=pltpu.PrefetchScalarGridSpec(
            num_scalar_prefetch=3,
            grid=(n_blocks,),
            in_specs=[
                pl.BlockSpec((n_planes, MOE_BM, LANE), row_blk),
                pl.BlockSpec((1, 1, d, two_ff), exp_blk),
                pl.BlockSpec((1, 1, 1, two_ff), exp_blk),
                pl.BlockSpec((1, 1, d_ff, d), exp_blk),
                pl.BlockSpec((1, 1, 1, d), exp_blk),
            ],
            out_specs=pl.BlockSpec((n_planes, MOE_BM, LANE), row_blk),
            scratch_shapes=[pltpu.VMEM((d, two_ff), BF16), pltpu.VMEM((d_ff, d), BF16)],
        ),
        out_shape=jax.ShapeDtypeStruct((n_planes, n_rows, LANE), jnp.uint32),
        compiler_params=_cparams("arbitrary"),
        name="moe_experts",
    )(blk_e, blk_valid, n_used, xs, w1, b1.reshape(n_l, n_e, 1, two_ff), w2, b2.reshape(n_l, n_e, 1, d))


def _sc_mesh():
    return plsc.VectorSubcoreMesh(core_axis_name="c", subcore_axis_name="s")


def _sc_dispatch(xw, dest_km, n_rows):
    n_planes, t, _ = xw.shape
    n_src = n_planes * t
    nw = n_src // SC_WINDOW
    assert t % SC_WINDOW == 0

    @functools.partial(pl.kernel, out_type=jax.ShapeDtypeStruct((n_planes * n_rows, LANE), xw.dtype),
                       mesh=_sc_mesh(), scratch_types=[], name="moe_dispatch_sc")
    def scatter_rows(x_hbm, i0_hbm, i1_hbm, i2_hbm, i3_hbm, o_hbm):
        def body(x_vmem, i0, i1, i2, i3):
            for idx in (i0, i1, i2, i3):
                pltpu.sync_copy(x_vmem, o_hbm.at[idx.at[0]])

        pltpu.emit_pipeline(
            body,
            grid=(nw,),
            in_specs=[pl.BlockSpec((SC_WINDOW, LANE), lambda i: (i, 0))]
            + [pl.BlockSpec((1, SC_WINDOW), functools.partial(lambda k, i: (0, k * nw + i), k))
               for k in range(TOP_K)],
            out_specs=[],
            core_axis_name=("c", "s"),
            dimension_semantics=(pltpu.PARALLEL,),
        )(x_hbm, i0_hbm, i1_hbm, i2_hbm, i3_hbm)

    plane_off = (jnp.arange(n_planes, dtype=jnp.int32) * n_rows)[None, :, None]
    idx = (dest_km.reshape(TOP_K, 1, t) + plane_off).reshape(1, TOP_K * n_src)
    return scatter_rows(xw.reshape(n_src, LANE), idx, idx, idx, idx).reshape(n_planes, n_rows, LANE)


def _sc_gather(yw, idx_flat):
    n_planes, n_rows, _ = yw.shape
    n = idx_flat.shape[0]
    assert n % SC_WINDOW == 0

    @functools.partial(pl.kernel, out_type=jax.ShapeDtypeStruct((n_planes * n, LANE), yw.dtype),
                       mesh=_sc_mesh(), scratch_types=[], name="moe_gather_sc")
    def gather_rows(y_hbm, i_hbm, o_hbm):
        def body(i_vmem, o_vmem):
            pltpu.sync_copy(y_hbm.at[i_vmem.at[0]], o_vmem)

        pltpu.emit_pipeline(
            body,
            grid=(n_planes * n // SC_WINDOW,),
            in_specs=[pl.BlockSpec((1, SC_WINDOW), lambda i: (0, i))],
            out_specs=[pl.BlockSpec((SC_WINDOW, LANE), lambda i: (i, 0))],
            core_axis_name=("c", "s"),
            dimension_semantics=(pltpu.PARALLEL,),
        )(i_hbm, o_hbm)

    plane_off = (jnp.arange(n_planes, dtype=jnp.int32) * n_rows)[:, None]
    idx = (idx_flat[None, :] + plane_off).reshape(1, n_planes * n)
    return gather_rows(yw.reshape(n_planes * n_rows, LANE), idx).reshape(n_planes, n, LANE)


def _combine_kernel(alpha, y0_ref, y1_ref, y2_ref, y3_ref, gate_ref, x_ref, g_ref, b_ref, o_ref):
    gate = gate_ref[...]
    acc = alpha * x_ref[...]
    for k, y_ref in enumerate((y0_ref, y1_ref, y2_ref, y3_ref)):
        words = jnp.concatenate([y_ref[p] for p in range(y_ref.shape[0])], axis=1)
        acc = acc + gate[:, k:k + 1] * _unpack_halves(words)
    o_ref[...] = _ln(acc, g_ref[...], b_ref[...])


def _combine(yg, gate_all, x_all, g, b, alpha):
    t, d = x_all.shape
    tm = _pick_tile(t, (512, 256, 128, 64))
    nt = t // tm
    y_spec = lambda k: pl.BlockSpec((yg.shape[0], tm, LANE), lambda i: (0, k * nt + i, 0))
    return pl.pallas_call(
        functools.partial(_combine_kernel, alpha),
        grid=(nt,),
        in_specs=[
            y_spec(0), y_spec(1), y_spec(2), y_spec(3),
            pl.BlockSpec((tm, LANE), lambda i: (i, 0)),
            pl.BlockSpec((tm, d), lambda i: (i, 0)),
            pl.BlockSpec((1, d), lambda i: (0, 0)),
            pl.BlockSpec((1, d), lambda i: (0, 0)),
        ],
        out_specs=pl.BlockSpec((tm, d), lambda i: (i, 0)),
        out_shape=jax.ShapeDtypeStruct((t, d), F32),
        input_output_aliases={5: 0},
        compiler_params=_cparams("arbitrary"),
        name="moe_combine_ln3",
    )(yg, yg, yg, yg, gate_all, x_all, g.reshape(1, d), b.reshape(1, d))


def _rank_kernel(ids_ref, rank_ref, cnt_ref, run_scr):
    i = pl.program_id(0)

    @pl.when(i == 0)
    def _():
        run_scr[...] = jnp.zeros_like(run_scr)

    ids = ids_ref[...].astype(F32)
    tm = ids.shape[0]
    lane = lax.broadcasted_iota(jnp.int32, ids.shape, 1).astype(F32)
    one_hots = [jnp.where(lane == ids[:, k:k + 1], 1.0, 0.0) for k in range(TOP_K)]
    multi_hot = one_hots[0] + one_hots[1] + one_hots[2] + one_hots[3]
    r = lax.broadcasted_iota(jnp.int32, (tm, tm), 0)
    c = lax.broadcasted_iota(jnp.int32, (tm, tm), 1)
    before = jnp.dot(jnp.where(c < r, 1.0, 0.0).astype(BF16), multi_hot.astype(BF16),
                     preferred_element_type=F32) + run_scr[0:1, :]
    rank = jnp.zeros(ids.shape, F32)
    for k in range(TOP_K):
        rank = jnp.where(lane == k, jnp.sum(one_hots[k] * before, -1, keepdims=True), rank)
    rank_ref[...] = rank.astype(jnp.int32)
    run_scr[0:1, :] = run_scr[0:1, :] + jnp.sum(multi_hot, 0, keepdims=True)
    cnt_ref[...] = run_scr[...]


def _dest_kernel(ids_ref, rank_ref, start_ref, dest_ref):
    ids = ids_ref[...].astype(F32)
    lane = lax.broadcasted_iota(jnp.int32, ids.shape, 1).astype(F32)
    start = start_ref[...]
    dest = rank_ref[...].astype(F32)
    for k in range(TOP_K):
        seg = jnp.sum(jnp.where(lane == ids[:, k:k + 1], start, 0.0), -1, keepdims=True)
        dest = jnp.where(lane == k, dest + seg, dest)
    dest_ref[...] = dest.astype(jnp.int32)


def _route(ids_all, n_experts):
    t = ids_all.shape[0]
    tk = t * TOP_K
    n_blocks = -(-tk // MOE_BM) + n_experts
    n_rows = n_blocks * MOE_BM
    assert n_rows < 2 ** 24
    tm = _pick_tile(t, (512, 256, 128, 64))
    tok = pl.BlockSpec((tm, LANE), lambda i: (i, 0))
    rank, cnt = pl.pallas_call(
        _rank_kernel,
        grid=(t // tm,),
        in_specs=[tok],
        out_specs=[tok, pl.BlockSpec((SUBLANE, LANE), lambda i: (0, 0))],
        out_shape=[jax.ShapeDtypeStruct((t, LANE), jnp.int32), jax.ShapeDtypeStruct((SUBLANE, LANE), F32)],
        scratch_shapes=[pltpu.VMEM((SUBLANE, LANE), F32)],
        compiler_params=_cparams("arbitrary"),
        name="moe_rank",
    )(ids_all)
    counts = cnt[0, 0:n_experts].astype(jnp.int32)
    padded = (counts + MOE_BM - 1) // MOE_BM * MOE_BM
    pad_end = jnp.cumsum(padded)
    pad_start = pad_end - padded
    start_row = jnp.zeros((1, LANE), F32).at[0, 0:n_experts].set(pad_start.astype(F32))
    dest = pl.pallas_call(
        _dest_kernel,
        grid=(t // tm,),
        in_specs=[tok, tok, pl.BlockSpec((1, LANE), lambda i: (0, 0))],
        out_specs=tok,
        out_shape=jax.ShapeDtypeStruct((t, LANE), jnp.int32),
        compiler_params=_cparams("arbitrary"),
        name="moe_dest",
    )(ids_all, rank, start_row)
    dest_km = dest[:, 0:TOP_K].T.reshape(-1)
    blk = jnp.arange(n_blocks, dtype=jnp.int32)
    blk_e = jnp.minimum(jnp.sum((blk[:, None] * MOE_BM >= pad_end[None, :]).astype(jnp.int32), axis=1),
                        n_experts - 1)
    blk_valid = jnp.clip(counts[blk_e] - (blk * MOE_BM - pad_start[blk_e]), 0, MOE_BM).astype(jnp.int32)
    n_used = (pad_end[-1] // MOE_BM).astype(jnp.int32).reshape(1)
    return dest_km, n_rows, blk_e, blk_valid, n_used


def _rope_tables(pos):
    half = HEAD_DIM // 2
    inv = ROPE_THETA ** (-jnp.arange(half, dtype=F32) / half)
    ang = pos.astype(F32)[:, None] * inv[None, :]
    cos = jnp.cos(ang)
    sin = jnp.sin(ang)
    cos = jnp.concatenate([cos, cos, cos, cos], axis=1)
    sin = jnp.concatenate([-sin, sin, -sin, sin], axis=1)
    return cos, sin


def _permute_w_in(w):
    d = w.shape[0]
    o_b = 4 * GDN_W
    o_q = o_b + 2 * GDN_HEADS
    o_sc = o_q + SWA_W + 2 * SWA_KV_W
    ba = jnp.concatenate([w[:, o_b:o_q], jnp.zeros((d, LANE - 2 * GDN_HEADS), w.dtype)], axis=1)
    return jnp.concatenate([w[:, 0:o_b], w[:, o_sc:], w[:, o_q:o_sc], ba], axis=1)


def kernel(x_prompt, x_sample, state_gdn_S, state_gdn_conv, cache_swa_k, cache_swa_v, state_sconv, cache_mem_k, cache_mem_v, mem_prompt, ln_in_g, ln_in_b, w_in, gdn_conv_w, gdn_a_log, gdn_dt_bias, gdn_norm_g, swa_sinks, sc_conv_w, w_o, ln1_g, ln1_b, w_cq, w_mk, w_mv, w_co, ln2_g, ln2_b, w_router, b_router, w1, b1, w2, b2, ln3_g, ln3_b):
    bp, lp, d = x_prompt.shape
    bs, ls, _ = x_sample.shape
    depth = w_in.shape[0]
    n_experts = w_router.shape[2]
    n_mem = mem_prompt.shape[1]
    tp, ts = bp * lp, bs * ls
    t_all = tp + ts
    alpha = (2 * depth) ** 0.25
    assert cache_swa_k.shape[2] == WINDOW and d == MIX_W

    x_all = _ln_in(x_prompt.reshape(tp, d), x_sample.reshape(ts, d), ln_in_g, ln_in_b)
    xb_all = jnp.zeros((d // (2 * LANE), t_all, LANE), jnp.uint32)
    gate_all = jnp.zeros((t_all, LANE), F32)
    ids_all = jnp.zeros((t_all, LANE), jnp.int32)

    cos_p, sin_p = _rope_tables(jnp.arange(lp))
    cos_s, sin_s = _rope_tables(PAST_LEN + jnp.arange(ls))
    mem_flat = mem_prompt.reshape(bp * n_mem, d)
    zeros_cbuf = jnp.zeros((bp, GDN_CONV - 1, 3 * GDN_W), F32)
    zeros_s = jnp.zeros((bp, GDN_HEADS, HEAD_DIM, HEAD_DIM), F32)
    zeros_sbuf = jnp.zeros((bp, SC_CONV - 1, SC_W), F32)
    zeros_hist = jnp.zeros((bp, WINDOW, SWA_KV_W), F32)

    outs = {k: [] for k in ("p_S", "p_conv", "p_k", "p_v", "p_sc", "p_mk", "p_mv",
                            "s_S", "s_conv", "s_k", "s_v", "s_sc")}
    for l in range(depth):
        w_in_l = _permute_w_in(w_in[l]).astype(BF16)
        gp = jnp.zeros((SUBLANE, LANE), F32)
        gp = gp.at[0, BA_A:BA_A + GDN_HEADS].set(gdn_a_log[l]).at[1, BA_A:BA_A + GDN_HEADS].set(gdn_dt_bias[l])
        ng = jnp.tile(gdn_norm_g[l].reshape(1, HEAD_DIM), (1, GDN_HEADS))
        sinks = jnp.zeros((1, LANE), F32).at[0, 0:SWA_Q_HEADS].set(swa_sinks[l])
        wr = jnp.concatenate([w_router[l], jnp.zeros((d, LANE - n_experts), F32)], axis=1).astype(BF16)
        br = jnp.concatenate([b_router[l], jnp.full((LANE - n_experts,), NEG_BIG, F32)]).reshape(1, LANE)

        mkv = _proj(mem_flat, jnp.concatenate([w_mk[l], w_mv[l]], axis=1).astype(BF16), "mem_kv")
        mk_p = mkv[:, 0:MEM_W].reshape(bp, n_mem, MEM_W)
        mv_p = mkv[:, MEM_W:2 * MEM_W].reshape(bp, n_mem, MEM_W)
        outs["p_mk"].append(mk_p.reshape(bp, n_mem, MEM_HEADS, HEAD_DIM))
        outs["p_mv"].append(mv_p.reshape(bp, n_mem, MEM_HEADS, HEAD_DIM))

        h_all = _proj(x_all, w_in_l, "w_in")

        v_p = h_all[0:tp, COL_SWA_V:COL_SWA_V + SWA_KV_W].reshape(bp, lp, SWA_KV_W)[:, lp - WINDOW:]
        v_s = h_all[tp:, COL_SWA_V:COL_SWA_V + SWA_KV_W]
        h_all, c_p, s_p, sc_p = _gdn_sconv(h_all, 0, bp, lp, zeros_cbuf, zeros_s, zeros_sbuf,
                                           gdn_conv_w[l], sc_conv_w[l], gp, ng)
        h_all, c_s, s_s, sc_s = _gdn_sconv(h_all, tp, bs, ls, state_gdn_conv[l], state_gdn_S[l], state_sconv[l],
                                           gdn_conv_w[l], sc_conv_w[l], gp, ng)
        h_all, kr_p = _swa(h_all, 0, bp, lp, zeros_hist, zeros_hist, False, cos_p, sin_p, sinks)
        h_all, kr_s = _swa(h_all, tp, bs, ls, cache_swa_k[l].reshape(bs, WINDOW, SWA_KV_W),
                           cache_swa_v[l].reshape(bs, WINDOW, SWA_KV_W), True, cos_s, sin_s, sinks)
        outs["p_S"].append(s_p); outs["p_conv"].append(c_p); outs["p_sc"].append(sc_p)
        outs["s_S"].append(s_s); outs["s_conv"].append(c_s); outs["s_sc"].append(sc_s)
        outs["p_k"].append(kr_p.reshape(bp, lp, SWA_KV_W)[:, lp - WINDOW:].reshape(bp, WINDOW, SWA_KV_HEADS, HEAD_DIM))
        outs["p_v"].append(v_p.reshape(bp, WINDOW, SWA_KV_HEADS, HEAD_DIM))
        outs["s_k"].append(kr_s.reshape(bs, ls, SWA_KV_HEADS, HEAD_DIM))
        outs["s_v"].append(v_s.reshape(bs, ls, SWA_KV_HEADS, HEAD_DIM))

        x_all = _oproj(h_all, x_all, w_o[l].astype(BF16), ln1_g[l], ln1_b[l], alpha)

        wcq = w_cq[l].astype(BF16)
        wco = w_co[l].astype(BF16)
        x_all, xb_all, gate_all, ids_all = _memattn_router(
            x_all, xb_all, gate_all, ids_all, 0, bp, lp, mk_p, mv_p, wcq, wco, ln2_g[l], ln2_b[l], wr, br, alpha)
        x_all, xb_all, gate_all, ids_all = _memattn_router(
            x_all, xb_all, gate_all, ids_all, tp, bs, ls,
            cache_mem_k[l].reshape(bs, n_mem, MEM_W), cache_mem_v[l].reshape(bs, n_mem, MEM_W),
            wcq, wco, ln2_g[l], ln2_b[l], wr, br, alpha)

        dest_km, n_rows, blk_e, blk_valid, n_used = _route(ids_all, n_experts)
        xs = _sc_dispatch(xb_all, dest_km, n_rows)
        ys = _experts(xs, blk_e, blk_valid, n_used, l, w1, b1, w2, b2)
        yg = _sc_gather(ys, dest_km)
        x_all = _combine(yg, gate_all, x_all, ln3_g[l], ln3_b[l], alpha)

    st = lambda k: jnp.stack(outs[k])
    return (x_all[0:tp].reshape(bp, lp, d), x_all[tp:].reshape(bs, ls, d),
            st("p_S"), st("p_conv"), st("p_k"), st("p_v"), st("p_sc"), st("p_mk"), st("p_mv"),
            st("s_S"), st("s_conv"), st("s_k"), st("s_v"), st("s_sc"))
```

```python
import functools

import jax
import jax.numpy as jnp
from jax import lax
from jax.experimental import pallas as pl
from jax.experimental.pallas import tpu as pltpu
from jax.experimental.pallas import tpu_sc as plsc

F32 = jnp.float32
BF16 = jnp.bfloat16

LANE = 128
SUBLANE = 8
VMEM_LIMIT_BYTES = 56 * 1024 * 1024

CHUNK = 64
CHUNK_SHIFT = 6
HEAD_DIM = 64
GDN_HEADS = 6
GDN_W = GDN_HEADS * HEAD_DIM
GDN_CONV = 4
SWA_Q_HEADS = 6
SWA_KV_HEADS = 2
SWA_GROUP = SWA_Q_HEADS // SWA_KV_HEADS
SWA_W = SWA_Q_HEADS * HEAD_DIM
SWA_KV_W = SWA_KV_HEADS * HEAD_DIM
WINDOW = 128
ROPE_THETA = 10000.0
SC_W = 256
SC_CONV = 3
MEM_HEADS = 4
MEM_W = MEM_HEADS * HEAD_DIM
TOP_K = 4
SWIGLU_LIMIT = 7.0
SWIGLU_ALPHA = 1.702
PAST_LEN = 2048
LN_EPS = 1e-5
RMS_EPS = 1e-6
NEG_BIG = -1e30

COL_QKV = 0
COL_Z = 1152
COL_SC = 1536
COL_SWA = 2304
COL_SWA_V = COL_SWA + SWA_W + SWA_KV_W
COL_BA = COL_SWA + SWA_W + 2 * SWA_KV_W
P_PAD = 3072
BA_B = 0
BA_A = GDN_HEADS
MIX_W = 1024

MOE_BM = 512
SC_WINDOW = 128


def _cparams(*sem):
    return pltpu.CompilerParams(dimension_semantics=sem, vmem_limit_bytes=VMEM_LIMIT_BYTES)


def _pick_tile(n, candidates):
    for c in candidates:
        if n % c == 0:
            return c
    raise ValueError(f"no tile in {candidates} divides {n}")


def _ln(x, g, b):
    mu = jnp.mean(x, -1, keepdims=True)
    xc = x - mu
    var = jnp.mean(xc * xc, -1, keepdims=True)
    return xc * lax.rsqrt(var + LN_EPS) * g + b


def _dot(a, b):
    return jnp.dot(a.astype(BF16), b.astype(BF16), preferred_element_type=F32)


def _dot_nt(a, b):
    return lax.dot_general(a.astype(BF16), b.astype(BF16), (((1,), (1,)), ((), ())),
                           preferred_element_type=F32)


def _silu(x):
    return x * jax.nn.sigmoid(x)


def _pack_halves(x):
    w = x.shape[1] // 2
    lo = lax.bitcast_convert_type(x[:, 0:w].astype(BF16).astype(F32), jnp.uint32)
    hi = lax.bitcast_convert_type(x[:, w:2 * w].astype(BF16).astype(F32), jnp.uint32)
    return (hi & jnp.uint32(0xFFFF0000)) | (lo >> 16)


def _unpack_halves(p):
    lo = lax.bitcast_convert_type(p << 16, F32)
    hi = lax.bitcast_convert_type(p & jnp.uint32(0xFFFF0000), F32)
    return jnp.concatenate([lo, hi], axis=1)


def _ln_in_kernel(n_p_tiles, xp_ref, xs_ref, g_ref, b_ref, o_ref):
    i = pl.program_id(0)

    @pl.when(i < n_p_tiles)
    def _():
        o_ref[...] = _ln(xp_ref[...], g_ref[...], b_ref[...])

    @pl.when(i >= n_p_tiles)
    def _():
        o_ref[...] = _ln(xs_ref[...], g_ref[...], b_ref[...])


def _ln_in(xp, xs, g, b):
    tp, d = xp.shape
    ts = xs.shape[0]
    tm = _pick_tile(ts, (512, 256, 128, 64))
    assert tp % tm == 0
    npt, nst = tp // tm, ts // tm
    return pl.pallas_call(
        functools.partial(_ln_in_kernel, npt),
        grid=(npt + nst,),
        in_specs=[
            pl.BlockSpec((tm, d), lambda i: (jnp.minimum(i, npt - 1), 0)),
            pl.BlockSpec((tm, d), lambda i: (jnp.maximum(i - npt, 0), 0)),
            pl.BlockSpec((1, d), lambda i: (0, 0)),
            pl.BlockSpec((1, d), lambda i: (0, 0)),
        ],
        out_specs=pl.BlockSpec((tm, d), lambda i: (i, 0)),
        out_shape=jax.ShapeDtypeStruct((tp + ts, d), F32),
        compiler_params=_cparams("arbitrary"),
        name="ln_in",
    )(xp, xs, g.reshape(1, d), b.reshape(1, d))


def _proj_kernel(x_ref, w_ref, o_ref):
    o_ref[...] = _dot(x_ref[...], w_ref[...])


def _proj(x, w, name):
    t, k = x.shape
    n = w.shape[1]
    tm = _pick_tile(t, (512, 256, 128, 64))
    return pl.pallas_call(
        _proj_kernel,
        grid=(t // tm,),
        in_specs=[pl.BlockSpec((tm, k), lambda i: (i, 0)),
                  pl.BlockSpec((k, n), lambda i: (0, 0))],
        out_specs=pl.BlockSpec((tm, n), lambda i: (i, 0)),
        out_shape=jax.ShapeDtypeStruct((t, n), F32),
        compiler_params=_cparams("arbitrary"),
        name=name,
    )(x, w)


def _split2(x):
    hi = x.astype(BF16)
    lo = (x - hi.astype(F32)).astype(BF16)
    return hi, lo


def _split3(x):
    hi = x.astype(BF16)
    r1 = x - hi.astype(F32)
    mid = r1.astype(BF16)
    lo = (r1 - mid.astype(F32)).astype(BF16)
    return hi, mid, lo


def _sel_dot2(x, sel2):
    return jnp.dot(jnp.concatenate(_split2(x), axis=1), sel2, preferred_element_type=F32)


def _gdn_pipe_kernel(lt, h_ref, cbuf_ref, s0_ref, sbuf_ref, cw_ref, scw_ref, gp_ref, ng_ref,
                     ho_ref, cnew_ref, snew_ref, scnew_ref, ext_scr, uext_scr, s_scr, o_scr, pb_scr, pf_scr, pc_scr):
    i = pl.program_id(1)
    n_chunks = lt // CHUNK
    n_pairs = GDN_HEADS // 2
    pad = SUBLANE

    @pl.when(i == 0)
    def _():
        pb_scr[...] = jnp.zeros_like(pb_scr)
        pf_scr[...] = jnp.zeros_like(pf_scr)
        pc_scr[...] = jnp.zeros_like(pc_scr)
        ext_scr[0:pad, :] = jnp.concatenate(
            [jnp.zeros((pad - (GDN_CONV - 1), 3 * GDN_W), F32), cbuf_ref[0]], axis=0)
        uext_scr[0:pad, :] = jnp.concatenate(
            [jnp.zeros((pad - (SC_CONV - 1), SC_W), F32), sbuf_ref[0]], axis=0)
        zero_blk = jnp.zeros((HEAD_DIM, HEAD_DIM), F32)
        for p in range(n_pairs):
            s_scr[p] = jnp.concatenate(
                [jnp.concatenate([s0_ref[0, 2 * p], zero_blk], axis=1),
                 jnp.concatenate([zero_blk, s0_ref[0, 2 * p + 1]], axis=1)], axis=0)

    r384 = lax.broadcasted_iota(jnp.int32, (GDN_W, GDN_W), 0)
    c384 = lax.broadcasted_iota(jnp.int32, (GDN_W, GDN_W), 1)
    head_ones = jnp.where((r384 >> CHUNK_SHIFT) == (c384 >> CHUNK_SHIFT), 1.0, 0.0).astype(BF16)
    r128 = lax.broadcasted_iota(jnp.int32, (LANE, GDN_W), 0)
    c128 = lax.broadcasted_iota(jnp.int32, (LANE, GDN_W), 1) >> CHUNK_SHIFT
    exp_a = jnp.where(r128 == c128 + BA_A, 1.0, 0.0).astype(BF16)
    exp_b = jnp.where(r128 == c128 + BA_B, 1.0, 0.0).astype(BF16)
    exp_a3 = jnp.concatenate([exp_a, exp_a, exp_a], axis=0)
    exp_b2 = jnp.concatenate([exp_b, exp_b], axis=0)
    head_sum = lambda x: jnp.dot(x.astype(BF16), head_ones, preferred_element_type=F32)
    rc = lax.broadcasted_iota(jnp.int32, (CHUNK, CHUNK), 0)
    cc = lax.broadcasted_iota(jnp.int32, (CHUNK, CHUNK), 1)
    ltri = jnp.where(cc <= rc, 1.0, 0.0).astype(BF16)
    row = lax.broadcasted_iota(jnp.int32, (CHUNK, LANE), 0)
    lane = lax.broadcasted_iota(jnp.int32, (CHUNK, LANE), 1)
    col = lane & (HEAD_DIM - 1)
    left = lane < HEAD_DIM
    m_incl = col <= row
    m_strict = col < row
    m_diag = col == row
    eye = jnp.where(m_diag, 1.0, 0.0)
    m_base = ((row >> 1) == (col >> 1)) & m_strict
    lvl_masks = [((row >> (sh + 1)) == (col >> (sh + 1))) & ((row >> sh) != (col >> sh)) & m_strict
                 for sh in range(1, CHUNK_SHIFT)]
    r2 = lax.broadcasted_iota(jnp.int32, (LANE, LANE), 0)
    c2 = lax.broadcasted_iota(jnp.int32, (LANE, LANE), 1)
    m_bd = (r2 >> CHUNK_SHIFT) == (c2 >> CHUNK_SHIFT)
    lane_ba = lax.broadcasted_iota(jnp.int32, (CHUNK, LANE), 1)
    is_a = (lane_ba >= BA_A) & (lane_ba < BA_A + GDN_HEADS)

    def bd(x):
        zero = jnp.zeros_like(x)
        return jnp.concatenate([jnp.where(left, x, zero), jnp.where(left, zero, x)], axis=0)

    units = [(c, p) for c in range(n_chunks) for p in range(n_pairs)]
    rsl = lambda c: slice(c * CHUNK, (c + 1) * CHUNK)
    psl = lambda p: slice(p * LANE, (p + 1) * LANE)

    wr = i & 1
    rd = 1 - wr
    qn_b, kn_b, kbeta_b, qg_b, kb_b, vb_b, kt_b = (pb_scr[rd, j] for j in range(7))
    gam = pf_scr[rd, 0]
    z_gate = pf_scr[rd, 1]
    ho_ref[:, GDN_W:MIX_W] = pc_scr[rd]

    ext_scr[pad:pad + lt, :] = h_ref[:, COL_QKV:COL_QKV + 3 * GDN_W]
    uext_scr[pad:pad + lt, :] = (h_ref[:, COL_SC + SC_W:COL_SC + 2 * SC_W]
                                 * h_ref[:, COL_SC + 2 * SC_W:COL_SC + 3 * SC_W])
    qkv_chunks = {}

    def front_conv(c):
        r0 = c * CHUNK
        conv = cw_ref[0:1, :] * ext_scr[pl.ds(r0 + pad - 3, CHUNK), :]
        for j in range(1, GDN_CONV):
            conv = conv + cw_ref[j:j + 1, :] * ext_scr[pl.ds(r0 + pad - 3 + j, CHUNK), :]
        qkv_chunks[c] = _silu(conv)
        y = scw_ref[0:1, :] * uext_scr[pl.ds(r0 + pad - 2, CHUNK), :]
        for j in range(1, SC_CONV):
            y = y + scw_ref[j:j + 1, :] * uext_scr[pl.ds(r0 + pad - 2 + j, CHUNK), :]
        pc_scr[wr, rsl(c), :] = jnp.concatenate(
            [h_ref[rsl(c), GDN_W:2 * GDN_W], h_ref[rsl(c), COL_SC:COL_SC + SC_W] * y], axis=1)
        pf_scr[wr, 1, rsl(c), :] = _silu(h_ref[rsl(c), COL_Z:COL_Z + GDN_W])

    def front_gates(c):
        qkv = qkv_chunks.pop(c)
        q = qkv[:, 0:GDN_W]
        k = qkv[:, GDN_W:2 * GDN_W]
        v = qkv[:, 2 * GDN_W:3 * GDN_W]
        qn = q * lax.rsqrt(head_sum(q * q) + RMS_EPS) * HEAD_DIM ** -0.5
        kn = k * lax.rsqrt(head_sum(k * k) + RMS_EPS)
        ba = h_ref[rsl(c), COL_BA:COL_BA + LANE]
        beta_blk = jax.nn.sigmoid(ba)
        sp_in = ba + gp_ref[1:2, :]
        softplus = jnp.maximum(sp_in, 0.0) + jnp.log(1.0 + jnp.exp(-jnp.abs(sp_in)))
        g_blk = jnp.where(is_a, -jnp.exp(gp_ref[0:1, :]) * softplus, 0.0)
        gam3 = jnp.dot(ltri, jnp.concatenate(_split3(g_blk), axis=1), preferred_element_type=F32)
        gam_blk = gam3[:, 0:LANE] + gam3[:, LANE:2 * LANE] + gam3[:, 2 * LANE:3 * LANE]
        gam_c = jnp.dot(jnp.concatenate(_split3(gam_blk), axis=1), exp_a3, preferred_element_type=F32)
        g_exp = jnp.exp(gam_c)
        k_fac = jnp.exp(gam_c[CHUNK - 1:CHUNK, :] - gam_c)
        beta = _sel_dot2(beta_blk, exp_b2)
        kbeta = kn * beta
        for j, val in enumerate((qn, kn, kbeta, qn * g_exp, kbeta * g_exp, v * beta, kn * k_fac)):
            pb_scr[wr, j, rsl(c), :] = val.astype(BF16)
        pf_scr[wr, 0, rsl(c), :] = gam_c

    pieces = [functools.partial(f, c) for c in range(n_chunks) for f in (front_conv, front_gates)]

    def emit(n=1):
        for _ in range(n):
            if pieces:
                pieces.pop(0)()

    a_b, t_inv, qd_b = {}, {}, {}
    for c, p in units:
        gc = gam[rsl(c), psl(p)]
        gr = jnp.sum(jnp.where(m_diag, gc, 0.0), axis=0, keepdims=True)
        dec = jnp.exp(jnp.where(m_incl, gc - gr, NEG_BIG))
        kq = lax.dot_general(jnp.concatenate([qn_b[rsl(c), psl(p)], kbeta_b[rsl(c), psl(p)]], axis=0),
                             bd(kn_b[rsl(c), psl(p)]), (((1,), (1,)), ((), ())), preferred_element_type=F32)
        qd_b[c, p] = (kq[0:CHUNK] * dec).astype(BF16)
        a_mat = jnp.where(m_strict, dec * kq[CHUNK:2 * CHUNK], 0.0)
        a_b[c, p] = a_mat.astype(BF16)
        t_inv[c, p] = eye - jnp.where(m_base, a_mat, 0.0)
    emit()
    zero_b = jnp.zeros((CHUNK, LANE), BF16)
    for lm in lvl_masks:
        w_lvl = {u_: jnp.dot(jnp.where(lm, a_b[u_], zero_b), bd(t_inv[u_].astype(BF16)),
                             preferred_element_type=F32) for u_ in units}
        for u_ in units:
            t_inv[u_] = t_inv[u_] - jnp.dot(t_inv[u_].astype(BF16), bd(w_lvl[u_].astype(BF16)),
                                            preferred_element_type=F32)
        emit()
    u0, w_b, kw_b, q_add, x_sols = {}, {}, {}, {}, {}
    for c, p in units:
        rhs = jnp.concatenate([bd(vb_b[rsl(c), psl(p)]), bd(kb_b[rsl(c), psl(p)])], axis=1)
        x_sols[c, p] = jnp.dot(t_inv[c, p].astype(BF16), rhs, preferred_element_type=F32)
    emit()
    for c, p in units:
        x_sol = x_sols[c, p]
        u0[c, p] = x_sol[:, 0:LANE]
        w_b[c, p] = x_sol[:, LANE:2 * LANE].astype(BF16)
        ktx = lax.dot_general(kt_b[rsl(c), psl(p)], x_sol.astype(BF16), (((0,), (0,)), ((), ())),
                              preferred_element_type=F32)
        q_add[c, p] = jnp.where(m_bd, ktx[:, 0:LANE], 0.0)
        kw_b[c, p] = jnp.where(m_bd, ktx[:, LANE:2 * LANE], 0.0).astype(BF16)
    emit()

    s_start = {}
    s_cur = [s_scr[p] for p in range(n_pairs)]
    for c in range(n_chunks):
        for p in range(n_pairs):
            s_bf = s_cur[p].astype(BF16)
            s_start[c, p] = s_bf
            e_last = jnp.exp(gam[(c + 1) * CHUNK - 1:(c + 1) * CHUNK, psl(p)])
            s_cur[p] = s_cur[p] * e_last - jnp.dot(kw_b[c, p], s_bf, preferred_element_type=F32) + q_add[c, p]
    for p in range(n_pairs):
        s_scr[p] = s_cur[p]

    wqs = {u_: jnp.dot(jnp.concatenate([w_b[u_], qg_b[rsl(u_[0]), psl(u_[1])]], axis=0), s_start[u_],
                       preferred_element_type=F32) for u_ in units}
    emit(len(pieces))
    for c, p in units:
        u = u0[c, p] - wqs[c, p][0:CHUNK]
        o_scr[rsl(c), psl(p)] = wqs[c, p][CHUNK:2 * CHUNK] + jnp.dot(qd_b[c, p], bd(u.astype(BF16)),
                                                                      preferred_element_type=F32)
    o = o_scr[...]
    o_ms = head_sum(o * o) * (1.0 / HEAD_DIM)
    ho_ref[:, 0:GDN_W] = o * lax.rsqrt(o_ms + RMS_EPS) * ng_ref[...] * z_gate

    cnew_ref[0] = ext_scr[pl.ds(lt + pad - 3, 3), :]
    ext_scr[0:pad, :] = ext_scr[lt:lt + pad, :]
    scnew_ref[0] = uext_scr[pl.ds(lt + pad - 2, 2), :]
    uext_scr[0:pad, :] = uext_scr[lt:lt + pad, :]

    @pl.when(i == pl.num_programs(1) - 1)
    def _():
        for p in range(n_pairs):
            snew_ref[0, 2 * p] = s_scr[p, 0:HEAD_DIM, 0:HEAD_DIM]
            snew_ref[0, 2 * p + 1] = s_scr[p, HEAD_DIM:LANE, HEAD_DIM:LANE]


def _gdn_sconv(h_all, row_off, nb, seq, cbuf, s0, sbuf, cw, scw, gp, ng):
    lt = _pick_tile(seq, (256, 128, 64))
    nt = seq // lt
    off = row_off // lt
    assert row_off % lt == 0
    tok_in = lambda b, i: (off + b * nt + jnp.minimum(i, nt - 1), 0)
    tok_out = lambda b, i: (off + b * nt + jnp.maximum(i - 1, 0), 0)
    return pl.pallas_call(
        functools.partial(_gdn_pipe_kernel, lt),
        grid=(nb, nt + 1),
        in_specs=[
            pl.BlockSpec((lt, P_PAD), tok_in),
            pl.BlockSpec((1, GDN_CONV - 1, 3 * GDN_W), lambda b, i: (b, 0, 0)),
            pl.BlockSpec((1, GDN_HEADS, HEAD_DIM, HEAD_DIM), lambda b, i: (b, 0, 0, 0)),
            pl.BlockSpec((1, SC_CONV - 1, SC_W), lambda b, i: (b, 0, 0)),
            pl.BlockSpec((GDN_CONV, 3 * GDN_W), lambda b, i: (0, 0)),
            pl.BlockSpec((SC_CONV, SC_W), lambda b, i: (0, 0)),
            pl.BlockSpec((SUBLANE, LANE), lambda b, i: (0, 0)),
            pl.BlockSpec((1, GDN_W), lambda b, i: (0, 0)),
        ],
        out_specs=[
            pl.BlockSpec((lt, MIX_W), tok_out),
            pl.BlockSpec((1, GDN_CONV - 1, 3 * GDN_W), lambda b, i: (b, 0, 0)),
            pl.BlockSpec((1, GDN_HEADS, HEAD_DIM, HEAD_DIM), lambda b, i: (b, 0, 0, 0)),
            pl.BlockSpec((1, SC_CONV - 1, SC_W), lambda b, i: (b, 0, 0)),
        ],
        out_shape=[
            jax.ShapeDtypeStruct(h_all.shape, F32),
            jax.ShapeDtypeStruct((nb, GDN_CONV - 1, 3 * GDN_W), F32),
            jax.ShapeDtypeStruct((nb, GDN_HEADS, HEAD_DIM, HEAD_DIM), F32),
            jax.ShapeDtypeStruct((nb, SC_CONV - 1, SC_W), F32),
        ],
        scratch_shapes=[
            pltpu.VMEM((lt + SUBLANE, 3 * GDN_W), F32),
            pltpu.VMEM((lt + SUBLANE, SC_W), F32),
            pltpu.VMEM((GDN_HEADS // 2, LANE, LANE), F32),
            pltpu.VMEM((lt, GDN_W), F32),
            pltpu.VMEM((2, 7, lt, GDN_W), BF16),
            pltpu.VMEM((2, 2, lt, GDN_W), F32),
            pltpu.VMEM((2, lt, MIX_W - GDN_W), F32),
        ],
        input_output_aliases={0: 0},
        compiler_params=_cparams("arbitrary", "arbitrary"),
        name="gdn_sconv",
    )(h_all, cbuf, s0, sbuf, cw, scw, gp, ng)


def _rope(x, cos, sin_signed):
    w = x.shape[1]
    nrep = w // LANE
    if nrep > 1:
        cos = jnp.concatenate([cos] * nrep, axis=1)
        sin_signed = jnp.concatenate([sin_signed] * nrep, axis=1)
    lane = lax.broadcasted_iota(jnp.int32, x.shape, 1) & (HEAD_DIM - 1)
    half = HEAD_DIM // 2
    swapped = jnp.where(lane < half, pltpu.roll(x, w - half, 1), pltpu.roll(x, half, 1))
    return x * cos + swapped * sin_signed


def _swa_kernel(lt, hist_valid, h_ref, hk_ref, hv_ref, cos_ref, sin_ref, sink_ref,
                o_ref, kr_ref, kall_scr, vall_scr):
    i = pl.program_id(1)

    @pl.when(i == 0)
    def _():
        kall_scr[0:WINDOW, :] = hk_ref[0]
        vall_scr[0:WINDOW, :] = hv_ref[0]

    cos = cos_ref[...]
    sin = sin_ref[...]
    q = _rope(h_ref[:, 0:SWA_W], cos, sin)
    k = _rope(h_ref[:, SWA_W:SWA_W + SWA_KV_W], cos, sin)
    kr_ref[...] = k
    kall_scr[WINDOW:WINDOW + lt, :] = k
    vall_scr[WINDOW:WINDOW + lt, :] = h_ref[:, SWA_W + SWA_KV_W:SWA_W + 2 * SWA_KV_W]
    q_b = q.astype(BF16)
    kall = kall_scr[...].astype(BF16)
    vall = vall_scr[...].astype(BF16)
    kall_sw = pltpu.roll(kall_scr[...], HEAD_DIM, 1).astype(BF16)
    vall_sw = pltpu.roll(vall_scr[...], HEAD_DIM, 1).astype(BF16)

    band = WINDOW + CHUNK
    half = 2 * LANE
    n_chunks = lt // CHUNK
    n_pairs = SWA_Q_HEADS // 2
    left = lax.broadcasted_iota(jnp.int32, (band, LANE), 1) < HEAD_DIM
    zero_band = jnp.zeros((band, LANE), BF16)
    zero_pad = jnp.zeros((half - band, LANE), BF16)
    k_src = ((kall, kall_sw), (kall, kall), (kall_sw, kall))
    v_src = ((vall, vall_sw), (vall, vall), (vall_sw, vall))

    def pair_bd(top, bot, ks):
        return jnp.concatenate([jnp.where(left, top[ks], zero_band), zero_pad,
                                jnp.where(left, zero_band, bot[ks]), zero_pad], axis=0)

    kb = lax.broadcasted_iota(jnp.int32, (1, half), 1)
    hist_ok = jnp.logical_or(hist_valid, i > 0)
    fill = [jnp.where(kb == band, sink_ref[0:1, hq:hq + 1], NEG_BIG) for hq in range(SWA_Q_HEADS)]
    r512 = lax.broadcasted_iota(jnp.int32, (2 * half, LANE), 0)
    l512 = lax.broadcasted_iota(jnp.int32, (2 * half, LANE), 1)
    ones_bd = jnp.where((r512 < half) == (l512 < HEAD_DIM), 1.0, 0.0).astype(BF16)
    units = [(c, p) for c in range(n_chunks) for p in range(n_pairs)]
    scores = {}
    for c, p in units:
        ks = slice(c * CHUNK, c * CHUNK + band)
        kbd = pair_bd(k_src[p][0], k_src[p][1], ks)
        scores[c, p] = lax.dot_general(q_b[c * CHUNK:(c + 1) * CHUNK, p * LANE:(p + 1) * LANE], kbd,
                                       (((1,), (1,)), ((), ())), preferred_element_type=F32)
    probs = {}
    for c, p in units:
        valid = (kb < band) & (hist_ok | ((kb >> CHUNK_SHIFT) + c >= WINDOW // CHUNK))
        halves = []
        for e_ in range(2):
            s = jnp.where(valid, scores[c, p][:, e_ * half:(e_ + 1) * half] * HEAD_DIM ** -0.5, fill[2 * p + e_])
            halves.append(jnp.exp(s - jnp.max(s, -1, keepdims=True)).astype(BF16))
        probs[c, p] = jnp.concatenate(halves, axis=1)
    for c, p in units:
        ks = slice(c * CHUNK, c * CHUNK + band)
        rhs = jnp.concatenate([pair_bd(v_src[p][0], v_src[p][1], ks), ones_bd], axis=1)
        od = jnp.dot(probs[c, p], rhs, preferred_element_type=F32)
        o_ref[c * CHUNK:(c + 1) * CHUNK, p * LANE:(p + 1) * LANE] = od[:, 0:LANE] / od[:, LANE:2 * LANE]

    if lt >= WINDOW:
        kall_scr[0:WINDOW, :] = kall_scr[lt:lt + WINDOW, :]
        vall_scr[0:WINDOW, :] = vall_scr[lt:lt + WINDOW, :]


def _swa(h_all, row_off, nb, seq, hist_k, hist_v, hist_valid, cos, sin, sinks):
    lt = _pick_tile(seq, (256, 128, 64))
    nt = seq // lt
    assert nt == 1 or lt >= WINDOW
    off = row_off // lt
    assert row_off % lt == 0
    swa_blk_w = P_PAD - COL_SWA
    return pl.pallas_call(
        functools.partial(_swa_kernel, lt, hist_valid),
        grid=(nb, nt),
        in_specs=[
            pl.BlockSpec((lt, swa_blk_w), lambda b, i: (off + b * nt + i, COL_SWA // swa_blk_w)),
            pl.BlockSpec((1, WINDOW, SWA_KV_W), lambda b, i: (b, 0, 0)),
            pl.BlockSpec((1, WINDOW, SWA_KV_W), lambda b, i: (b, 0, 0)),
            pl.BlockSpec((lt, LANE), lambda b, i: (i, 0)),
            pl.BlockSpec((lt, LANE), lambda b, i: (i, 0)),
            pl.BlockSpec((1, LANE), lambda b, i: (0, 0)),
        ],
        out_specs=[
            pl.BlockSpec((lt, SWA_W), lambda b, i: (off + b * nt + i, 1)),
            pl.BlockSpec((lt, SWA_KV_W), lambda b, i: (b * nt + i, 0)),
        ],
        out_shape=[
            jax.ShapeDtypeStruct(h_all.shape, F32),
            jax.ShapeDtypeStruct((nb * seq, SWA_KV_W), F32),
        ],
        scratch_shapes=[
            pltpu.VMEM((lt + WINDOW, SWA_KV_W), F32),
            pltpu.VMEM((lt + WINDOW, SWA_KV_W), F32),
        ],
        input_output_aliases={0: 0},
        compiler_params=_cparams("arbitrary", "arbitrary"),
        name="swa",
    )(h_all, hist_k, hist_v, cos, sin, sinks)


def _memattn_kernel(alpha, x_ref, xb_in_ref, gate_in_ref, ids_in_ref, mix_ref, wo_ref, g1_ref, b1_ref,
                    mk_ref, mv_ref, wcq_ref, wco_ref, g_ref, b_ref, wr_ref, br_ref,
                    x_out_ref, xb_ref, gate_ref, ids_ref, cnt_ref):
    del xb_in_ref, gate_in_ref, ids_in_ref
    x = _ln(alpha * x_ref[...] + _dot(mix_ref[...], wo_ref[...]), g1_ref[...], b1_ref[...])
    q = _dot(x, wcq_ref[...])
    mk = mk_ref[0].astype(BF16)
    mv = mv_ref[0].astype(BF16)
    n_mem = mk.shape[0]
    q_b = q.astype(BF16)
    left = lax.broadcasted_iota(jnp.int32, (n_mem, LANE), 1) < HEAD_DIM
    zero_kv = jnp.zeros((n_mem, LANE), BF16)

    def pair_bd(x):
        return jnp.concatenate([jnp.where(left, x, zero_kv), jnp.where(left, zero_kv, x)], axis=0)

    pairs = range(MEM_HEADS // 2)
    scores = [lax.dot_general(q_b[:, p * LANE:(p + 1) * LANE], pair_bd(mk[:, p * LANE:(p + 1) * LANE]),
                              (((1,), (1,)), ((), ())), preferred_element_type=F32) for p in pairs]
    probs = []
    for p in pairs:
        halves = []
        for e_ in range(2):
            s = scores[p][:, e_ * n_mem:(e_ + 1) * n_mem] * HEAD_DIM ** -0.5
            halves.append(jnp.exp(s - jnp.max(s, -1, keepdims=True)).astype(BF16))
        probs.append(jnp.concatenate(halves, axis=1))
    r2 = lax.broadcasted_iota(jnp.int32, (2 * n_mem, LANE), 0)
    l2 = lax.broadcasted_iota(jnp.int32, (2 * n_mem, LANE), 1)
    ones_bd = jnp.where((r2 < n_mem) == (l2 < HEAD_DIM), 1.0, 0.0).astype(BF16)
    outs = []
    for p in pairs:
        rhs = jnp.concatenate([pair_bd(mv[:, p * LANE:(p + 1) * LANE]), ones_bd], axis=1)
        od = jnp.dot(probs[p], rhs, preferred_element_type=F32)
        outs.append(od[:, 0:LANE] / od[:, LANE:2 * LANE])
    o = _dot(jnp.concatenate(outs, axis=1), wco_ref[...])
    x2 = _ln(alpha * x + o, g_ref[...], b_ref[...])
    x_out_ref[...] = x2
    packed = _pack_halves(x2)
    for p in range(xb_ref.shape[0]):
        xb_ref[p] = packed[:, p * LANE:(p + 1) * LANE]

    logits = _dot(x2, wr_ref[...]) + br_ref[...]
    lane = lax.broadcasted_iota(jnp.int32, logits.shape, 1).astype(F32)
    vals, idxs = [], []
    for _ in range(TOP_K):
        m = jnp.max(logits, -1, keepdims=True)
        idx = jnp.min(jnp.where(logits == m, lane, float(LANE)), -1, keepdims=True)
        vals.append(m)
        idxs.append(idx)
        logits = jnp.where(lane == idx, 2 * NEG_BIG, logits)
    es = [jnp.exp(val - vals[0]) for val in vals]
    den = es[0] + es[1] + es[2] + es[3]
    gate = jnp.zeros(logits.shape, F32)
    ids = jnp.zeros(logits.shape, F32)
    for kk in range(TOP_K):
        gate = jnp.where(lane == kk, es[kk] / den, gate)
        ids = jnp.where(lane == kk, idxs[kk], ids)
    gate_ref[...] = gate
    ids_ref[...] = ids.astype(jnp.int32)
    picked = jnp.sum(jnp.where(logits == 2 * NEG_BIG, 1.0, 0.0), 0, keepdims=True)
    cnt_ref[...] = jnp.broadcast_to(picked, cnt_ref.shape)


def _memattn_router(x_all, xb_all, gate_all, ids_all, h_all, w_o, g1, b1, row_off, nb, seq, mk, mv, wcq, wco,
                    g, b, wr, br, alpha):
    t, d = x_all.shape
    tl = _pick_tile(seq, (512, 256, 128, 64))
    nt = seq // tl
    off = row_off // tl
    assert row_off % tl == 0
    n_mem = mk.shape[1]
    tok = lambda bb, i: (off + bb * nt + i, 0)
    const = lambda bb, i: (0, 0)
    return pl.pallas_call(
        functools.partial(_memattn_kernel, alpha),
        grid=(nb, nt),
        in_specs=[
            pl.BlockSpec((tl, d), tok),
            pl.BlockSpec(memory_space=pl.ANY),
            pl.BlockSpec(memory_space=pl.ANY),
            pl.BlockSpec(memory_space=pl.ANY),
            pl.BlockSpec((tl, MIX_W), tok),
            pl.BlockSpec((MIX_W, d), const),
            pl.BlockSpec((1, d), const),
            pl.BlockSpec((1, d), const),
            pl.BlockSpec((1, n_mem, MEM_W), lambda bb, i: (bb, 0, 0)),
            pl.BlockSpec((1, n_mem, MEM_W), lambda bb, i: (bb, 0, 0)),
            pl.BlockSpec((d, MEM_W), const),
            pl.BlockSpec((MEM_W, d), const),
            pl.BlockSpec((1, d), const),
            pl.BlockSpec((1, d), const),
            pl.BlockSpec((d, LANE), const),
            pl.BlockSpec((1, LANE), const),
        ],
        out_specs=[
            pl.BlockSpec((tl, d), tok),
            pl.BlockSpec((d // (2 * LANE), tl, LANE), lambda bb, i: (0, off + bb * nt + i, 0)),
            pl.BlockSpec((tl, LANE), tok),
            pl.BlockSpec((tl, LANE), tok),
            pl.BlockSpec((SUBLANE, LANE), lambda bb, i: (bb * nt + i, 0)),
        ],
        out_shape=[
            jax.ShapeDtypeStruct((t, d), F32),
            jax.ShapeDtypeStruct((d // (2 * LANE), t, LANE), jnp.uint32),
            jax.ShapeDtypeStruct((t, LANE), F32),
            jax.ShapeDtypeStruct((t, LANE), jnp.int32),
            jax.ShapeDtypeStruct((nb * nt * SUBLANE, LANE), F32),
        ],
        input_output_aliases={0: 0, 1: 1, 2: 2, 3: 3},
        compiler_params=_cparams("arbitrary", "arbitrary"),
        name="memattn_router",
    )(x_all, xb_all, gate_all, ids_all, h_all, w_o, g1.reshape(1, d), b1.reshape(1, d), mk, mv, wcq, wco,
      g.reshape(1, d), b.reshape(1, d), wr, br)


def _expert_kernel(d_ff, be_ref, bv_ref, nu_ref, xs_ref, w1_ref, b1_ref, w2_ref, b2_ref, ys_ref, w1b_scr, w2b_scr):
    i = pl.program_id(0)
    prev = be_ref[jnp.maximum(i - 1, 0)]
    new_expert = (i == 0) | (be_ref[i] != prev)

    @pl.when(new_expert & (i < nu_ref[0]))
    def _():
        w1b_scr[...] = w1_ref[0, 0].astype(BF16)
        w2b_scr[...] = w2_ref[0, 0].astype(BF16)

    @pl.when(i < nu_ref[0])
    def _():
        row = lax.broadcasted_iota(jnp.int32, (xs_ref.shape[1], 1), 0)
        words = jnp.concatenate([xs_ref[p] for p in range(xs_ref.shape[0])], axis=1)
        xs = _unpack_halves(jnp.where(row < bv_ref[i], words, jnp.zeros_like(words))).astype(BF16)
        hh = jnp.dot(xs, w1b_scr[...], preferred_element_type=F32) + b1_ref[0, 0]
        glu = jnp.minimum(hh[:, 0:d_ff], SWIGLU_LIMIT)
        lin = jnp.clip(hh[:, d_ff:2 * d_ff], -SWIGLU_LIMIT, SWIGLU_LIMIT)
        act = glu * jax.nn.sigmoid(SWIGLU_ALPHA * glu) * (lin + 1.0)
        y = jnp.dot(act.astype(BF16), w2b_scr[...], preferred_element_type=F32) + b2_ref[0, 0]
        packed = _pack_halves(y)
        for p in range(ys_ref.shape[0]):
            ys_ref[p] = packed[:, p * LANE:(p + 1) * LANE]


def _experts(xs, blk_e, blk_valid, n_used, layer, w1, b1, w2, b2):
    n_planes, n_rows, _ = xs.shape
    n_l, n_e, d, two_ff = w1.shape
    d_ff = two_ff // 2
    n_blocks = n_rows // MOE_BM
    row_blk = lambda i, be, bv, nu: (0, jnp.minimum(i, nu[0] - 1), 0)
    exp_blk = lambda i, be, bv, nu: (layer, be[jnp.minimum(i, nu[0] - 1)], 0, 0)
    return pl.pallas_call(
        functools.partial(_expert_kernel, d_ff),
        grid_spec=pltpu.PrefetchScalarGridSpec(
            num_scalar_prefetch=3,
            grid=(n_blocks,),
            in_specs=[
                pl.BlockSpec((n_planes, MOE_BM, LANE), row_blk),
                pl.BlockSpec((1, 1, d, two_ff), exp_blk),
                pl.BlockSpec((1, 1, 1, two_ff), exp_blk),
                pl.BlockSpec((1, 1, d_ff, d), exp_blk),
                pl.BlockSpec((1, 1, 1, d), exp_blk),
            ],
            out_specs=pl.BlockSpec((n_planes, MOE_BM, LANE), row_blk),
            scratch_shapes=[pltpu.VMEM((d, two_ff), BF16), pltpu.VMEM((d_ff, d), BF16)],
        ),
        out_shape=jax.ShapeDtypeStruct((n_planes, n_rows, LANE), jnp.uint32),
        compiler_params=_cparams("arbitrary"),
        name="moe_experts",
    )(blk_e, blk_valid, n_used, xs, w1, b1.reshape(n_l, n_e, 1, two_ff), w2, b2.reshape(n_l, n_e, 1, d))


def _sc_mesh():
    return plsc.VectorSubcoreMesh(core_axis_name="c", subcore_axis_name="s")


def _sc_dispatch(xw, dest_km, n_rows):
    n_planes, t, _ = xw.shape
    n_src = n_planes * t
    nw = n_src // SC_WINDOW
    assert t % SC_WINDOW == 0

    @functools.partial(pl.kernel, out_type=jax.ShapeDtypeStruct((n_planes * n_rows, LANE), xw.dtype),
                       mesh=_sc_mesh(), scratch_types=[], name="moe_dispatch_sc")
    def scatter_rows(x_hbm, i0_hbm, i1_hbm, i2_hbm, i3_hbm, o_hbm):
        def body(x_vmem, i0, i1, i2, i3):
            for idx in (i0, i1, i2, i3):
                pltpu.sync_copy(x_vmem, o_hbm.at[idx.at[0]])

        pltpu.emit_pipeline(
            body,
            grid=(nw,),
            in_specs=[pl.BlockSpec((SC_WINDOW, LANE), lambda i: (i, 0))]
            + [pl.BlockSpec((1, SC_WINDOW), functools.partial(lambda k, i: (0, k * nw + i), k))
               for k in range(TOP_K)],
            out_specs=[],
            core_axis_name=("c", "s"),
            dimension_semantics=(pltpu.PARALLEL,),
        )(x_hbm, i0_hbm, i1_hbm, i2_hbm, i3_hbm)

    plane_off = (jnp.arange(n_planes, dtype=jnp.int32) * n_rows)[None, :, None]
    idx = (dest_km.reshape(TOP_K, 1, t) + plane_off).reshape(1, TOP_K * n_src)
    return scatter_rows(xw.reshape(n_src, LANE), idx, idx, idx, idx).reshape(n_planes, n_rows, LANE)


def _sc_gather(yw, idx_flat):
    n_planes, n_rows, _ = yw.shape
    n = idx_flat.shape[0]
    assert n % SC_WINDOW == 0

    @functools.partial(pl.kernel, out_type=jax.ShapeDtypeStruct((n_planes * n, LANE), yw.dtype),
                       mesh=_sc_mesh(), scratch_types=[], name="moe_gather_sc")
    def gather_rows(y_hbm, i_hbm, o_hbm):
        def body(i_vmem, o_vmem):
            pltpu.sync_copy(y_hbm.at[i_vmem.at[0]], o_vmem)

        pltpu.emit_pipeline(
            body,
            grid=(n_planes * n // SC_WINDOW,),
            in_specs=[pl.BlockSpec((1, SC_WINDOW), lambda i: (0, i))],
            out_specs=[pl.BlockSpec((SC_WINDOW, LANE), lambda i: (i, 0))],
            core_axis_name=("c", "s"),
            dimension_semantics=(pltpu.PARALLEL,),
        )(i_hbm, o_hbm)

    plane_off = (jnp.arange(n_planes, dtype=jnp.int32) * n_rows)[:, None]
    idx = (idx_flat[None, :] + plane_off).reshape(1, n_planes * n)
    return gather_rows(yw.reshape(n_planes * n_rows, LANE), idx).reshape(n_planes, n, LANE)


def _combine_kernel(alpha, y0_ref, y1_ref, y2_ref, y3_ref, gate_ref, x_ref, g_ref, b_ref, o_ref):
    gate = gate_ref[...]
    acc = alpha * x_ref[...]
    for k, y_ref in enumerate((y0_ref, y1_ref, y2_ref, y3_ref)):
        words = jnp.concatenate([y_ref[p] for p in range(y_ref.shape[0])], axis=1)
        acc = acc + gate[:, k:k + 1] * _unpack_halves(words)
    o_ref[...] = _ln(acc, g_ref[...], b_ref[...])


def _combine(yg, gate_all, x_all, g, b, alpha):
    t, d = x_all.shape
    tm = _pick_tile(t, (512, 256, 128, 64))
    nt = t // tm
    y_spec = lambda k: pl.BlockSpec((yg.shape[0], tm, LANE), lambda i: (0, k * nt + i, 0))
    return pl.pallas_call(
        functools.partial(_combine_kernel, alpha),
        grid=(nt,),
        in_specs=[
            y_spec(0), y_spec(1), y_spec(2), y_spec(3),
            pl.BlockSpec((tm, LANE), lambda i: (i, 0)),
            pl.BlockSpec((tm, d), lambda i: (i, 0)),
            pl.BlockSpec((1, d), lambda i: (0, 0)),
            pl.BlockSpec((1, d), lambda i: (0, 0)),
        ],
        out_specs=pl.BlockSpec((tm, d), lambda i: (i, 0)),
        out_shape=jax.ShapeDtypeStruct((t, d), F32),
        input_output_aliases={5: 0},
        compiler_params=_cparams("arbitrary"),
        name="moe_combine_ln3",
    )(yg, yg, yg, yg, gate_all, x_all, g.reshape(1, d), b.reshape(1, d))


def _dest_kernel(ids_ref, start_ref, dest_ref, run_scr):
    i = pl.program_id(0)

    @pl.when(i == 0)
    def _():
        run_scr[...] = jnp.zeros_like(run_scr)

    ids = ids_ref[...].astype(F32)
    tm = ids.shape[0]
    lane = lax.broadcasted_iota(jnp.int32, ids.shape, 1).astype(F32)
    one_hots = [jnp.where(lane == ids[:, k:k + 1], 1.0, 0.0) for k in range(TOP_K)]
    multi_hot = one_hots[0] + one_hots[1] + one_hots[2] + one_hots[3]
    r = lax.broadcasted_iota(jnp.int32, (tm, tm), 0)
    c = lax.broadcasted_iota(jnp.int32, (tm, tm), 1)
    before = jnp.dot(jnp.where(c < r, 1.0, 0.0).astype(BF16), multi_hot.astype(BF16),
                     preferred_element_type=F32) + (run_scr[0:1, :] + start_ref[...])
    dest = jnp.zeros(ids.shape, F32)
    for k in range(TOP_K):
        dest = jnp.where(lane == k, jnp.sum(one_hots[k] * before, -1, keepdims=True), dest)
    dest_ref[...] = dest.astype(jnp.int32)
    run_scr[0:1, :] = run_scr[0:1, :] + jnp.sum(multi_hot, 0, keepdims=True)


def _route(ids_all, counts, n_experts):
    t = ids_all.shape[0]
    tk = t * TOP_K
    n_blocks = -(-tk // MOE_BM) + n_experts
    n_rows = n_blocks * MOE_BM
    assert n_rows < 2 ** 24
    padded = (counts + MOE_BM - 1) // MOE_BM * MOE_BM
    pad_end = jnp.cumsum(padded)
    pad_start = pad_end - padded
    start_row = jnp.zeros((1, LANE), F32).at[0, 0:n_experts].set(pad_start.astype(F32))
    tm = _pick_tile(t, (512, 256, 128, 64))
    tok = pl.BlockSpec((tm, LANE), lambda i: (i, 0))
    dest = pl.pallas_call(
        _dest_kernel,
        grid=(t // tm,),
        in_specs=[tok, pl.BlockSpec((1, LANE), lambda i: (0, 0))],
        out_specs=tok,
        out_shape=jax.ShapeDtypeStruct((t, LANE), jnp.int32),
        scratch_shapes=[pltpu.VMEM((SUBLANE, LANE), F32)],
        compiler_params=_cparams("arbitrary"),
        name="moe_dest",
    )(ids_all, start_row)
    dest_km = dest[:, 0:TOP_K].T.reshape(-1)
    blk = jnp.arange(n_blocks, dtype=jnp.int32)
    blk_e = jnp.minimum(jnp.sum((blk[:, None] * MOE_BM >= pad_end[None, :]).astype(jnp.int32), axis=1),
                        n_experts - 1)
    blk_valid = jnp.clip(counts[blk_e] - (blk * MOE_BM - pad_start[blk_e]), 0, MOE_BM).astype(jnp.int32)
    n_used = (pad_end[-1] // MOE_BM).astype(jnp.int32).reshape(1)
    return dest_km, n_rows, blk_e, blk_valid, n_used


def _rope_tables(pos):
    half = HEAD_DIM // 2
    inv = ROPE_THETA ** (-jnp.arange(half, dtype=F32) / half)
    ang = pos.astype(F32)[:, None] * inv[None, :]
    cos = jnp.cos(ang)
    sin = jnp.sin(ang)
    cos = jnp.concatenate([cos, cos, cos, cos], axis=1)
    sin = jnp.concatenate([-sin, sin, -sin, sin], axis=1)
    return cos, sin


def _permute_w_in(w):
    d = w.shape[0]
    o_b = 4 * GDN_W
    o_q = o_b + 2 * GDN_HEADS
    o_sc = o_q + SWA_W + 2 * SWA_KV_W
    ba = jnp.concatenate([w[:, o_b:o_q], jnp.zeros((d, LANE - 2 * GDN_HEADS), w.dtype)], axis=1)
    return jnp.concatenate([w[:, 0:o_b], w[:, o_sc:], w[:, o_q:o_sc], ba], axis=1)


def kernel(x_prompt, x_sample, state_gdn_S, state_gdn_conv, cache_swa_k, cache_swa_v, state_sconv, cache_mem_k, cache_mem_v, mem_prompt, ln_in_g, ln_in_b, w_in, gdn_conv_w, gdn_a_log, gdn_dt_bias, gdn_norm_g, swa_sinks, sc_conv_w, w_o, ln1_g, ln1_b, w_cq, w_mk, w_mv, w_co, ln2_g, ln2_b, w_router, b_router, w1, b1, w2, b2, ln3_g, ln3_b):
    bp, lp, d = x_prompt.shape
    bs, ls, _ = x_sample.shape
    depth = w_in.shape[0]
    n_experts = w_router.shape[2]
    n_mem = mem_prompt.shape[1]
    tp, ts = bp * lp, bs * ls
    t_all = tp + ts
    alpha = (2 * depth) ** 0.25
    assert cache_swa_k.shape[2] == WINDOW and d == MIX_W

    x_all = _ln_in(x_prompt.reshape(tp, d), x_sample.reshape(ts, d), ln_in_g, ln_in_b)
    xb_all = jnp.zeros((d // (2 * LANE), t_all, LANE), jnp.uint32)
    gate_all = jnp.zeros((t_all, LANE), F32)
    ids_all = jnp.zeros((t_all, LANE), jnp.int32)

    cos_p, sin_p = _rope_tables(jnp.arange(lp))
    cos_s, sin_s = _rope_tables(PAST_LEN + jnp.arange(ls))
    mem_flat = mem_prompt.reshape(bp * n_mem, d)
    zeros_cbuf = jnp.zeros((bp, GDN_CONV - 1, 3 * GDN_W), F32)
    zeros_s = jnp.zeros((bp, GDN_HEADS, HEAD_DIM, HEAD_DIM), F32)
    zeros_sbuf = jnp.zeros((bp, SC_CONV - 1, SC_W), F32)
    zeros_hist = jnp.zeros((bp, WINDOW, SWA_KV_W), F32)

    outs = {k: [] for k in ("p_S", "p_conv", "p_k", "p_v", "p_sc", "p_mk", "p_mv",
                            "s_S", "s_conv", "s_k", "s_v", "s_sc")}
    for l in range(depth):
        w_in_l = _permute_w_in(w_in[l]).astype(BF16)
        gp = jnp.zeros((SUBLANE, LANE), F32)
        gp = gp.at[0, BA_A:BA_A + GDN_HEADS].set(gdn_a_log[l]).at[1, BA_A:BA_A + GDN_HEADS].set(gdn_dt_bias[l])
        ng = jnp.tile(gdn_norm_g[l].reshape(1, HEAD_DIM), (1, GDN_HEADS))
        sinks = jnp.zeros((1, LANE), F32).at[0, 0:SWA_Q_HEADS].set(swa_sinks[l])
        wr = jnp.concatenate([w_router[l], jnp.zeros((d, LANE - n_experts), F32)], axis=1).astype(BF16)
        br = jnp.concatenate([b_router[l], jnp.full((LANE - n_experts,), NEG_BIG, F32)]).reshape(1, LANE)

        mkv = _proj(mem_flat, jnp.concatenate([w_mk[l], w_mv[l]], axis=1).astype(BF16), "mem_kv")
        mk_p = mkv[:, 0:MEM_W].reshape(bp, n_mem, MEM_W)
        mv_p = mkv[:, MEM_W:2 * MEM_W].reshape(bp, n_mem, MEM_W)
        outs["p_mk"].append(mk_p.reshape(bp, n_mem, MEM_HEADS, HEAD_DIM))
        outs["p_mv"].append(mv_p.reshape(bp, n_mem, MEM_HEADS, HEAD_DIM))

        h_all = _proj(x_all, w_in_l, "w_in")

        v_p = h_all[0:tp, COL_SWA_V:COL_SWA_V + SWA_KV_W].reshape(bp, lp, SWA_KV_W)[:, lp - WINDOW:]
        v_s = h_all[tp:, COL_SWA_V:COL_SWA_V + SWA_KV_W]
        h_all, c_p, s_p, sc_p = _gdn_sconv(h_all, 0, bp, lp, zeros_cbuf, zeros_s, zeros_sbuf,
                                           gdn_conv_w[l], sc_conv_w[l], gp, ng)
        h_all, c_s, s_s, sc_s = _gdn_sconv(h_all, tp, bs, ls, state_gdn_conv[l], state_gdn_S[l], state_sconv[l],
                                           gdn_conv_w[l], sc_conv_w[l], gp, ng)
        h_all, kr_p = _swa(h_all, 0, bp, lp, zeros_hist, zeros_hist, False, cos_p, sin_p, sinks)
        h_all, kr_s = _swa(h_all, tp, bs, ls, cache_swa_k[l].reshape(bs, WINDOW, SWA_KV_W),
                           cache_swa_v[l].reshape(bs, WINDOW, SWA_KV_W), True, cos_s, sin_s, sinks)
        outs["p_S"].append(s_p); outs["p_conv"].append(c_p); outs["p_sc"].append(sc_p)
        outs["s_S"].append(s_s); outs["s_conv"].append(c_s); outs["s_sc"].append(sc_s)
        outs["p_k"].append(kr_p.reshape(bp, lp, SWA_KV_W)[:, lp - WINDOW:].reshape(bp, WINDOW, SWA_KV_HEADS, HEAD_DIM))
        outs["p_v"].append(v_p.reshape(bp, WINDOW, SWA_KV_HEADS, HEAD_DIM))
        outs["s_k"].append(kr_s.reshape(bs, ls, SWA_KV_HEADS, HEAD_DIM))
        outs["s_v"].append(v_s.reshape(bs, ls, SWA_KV_HEADS, HEAD_DIM))

        wo = w_o[l].astype(BF16)
        wcq = w_cq[l].astype(BF16)
        wco = w_co[l].astype(BF16)
        x_all, xb_all, gate_all, ids_all, cnt_p = _memattn_router(
            x_all, xb_all, gate_all, ids_all, h_all, wo, ln1_g[l], ln1_b[l], 0, bp, lp, mk_p, mv_p,
            wcq, wco, ln2_g[l], ln2_b[l], wr, br, alpha)
        x_all, xb_all, gate_all, ids_all, cnt_s = _memattn_router(
            x_all, xb_all, gate_all, ids_all, h_all, wo, ln1_g[l], ln1_b[l], tp, bs, ls,
            cache_mem_k[l].reshape(bs, n_mem, MEM_W), cache_mem_v[l].reshape(bs, n_mem, MEM_W),
            wcq, wco, ln2_g[l], ln2_b[l], wr, br, alpha)

        counts = sum(c_.reshape(-1, SUBLANE, LANE)[:, 0, 0:n_experts].sum(0) for c_ in (cnt_p, cnt_s))
        dest_km, n_rows, blk_e, blk_valid, n_used = _route(ids_all, counts.astype(jnp.int32), n_experts)
        xs = _sc_dispatch(xb_all, dest_km, n_rows)
        ys = _experts(xs, blk_e, blk_valid, n_used, l, w1, b1, w2, b2)
        yg = _sc_gather(ys, dest_km)
        x_all = _combine(yg, gate_all, x_all, ln3_g[l], ln3_b[l], alpha)

    st = lambda k: jnp.stack(outs[k])
    return (x_all[0:tp].reshape(bp, lp, d), x_all[tp:].reshape(bs, ls, d),
            st("p_S"), st("p_conv"), st("p_k"), st("p_v"), st("p_sc"), st("p_mk"), st("p_mv"),
            st("s_S"), st("s_conv"), st("s_k"), st("s_v"), st("s_sc"))
```

```python
import functools

import jax
import jax.numpy as jnp
from jax import lax
from jax.experimental import pallas as pl
from jax.experimental.pallas import tpu as pltpu
from jax.experimental.pallas import tpu_sc as plsc

F32 = jnp.float32
BF16 = jnp.bfloat16

LANE = 128
SUBLANE = 8
VMEM_LIMIT_BYTES = 56 * 1024 * 1024

CHUNK = 64
CHUNK_SHIFT = 6
HEAD_DIM = 64
GDN_HEADS = 6
GDN_W = GDN_HEADS * HEAD_DIM
GDN_CONV = 4
SWA_Q_HEADS = 6
SWA_KV_HEADS = 2
SWA_GROUP = SWA_Q_HEADS // SWA_KV_HEADS
SWA_W = SWA_Q_HEADS * HEAD_DIM
SWA_KV_W = SWA_KV_HEADS * HEAD_DIM
WINDOW = 128
ROPE_THETA = 10000.0
SC_W = 256
SC_CONV = 3
MEM_HEADS = 4
MEM_W = MEM_HEADS * HEAD_DIM
TOP_K = 4
SWIGLU_LIMIT = 7.0
SWIGLU_ALPHA = 1.702
PAST_LEN = 2048
LN_EPS = 1e-5
RMS_EPS = 1e-6
NEG_BIG = -1e30

COL_QKV = 0
COL_Z = 1152
COL_SC = 1536
COL_SWA = 2304
COL_SWA_V = COL_SWA + SWA_W + SWA_KV_W
COL_BA = COL_SWA + SWA_W + 2 * SWA_KV_W
P_PAD = 3072
BA_B = 0
BA_A = GDN_HEADS
MIX_W = 1024

MOE_BM = 512
SC_WINDOW = 128
MOE_PARTS = 2


def _cparams(*sem):
    return pltpu.CompilerParams(dimension_semantics=sem, vmem_limit_bytes=VMEM_LIMIT_BYTES)


def _pick_tile(n, candidates):
    for c in candidates:
        if n % c == 0:
            return c
    raise ValueError(f"no tile in {candidates} divides {n}")


def _ln(x, g, b):
    mu = jnp.mean(x, -1, keepdims=True)
    xc = x - mu
    var = jnp.mean(xc * xc, -1, keepdims=True)
    return xc * lax.rsqrt(var + LN_EPS) * g + b


def _dot(a, b):
    return jnp.dot(a.astype(BF16), b.astype(BF16), preferred_element_type=F32)


def _dot_nt(a, b):
    return lax.dot_general(a.astype(BF16), b.astype(BF16), (((1,), (1,)), ((), ())),
                           preferred_element_type=F32)


def _silu(x):
    return x * jax.nn.sigmoid(x)


def _pack_halves(x):
    w = x.shape[1] // 2
    lo = lax.bitcast_convert_type(x[:, 0:w].astype(BF16).astype(F32), jnp.uint32)
    hi = lax.bitcast_convert_type(x[:, w:2 * w].astype(BF16).astype(F32), jnp.uint32)
    return (hi & jnp.uint32(0xFFFF0000)) | (lo >> 16)


def _unpack_halves(p):
    lo = lax.bitcast_convert_type(p << 16, F32)
    hi = lax.bitcast_convert_type(p & jnp.uint32(0xFFFF0000), F32)
    return jnp.concatenate([lo, hi], axis=1)


def _ln_in_kernel(n_p_tiles, xp_ref, xs_ref, g_ref, b_ref, o_ref):
    i = pl.program_id(0)

    @pl.when(i < n_p_tiles)
    def _():
        o_ref[...] = _ln(xp_ref[...], g_ref[...], b_ref[...])

    @pl.when(i >= n_p_tiles)
    def _():
        o_ref[...] = _ln(xs_ref[...], g_ref[...], b_ref[...])


def _ln_in(xp, xs, g, b):
    tp, d = xp.shape
    ts = xs.shape[0]
    tm = _pick_tile(ts, (512, 256, 128, 64))
    assert tp % tm == 0
    npt, nst = tp // tm, ts // tm
    return pl.pallas_call(
        functools.partial(_ln_in_kernel, npt),
        grid=(npt + nst,),
        in_specs=[
            pl.BlockSpec((tm, d), lambda i: (jnp.minimum(i, npt - 1), 0)),
            pl.BlockSpec((tm, d), lambda i: (jnp.maximum(i - npt, 0), 0)),
            pl.BlockSpec((1, d), lambda i: (0, 0)),
            pl.BlockSpec((1, d), lambda i: (0, 0)),
        ],
        out_specs=pl.BlockSpec((tm, d), lambda i: (i, 0)),
        out_shape=jax.ShapeDtypeStruct((tp + ts, d), F32),
        compiler_params=_cparams("arbitrary"),
        name="ln_in",
    )(xp, xs, g.reshape(1, d), b.reshape(1, d))


def _proj_kernel(x_ref, w_ref, o_ref):
    o_ref[...] = _dot(x_ref[...], w_ref[...])


def _proj(x, w, name):
    t, k = x.shape
    n = w.shape[1]
    tm = _pick_tile(t, (512, 256, 128, 64))
    return pl.pallas_call(
        _proj_kernel,
        grid=(t // tm,),
        in_specs=[pl.BlockSpec((tm, k), lambda i: (i, 0)),
                  pl.BlockSpec((k, n), lambda i: (0, 0))],
        out_specs=pl.BlockSpec((tm, n), lambda i: (i, 0)),
        out_shape=jax.ShapeDtypeStruct((t, n), F32),
        compiler_params=_cparams("arbitrary"),
        name=name,
    )(x, w)


def _split2(x):
    hi = x.astype(BF16)
    lo = (x - hi.astype(F32)).astype(BF16)
    return hi, lo


def _split3(x):
    hi = x.astype(BF16)
    r1 = x - hi.astype(F32)
    mid = r1.astype(BF16)
    lo = (r1 - mid.astype(F32)).astype(BF16)
    return hi, mid, lo


def _sel_dot2(x, sel2):
    return jnp.dot(jnp.concatenate(_split2(x), axis=1), sel2, preferred_element_type=F32)


def _gdn_pipe_kernel(lt, h_ref, cbuf_ref, s0_ref, sbuf_ref, cw_ref, scw_ref, gp_ref, ng_ref,
                     ho_ref, cnew_ref, snew_ref, scnew_ref, ext_scr, uext_scr, s_scr, o_scr, pb_scr, pf_scr, pc_scr):
    i = pl.program_id(1)
    n_chunks = lt // CHUNK
    n_pairs = GDN_HEADS // 2
    pad = SUBLANE

    @pl.when(i == 0)
    def _():
        pb_scr[...] = jnp.zeros_like(pb_scr)
        pf_scr[...] = jnp.zeros_like(pf_scr)
        pc_scr[...] = jnp.zeros_like(pc_scr)
        ext_scr[0:pad, :] = jnp.concatenate(
            [jnp.zeros((pad - (GDN_CONV - 1), 3 * GDN_W), F32), cbuf_ref[0]], axis=0)
        uext_scr[0:pad, :] = jnp.concatenate(
            [jnp.zeros((pad - (SC_CONV - 1), SC_W), F32), sbuf_ref[0]], axis=0)
        zero_blk = jnp.zeros((HEAD_DIM, HEAD_DIM), F32)
        for p in range(n_pairs):
            s_scr[p] = jnp.concatenate(
                [jnp.concatenate([s0_ref[0, 2 * p], zero_blk], axis=1),
                 jnp.concatenate([zero_blk, s0_ref[0, 2 * p + 1]], axis=1)], axis=0)

    r384 = lax.broadcasted_iota(jnp.int32, (GDN_W, GDN_W), 0)
    c384 = lax.broadcasted_iota(jnp.int32, (GDN_W, GDN_W), 1)
    head_ones = jnp.where((r384 >> CHUNK_SHIFT) == (c384 >> CHUNK_SHIFT), 1.0, 0.0).astype(BF16)
    r128 = lax.broadcasted_iota(jnp.int32, (LANE, GDN_W), 0)
    c128 = lax.broadcasted_iota(jnp.int32, (LANE, GDN_W), 1) >> CHUNK_SHIFT
    exp_a = jnp.where(r128 == c128 + BA_A, 1.0, 0.0).astype(BF16)
    exp_b = jnp.where(r128 == c128 + BA_B, 1.0, 0.0).astype(BF16)
    exp_a3 = jnp.concatenate([exp_a, exp_a, exp_a], axis=0)
    exp_b2 = jnp.concatenate([exp_b, exp_b], axis=0)
    head_sum = lambda x: jnp.dot(x.astype(BF16), head_ones, preferred_element_type=F32)
    rc = lax.broadcasted_iota(jnp.int32, (CHUNK, CHUNK), 0)
    cc = lax.broadcasted_iota(jnp.int32, (CHUNK, CHUNK), 1)
    ltri = jnp.where(cc <= rc, 1.0, 0.0).astype(BF16)
    row = lax.broadcasted_iota(jnp.int32, (CHUNK, LANE), 0)
    lane = lax.broadcasted_iota(jnp.int32, (CHUNK, LANE), 1)
    col = lane & (HEAD_DIM - 1)
    left = lane < HEAD_DIM
    m_incl = col <= row
    m_strict = col < row
    m_diag = col == row
    eye = jnp.where(m_diag, 1.0, 0.0)
    m_base = ((row >> 1) == (col >> 1)) & m_strict
    lvl_masks = [((row >> (sh + 1)) == (col >> (sh + 1))) & ((row >> sh) != (col >> sh)) & m_strict
                 for sh in range(1, CHUNK_SHIFT)]
    r2 = lax.broadcasted_iota(jnp.int32, (LANE, LANE), 0)
    c2 = lax.broadcasted_iota(jnp.int32, (LANE, LANE), 1)
    m_bd = (r2 >> CHUNK_SHIFT) == (c2 >> CHUNK_SHIFT)
    lane_ba = lax.broadcasted_iota(jnp.int32, (CHUNK, LANE), 1)
    is_a = (lane_ba >= BA_A) & (lane_ba < BA_A + GDN_HEADS)

    def bd(x):
        zero = jnp.zeros_like(x)
        return jnp.concatenate([jnp.where(left, x, zero), jnp.where(left, zero, x)], axis=0)

    units = [(c, p) for c in range(n_chunks) for p in range(n_pairs)]
    rsl = lambda c: slice(c * CHUNK, (c + 1) * CHUNK)
    psl = lambda p: slice(p * LANE, (p + 1) * LANE)

    wr = i & 1
    rd = 1 - wr
    qn_b, kn_b, kbeta_b, qg_b, kb_b, vb_b, kt_b = (pb_scr[rd, j] for j in range(7))
    gam = pf_scr[rd, 0]
    z_gate = pf_scr[rd, 1]
    ho_ref[:, GDN_W:MIX_W] = pc_scr[rd]

    ext_scr[pad:pad + lt, :] = h_ref[:, COL_QKV:COL_QKV + 3 * GDN_W]
    uext_scr[pad:pad + lt, :] = (h_ref[:, COL_SC + SC_W:COL_SC + 2 * SC_W]
                                 * h_ref[:, COL_SC + 2 * SC_W:COL_SC + 3 * SC_W])
    qkv_chunks = {}

    def front_conv(c):
        r0 = c * CHUNK
        conv = cw_ref[0:1, :] * ext_scr[pl.ds(r0 + pad - 3, CHUNK), :]
        for j in range(1, GDN_CONV):
            conv = conv + cw_ref[j:j + 1, :] * ext_scr[pl.ds(r0 + pad - 3 + j, CHUNK), :]
        qkv_chunks[c] = _silu(conv)
        y = scw_ref[0:1, :] * uext_scr[pl.ds(r0 + pad - 2, CHUNK), :]
        for j in range(1, SC_CONV):
            y = y + scw_ref[j:j + 1, :] * uext_scr[pl.ds(r0 + pad - 2 + j, CHUNK), :]
        pc_scr[wr, rsl(c), :] = jnp.concatenate(
            [h_ref[rsl(c), GDN_W:2 * GDN_W], h_ref[rsl(c), COL_SC:COL_SC + SC_W] * y], axis=1)
        pf_scr[wr, 1, rsl(c), :] = _silu(h_ref[rsl(c), COL_Z:COL_Z + GDN_W])

    def front_gates(c):
        qkv = qkv_chunks.pop(c)
        q = qkv[:, 0:GDN_W]
        k = qkv[:, GDN_W:2 * GDN_W]
        v = qkv[:, 2 * GDN_W:3 * GDN_W]
        ss = head_sum(jnp.concatenate([q * q, k * k], axis=0))
        qn = q * lax.rsqrt(ss[0:CHUNK] + RMS_EPS) * HEAD_DIM ** -0.5
        kn = k * lax.rsqrt(ss[CHUNK:2 * CHUNK] + RMS_EPS)
        ba = h_ref[rsl(c), COL_BA:COL_BA + LANE]
        beta_blk = jax.nn.sigmoid(ba)
        sp_in = ba + gp_ref[1:2, :]
        softplus = jnp.maximum(sp_in, 0.0) + jnp.log(1.0 + jnp.exp(-jnp.abs(sp_in)))
        g_blk = jnp.where(is_a, -jnp.exp(gp_ref[0:1, :]) * softplus, 0.0)
        gam3 = jnp.dot(ltri, jnp.concatenate(_split3(g_blk), axis=1), preferred_element_type=F32)
        gam_blk = gam3[:, 0:LANE] + gam3[:, LANE:2 * LANE] + gam3[:, 2 * LANE:3 * LANE]
        gam_c = jnp.dot(jnp.concatenate(_split3(gam_blk), axis=1), exp_a3, preferred_element_type=F32)
        g_exp = jnp.exp(gam_c)
        k_fac = jnp.exp(gam_c[CHUNK - 1:CHUNK, :] - gam_c)
        beta = _sel_dot2(beta_blk, exp_b2)
        kbeta = kn * beta
        for j, val in enumerate((qn, kn, kbeta, qn * g_exp, kbeta * g_exp, v * beta, kn * k_fac)):
            pb_scr[wr, j, rsl(c), :] = val.astype(BF16)
        pf_scr[wr, 0, rsl(c), :] = gam_c

    pieces = [functools.partial(f, c) for c in range(n_chunks) for f in (front_conv, front_gates)]

    def emit(n=1):
        for _ in range(n):
            if pieces:
                pieces.pop(0)()

    a_b, t_inv, qd_b = {}, {}, {}
    for c, p in units:
        gc = gam[rsl(c), psl(p)]
        gr = jnp.sum(jnp.where(m_diag, gc, 0.0), axis=0, keepdims=True)
        dec = jnp.exp(jnp.where(m_incl, gc - gr, NEG_BIG))
        kq = lax.dot_general(jnp.concatenate([qn_b[rsl(c), psl(p)], kbeta_b[rsl(c), psl(p)]], axis=0),
                             bd(kn_b[rsl(c), psl(p)]), (((1,), (1,)), ((), ())), preferred_element_type=F32)
        qd_b[c, p] = (kq[0:CHUNK] * dec).astype(BF16)
        a_mat = jnp.where(m_strict, dec * kq[CHUNK:2 * CHUNK], 0.0)
        a_b[c, p] = a_mat.astype(BF16)
        t_inv[c, p] = eye - jnp.where(m_base, a_mat, 0.0)
    emit()
    zero_b = jnp.zeros((CHUNK, LANE), BF16)
    for lm in lvl_masks:
        w_lvl = {u_: jnp.dot(jnp.where(lm, a_b[u_], zero_b), bd(t_inv[u_].astype(BF16)),
                             preferred_element_type=F32) for u_ in units}
        for u_ in units:
            t_inv[u_] = t_inv[u_] - jnp.dot(t_inv[u_].astype(BF16), bd(w_lvl[u_].astype(BF16)),
                                            preferred_element_type=F32)
        emit()
    u0, w_b, kw_b, q_add, x_sols = {}, {}, {}, {}, {}
    for c, p in units:
        rhs = jnp.concatenate([bd(vb_b[rsl(c), psl(p)]), bd(kb_b[rsl(c), psl(p)])], axis=1)
        x_sols[c, p] = jnp.dot(t_inv[c, p].astype(BF16), rhs, preferred_element_type=F32)
    emit()
    for c, p in units:
        x_sol = x_sols[c, p]
        u0[c, p] = x_sol[:, 0:LANE]
        w_b[c, p] = x_sol[:, LANE:2 * LANE].astype(BF16)
        ktx = lax.dot_general(kt_b[rsl(c), psl(p)], x_sol.astype(BF16), (((0,), (0,)), ((), ())),
                              preferred_element_type=F32)
        q_add[c, p] = jnp.where(m_bd, ktx[:, 0:LANE], 0.0)
        kw_b[c, p] = jnp.where(m_bd, ktx[:, LANE:2 * LANE], 0.0).astype(BF16)
    emit()

    s_start = {}
    s_cur = [s_scr[p] for p in range(n_pairs)]
    for c in range(n_chunks):
        for p in range(n_pairs):
            s_bf = s_cur[p].astype(BF16)
            s_start[c, p] = s_bf
            e_last = jnp.exp(gam[(c + 1) * CHUNK - 1:(c + 1) * CHUNK, psl(p)])
            s_cur[p] = s_cur[p] * e_last - jnp.dot(kw_b[c, p], s_bf, preferred_element_type=F32) + q_add[c, p]
    for p in range(n_pairs):
        s_scr[p] = s_cur[p]

    wqs = {u_: jnp.dot(jnp.concatenate([w_b[u_], qg_b[rsl(u_[0]), psl(u_[1])]], axis=0), s_start[u_],
                       preferred_element_type=F32) for u_ in units}
    emit(len(pieces))
    for c, p in units:
        u = u0[c, p] - wqs[c, p][0:CHUNK]
        o_scr[rsl(c), psl(p)] = wqs[c, p][CHUNK:2 * CHUNK] + jnp.dot(qd_b[c, p], bd(u.astype(BF16)),
                                                                      preferred_element_type=F32)
    o = o_scr[...]
    o_ms = head_sum(o * o) * (1.0 / HEAD_DIM)
    ho_ref[:, 0:GDN_W] = o * lax.rsqrt(o_ms + RMS_EPS) * ng_ref[...] * z_gate

    cnew_ref[0] = ext_scr[pl.ds(lt + pad - 3, 3), :]
    ext_scr[0:pad, :] = ext_scr[lt:lt + pad, :]
    scnew_ref[0] = uext_scr[pl.ds(lt + pad - 2, 2), :]
    uext_scr[0:pad, :] = uext_scr[lt:lt + pad, :]

    @pl.when(i == pl.num_programs(1) - 1)
    def _():
        for p in range(n_pairs):
            snew_ref[0, 2 * p] = s_scr[p, 0:HEAD_DIM, 0:HEAD_DIM]
            snew_ref[0, 2 * p + 1] = s_scr[p, HEAD_DIM:LANE, HEAD_DIM:LANE]


def _gdn_sconv(h_all, row_off, nb, seq, cbuf, s0, sbuf, cw, scw, gp, ng):
    lt = _pick_tile(seq, (256, 128, 64))
    nt = seq // lt
    off = row_off // lt
    assert row_off % lt == 0
    tok_in = lambda b, i: (off + b * nt + jnp.minimum(i, nt - 1), 0)
    tok_out = lambda b, i: (off + b * nt + jnp.maximum(i - 1, 0), 0)
    return pl.pallas_call(
        functools.partial(_gdn_pipe_kernel, lt),
        grid=(nb, nt + 1),
        in_specs=[
            pl.BlockSpec((lt, P_PAD), tok_in),
            pl.BlockSpec((1, GDN_CONV - 1, 3 * GDN_W), lambda b, i: (b, 0, 0)),
            pl.BlockSpec((1, GDN_HEADS, HEAD_DIM, HEAD_DIM), lambda b, i: (b, 0, 0, 0)),
            pl.BlockSpec((1, SC_CONV - 1, SC_W), lambda b, i: (b, 0, 0)),
            pl.BlockSpec((GDN_CONV, 3 * GDN_W), lambda b, i: (0, 0)),
            pl.BlockSpec((SC_CONV, SC_W), lambda b, i: (0, 0)),
            pl.BlockSpec((SUBLANE, LANE), lambda b, i: (0, 0)),
            pl.BlockSpec((1, GDN_W), lambda b, i: (0, 0)),
        ],
        out_specs=[
            pl.BlockSpec((lt, MIX_W), tok_out),
            pl.BlockSpec((1, GDN_CONV - 1, 3 * GDN_W), lambda b, i: (b, 0, 0)),
            pl.BlockSpec((1, GDN_HEADS, HEAD_DIM, HEAD_DIM), lambda b, i: (b, 0, 0, 0)),
            pl.BlockSpec((1, SC_CONV - 1, SC_W), lambda b, i: (b, 0, 0)),
        ],
        out_shape=[
            jax.ShapeDtypeStruct(h_all.shape, F32),
            jax.ShapeDtypeStruct((nb, GDN_CONV - 1, 3 * GDN_W), F32),
            jax.ShapeDtypeStruct((nb, GDN_HEADS, HEAD_DIM, HEAD_DIM), F32),
            jax.ShapeDtypeStruct((nb, SC_CONV - 1, SC_W), F32),
        ],
        scratch_shapes=[
            pltpu.VMEM((lt + SUBLANE, 3 * GDN_W), F32),
            pltpu.VMEM((lt + SUBLANE, SC_W), F32),
            pltpu.VMEM((GDN_HEADS // 2, LANE, LANE), F32),
            pltpu.VMEM((lt, GDN_W), F32),
            pltpu.VMEM((2, 7, lt, GDN_W), BF16),
            pltpu.VMEM((2, 2, lt, GDN_W), F32),
            pltpu.VMEM((2, lt, MIX_W - GDN_W), F32),
        ],
        input_output_aliases={0: 0},
        compiler_params=_cparams("arbitrary", "arbitrary"),
        name="gdn_sconv",
    )(h_all, cbuf, s0, sbuf, cw, scw, gp, ng)


def _rope(x, cos, sin_signed):
    w = x.shape[1]
    nrep = w // LANE
    if nrep > 1:
        cos = jnp.concatenate([cos] * nrep, axis=1)
        sin_signed = jnp.concatenate([sin_signed] * nrep, axis=1)
    lane = lax.broadcasted_iota(jnp.int32, x.shape, 1) & (HEAD_DIM - 1)
    half = HEAD_DIM // 2
    swapped = jnp.where(lane < half, pltpu.roll(x, w - half, 1), pltpu.roll(x, half, 1))
    return x * cos + swapped * sin_signed


def _swa_kernel(lt, hist_valid, h_ref, hk_ref, hv_ref, cos_ref, sin_ref, sink_ref,
                o_ref, kr_ref, kall_scr, vall_scr):
    i = pl.program_id(1)

    @pl.when(i == 0)
    def _():
        kall_scr[0:WINDOW, :] = hk_ref[0]
        vall_scr[0:WINDOW, :] = hv_ref[0]

    cos = cos_ref[...]
    sin = sin_ref[...]
    q = _rope(h_ref[:, 0:SWA_W], cos, sin)
    k = _rope(h_ref[:, SWA_W:SWA_W + SWA_KV_W], cos, sin)
    kr_ref[...] = k
    kall_scr[WINDOW:WINDOW + lt, :] = k
    vall_scr[WINDOW:WINDOW + lt, :] = h_ref[:, SWA_W + SWA_KV_W:SWA_W + 2 * SWA_KV_W]
    q_b = q.astype(BF16)
    kall = kall_scr[...].astype(BF16)
    vall = vall_scr[...].astype(BF16)
    kall_sw = pltpu.roll(kall_scr[...], HEAD_DIM, 1).astype(BF16)
    vall_sw = pltpu.roll(vall_scr[...], HEAD_DIM, 1).astype(BF16)

    band = WINDOW + CHUNK
    half = 2 * LANE
    n_chunks = lt // CHUNK
    n_pairs = SWA_Q_HEADS // 2
    left = lax.broadcasted_iota(jnp.int32, (band, LANE), 1) < HEAD_DIM
    zero_band = jnp.zeros((band, LANE), BF16)
    zero_pad = jnp.zeros((half - band, LANE), BF16)
    k_src = ((kall, kall_sw), (kall, kall), (kall_sw, kall))
    v_src = ((vall, vall_sw), (vall, vall), (vall_sw, vall))

    def pair_bd(top, bot, ks):
        return jnp.concatenate([jnp.where(left, top[ks], zero_band), zero_pad,
                                jnp.where(left, zero_band, bot[ks]), zero_pad], axis=0)

    kb = lax.broadcasted_iota(jnp.int32, (1, half), 1)
    hist_ok = jnp.logical_or(hist_valid, i > 0)
    fill = [jnp.where(kb == band, sink_ref[0:1, hq:hq + 1], NEG_BIG) for hq in range(SWA_Q_HEADS)]
    r512 = lax.broadcasted_iota(jnp.int32, (2 * half, LANE), 0)
    l512 = lax.broadcasted_iota(jnp.int32, (2 * half, LANE), 1)
    ones_bd = jnp.where((r512 < half) == (l512 < HEAD_DIM), 1.0, 0.0).astype(BF16)
    units = [(c, p) for c in range(n_chunks) for p in range(n_pairs)]
    scores = {}
    for c, p in units:
        ks = slice(c * CHUNK, c * CHUNK + band)
        kbd = pair_bd(k_src[p][0], k_src[p][1], ks)
        scores[c, p] = lax.dot_general(q_b[c * CHUNK:(c + 1) * CHUNK, p * LANE:(p + 1) * LANE], kbd,
                                       (((1,), (1,)), ((), ())), preferred_element_type=F32)
    probs = {}
    for c, p in units:
        valid = (kb < band) & (hist_ok | ((kb >> CHUNK_SHIFT) + c >= WINDOW // CHUNK))
        halves = []
        for e_ in range(2):
            s = jnp.where(valid, scores[c, p][:, e_ * half:(e_ + 1) * half] * HEAD_DIM ** -0.5, fill[2 * p + e_])
            halves.append(jnp.exp(s - jnp.max(s, -1, keepdims=True)).astype(BF16))
        probs[c, p] = jnp.concatenate(halves, axis=1)
    for c, p in units:
        ks = slice(c * CHUNK, c * CHUNK + band)
        rhs = jnp.concatenate([pair_bd(v_src[p][0], v_src[p][1], ks), ones_bd], axis=1)
        od = jnp.dot(probs[c, p], rhs, preferred_element_type=F32)
        o_ref[c * CHUNK:(c + 1) * CHUNK, p * LANE:(p + 1) * LANE] = od[:, 0:LANE] / od[:, LANE:2 * LANE]

    if lt >= WINDOW:
        kall_scr[0:WINDOW, :] = kall_scr[lt:lt + WINDOW, :]
        vall_scr[0:WINDOW, :] = vall_scr[lt:lt + WINDOW, :]


def _swa(h_all, row_off, nb, seq, hist_k, hist_v, hist_valid, cos, sin, sinks):
    lt = _pick_tile(seq, (256, 128, 64))
    nt = seq // lt
    assert nt == 1 or lt >= WINDOW
    off = row_off // lt
    assert row_off % lt == 0
    swa_blk_w = P_PAD - COL_SWA
    return pl.pallas_call(
        functools.partial(_swa_kernel, lt, hist_valid),
        grid=(nb, nt),
        in_specs=[
            pl.BlockSpec((lt, swa_blk_w), lambda b, i: (off + b * nt + i, COL_SWA // swa_blk_w)),
            pl.BlockSpec((1, WINDOW, SWA_KV_W), lambda b, i: (b, 0, 0)),
            pl.BlockSpec((1, WINDOW, SWA_KV_W), lambda b, i: (b, 0, 0)),
            pl.BlockSpec((lt, LANE), lambda b, i: (i, 0)),
            pl.BlockSpec((lt, LANE), lambda b, i: (i, 0)),
            pl.BlockSpec((1, LANE), lambda b, i: (0, 0)),
        ],
        out_specs=[
            pl.BlockSpec((lt, SWA_W), lambda b, i: (off + b * nt + i, 1)),
            pl.BlockSpec((lt, SWA_KV_W), lambda b, i: (b * nt + i, 0)),
        ],
        out_shape=[
            jax.ShapeDtypeStruct(h_all.shape, F32),
            jax.ShapeDtypeStruct((nb * seq, SWA_KV_W), F32),
        ],
        scratch_shapes=[
            pltpu.VMEM((lt + WINDOW, SWA_KV_W), F32),
            pltpu.VMEM((lt + WINDOW, SWA_KV_W), F32),
        ],
        input_output_aliases={0: 0},
        compiler_params=_cparams("arbitrary", "arbitrary"),
        name="swa",
    )(h_all, hist_k, hist_v, cos, sin, sinks)


def _memattn_kernel(alpha, x_ref, xb_in_ref, gate_in_ref, ids_in_ref, mix_ref, wo_ref, g1_ref, b1_ref,
                    mk_ref, mv_ref, wcq_ref, wco_ref, g_ref, b_ref, wr_ref, br_ref,
                    x_out_ref, xb_ref, gate_ref, ids_ref, cnt_ref):
    del xb_in_ref, gate_in_ref, ids_in_ref
    x = _ln(alpha * x_ref[...] + _dot(mix_ref[...], wo_ref[...]), g1_ref[...], b1_ref[...])
    q = _dot(x, wcq_ref[...])
    mk = mk_ref[0].astype(BF16)
    mv = mv_ref[0].astype(BF16)
    n_mem = mk.shape[0]
    q_b = q.astype(BF16)
    left = lax.broadcasted_iota(jnp.int32, (n_mem, LANE), 1) < HEAD_DIM
    zero_kv = jnp.zeros((n_mem, LANE), BF16)

    def pair_bd(x):
        return jnp.concatenate([jnp.where(left, x, zero_kv), jnp.where(left, zero_kv, x)], axis=0)

    pairs = range(MEM_HEADS // 2)
    scores = [lax.dot_general(q_b[:, p * LANE:(p + 1) * LANE], pair_bd(mk[:, p * LANE:(p + 1) * LANE]),
                              (((1,), (1,)), ((), ())), preferred_element_type=F32) for p in pairs]
    probs = []
    for p in pairs:
        halves = []
        for e_ in range(2):
            s = scores[p][:, e_ * n_mem:(e_ + 1) * n_mem] * HEAD_DIM ** -0.5
            halves.append(jnp.exp(s - jnp.max(s, -1, keepdims=True)).astype(BF16))
        probs.append(jnp.concatenate(halves, axis=1))
    r2 = lax.broadcasted_iota(jnp.int32, (2 * n_mem, LANE), 0)
    l2 = lax.broadcasted_iota(jnp.int32, (2 * n_mem, LANE), 1)
    ones_bd = jnp.where((r2 < n_mem) == (l2 < HEAD_DIM), 1.0, 0.0).astype(BF16)
    outs = []
    for p in pairs:
        rhs = jnp.concatenate([pair_bd(mv[:, p * LANE:(p + 1) * LANE]), ones_bd], axis=1)
        od = jnp.dot(probs[p], rhs, preferred_element_type=F32)
        outs.append(od[:, 0:LANE] / od[:, LANE:2 * LANE])
    o = _dot(jnp.concatenate(outs, axis=1), wco_ref[...])
    x2 = _ln(alpha * x + o, g_ref[...], b_ref[...])
    x_out_ref[...] = x2
    packed = _pack_halves(x2)
    for p in range(xb_ref.shape[0]):
        xb_ref[p] = packed[:, p * LANE:(p + 1) * LANE]

    logits = _dot(x2, wr_ref[...]) + br_ref[...]
    lane = lax.broadcasted_iota(jnp.int32, logits.shape, 1).astype(F32)
    vals, idxs = [], []
    for _ in range(TOP_K):
        m = jnp.max(logits, -1, keepdims=True)
        idx = jnp.min(jnp.where(logits == m, lane, float(LANE)), -1, keepdims=True)
        vals.append(m)
        idxs.append(idx)
        logits = jnp.where(lane == idx, 2 * NEG_BIG, logits)
    es = [jnp.exp(val - vals[0]) for val in vals]
    den = es[0] + es[1] + es[2] + es[3]
    gate = jnp.zeros(logits.shape, F32)
    ids = jnp.zeros(logits.shape, F32)
    for kk in range(TOP_K):
        gate = jnp.where(lane == kk, es[kk] / den, gate)
        ids = jnp.where(lane == kk, idxs[kk], ids)
    gate_ref[...] = gate
    ids_ref[...] = ids.astype(jnp.int32)
    picked = jnp.sum(jnp.where(logits == 2 * NEG_BIG, 1.0, 0.0), 0, keepdims=True)
    cnt_ref[...] = jnp.broadcast_to(picked, cnt_ref.shape)


def _memattn_router(x_all, xb_all, gate_all, ids_all, h_all, w_o, g1, b1, row_off, nb, seq, mk, mv, wcq, wco,
                    g, b, wr, br, alpha):
    t, d = x_all.shape
    tl = _pick_tile(seq, (512, 256, 128, 64))
    nt = seq // tl
    off = row_off // tl
    assert row_off % tl == 0
    n_mem = mk.shape[1]
    tok = lambda bb, i: (off + bb * nt + i, 0)
    const = lambda bb, i: (0, 0)
    return pl.pallas_call(
        functools.partial(_memattn_kernel, alpha),
        grid=(nb, nt),
        in_specs=[
            pl.BlockSpec((tl, d), tok),
            pl.BlockSpec(memory_space=pl.ANY),
            pl.BlockSpec(memory_space=pl.ANY),
            pl.BlockSpec(memory_space=pl.ANY),
            pl.BlockSpec((tl, MIX_W), tok),
            pl.BlockSpec((MIX_W, d), const),
            pl.BlockSpec((1, d), const),
            pl.BlockSpec((1, d), const),
            pl.BlockSpec((1, n_mem, MEM_W), lambda bb, i: (bb, 0, 0)),
            pl.BlockSpec((1, n_mem, MEM_W), lambda bb, i: (bb, 0, 0)),
            pl.BlockSpec((d, MEM_W), const),
            pl.BlockSpec((MEM_W, d), const),
            pl.BlockSpec((1, d), const),
            pl.BlockSpec((1, d), const),
            pl.BlockSpec((d, LANE), const),
            pl.BlockSpec((1, LANE), const),
        ],
        out_specs=[
            pl.BlockSpec((tl, d), tok),
            pl.BlockSpec((d // (2 * LANE), tl, LANE), lambda bb, i: (0, off + bb * nt + i, 0)),
            pl.BlockSpec((tl, LANE), tok),
            pl.BlockSpec((tl, LANE), tok),
            pl.BlockSpec((SUBLANE, LANE), lambda bb, i: (bb * nt + i, 0)),
        ],
        out_shape=[
            jax.ShapeDtypeStruct((t, d), F32),
            jax.ShapeDtypeStruct((d // (2 * LANE), t, LANE), jnp.uint32),
            jax.ShapeDtypeStruct((t, LANE), F32),
            jax.ShapeDtypeStruct((t, LANE), jnp.int32),
            jax.ShapeDtypeStruct((nb * nt * SUBLANE, LANE), F32),
        ],
        input_output_aliases={0: 0, 1: 1, 2: 2, 3: 3},
        compiler_params=_cparams("arbitrary", "arbitrary"),
        name="memattn_router",
    )(x_all, xb_all, gate_all, ids_all, h_all, w_o, g1.reshape(1, d), b1.reshape(1, d), mk, mv, wcq, wco,
      g.reshape(1, d), b.reshape(1, d), wr, br)


def _expert_kernel(d_ff, be_ref, bv_ref, nu_ref, xs_ref, w1_ref, b1_ref, w2_ref, b2_ref, ys_ref, w1b_scr, w2b_scr):
    i = pl.program_id(0)
    prev = be_ref[jnp.maximum(i - 1, 0)]
    new_expert = (i == 0) | (be_ref[i] != prev)

    @pl.when(new_expert & (i < nu_ref[0]))
    def _():
        w1b_scr[...] = w1_ref[0, 0].astype(BF16)
        w2b_scr[...] = w2_ref[0, 0].astype(BF16)

    @pl.when(i < nu_ref[0])
    def _():
        row = lax.broadcasted_iota(jnp.int32, (xs_ref.shape[1], 1), 0)
        words = jnp.concatenate([xs_ref[p] for p in range(xs_ref.shape[0])], axis=1)
        xs = _unpack_halves(jnp.where(row < bv_ref[i], words, jnp.zeros_like(words))).astype(BF16)
        hh = jnp.dot(xs, w1b_scr[...], preferred_element_type=F32) + b1_ref[0, 0]
        glu = jnp.minimum(hh[:, 0:d_ff], SWIGLU_LIMIT)
        lin = jnp.clip(hh[:, d_ff:2 * d_ff], -SWIGLU_LIMIT, SWIGLU_LIMIT)
        act = glu * jax.nn.sigmoid(SWIGLU_ALPHA * glu) * (lin + 1.0)
        y = jnp.dot(act.astype(BF16), w2b_scr[...], preferred_element_type=F32) + b2_ref[0, 0]
        packed = _pack_halves(y)
        for p in range(ys_ref.shape[0]):
            ys_ref[p] = packed[:, p * LANE:(p + 1) * LANE]


def _experts(xs, blk_e, blk_valid, n_used, layer, w1, b1, w2, b2):
    n_planes, n_rows, _ = xs.shape
    n_l, n_e, d, two_ff = w1.shape
    d_ff = two_ff // 2
    n_blocks = n_rows // MOE_BM
    row_blk = lambda i, be, bv, nu: (0, jnp.minimum(i, nu[0] - 1), 0)
    exp_blk = lambda i, be, bv, nu: (layer, be[jnp.minimum(i, nu[0] - 1)], 0, 0)
    return pl.pallas_call(
        functools.partial(_expert_kernel, d_ff),
        grid_spec=pltpu.PrefetchScalarGridSpec(
            num_scalar_prefetch=3,
            grid=(n_blocks,),
            in_specs=[
                pl.BlockSpec((n_planes, MOE_BM, LANE), row_blk),
                pl.BlockSpec((1, 1, d, two_ff), exp_blk),
                pl.BlockSpec((1, 1, 1, two_ff), exp_blk),
                pl.BlockSpec((1, 1, d_ff, d), exp_blk),
                pl.BlockSpec((1, 1, 1, d), exp_blk),
            ],
            out_specs=pl.BlockSpec((n_planes, MOE_BM, LANE), row_blk),
            scratch_shapes=[pltpu.VMEM((d, two_ff), BF16), pltpu.VMEM((d_ff, d), BF16)],
        ),
        out_shape=jax.ShapeDtypeStruct((n_planes, n_rows, LANE), jnp.uint32),
        compiler_params=_cparams("arbitrary"),
        name="moe_experts",
    )(blk_e, blk_valid, n_used, xs, w1, b1.reshape(n_l, n_e, 1, two_ff), w2, b2.reshape(n_l, n_e, 1, d))


def _sc_mesh():
    return plsc.VectorSubcoreMesh(core_axis_name="c", subcore_axis_name="s")


def _sc_dispatch(xw, dest_km, n_rows):
    n_planes, t, _ = xw.shape
    n_src = n_planes * t
    nw = n_src // SC_WINDOW
    assert t % SC_WINDOW == 0

    @functools.partial(pl.kernel, out_type=jax.ShapeDtypeStruct((n_planes * n_rows, LANE), xw.dtype),
                       mesh=_sc_mesh(), scratch_types=[], name="moe_dispatch_sc")
    def scatter_rows(x_hbm, i0_hbm, i1_hbm, i2_hbm, i3_hbm, o_hbm):
        def body(x_vmem, i0, i1, i2, i3):
            for idx in (i0, i1, i2, i3):
                pltpu.sync_copy(x_vmem, o_hbm.at[idx.at[0]])

        pltpu.emit_pipeline(
            body,
            grid=(nw,),
            in_specs=[pl.BlockSpec((SC_WINDOW, LANE), lambda i: (i, 0))]
            + [pl.BlockSpec((1, SC_WINDOW), functools.partial(lambda k, i: (0, k * nw + i), k))
               for k in range(TOP_K)],
            out_specs=[],
            core_axis_name=("c", "s"),
            dimension_semantics=(pltpu.PARALLEL,),
        )(x_hbm, i0_hbm, i1_hbm, i2_hbm, i3_hbm)

    plane_off = (jnp.arange(n_planes, dtype=jnp.int32) * n_rows)[None, :, None]
    idx = (dest_km.reshape(TOP_K, 1, t) + plane_off).reshape(1, TOP_K * n_src)
    return scatter_rows(xw.reshape(n_src, LANE), idx, idx, idx, idx).reshape(n_planes, n_rows, LANE)


def _sc_gather(yw, idx_flat):
    n_planes, n_rows, _ = yw.shape
    n = idx_flat.shape[0]
    assert n % SC_WINDOW == 0

    @functools.partial(pl.kernel, out_type=jax.ShapeDtypeStruct((n_planes * n, LANE), yw.dtype),
                       mesh=_sc_mesh(), scratch_types=[], name="moe_gather_sc")
    def gather_rows(y_hbm, i_hbm, o_hbm):
        def body(i_vmem, o_vmem):
            pltpu.sync_copy(y_hbm.at[i_vmem.at[0]], o_vmem)

        pltpu.emit_pipeline(
            body,
            grid=(n_planes * n // SC_WINDOW,),
            in_specs=[pl.BlockSpec((1, SC_WINDOW), lambda i: (0, i))],
            out_specs=[pl.BlockSpec((SC_WINDOW, LANE), lambda i: (i, 0))],
            core_axis_name=("c", "s"),
            dimension_semantics=(pltpu.PARALLEL,),
        )(i_hbm, o_hbm)

    plane_off = (jnp.arange(n_planes, dtype=jnp.int32) * n_rows)[:, None]
    idx = (idx_flat[None, :] + plane_off).reshape(1, n_planes * n)
    return gather_rows(yw.reshape(n_planes * n_rows, LANE), idx).reshape(n_planes, n, LANE)


def _combine_kernel(alpha, y0_ref, y1_ref, y2_ref, y3_ref, gate_ref, x_ref, g_ref, b_ref, o_ref):
    gate = gate_ref[...]
    acc = alpha * x_ref[...]
    for k, y_ref in enumerate((y0_ref, y1_ref, y2_ref, y3_ref)):
        words = jnp.concatenate([y_ref[p] for p in range(y_ref.shape[0])], axis=1)
        acc = acc + gate[:, k:k + 1] * _unpack_halves(words)
    o_ref[...] = _ln(acc, g_ref[...], b_ref[...])


def _combine(yg, gate_all, x_all, g, b, alpha, row_off):
    t, d = x_all.shape
    tc = yg.shape[1] // TOP_K
    tm = _pick_tile(tc, (512, 256, 128, 64))
    nt = tc // tm
    off = row_off // tm
    assert row_off % tm == 0
    y_spec = lambda k: pl.BlockSpec((yg.shape[0], tm, LANE), lambda i: (0, k * nt + i, 0))
    return pl.pallas_call(
        functools.partial(_combine_kernel, alpha),
        grid=(nt,),
        in_specs=[
            y_spec(0), y_spec(1), y_spec(2), y_spec(3),
            pl.BlockSpec((tm, LANE), lambda i: (off + i, 0)),
            pl.BlockSpec((tm, d), lambda i: (off + i, 0)),
            pl.BlockSpec((1, d), lambda i: (0, 0)),
            pl.BlockSpec((1, d), lambda i: (0, 0)),
        ],
        out_specs=pl.BlockSpec((tm, d), lambda i: (off + i, 0)),
        out_shape=jax.ShapeDtypeStruct((t, d), F32),
        input_output_aliases={5: 0},
        compiler_params=_cparams("arbitrary"),
        name="moe_combine_ln3",
    )(yg, yg, yg, yg, gate_all, x_all, g.reshape(1, d), b.reshape(1, d))


def _dest_kernel(ids_ref, start_ref, dest_ref, run_scr):
    i = pl.program_id(0)

    @pl.when(i == 0)
    def _():
        run_scr[...] = jnp.zeros_like(run_scr)

    ids = ids_ref[...].astype(F32)
    tm = ids.shape[0]
    lane = lax.broadcasted_iota(jnp.int32, ids.shape, 1).astype(F32)
    one_hots = [jnp.where(lane == ids[:, k:k + 1], 1.0, 0.0) for k in range(TOP_K)]
    multi_hot = one_hots[0] + one_hots[1] + one_hots[2] + one_hots[3]
    r = lax.broadcasted_iota(jnp.int32, (tm, tm), 0)
    c = lax.broadcasted_iota(jnp.int32, (tm, tm), 1)
    before = jnp.dot(jnp.where(c < r, 1.0, 0.0).astype(BF16), multi_hot.astype(BF16),
                     preferred_element_type=F32) + (run_scr[0:1, :] + start_ref[...])
    dest = jnp.zeros(ids.shape, F32)
    for k in range(TOP_K):
        dest = jnp.where(lane == k, jnp.sum(one_hots[k] * before, -1, keepdims=True), dest)
    dest_ref[...] = dest.astype(jnp.int32)
    run_scr[0:1, :] = run_scr[0:1, :] + jnp.sum(multi_hot, 0, keepdims=True)


def _route(ids_all, counts, n_experts):
    t = ids_all.shape[0]
    tk = t * TOP_K
    n_blocks = -(-tk // MOE_BM) + n_experts
    n_rows = n_blocks * MOE_BM
    assert n_rows < 2 ** 24
    padded = (counts + MOE_BM - 1) // MOE_BM * MOE_BM
    pad_end = jnp.cumsum(padded)
    pad_start = pad_end - padded
    start_row = jnp.zeros((1, LANE), F32).at[0, 0:n_experts].set(pad_start.astype(F32))
    tm = _pick_tile(t, (512, 256, 128, 64))
    tok = pl.BlockSpec((tm, LANE), lambda i: (i, 0))
    dest = pl.pallas_call(
        _dest_kernel,
        grid=(t // tm,),
        in_specs=[tok, pl.BlockSpec((1, LANE), lambda i: (0, 0))],
        out_specs=tok,
        out_shape=jax.ShapeDtypeStruct((t, LANE), jnp.int32),
        scratch_shapes=[pltpu.VMEM((SUBLANE, LANE), F32)],
        compiler_params=_cparams("arbitrary"),
        name="moe_dest",
    )(ids_all, start_row)
    dest_km = dest[:, 0:TOP_K].T.reshape(-1)
    blk = jnp.arange(n_blocks, dtype=jnp.int32)
    blk_e = jnp.minimum(jnp.sum((blk[:, None] * MOE_BM >= pad_end[None, :]).astype(jnp.int32), axis=1),
                        n_experts - 1)
    blk_valid = jnp.clip(counts[blk_e] - (blk * MOE_BM - pad_start[blk_e]), 0, MOE_BM).astype(jnp.int32)
    n_used = (pad_end[-1] // MOE_BM).astype(jnp.int32).reshape(1)
    return dest_km, n_rows, blk_e, blk_valid, n_used


def _rope_tables(pos):
    half = HEAD_DIM // 2
    inv = ROPE_THETA ** (-jnp.arange(half, dtype=F32) / half)
    ang = pos.astype(F32)[:, None] * inv[None, :]
    cos = jnp.cos(ang)
    sin = jnp.sin(ang)
    cos = jnp.concatenate([cos, cos, cos, cos], axis=1)
    sin = jnp.concatenate([-sin, sin, -sin, sin], axis=1)
    return cos, sin


def _permute_w_in(w):
    d = w.shape[0]
    o_b = 4 * GDN_W
    o_q = o_b + 2 * GDN_HEADS
    o_sc = o_q + SWA_W + 2 * SWA_KV_W
    ba = jnp.concatenate([w[:, o_b:o_q], jnp.zeros((d, LANE - 2 * GDN_HEADS), w.dtype)], axis=1)
    return jnp.concatenate([w[:, 0:o_b], w[:, o_sc:], w[:, o_q:o_sc], ba], axis=1)


def kernel(x_prompt, x_sample, state_gdn_S, state_gdn_conv, cache_swa_k, cache_swa_v, state_sconv, cache_mem_k, cache_mem_v, mem_prompt, ln_in_g, ln_in_b, w_in, gdn_conv_w, gdn_a_log, gdn_dt_bias, gdn_norm_g, swa_sinks, sc_conv_w, w_o, ln1_g, ln1_b, w_cq, w_mk, w_mv, w_co, ln2_g, ln2_b, w_router, b_router, w1, b1, w2, b2, ln3_g, ln3_b):
    bp, lp, d = x_prompt.shape
    bs, ls, _ = x_sample.shape
    depth = w_in.shape[0]
    n_experts = w_router.shape[2]
    n_mem = mem_prompt.shape[1]
    tp, ts = bp * lp, bs * ls
    t_all = tp + ts
    alpha = (2 * depth) ** 0.25
    assert cache_swa_k.shape[2] == WINDOW and d == MIX_W

    x_all = _ln_in(x_prompt.reshape(tp, d), x_sample.reshape(ts, d), ln_in_g, ln_in_b)
    xb_all = jnp.zeros((d // (2 * LANE), t_all, LANE), jnp.uint32)
    gate_all = jnp.zeros((t_all, LANE), F32)
    ids_all = jnp.zeros((t_all, LANE), jnp.int32)

    cos_p, sin_p = _rope_tables(jnp.arange(lp))
    cos_s, sin_s = _rope_tables(PAST_LEN + jnp.arange(ls))
    mem_flat = mem_prompt.reshape(bp * n_mem, d)
    zeros_cbuf = jnp.zeros((bp, GDN_CONV - 1, 3 * GDN_W), F32)
    zeros_s = jnp.zeros((bp, GDN_HEADS, HEAD_DIM, HEAD_DIM), F32)
    zeros_sbuf = jnp.zeros((bp, SC_CONV - 1, SC_W), F32)
    zeros_hist = jnp.zeros((bp, WINDOW, SWA_KV_W), F32)

    outs = {k: [] for k in ("p_S", "p_conv", "p_k", "p_v", "p_sc", "p_mk", "p_mv",
                            "s_S", "s_conv", "s_k", "s_v", "s_sc")}
    for l in range(depth):
        w_in_l = _permute_w_in(w_in[l]).astype(BF16)
        gp = jnp.zeros((SUBLANE, LANE), F32)
        gp = gp.at[0, BA_A:BA_A + GDN_HEADS].set(gdn_a_log[l]).at[1, BA_A:BA_A + GDN_HEADS].set(gdn_dt_bias[l])
        ng = jnp.tile(gdn_norm_g[l].reshape(1, HEAD_DIM), (1, GDN_HEADS))
        sinks = jnp.zeros((1, LANE), F32).at[0, 0:SWA_Q_HEADS].set(swa_sinks[l])
        wr = jnp.concatenate([w_router[l], jnp.zeros((d, LANE - n_experts), F32)], axis=1).astype(BF16)
        br = jnp.concatenate([b_router[l], jnp.full((LANE - n_experts,), NEG_BIG, F32)]).reshape(1, LANE)

        mkv = _proj(mem_flat, jnp.concatenate([w_mk[l], w_mv[l]], axis=1).astype(BF16), "mem_kv")
        mk_p = mkv[:, 0:MEM_W].reshape(bp, n_mem, MEM_W)
        mv_p = mkv[:, MEM_W:2 * MEM_W].reshape(bp, n_mem, MEM_W)
        outs["p_mk"].append(mk_p.reshape(bp, n_mem, MEM_HEADS, HEAD_DIM))
        outs["p_mv"].append(mv_p.reshape(bp, n_mem, MEM_HEADS, HEAD_DIM))

        h_all = _proj(x_all, w_in_l, "w_in")

        v_p = h_all[0:tp, COL_SWA_V:COL_SWA_V + SWA_KV_W].reshape(bp, lp, SWA_KV_W)[:, lp - WINDOW:]
        v_s = h_all[tp:, COL_SWA_V:COL_SWA_V + SWA_KV_W]
        h_all, c_p, s_p, sc_p = _gdn_sconv(h_all, 0, bp, lp, zeros_cbuf, zeros_s, zeros_sbuf,
                                           gdn_conv_w[l], sc_conv_w[l], gp, ng)
        h_all, c_s, s_s, sc_s = _gdn_sconv(h_all, tp, bs, ls, state_gdn_conv[l], state_gdn_S[l], state_sconv[l],
                                           gdn_conv_w[l], sc_conv_w[l], gp, ng)
        h_all, kr_p = _swa(h_all, 0, bp, lp, zeros_hist, zeros_hist, False, cos_p, sin_p, sinks)
        h_all, kr_s = _swa(h_all, tp, bs, ls, cache_swa_k[l].reshape(bs, WINDOW, SWA_KV_W),
                           cache_swa_v[l].reshape(bs, WINDOW, SWA_KV_W), True, cos_s, sin_s, sinks)
        outs["p_S"].append(s_p); outs["p_conv"].append(c_p); outs["p_sc"].append(sc_p)
        outs["s_S"].append(s_s); outs["s_conv"].append(c_s); outs["s_sc"].append(sc_s)
        outs["p_k"].append(kr_p.reshape(bp, lp, SWA_KV_W)[:, lp - WINDOW:].reshape(bp, WINDOW, SWA_KV_HEADS, HEAD_DIM))
        outs["p_v"].append(v_p.reshape(bp, WINDOW, SWA_KV_HEADS, HEAD_DIM))
        outs["s_k"].append(kr_s.reshape(bs, ls, SWA_KV_HEADS, HEAD_DIM))
        outs["s_v"].append(v_s.reshape(bs, ls, SWA_KV_HEADS, HEAD_DIM))

        wo = w_o[l].astype(BF16)
        wcq = w_cq[l].astype(BF16)
        wco = w_co[l].astype(BF16)
        x_all, xb_all, gate_all, ids_all, cnt_p = _memattn_router(
            x_all, xb_all, gate_all, ids_all, h_all, wo, ln1_g[l], ln1_b[l], 0, bp, lp, mk_p, mv_p,
            wcq, wco, ln2_g[l], ln2_b[l], wr, br, alpha)
        x_all, xb_all, gate_all, ids_all, cnt_s = _memattn_router(
            x_all, xb_all, gate_all, ids_all, h_all, wo, ln1_g[l], ln1_b[l], tp, bs, ls,
            cache_mem_k[l].reshape(bs, n_mem, MEM_W), cache_mem_v[l].reshape(bs, n_mem, MEM_W),
            wcq, wco, ln2_g[l], ln2_b[l], wr, br, alpha)

        counts = sum(c_.reshape(-1, SUBLANE, LANE)[:, 0, 0:n_experts].sum(0) for c_ in (cnt_p, cnt_s))
        dest_km, n_rows, blk_e, blk_valid, n_used = _route(ids_all, counts.astype(jnp.int32), n_experts)
        xs = _sc_dispatch(xb_all, dest_km, n_rows)
        ys = _experts(xs, blk_e, blk_valid, n_used, l, w1, b1, w2, b2)
        dest_tk = dest_km.reshape(TOP_K, t_all)
        for part in range(MOE_PARTS):
            lo = part * (t_all // MOE_PARTS)
            yg = _sc_gather(ys, dest_tk[:, lo:lo + t_all // MOE_PARTS].reshape(-1))
            x_all = _combine(yg, gate_all, x_all, ln3_g[l], ln3_b[l], alpha, lo)

    st = lambda k: jnp.stack(outs[k])
    return (x_all[0:tp].reshape(bp, lp, d), x_all[tp:].reshape(bs, ls, d),
            st("p_S"), st("p_conv"), st("p_k"), st("p_v"), st("p_sc"), st("p_mk"), st("p_mv"),
            st("s_S"), st("s_conv"), st("s_k"), st("s_v"), st("s_sc"))
```

```python
import functools

import jax
import jax.numpy as jnp
from jax import lax
from jax.experimental import pallas as pl
from jax.experimental.pallas import tpu as pltpu
from jax.experimental.pallas import tpu_sc as plsc

F32 = jnp.float32
BF16 = jnp.bfloat16

LANE = 128
SUBLANE = 8
VMEM_LIMIT_BYTES = 56 * 1024 * 1024

CHUNK = 64
CHUNK_SHIFT = 6
HEAD_DIM = 64
GDN_HEADS = 6
GDN_W = GDN_HEADS * HEAD_DIM
GDN_CONV = 4
SWA_Q_HEADS = 6
SWA_KV_HEADS = 2
SWA_W = SWA_Q_HEADS * HEAD_DIM
SWA_KV_W = SWA_KV_HEADS * HEAD_DIM
WINDOW = 128
ROPE_THETA = 10000.0
SC_W = 256
SC_CONV = 3
MEM_HEADS = 4
MEM_W = MEM_HEADS * HEAD_DIM
TOP_K = 4
SWIGLU_LIMIT = 7.0
SWIGLU_ALPHA = 1.702
PAST_LEN = 2048
LN_EPS = 1e-5
RMS_EPS = 1e-6
NEG_BIG = -1e30

COL_QKV = 0
COL_Z = 1152
COL_SC = 1536
COL_SWA = 2304
COL_SWA_V = COL_SWA + SWA_W + SWA_KV_W
COL_BA = COL_SWA + SWA_W + 2 * SWA_KV_W
P_PAD = 3072
BA_B = 0
BA_A = GDN_HEADS
MIX_W = 1024

MOE_BM = 512
SC_WINDOW = 128
MOE_PARTS = 2


def _cparams(*sem):
    return pltpu.CompilerParams(dimension_semantics=sem, vmem_limit_bytes=VMEM_LIMIT_BYTES)


def _pick_tile(n, candidates):
    for c in candidates:
        if n % c == 0:
            return c
    raise ValueError(f"no tile in {candidates} divides {n}")


def _ln(x, g, b):
    mu = jnp.mean(x, -1, keepdims=True)
    xc = x - mu
    var = jnp.mean(xc * xc, -1, keepdims=True)
    return xc * lax.rsqrt(var + LN_EPS) * g + b


def _dot(a, b):
    return jnp.dot(a.astype(BF16), b.astype(BF16), preferred_element_type=F32)


def _silu(x):
    return x * jax.nn.sigmoid(x)


def _pack_halves(x):
    w = x.shape[1] // 2
    lo = lax.bitcast_convert_type(x[:, 0:w].astype(BF16).astype(F32), jnp.uint32)
    hi = lax.bitcast_convert_type(x[:, w:2 * w].astype(BF16).astype(F32), jnp.uint32)
    return (hi & jnp.uint32(0xFFFF0000)) | (lo >> 16)


def _unpack_halves(p):
    lo = lax.bitcast_convert_type(p << 16, F32)
    hi = lax.bitcast_convert_type(p & jnp.uint32(0xFFFF0000), F32)
    return jnp.concatenate([lo, hi], axis=1)


def _ln_in_kernel(n_p_tiles, xp_ref, xs_ref, g_ref, b_ref, o_ref):
    i = pl.program_id(0)

    @pl.when(i < n_p_tiles)
    def _():
        o_ref[...] = _ln(xp_ref[...], g_ref[...], b_ref[...])

    @pl.when(i >= n_p_tiles)
    def _():
        o_ref[...] = _ln(xs_ref[...], g_ref[...], b_ref[...])


def _ln_in(xp, xs, g, b):
    tp, d = xp.shape
    ts = xs.shape[0]
    tm = _pick_tile(ts, (512, 256, 128, 64))
    assert tp % tm == 0
    npt, nst = tp // tm, ts // tm
    return pl.pallas_call(
        functools.partial(_ln_in_kernel, npt),
        grid=(npt + nst,),
        in_specs=[
            pl.BlockSpec((tm, d), lambda i: (jnp.minimum(i, npt - 1), 0)),
            pl.BlockSpec((tm, d), lambda i: (jnp.maximum(i - npt, 0), 0)),
            pl.BlockSpec((1, d), lambda i: (0, 0)),
            pl.BlockSpec((1, d), lambda i: (0, 0)),
        ],
        out_specs=pl.BlockSpec((tm, d), lambda i: (i, 0)),
        out_shape=jax.ShapeDtypeStruct((tp + ts, d), F32),
        compiler_params=_cparams("arbitrary"),
        name="ln_in",
    )(xp, xs, g.reshape(1, d), b.reshape(1, d))


def _proj_kernel(x_ref, w_ref, o_ref):
    o_ref[...] = _dot(x_ref[...], w_ref[...])


def _proj(x, w, name):
    t, k = x.shape
    n = w.shape[1]
    tm = _pick_tile(t, (512, 256, 128, 64))
    return pl.pallas_call(
        _proj_kernel,
        grid=(t // tm,),
        in_specs=[pl.BlockSpec((tm, k), lambda i: (i, 0)),
                  pl.BlockSpec((k, n), lambda i: (0, 0))],
        out_specs=pl.BlockSpec((tm, n), lambda i: (i, 0)),
        out_shape=jax.ShapeDtypeStruct((t, n), F32),
        compiler_params=_cparams("arbitrary"),
        name=name,
    )(x, w)


def _split2(x):
    hi = x.astype(BF16)
    lo = (x - hi.astype(F32)).astype(BF16)
    return hi, lo


def _split3(x):
    hi = x.astype(BF16)
    r1 = x - hi.astype(F32)
    mid = r1.astype(BF16)
    lo = (r1 - mid.astype(F32)).astype(BF16)
    return hi, mid, lo


def _sel_dot2(x, sel2):
    return jnp.dot(jnp.concatenate(_split2(x), axis=1), sel2, preferred_element_type=F32)


def _gdn_pipe_kernel(lt, one_tile_seqs, h_ref, cbuf_ref, s0_ref, sbuf_ref, cw_ref, scw_ref, gp_ref, ng_ref,
                     ho_ref, cnew_ref, snew_ref, scnew_ref, ext_scr, uext_scr, s_scr, o_scr, pb_scr, pf_scr, pc_scr):
    i = pl.program_id(1)
    n_chunks = lt // CHUNK
    n_pairs = GDN_HEADS // 2
    pad = SUBLANE

    def load_seq_state():
        ext_scr[0:pad, :] = jnp.concatenate(
            [jnp.zeros((pad - (GDN_CONV - 1), 3 * GDN_W), F32), cbuf_ref[0]], axis=0)
        uext_scr[0:pad, :] = jnp.concatenate(
            [jnp.zeros((pad - (SC_CONV - 1), SC_W), F32), sbuf_ref[0]], axis=0)
        zero_blk = jnp.zeros((HEAD_DIM, HEAD_DIM), F32)
        for p in range(n_pairs):
            s_scr[p] = jnp.concatenate(
                [jnp.concatenate([s0_ref[0, 2 * p], zero_blk], axis=1),
                 jnp.concatenate([zero_blk, s0_ref[0, 2 * p + 1]], axis=1)], axis=0)

    @pl.when(i == 0)
    def _():
        pb_scr[...] = jnp.zeros_like(pb_scr)
        pf_scr[...] = jnp.zeros_like(pf_scr)
        pc_scr[...] = jnp.zeros_like(pc_scr)
        if not one_tile_seqs:
            load_seq_state()

    if one_tile_seqs:
        load_seq_state()

    r384 = lax.broadcasted_iota(jnp.int32, (GDN_W, GDN_W), 0)
    c384 = lax.broadcasted_iota(jnp.int32, (GDN_W, GDN_W), 1)
    head_ones = jnp.where((r384 >> CHUNK_SHIFT) == (c384 >> CHUNK_SHIFT), 1.0, 0.0).astype(BF16)
    r128 = lax.broadcasted_iota(jnp.int32, (LANE, GDN_W), 0)
    c128 = lax.broadcasted_iota(jnp.int32, (LANE, GDN_W), 1) >> CHUNK_SHIFT
    exp_a = jnp.where(r128 == c128 + BA_A, 1.0, 0.0).astype(BF16)
    exp_b = jnp.where(r128 == c128 + BA_B, 1.0, 0.0).astype(BF16)
    exp_a3 = jnp.concatenate([exp_a, exp_a, exp_a], axis=0)
    exp_b2 = jnp.concatenate([exp_b, exp_b], axis=0)
    head_sum = lambda x: jnp.dot(x.astype(BF16), head_ones, preferred_element_type=F32)
    rc = lax.broadcasted_iota(jnp.int32, (CHUNK, CHUNK), 0)
    cc = lax.broadcasted_iota(jnp.int32, (CHUNK, CHUNK), 1)
    ltri = jnp.where(cc <= rc, 1.0, 0.0).astype(BF16)
    row = lax.broadcasted_iota(jnp.int32, (CHUNK, LANE), 0)
    lane = lax.broadcasted_iota(jnp.int32, (CHUNK, LANE), 1)
    col = lane & (HEAD_DIM - 1)
    left = lane < HEAD_DIM
    m_incl = col <= row
    m_strict = col < row
    m_diag = col == row
    eye = jnp.where(m_diag, 1.0, 0.0)
    m_base = ((row >> 1) == (col >> 1)) & m_strict
    lvl_masks = [((row >> (sh + 1)) == (col >> (sh + 1))) & ((row >> sh) != (col >> sh)) & m_strict
                 for sh in range(1, CHUNK_SHIFT)]
    r2 = lax.broadcasted_iota(jnp.int32, (LANE, LANE), 0)
    c2 = lax.broadcasted_iota(jnp.int32, (LANE, LANE), 1)
    m_bd = (r2 >> CHUNK_SHIFT) == (c2 >> CHUNK_SHIFT)
    lane_ba = lax.broadcasted_iota(jnp.int32, (CHUNK, LANE), 1)
    is_a = (lane_ba >= BA_A) & (lane_ba < BA_A + GDN_HEADS)

    def bd(x):
        zero = jnp.zeros_like(x)
        return jnp.concatenate([jnp.where(left, x, zero), jnp.where(left, zero, x)], axis=0)

    units = [(c, p) for c in range(n_chunks) for p in range(n_pairs)]
    rsl = lambda c: slice(c * CHUNK, (c + 1) * CHUNK)
    psl = lambda p: slice(p * LANE, (p + 1) * LANE)

    wr = i & 1
    rd = 1 - wr
    qn_b, kn_b, kbeta_b, qg_b, kb_b, vb_b, kt_b = (pb_scr[rd, j] for j in range(7))
    gam = pf_scr[rd, 0]
    z_gate = pf_scr[rd, 1]
    ho_ref[:, GDN_W:MIX_W] = pc_scr[rd]

    ext_scr[pad:pad + lt, :] = h_ref[:, COL_QKV:COL_QKV + 3 * GDN_W]
    uext_scr[pad:pad + lt, :] = (h_ref[:, COL_SC + SC_W:COL_SC + 2 * SC_W]
                                 * h_ref[:, COL_SC + 2 * SC_W:COL_SC + 3 * SC_W])
    qkv_chunks = {}

    def front_conv(c):
        r0 = c * CHUNK
        conv = cw_ref[0:1, :] * ext_scr[pl.ds(r0 + pad - 3, CHUNK), :]
        for j in range(1, GDN_CONV):
            conv = conv + cw_ref[j:j + 1, :] * ext_scr[pl.ds(r0 + pad - 3 + j, CHUNK), :]
        qkv_chunks[c] = _silu(conv)
        y = scw_ref[0:1, :] * uext_scr[pl.ds(r0 + pad - 2, CHUNK), :]
        for j in range(1, SC_CONV):
            y = y + scw_ref[j:j + 1, :] * uext_scr[pl.ds(r0 + pad - 2 + j, CHUNK), :]
        pc_scr[wr, rsl(c), :] = jnp.concatenate(
            [h_ref[rsl(c), GDN_W:2 * GDN_W], h_ref[rsl(c), COL_SC:COL_SC + SC_W] * y], axis=1)
        pf_scr[wr, 1, rsl(c), :] = _silu(h_ref[rsl(c), COL_Z:COL_Z + GDN_W])

    def front_gates(c):
        qkv = qkv_chunks.pop(c)
        q = qkv[:, 0:GDN_W]
        k = qkv[:, GDN_W:2 * GDN_W]
        v = qkv[:, 2 * GDN_W:3 * GDN_W]
        ss = head_sum(jnp.concatenate([q * q, k * k], axis=0))
        qn = q * lax.rsqrt(ss[0:CHUNK] + RMS_EPS) * HEAD_DIM ** -0.5
        kn = k * lax.rsqrt(ss[CHUNK:2 * CHUNK] + RMS_EPS)
        ba = h_ref[rsl(c), COL_BA:COL_BA + LANE]
        beta_blk = jax.nn.sigmoid(ba)
        sp_in = ba + gp_ref[1:2, :]
        softplus = jnp.maximum(sp_in, 0.0) + jnp.log(1.0 + jnp.exp(-jnp.abs(sp_in)))
        g_blk = jnp.where(is_a, -jnp.exp(gp_ref[0:1, :]) * softplus, 0.0)
        gam3 = jnp.dot(ltri, jnp.concatenate(_split3(g_blk), axis=1), preferred_element_type=F32)
        gam_blk = gam3[:, 0:LANE] + gam3[:, LANE:2 * LANE] + gam3[:, 2 * LANE:3 * LANE]
        gam_c = jnp.dot(jnp.concatenate(_split3(gam_blk), axis=1), exp_a3, preferred_element_type=F32)
        g_exp = jnp.exp(gam_c)
        k_fac = jnp.exp(gam_c[CHUNK - 1:CHUNK, :] - gam_c)
        beta = _sel_dot2(beta_blk, exp_b2)
        kbeta = kn * beta
        for j, val in enumerate((qn, kn, kbeta, qn * g_exp, kbeta * g_exp, v * beta, kn * k_fac)):
            pb_scr[wr, j, rsl(c), :] = val.astype(BF16)
        pf_scr[wr, 0, rsl(c), :] = gam_c

    pieces = [functools.partial(f, c) for c in range(n_chunks) for f in (front_conv, front_gates)]

    def emit(n=1):
        for _ in range(n):
            if pieces:
                pieces.pop(0)()

    a_b, t_inv, qd_b = {}, {}, {}
    for c, p in units:
        gc = gam[rsl(c), psl(p)]
        gr = jnp.sum(jnp.where(m_diag, gc, 0.0), axis=0, keepdims=True)
        dec = jnp.exp(jnp.where(m_incl, gc - gr, NEG_BIG))
        kq = lax.dot_general(jnp.concatenate([qn_b[rsl(c), psl(p)], kbeta_b[rsl(c), psl(p)]], axis=0),
                             bd(kn_b[rsl(c), psl(p)]), (((1,), (1,)), ((), ())), preferred_element_type=F32)
        qd_b[c, p] = (kq[0:CHUNK] * dec).astype(BF16)
        a_mat = jnp.where(m_strict, dec * kq[CHUNK:2 * CHUNK], 0.0)
        a_b[c, p] = a_mat.astype(BF16)
        t_inv[c, p] = eye - jnp.where(m_base, a_mat, 0.0)
    emit()
    zero_b = jnp.zeros((CHUNK, LANE), BF16)
    for lm in lvl_masks:
        w_lvl = {u_: jnp.dot(jnp.where(lm, a_b[u_], zero_b), bd(t_inv[u_].astype(BF16)),
                             preferred_element_type=F32) for u_ in units}
        for u_ in units:
            t_inv[u_] = t_inv[u_] - jnp.dot(t_inv[u_].astype(BF16), bd(w_lvl[u_].astype(BF16)),
                                            preferred_element_type=F32)
        emit()
    u0, w_b, kw_b, q_add, x_sols = {}, {}, {}, {}, {}
    for c, p in units:
        rhs = jnp.concatenate([bd(vb_b[rsl(c), psl(p)]), bd(kb_b[rsl(c), psl(p)])], axis=1)
        x_sols[c, p] = jnp.dot(t_inv[c, p].astype(BF16), rhs, preferred_element_type=F32)
    emit()
    for c, p in units:
        x_sol = x_sols[c, p]
        u0[c, p] = x_sol[:, 0:LANE]
        w_b[c, p] = x_sol[:, LANE:2 * LANE].astype(BF16)
        ktx = lax.dot_general(kt_b[rsl(c), psl(p)], x_sol.astype(BF16), (((0,), (0,)), ((), ())),
                              preferred_element_type=F32)
        q_add[c, p] = jnp.where(m_bd, ktx[:, 0:LANE], 0.0)
        kw_b[c, p] = jnp.where(m_bd, ktx[:, LANE:2 * LANE], 0.0).astype(BF16)
    emit()

    s_start = {}
    s_cur = [s_scr[p] for p in range(n_pairs)]
    for c in range(n_chunks):
        for p in range(n_pairs):
            s_bf = s_cur[p].astype(BF16)
            s_start[c, p] = s_bf
            e_last = jnp.exp(gam[(c + 1) * CHUNK - 1:(c + 1) * CHUNK, psl(p)])
            s_cur[p] = s_cur[p] * e_last - jnp.dot(kw_b[c, p], s_bf, preferred_element_type=F32) + q_add[c, p]
    for p in range(n_pairs):
        s_scr[p] = s_cur[p]

    wqs = {u_: jnp.dot(jnp.concatenate([w_b[u_], qg_b[rsl(u_[0]), psl(u_[1])]], axis=0), s_start[u_],
                       preferred_element_type=F32) for u_ in units}
    emit(len(pieces))
    for c, p in units:
        u = u0[c, p] - wqs[c, p][0:CHUNK]
        o_scr[rsl(c), psl(p)] = wqs[c, p][CHUNK:2 * CHUNK] + jnp.dot(qd_b[c, p], bd(u.astype(BF16)),
                                                                      preferred_element_type=F32)
    o = o_scr[...]
    o_ms = head_sum(o * o) * (1.0 / HEAD_DIM)
    ho_ref[:, 0:GDN_W] = o * lax.rsqrt(o_ms + RMS_EPS) * ng_ref[...] * z_gate

    cnew_ref[0] = ext_scr[pl.ds(lt + pad - 3, 3), :]
    ext_scr[0:pad, :] = ext_scr[lt:lt + pad, :]
    scnew_ref[0] = uext_scr[pl.ds(lt + pad - 2, 2), :]
    uext_scr[0:pad, :] = uext_scr[lt:lt + pad, :]

    def store_state():
        for p in range(n_pairs):
            snew_ref[0, 2 * p] = s_scr[p, 0:HEAD_DIM, 0:HEAD_DIM]
            snew_ref[0, 2 * p + 1] = s_scr[p, HEAD_DIM:LANE, HEAD_DIM:LANE]

    if one_tile_seqs:
        store_state()
    else:
        pl.when(i == pl.num_programs(1) - 1)(store_state)


def _gdn_sconv(h_all, row_off, nb, seq, cbuf, s0, sbuf, cw, scw, gp, ng):
    lt = _pick_tile(seq, (256, 128, 64))
    nt = seq // lt
    off = row_off // lt
    assert row_off % lt == 0
    one_tile_seqs = nt == 1
    if one_tile_seqs:
        grid = (1, nb + 1)
        f_seq = lambda b, i: jnp.minimum(i, nb - 1)
        b_seq = lambda b, i: jnp.maximum(i - 1, 0)
        f_tile = b_tile = lambda b, i: 0
    else:
        grid = (nb, nt + 1)
        f_seq = b_seq = lambda b, i: b
        f_tile = lambda b, i: jnp.minimum(i, nt - 1)
        b_tile = lambda b, i: jnp.maximum(i - 1, 0)
    tok_in = lambda b, i: (off + f_seq(b, i) * nt + f_tile(b, i), 0)
    tok_out = lambda b, i: (off + b_seq(b, i) * nt + b_tile(b, i), 0)
    return pl.pallas_call(
        functools.partial(_gdn_pipe_kernel, lt, one_tile_seqs),
        grid=grid,
        in_specs=[
            pl.BlockSpec((lt, P_PAD), tok_in),
            pl.BlockSpec((1, GDN_CONV - 1, 3 * GDN_W), lambda b, i: (f_seq(b, i), 0, 0)),
            pl.BlockSpec((1, GDN_HEADS, HEAD_DIM, HEAD_DIM), lambda b, i: (b_seq(b, i), 0, 0, 0)),
            pl.BlockSpec((1, SC_CONV - 1, SC_W), lambda b, i: (f_seq(b, i), 0, 0)),
            pl.BlockSpec((GDN_CONV, 3 * GDN_W), lambda b, i: (0, 0)),
            pl.BlockSpec((SC_CONV, SC_W), lambda b, i: (0, 0)),
            pl.BlockSpec((SUBLANE, LANE), lambda b, i: (0, 0)),
            pl.BlockSpec((1, GDN_W), lambda b, i: (0, 0)),
        ],
        out_specs=[
            pl.BlockSpec((lt, MIX_W), tok_out),
            pl.BlockSpec((1, GDN_CONV - 1, 3 * GDN_W), lambda b, i: (f_seq(b, i), 0, 0)),
            pl.BlockSpec((1, GDN_HEADS, HEAD_DIM, HEAD_DIM), lambda b, i: (b_seq(b, i), 0, 0, 0)),
            pl.BlockSpec((1, SC_CONV - 1, SC_W), lambda b, i: (f_seq(b, i), 0, 0)),
        ],
        out_shape=[
            jax.ShapeDtypeStruct(h_all.shape, F32),
            jax.ShapeDtypeStruct((nb, GDN_CONV - 1, 3 * GDN_W), F32),
            jax.ShapeDtypeStruct((nb, GDN_HEADS, HEAD_DIM, HEAD_DIM), F32),
            jax.ShapeDtypeStruct((nb, SC_CONV - 1, SC_W), F32),
        ],
        scratch_shapes=[
            pltpu.VMEM((lt + SUBLANE, 3 * GDN_W), F32),
            pltpu.VMEM((lt + SUBLANE, SC_W), F32),
            pltpu.VMEM((GDN_HEADS // 2, LANE, LANE), F32),
            pltpu.VMEM((lt, GDN_W), F32),
            pltpu.VMEM((2, 7, lt, GDN_W), BF16),
            pltpu.VMEM((2, 2, lt, GDN_W), F32),
            pltpu.VMEM((2, lt, MIX_W - GDN_W), F32),
        ],
        input_output_aliases={0: 0},
        compiler_params=_cparams("arbitrary", "arbitrary"),
        name="gdn_sconv",
    )(h_all, cbuf, s0, sbuf, cw, scw, gp, ng)


def _rope(x, cos, sin_signed):
    w = x.shape[1]
    nrep = w // LANE
    if nrep > 1:
        cos = jnp.concatenate([cos] * nrep, axis=1)
        sin_signed = jnp.concatenate([sin_signed] * nrep, axis=1)
    lane = lax.broadcasted_iota(jnp.int32, x.shape, 1) & (HEAD_DIM - 1)
    half = HEAD_DIM // 2
    swapped = jnp.where(lane < half, pltpu.roll(x, w - half, 1), pltpu.roll(x, half, 1))
    return x * cos + swapped * sin_signed


def _swa_kernel(lt, hist_valid, h_ref, hk_ref, hv_ref, cos_ref, sin_ref, sink_ref,
                o_ref, kr_ref, kall_scr, vall_scr):
    i = pl.program_id(1)

    @pl.when(i == 0)
    def _():
        kall_scr[0:WINDOW, :] = hk_ref[0]
        vall_scr[0:WINDOW, :] = hv_ref[0]

    cos = cos_ref[...]
    sin = sin_ref[...]
    q = _rope(h_ref[:, 0:SWA_W], cos, sin)
    k = _rope(h_ref[:, SWA_W:SWA_W + SWA_KV_W], cos, sin)
    kr_ref[...] = k
    kall_scr[WINDOW:WINDOW + lt, :] = k
    vall_scr[WINDOW:WINDOW + lt, :] = h_ref[:, SWA_W + SWA_KV_W:SWA_W + 2 * SWA_KV_W]
    q_b = q.astype(BF16)
    kall = kall_scr[...].astype(BF16)
    vall = vall_scr[...].astype(BF16)
    kall_sw = pltpu.roll(kall_scr[...], HEAD_DIM, 1).astype(BF16)
    vall_sw = pltpu.roll(vall_scr[...], HEAD_DIM, 1).astype(BF16)

    band = WINDOW + CHUNK
    half = 2 * LANE
    n_chunks = lt // CHUNK
    n_pairs = SWA_Q_HEADS // 2
    left = lax.broadcasted_iota(jnp.int32, (band, LANE), 1) < HEAD_DIM
    zero_band = jnp.zeros((band, LANE), BF16)
    zero_pad = jnp.zeros((half - band, LANE), BF16)
    k_src = ((kall, kall_sw), (kall, kall), (kall_sw, kall))
    v_src = ((vall, vall_sw), (vall, vall), (vall_sw, vall))

    def pair_bd(top, bot, ks):
        return jnp.concatenate([jnp.where(left, top[ks], zero_band), zero_pad,
                                jnp.where(left, zero_band, bot[ks]), zero_pad], axis=0)

    kb = lax.broadcasted_iota(jnp.int32, (1, half), 1)
    hist_ok = jnp.logical_or(hist_valid, i > 0)
    fill = [jnp.where(kb == band, sink_ref[0:1, hq:hq + 1], NEG_BIG) for hq in range(SWA_Q_HEADS)]
    r512 = lax.broadcasted_iota(jnp.int32, (2 * half, LANE), 0)
    l512 = lax.broadcasted_iota(jnp.int32, (2 * half, LANE), 1)
    ones_bd = jnp.where((r512 < half) == (l512 < HEAD_DIM), 1.0, 0.0).astype(BF16)
    units = [(c, p) for c in range(n_chunks) for p in range(n_pairs)]
    scores = {}
    for c, p in units:
        ks = slice(c * CHUNK, c * CHUNK + band)
        kbd = pair_bd(k_src[p][0], k_src[p][1], ks)
        scores[c, p] = lax.dot_general(q_b[c * CHUNK:(c + 1) * CHUNK, p * LANE:(p + 1) * LANE], kbd,
                                       (((1,), (1,)), ((), ())), preferred_element_type=F32)
    probs = {}
    for c, p in units:
        valid = (kb < band) & (hist_ok | ((kb >> CHUNK_SHIFT) + c >= WINDOW // CHUNK))
        halves = []
        for e_ in range(2):
            s = jnp.where(valid, scores[c, p][:, e_ * half:(e_ + 1) * half] * HEAD_DIM ** -0.5, fill[2 * p + e_])
            halves.append(jnp.exp(s - jnp.max(s, -1, keepdims=True)).astype(BF16))
        probs[c, p] = jnp.concatenate(halves, axis=1)
    for c, p in units:
        ks = slice(c * CHUNK, c * CHUNK + band)
        rhs = jnp.concatenate([pair_bd(v_src[p][0], v_src[p][1], ks), ones_bd], axis=1)
        od = jnp.dot(probs[c, p], rhs, preferred_element_type=F32)
        o_ref[c * CHUNK:(c + 1) * CHUNK, p * LANE:(p + 1) * LANE] = od[:, 0:LANE] / od[:, LANE:2 * LANE]

    if lt >= WINDOW:
        kall_scr[0:WINDOW, :] = kall_scr[lt:lt + WINDOW, :]
        vall_scr[0:WINDOW, :] = vall_scr[lt:lt + WINDOW, :]


def _swa(h_all, row_off, nb, seq, hist_k, hist_v, hist_valid, cos, sin, sinks):
    lt = _pick_tile(seq, (256, 128, 64))
    nt = seq // lt
    assert nt == 1 or lt >= WINDOW
    off = row_off // lt
    assert row_off % lt == 0
    swa_blk_w = P_PAD - COL_SWA
    return pl.pallas_call(
        functools.partial(_swa_kernel, lt, hist_valid),
        grid=(nb, nt),
        in_specs=[
            pl.BlockSpec((lt, swa_blk_w), lambda b, i: (off + b * nt + i, COL_SWA // swa_blk_w)),
            pl.BlockSpec((1, WINDOW, SWA_KV_W), lambda b, i: (b, 0, 0)),
            pl.BlockSpec((1, WINDOW, SWA_KV_W), lambda b, i: (b, 0, 0)),
            pl.BlockSpec((lt, LANE), lambda b, i: (i, 0)),
            pl.BlockSpec((lt, LANE), lambda b, i: (i, 0)),
            pl.BlockSpec((1, LANE), lambda b, i: (0, 0)),
        ],
        out_specs=[
            pl.BlockSpec((lt, SWA_W), lambda b, i: (off + b * nt + i, 1)),
            pl.BlockSpec((lt, SWA_KV_W), lambda b, i: (b * nt + i, 0)),
        ],
        out_shape=[
            jax.ShapeDtypeStruct(h_all.shape, F32),
            jax.ShapeDtypeStruct((nb * seq, SWA_KV_W), F32),
        ],
        scratch_shapes=[
            pltpu.VMEM((lt + WINDOW, SWA_KV_W), F32),
            pltpu.VMEM((lt + WINDOW, SWA_KV_W), F32),
        ],
        input_output_aliases={0: 0},
        compiler_params=_cparams("arbitrary", "arbitrary"),
        name="swa",
    )(h_all, hist_k, hist_v, cos, sin, sinks)


def _memattn_kernel(alpha, x_ref, xb_in_ref, gate_in_ref, ids_in_ref, mix_ref, wo_ref, g1_ref, b1_ref,
                    mk_ref, mv_ref, wcq_ref, wco_ref, g_ref, b_ref, wr_ref, br_ref,
                    x_out_ref, xb_ref, gate_ref, ids_ref, cnt_ref):
    del xb_in_ref, gate_in_ref, ids_in_ref
    x = _ln(alpha * x_ref[...] + _dot(mix_ref[...], wo_ref[...]), g1_ref[...], b1_ref[...])
    q = _dot(x, wcq_ref[...])
    mk = mk_ref[0].astype(BF16)
    mv = mv_ref[0].astype(BF16)
    n_mem = mk.shape[0]
    q_b = q.astype(BF16)
    left = lax.broadcasted_iota(jnp.int32, (n_mem, LANE), 1) < HEAD_DIM
    zero_kv = jnp.zeros((n_mem, LANE), BF16)

    def pair_bd(x):
        return jnp.concatenate([jnp.where(left, x, zero_kv), jnp.where(left, zero_kv, x)], axis=0)

    pairs = range(MEM_HEADS // 2)
    scores = [lax.dot_general(q_b[:, p * LANE:(p + 1) * LANE], pair_bd(mk[:, p * LANE:(p + 1) * LANE]),
                              (((1,), (1,)), ((), ())), preferred_element_type=F32) for p in pairs]
    probs = []
    for p in pairs:
        halves = []
        for e_ in range(2):
            s = scores[p][:, e_ * n_mem:(e_ + 1) * n_mem] * HEAD_DIM ** -0.5
            halves.append(jnp.exp(s - jnp.max(s, -1, keepdims=True)).astype(BF16))
        probs.append(jnp.concatenate(halves, axis=1))
    r2 = lax.broadcasted_iota(jnp.int32, (2 * n_mem, LANE), 0)
    l2 = lax.broadcasted_iota(jnp.int32, (2 * n_mem, LANE), 1)
    ones_bd = jnp.where((r2 < n_mem) == (l2 < HEAD_DIM), 1.0, 0.0).astype(BF16)
    outs = []
    for p in pairs:
        rhs = jnp.concatenate([pair_bd(mv[:, p * LANE:(p + 1) * LANE]), ones_bd], axis=1)
        od = jnp.dot(probs[p], rhs, preferred_element_type=F32)
        outs.append(od[:, 0:LANE] / od[:, LANE:2 * LANE])
    o = _dot(jnp.concatenate(outs, axis=1), wco_ref[...])
    x2 = _ln(alpha * x + o, g_ref[...], b_ref[...])
    x_out_ref[...] = x2
    packed = _pack_halves(x2)
    for p in range(xb_ref.shape[0]):
        xb_ref[p] = packed[:, p * LANE:(p + 1) * LANE]

    logits = _dot(x2, wr_ref[...]) + br_ref[...]
    lane = lax.broadcasted_iota(jnp.int32, logits.shape, 1).astype(F32)
    vals, idxs = [], []
    for _ in range(TOP_K):
        m = jnp.max(logits, -1, keepdims=True)
        idx = jnp.min(jnp.where(logits == m, lane, float(LANE)), -1, keepdims=True)
        vals.append(m)
        idxs.append(idx)
        logits = jnp.where(lane == idx, 2 * NEG_BIG, logits)
    es = [jnp.exp(val - vals[0]) for val in vals]
    den = es[0] + es[1] + es[2] + es[3]
    gate = jnp.zeros(logits.shape, F32)
    ids = jnp.zeros(logits.shape, F32)
    for kk in range(TOP_K):
        gate = jnp.where(lane == kk, es[kk] / den, gate)
        ids = jnp.where(lane == kk, idxs[kk], ids)
    gate_ref[...] = gate
    ids_ref[...] = ids.astype(jnp.int32)
    picked = jnp.sum(jnp.where(logits == 2 * NEG_BIG, 1.0, 0.0), 0, keepdims=True)
    cnt_ref[...] = jnp.broadcast_to(picked, cnt_ref.shape)


def _memattn_router(x_all, xb_all, gate_all, ids_all, h_all, w_o, g1, b1, row_off, nb, seq, mk, mv, wcq, wco,
                    g, b, wr, br, alpha):
    t, d = x_all.shape
    tl = _pick_tile(seq, (512, 256, 128, 64))
    nt = seq // tl
    off = row_off // tl
    assert row_off % tl == 0
    n_mem = mk.shape[1]
    tok = lambda bb, i: (off + bb * nt + i, 0)
    const = lambda bb, i: (0, 0)
    return pl.pallas_call(
        functools.partial(_memattn_kernel, alpha),
        grid=(nb, nt),
        in_specs=[
            pl.BlockSpec((tl, d), tok),
            pl.BlockSpec(memory_space=pl.ANY),
            pl.BlockSpec(memory_space=pl.ANY),
            pl.BlockSpec(memory_space=pl.ANY),
            pl.BlockSpec((tl, MIX_W), tok),
            pl.BlockSpec((MIX_W, d), const),
            pl.BlockSpec((1, d), const),
            pl.BlockSpec((1, d), const),
            pl.BlockSpec((1, n_mem, MEM_W), lambda bb, i: (bb, 0, 0)),
            pl.BlockSpec((1, n_mem, MEM_W), lambda bb, i: (bb, 0, 0)),
            pl.BlockSpec((d, MEM_W), const),
            pl.BlockSpec((MEM_W, d), const),
            pl.BlockSpec((1, d), const),
            pl.BlockSpec((1, d), const),
            pl.BlockSpec((d, LANE), const),
            pl.BlockSpec((1, LANE), const),
        ],
        out_specs=[
            pl.BlockSpec((tl, d), tok),
            pl.BlockSpec((d // (2 * LANE), tl, LANE), lambda bb, i: (0, off + bb * nt + i, 0)),
            pl.BlockSpec((tl, LANE), tok),
            pl.BlockSpec((tl, LANE), tok),
            pl.BlockSpec((SUBLANE, LANE), lambda bb, i: (bb * nt + i, 0)),
        ],
        out_shape=[
            jax.ShapeDtypeStruct((t, d), F32),
            jax.ShapeDtypeStruct((d // (2 * LANE), t, LANE), jnp.uint32),
            jax.ShapeDtypeStruct((t, LANE), F32),
            jax.ShapeDtypeStruct((t, LANE), jnp.int32),
            jax.ShapeDtypeStruct((nb * nt * SUBLANE, LANE), F32),
        ],
        input_output_aliases={0: 0, 1: 1, 2: 2, 3: 3},
        compiler_params=_cparams("arbitrary", "arbitrary"),
        name="memattn_router",
    )(x_all, xb_all, gate_all, ids_all, h_all, w_o, g1.reshape(1, d), b1.reshape(1, d), mk, mv, wcq, wco,
      g.reshape(1, d), b.reshape(1, d), wr, br)


def _expert_kernel(d_ff, be_ref, bv_ref, nu_ref, xs_ref, w1_ref, b1_ref, w2_ref, b2_ref, ys_ref, w1b_scr, w2b_scr):
    i = pl.program_id(0)
    prev = be_ref[jnp.maximum(i - 1, 0)]
    new_expert = (i == 0) | (be_ref[i] != prev)

    @pl.when(new_expert & (i < nu_ref[0]))
    def _():
        w1b_scr[...] = w1_ref[0, 0].astype(BF16)
        w2b_scr[...] = w2_ref[0, 0].astype(BF16)

    @pl.when(i < nu_ref[0])
    def _():
        row = lax.broadcasted_iota(jnp.int32, (xs_ref.shape[1], 1), 0)
        words = jnp.concatenate([xs_ref[p] for p in range(xs_ref.shape[0])], axis=1)
        xs = _unpack_halves(jnp.where(row < bv_ref[i], words, jnp.zeros_like(words))).astype(BF16)
        hh = jnp.dot(xs, w1b_scr[...], preferred_element_type=F32) + b1_ref[0, 0]
        glu = jnp.minimum(hh[:, 0:d_ff], SWIGLU_LIMIT)
        lin = jnp.clip(hh[:, d_ff:2 * d_ff], -SWIGLU_LIMIT, SWIGLU_LIMIT)
        act = glu * jax.nn.sigmoid(SWIGLU_ALPHA * glu) * (lin + 1.0)
        y = jnp.dot(act.astype(BF16), w2b_scr[...], preferred_element_type=F32) + b2_ref[0, 0]
        packed = _pack_halves(y)
        for p in range(ys_ref.shape[0]):
            ys_ref[p] = packed[:, p * LANE:(p + 1) * LANE]


def _experts(xs, blk_e, blk_valid, n_used, layer, w1, b1, w2, b2):
    n_planes, n_rows, _ = xs.shape
    n_l, n_e, d, two_ff = w1.shape
    d_ff = two_ff // 2
    n_blocks = n_rows // MOE_BM
    row_blk = lambda i, be, bv, nu: (0, jnp.minimum(i, nu[0] - 1), 0)
    exp_blk = lambda i, be, bv, nu: (layer, be[jnp.minimum(i, nu[0] - 1)], 0, 0)
    return pl.pallas_call(
        functools.partial(_expert_kernel, d_ff),
        grid_spec=pltpu.PrefetchScalarGridSpec(
            num_scalar_prefetch=3,
            grid=(n_blocks,),
            in_specs=[
                pl.BlockSpec((n_planes, MOE_BM, LANE), row_blk),
                pl.BlockSpec((1, 1, d, two_ff), exp_blk),
                pl.BlockSpec((1, 1, 1, two_ff), exp_blk),
                pl.BlockSpec((1, 1, d_ff, d), exp_blk),
                pl.BlockSpec((1, 1, 1, d), exp_blk),
            ],
            out_specs=pl.BlockSpec((n_planes, MOE_BM, LANE), row_blk),
            scratch_shapes=[pltpu.VMEM((d, two_ff), BF16), pltpu.VMEM((d_ff, d), BF16)],
        ),
        out_shape=jax.ShapeDtypeStruct((n_planes, n_rows, LANE), jnp.uint32),
        compiler_params=_cparams("arbitrary"),
        name="moe_experts",
    )(blk_e, blk_valid, n_used, xs, w1, b1.reshape(n_l, n_e, 1, two_ff), w2, b2.reshape(n_l, n_e, 1, d))


def _sc_mesh():
    return plsc.VectorSubcoreMesh(core_axis_name="c", subcore_axis_name="s")


def _sc_dispatch(xw, dest_km, n_rows):
    n_planes, t, _ = xw.shape
    n_src = n_planes * t
    nw = n_src // SC_WINDOW
    assert t % SC_WINDOW == 0

    @functools.partial(pl.kernel, out_type=jax.ShapeDtypeStruct((n_planes * n_rows, LANE), xw.dtype),
                       mesh=_sc_mesh(), scratch_types=[], name="moe_dispatch_sc")
    def scatter_rows(x_hbm, i0_hbm, i1_hbm, i2_hbm, i3_hbm, o_hbm):
        def body(x_vmem, i0, i1, i2, i3):
            for idx in (i0, i1, i2, i3):
                pltpu.sync_copy(x_vmem, o_hbm.at[idx.at[0]])

        pltpu.emit_pipeline(
            body,
            grid=(nw,),
            in_specs=[pl.BlockSpec((SC_WINDOW, LANE), lambda i: (i, 0))]
            + [pl.BlockSpec((1, SC_WINDOW), functools.partial(lambda k, i: (0, k * nw + i), k))
               for k in range(TOP_K)],
            out_specs=[],
            core_axis_name=("c", "s"),
            dimension_semantics=(pltpu.PARALLEL,),
        )(x_hbm, i0_hbm, i1_hbm, i2_hbm, i3_hbm)

    plane_off = (jnp.arange(n_planes, dtype=jnp.int32) * n_rows)[None, :, None]
    idx = (dest_km.reshape(TOP_K, 1, t) + plane_off).reshape(1, TOP_K * n_src)
    return scatter_rows(xw.reshape(n_src, LANE), idx, idx, idx, idx).reshape(n_planes, n_rows, LANE)


def _sc_gather(yw, idx_flat):
    n_planes, n_rows, _ = yw.shape
    n = idx_flat.shape[0]
    assert n % SC_WINDOW == 0

    @functools.partial(pl.kernel, out_type=jax.ShapeDtypeStruct((n_planes * n, LANE), yw.dtype),
                       mesh=_sc_mesh(), scratch_types=[], name="moe_gather_sc")
    def gather_rows(y_hbm, i_hbm, o_hbm):
        def body(i_vmem, o_vmem):
            pltpu.sync_copy(y_hbm.at[i_vmem.at[0]], o_vmem)

        pltpu.emit_pipeline(
            body,
            grid=(n_planes * n // SC_WINDOW,),
            in_specs=[pl.BlockSpec((1, SC_WINDOW), lambda i: (0, i))],
            out_specs=[pl.BlockSpec((SC_WINDOW, LANE), lambda i: (i, 0))],
            core_axis_name=("c", "s"),
            dimension_semantics=(pltpu.PARALLEL,),
        )(i_hbm, o_hbm)

    plane_off = (jnp.arange(n_planes, dtype=jnp.int32) * n_rows)[:, None]
    idx = (idx_flat[None, :] + plane_off).reshape(1, n_planes * n)
    return gather_rows(yw.reshape(n_planes * n_rows, LANE), idx).reshape(n_planes, n, LANE)


def _combine_kernel(alpha, y0_ref, y1_ref, y2_ref, y3_ref, gate_ref, x_ref, g_ref, b_ref, o_ref):
    gate = gate_ref[...]
    acc = alpha * x_ref[...]
    for k, y_ref in enumerate((y0_ref, y1_ref, y2_ref, y3_ref)):
        words = jnp.concatenate([y_ref[p] for p in range(y_ref.shape[0])], axis=1)
        acc = acc + gate[:, k:k + 1] * _unpack_halves(words)
    o_ref[...] = _ln(acc, g_ref[...], b_ref[...])


def _combine(yg, gate_all, x_all, g, b, alpha, row_off):
    t, d = x_all.shape
    tc = yg.shape[1] // TOP_K
    tm = _pick_tile(tc, (512, 256, 128, 64))
    nt = tc // tm
    off = row_off // tm
    assert row_off % tm == 0
    y_spec = lambda k: pl.BlockSpec((yg.shape[0], tm, LANE), lambda i: (0, k * nt + i, 0))
    return pl.pallas_call(
        functools.partial(_combine_kernel, alpha),
        grid=(nt,),
        in_specs=[
            y_spec(0), y_spec(1), y_spec(2), y_spec(3),
            pl.BlockSpec((tm, LANE), lambda i: (off + i, 0)),
            pl.BlockSpec((tm, d), lambda i: (off + i, 0)),
            pl.BlockSpec((1, d), lambda i: (0, 0)),
            pl.BlockSpec((1, d), lambda i: (0, 0)),
        ],
        out_specs=pl.BlockSpec((tm, d), lambda i: (off + i, 0)),
        out_shape=jax.ShapeDtypeStruct((t, d), F32),
        input_output_aliases={5: 0},
        compiler_params=_cparams("arbitrary"),
        name="moe_combine_ln3",
    )(yg, yg, yg, yg, gate_all, x_all, g.reshape(1, d), b.reshape(1, d))


def _dest_kernel(ids_ref, start_ref, dest_ref, run_scr):
    i = pl.program_id(0)

    @pl.when(i == 0)
    def _():
        run_scr[...] = jnp.zeros_like(run_scr)

    ids = ids_ref[...].astype(F32)
    tm = ids.shape[0]
    lane = lax.broadcasted_iota(jnp.int32, ids.shape, 1).astype(F32)
    one_hots = [jnp.where(lane == ids[:, k:k + 1], 1.0, 0.0) for k in range(TOP_K)]
    multi_hot = one_hots[0] + one_hots[1] + one_hots[2] + one_hots[3]
    r = lax.broadcasted_iota(jnp.int32, (tm, tm), 0)
    c = lax.broadcasted_iota(jnp.int32, (tm, tm), 1)
    before = jnp.dot(jnp.where(c < r, 1.0, 0.0).astype(BF16), multi_hot.astype(BF16),
                     preferred_element_type=F32) + (run_scr[0:1, :] + start_ref[...])
    dest = jnp.zeros(ids.shape, F32)
    for k in range(TOP_K):
        dest = jnp.where(lane == k, jnp.sum(one_hots[k] * before, -1, keepdims=True), dest)
    dest_ref[...] = dest.astype(jnp.int32)
    run_scr[0:1, :] = run_scr[0:1, :] + jnp.sum(multi_hot, 0, keepdims=True)


def _route(ids_all, counts, n_experts):
    t = ids_all.shape[0]
    tk = t * TOP_K
    n_blocks = -(-tk // MOE_BM) + n_experts
    n_rows = n_blocks * MOE_BM
    assert n_rows < 2 ** 24
    padded = (counts + MOE_BM - 1) // MOE_BM * MOE_BM
    pad_end = jnp.cumsum(padded)
    pad_start = pad_end - padded
    start_row = jnp.zeros((1, LANE), F32).at[0, 0:n_experts].set(pad_start.astype(F32))
    tm = _pick_tile(t, (512, 256, 128, 64))
    tok = pl.BlockSpec((tm, LANE), lambda i: (i, 0))
    dest = pl.pallas_call(
        _dest_kernel,
        grid=(t // tm,),
        in_specs=[tok, pl.BlockSpec((1, LANE), lambda i: (0, 0))],
        out_specs=tok,
        out_shape=jax.ShapeDtypeStruct((t, LANE), jnp.int32),
        scratch_shapes=[pltpu.VMEM((SUBLANE, LANE), F32)],
        compiler_params=_cparams("arbitrary"),
        name="moe_dest",
    )(ids_all, start_row)
    dest_km = dest[:, 0:TOP_K].T.reshape(-1)
    blk = jnp.arange(n_blocks, dtype=jnp.int32)
    blk_e = jnp.minimum(jnp.sum((blk[:, None] * MOE_BM >= pad_end[None, :]).astype(jnp.int32), axis=1),
                        n_experts - 1)
    blk_valid = jnp.clip(counts[blk_e] - (blk * MOE_BM - pad_start[blk_e]), 0, MOE_BM).astype(jnp.int32)
    n_used = (pad_end[-1] // MOE_BM).astype(jnp.int32).reshape(1)
    return dest_km, n_rows, blk_e, blk_valid, n_used


def _rope_tables(pos):
    half = HEAD_DIM // 2
    inv = ROPE_THETA ** (-jnp.arange(half, dtype=F32) / half)
    ang = pos.astype(F32)[:, None] * inv[None, :]
    cos = jnp.cos(ang)
    sin = jnp.sin(ang)
    cos = jnp.concatenate([cos, cos, cos, cos], axis=1)
    sin = jnp.concatenate([-sin, sin, -sin, sin], axis=1)
    return cos, sin


def _permute_w_in(w):
    d = w.shape[0]
    o_b = 4 * GDN_W
    o_q = o_b + 2 * GDN_HEADS
    o_sc = o_q + SWA_W + 2 * SWA_KV_W
    ba = jnp.concatenate([w[:, o_b:o_q], jnp.zeros((d, LANE - 2 * GDN_HEADS), w.dtype)], axis=1)
    return jnp.concatenate([w[:, 0:o_b], w[:, o_sc:], w[:, o_q:o_sc], ba], axis=1)


def kernel(x_prompt, x_sample, state_gdn_S, state_gdn_conv, cache_swa_k, cache_swa_v, state_sconv, cache_mem_k, cache_mem_v, mem_prompt, ln_in_g, ln_in_b, w_in, gdn_conv_w, gdn_a_log, gdn_dt_bias, gdn_norm_g, swa_sinks, sc_conv_w, w_o, ln1_g, ln1_b, w_cq, w_mk, w_mv, w_co, ln2_g, ln2_b, w_router, b_router, w1, b1, w2, b2, ln3_g, ln3_b):
    bp, lp, d = x_prompt.shape
    bs, ls, _ = x_sample.shape
    depth = w_in.shape[0]
    n_experts = w_router.shape[2]
    n_mem = mem_prompt.shape[1]
    tp, ts = bp * lp, bs * ls
    t_all = tp + ts
    alpha = (2 * depth) ** 0.25
    assert cache_swa_k.shape[2] == WINDOW and d == MIX_W

    x_all = _ln_in(x_prompt.reshape(tp, d), x_sample.reshape(ts, d), ln_in_g, ln_in_b)
    xb_all = jnp.zeros((d // (2 * LANE), t_all, LANE), jnp.uint32)
    gate_all = jnp.zeros((t_all, LANE), F32)
    ids_all = jnp.zeros((t_all, LANE), jnp.int32)

    cos_p, sin_p = _rope_tables(jnp.arange(lp))
    cos_s, sin_s = _rope_tables(PAST_LEN + jnp.arange(ls))
    mem_flat = mem_prompt.reshape(bp * n_mem, d)
    zeros_cbuf = jnp.zeros((bp, GDN_CONV - 1, 3 * GDN_W), F32)
    zeros_s = jnp.zeros((bp, GDN_HEADS, HEAD_DIM, HEAD_DIM), F32)
    zeros_sbuf = jnp.zeros((bp, SC_CONV - 1, SC_W), F32)
    zeros_hist = jnp.zeros((bp, WINDOW, SWA_KV_W), F32)

    outs = {k: [] for k in ("p_S", "p_conv", "p_k", "p_v", "p_sc", "p_mk", "p_mv",
                            "s_S", "s_conv", "s_k", "s_v", "s_sc")}
    for l in range(depth):
        w_in_l = _permute_w_in(w_in[l]).astype(BF16)
        gp = jnp.zeros((SUBLANE, LANE), F32)
        gp = gp.at[0, BA_A:BA_A + GDN_HEADS].set(gdn_a_log[l]).at[1, BA_A:BA_A + GDN_HEADS].set(gdn_dt_bias[l])
        ng = jnp.tile(gdn_norm_g[l].reshape(1, HEAD_DIM), (1, GDN_HEADS))
        sinks = jnp.zeros((1, LANE), F32).at[0, 0:SWA_Q_HEADS].set(swa_sinks[l])
        wr = jnp.concatenate([w_router[l], jnp.zeros((d, LANE - n_experts), F32)], axis=1).astype(BF16)
        br = jnp.concatenate([b_router[l], jnp.full((LANE - n_experts,), NEG_BIG, F32)]).reshape(1, LANE)

        mkv = _proj(mem_flat, jnp.concatenate([w_mk[l], w_mv[l]], axis=1).astype(BF16), "mem_kv")
        mk_p = mkv[:, 0:MEM_W].reshape(bp, n_mem, MEM_W)
        mv_p = mkv[:, MEM_W:2 * MEM_W].reshape(bp, n_mem, MEM_W)
        outs["p_mk"].append(mk_p.reshape(bp, n_mem, MEM_HEADS, HEAD_DIM))
        outs["p_mv"].append(mv_p.reshape(bp, n_mem, MEM_HEADS, HEAD_DIM))

        h_all = _proj(x_all, w_in_l, "w_in")

        v_p = h_all[0:tp, COL_SWA_V:COL_SWA_V + SWA_KV_W].reshape(bp, lp, SWA_KV_W)[:, lp - WINDOW:]
        v_s = h_all[tp:, COL_SWA_V:COL_SWA_V + SWA_KV_W]
        h_all, c_p, s_p, sc_p = _gdn_sconv(h_all, 0, bp, lp, zeros_cbuf, zeros_s, zeros_sbuf,
                                           gdn_conv_w[l], sc_conv_w[l], gp, ng)
        h_all, c_s, s_s, sc_s = _gdn_sconv(h_all, tp, bs, ls, state_gdn_conv[l], state_gdn_S[l], state_sconv[l],
                                           gdn_conv_w[l], sc_conv_w[l], gp, ng)
        h_all, kr_p = _swa(h_all, 0, bp, lp, zeros_hist, zeros_hist, False, cos_p, sin_p, sinks)
        h_all, kr_s = _swa(h_all, tp, bs, ls, cache_swa_k[l].reshape(bs, WINDOW, SWA_KV_W),
                           cache_swa_v[l].reshape(bs, WINDOW, SWA_KV_W), True, cos_s, sin_s, sinks)
        outs["p_S"].append(s_p); outs["p_conv"].append(c_p); outs["p_sc"].append(sc_p)
        outs["s_S"].append(s_s); outs["s_conv"].append(c_s); outs["s_sc"].append(sc_s)
        outs["p_k"].append(kr_p.reshape(bp, lp, SWA_KV_W)[:, lp - WINDOW:].reshape(bp, WINDOW, SWA_KV_HEADS, HEAD_DIM))
        outs["p_v"].append(v_p.reshape(bp, WINDOW, SWA_KV_HEADS, HEAD_DIM))
        outs["s_k"].append(kr_s.reshape(bs, ls, SWA_KV_HEADS, HEAD_DIM))
        outs["s_v"].append(v_s.reshape(bs, ls, SWA_KV_HEADS, HEAD_DIM))

        wo = w_o[l].astype(BF16)
        wcq = w_cq[l].astype(BF16)
        wco = w_co[l].astype(BF16)
        x_all, xb_all, gate_all, ids_all, cnt_p = _memattn_router(
            x_all, xb_all, gate_all, ids_all, h_all, wo, ln1_g[l], ln1_b[l], 0, bp, lp, mk_p, mv_p,
            wcq, wco, ln2_g[l], ln2_b[l], wr, br, alpha)
        x_all, xb_all, gate_all, ids_all, cnt_s = _memattn_router(
            x_all, xb_all, gate_all, ids_all, h_all, wo, ln1_g[l], ln1_b[l], tp, bs, ls,
            cache_mem_k[l].reshape(bs, n_mem, MEM_W), cache_mem_v[l].reshape(bs, n_mem, MEM_W),
            wcq, wco, ln2_g[l], ln2_b[l], wr, br, alpha)

        counts = sum(c_.reshape(-1, SUBLANE, LANE)[:, 0, 0:n_experts].sum(0) for c_ in (cnt_p, cnt_s))
        dest_km, n_rows, blk_e, blk_valid, n_used = _route(ids_all, counts.astype(jnp.int32), n_experts)
        xs = _sc_dispatch(xb_all, dest_km, n_rows)
        ys = _experts(xs, blk_e, blk_valid, n_used, l, w1, b1, w2, b2)
        dest_tk = dest_km.reshape(TOP_K, t_all)
        for part in range(MOE_PARTS):
            lo = part * (t_all // MOE_PARTS)
            yg = _sc_gather(ys, dest_tk[:, lo:lo + t_all // MOE_PARTS].reshape(-1))
            x_all = _combine(yg, gate_all, x_all, ln3_g[l], ln3_b[l], alpha, lo)

    st = lambda k: jnp.stack(outs[k])
    return (x_all[0:tp].reshape(bp, lp, d), x_all[tp:].reshape(bs, ls, d),
            st("p_S"), st("p_conv"), st("p_k"), st("p_v"), st("p_sc"), st("p_mk"), st("p_mv"),
            st("s_S"), st("s_conv"), st("s_k"), st("s_v"), st("s_sc"))
```

```python
import functools

import jax
import jax.numpy as jnp
from jax import lax
from jax.experimental import pallas as pl
from jax.experimental.pallas import tpu as pltpu
from jax.experimental.pallas import tpu_sc as plsc

F32 = jnp.float32
BF16 = jnp.bfloat16

LANE = 128
SUBLANE = 8
VMEM_LIMIT_BYTES = 56 * 1024 * 1024

CHUNK = 64
CHUNK_SHIFT = 6
HEAD_DIM = 64
GDN_HEADS = 6
GDN_W = GDN_HEADS * HEAD_DIM
GDN_CONV = 4
SWA_Q_HEADS = 6
SWA_KV_HEADS = 2
SWA_W = SWA_Q_HEADS * HEAD_DIM
SWA_KV_W = SWA_KV_HEADS * HEAD_DIM
WINDOW = 128
ROPE_THETA = 10000.0
SC_W = 256
SC_CONV = 3
MEM_HEADS = 4
MEM_W = MEM_HEADS * HEAD_DIM
TOP_K = 4
SWIGLU_LIMIT = 7.0
SWIGLU_ALPHA = 1.702
PAST_LEN = 2048
LN_EPS = 1e-5
RMS_EPS = 1e-6
NEG_BIG = -1e30

COL_QKV = 0
COL_Z = 1152
COL_SC = 1536
COL_SWA = 2304
COL_SWA_V = COL_SWA + SWA_W + SWA_KV_W
COL_BA = COL_SWA + SWA_W + 2 * SWA_KV_W
P_PAD = 3072
BA_B = 0
BA_A = GDN_HEADS
MIX_W = 1024

MOE_BM = 512
SC_WINDOW = 128
MOE_PARTS = 2


def _cparams(*sem):
    return pltpu.CompilerParams(dimension_semantics=sem, vmem_limit_bytes=VMEM_LIMIT_BYTES)


def _pick_tile(n, candidates):
    for c in candidates:
        if n % c == 0:
            return c
    raise ValueError(f"no tile in {candidates} divides {n}")


def _ln(x, g, b):
    mu = jnp.mean(x, -1, keepdims=True)
    xc = x - mu
    var = jnp.mean(xc * xc, -1, keepdims=True)
    return xc * lax.rsqrt(var + LN_EPS) * g + b


def _dot(a, b):
    return jnp.dot(a.astype(BF16), b.astype(BF16), preferred_element_type=F32)


def _silu(x):
    return x * jax.nn.sigmoid(x)


def _pack_halves(x):
    w = x.shape[1] // 2
    lo = lax.bitcast_convert_type(x[:, 0:w].astype(BF16).astype(F32), jnp.uint32)
    hi = lax.bitcast_convert_type(x[:, w:2 * w].astype(BF16).astype(F32), jnp.uint32)
    return (hi & jnp.uint32(0xFFFF0000)) | (lo >> 16)


def _unpack_halves(p):
    lo = lax.bitcast_convert_type(p << 16, F32)
    hi = lax.bitcast_convert_type(p & jnp.uint32(0xFFFF0000), F32)
    return jnp.concatenate([lo, hi], axis=1)


def _ln_in_kernel(n_p_tiles, xp_ref, xs_ref, g_ref, b_ref, o_ref):
    i = pl.program_id(0)

    @pl.when(i < n_p_tiles)
    def _():
        o_ref[...] = _ln(xp_ref[...], g_ref[...], b_ref[...])

    @pl.when(i >= n_p_tiles)
    def _():
        o_ref[...] = _ln(xs_ref[...], g_ref[...], b_ref[...])


def _ln_in(xp, xs, g, b):
    tp, d = xp.shape
    ts = xs.shape[0]
    tm = _pick_tile(ts, (512, 256, 128, 64))
    assert tp % tm == 0
    npt, nst = tp // tm, ts // tm
    return pl.pallas_call(
        functools.partial(_ln_in_kernel, npt),
        grid=(npt + nst,),
        in_specs=[
            pl.BlockSpec((tm, d), lambda i: (jnp.minimum(i, npt - 1), 0)),
            pl.BlockSpec((tm, d), lambda i: (jnp.maximum(i - npt, 0), 0)),
            pl.BlockSpec((1, d), lambda i: (0, 0)),
            pl.BlockSpec((1, d), lambda i: (0, 0)),
        ],
        out_specs=pl.BlockSpec((tm, d), lambda i: (i, 0)),
        out_shape=jax.ShapeDtypeStruct((tp + ts, d), F32),
        compiler_params=_cparams("arbitrary"),
        name="ln_in",
    )(xp, xs, g.reshape(1, d), b.reshape(1, d))


def _proj_kernel(x_ref, w_ref, o_ref):
    o_ref[...] = _dot(x_ref[...], w_ref[...])


def _proj(x, w, name):
    t, k = x.shape
    n = w.shape[1]
    tm = _pick_tile(t, (512, 256, 128, 64))
    return pl.pallas_call(
        _proj_kernel,
        grid=(t // tm,),
        in_specs=[pl.BlockSpec((tm, k), lambda i: (i, 0)),
                  pl.BlockSpec((k, n), lambda i: (0, 0))],
        out_specs=pl.BlockSpec((tm, n), lambda i: (i, 0)),
        out_shape=jax.ShapeDtypeStruct((t, n), F32),
        compiler_params=_cparams("arbitrary"),
        name=name,
    )(x, w)


def _split2(x):
    hi = x.astype(BF16)
    lo = (x - hi.astype(F32)).astype(BF16)
    return hi, lo


def _split3(x):
    hi = x.astype(BF16)
    r1 = x - hi.astype(F32)
    mid = r1.astype(BF16)
    lo = (r1 - mid.astype(F32)).astype(BF16)
    return hi, mid, lo


def _sel_dot2(x, sel2):
    return jnp.dot(jnp.concatenate(_split2(x), axis=1), sel2, preferred_element_type=F32)


def _gdn_pipe_kernel(lt, one_tile_seqs, h_ref, cbuf_ref, s0_ref, sbuf_ref, cw_ref, scw_ref, gp_ref, ng_ref,
                     ho_ref, cnew_ref, snew_ref, scnew_ref, ext_scr, uext_scr, s_scr, o_scr, pb_scr, pf_scr, pc_scr):
    i = pl.program_id(1)
    n_chunks = lt // CHUNK
    n_pairs = GDN_HEADS // 2
    pad = SUBLANE

    def load_seq_state():
        ext_scr[0:pad, :] = jnp.concatenate(
            [jnp.zeros((pad - (GDN_CONV - 1), 3 * GDN_W), F32), cbuf_ref[0]], axis=0)
        uext_scr[0:pad, :] = jnp.concatenate(
            [jnp.zeros((pad - (SC_CONV - 1), SC_W), F32), sbuf_ref[0]], axis=0)
        zero_blk = jnp.zeros((HEAD_DIM, HEAD_DIM), F32)
        for p in range(n_pairs):
            s_scr[p] = jnp.concatenate(
                [jnp.concatenate([s0_ref[0, 2 * p], zero_blk], axis=1),
                 jnp.concatenate([zero_blk, s0_ref[0, 2 * p + 1]], axis=1)], axis=0)

    @pl.when(i == 0)
    def _():
        pb_scr[...] = jnp.zeros_like(pb_scr)
        pf_scr[...] = jnp.zeros_like(pf_scr)
        pc_scr[...] = jnp.zeros_like(pc_scr)
        if not one_tile_seqs:
            load_seq_state()

    if one_tile_seqs:
        load_seq_state()

    r384 = lax.broadcasted_iota(jnp.int32, (GDN_W, GDN_W), 0)
    c384 = lax.broadcasted_iota(jnp.int32, (GDN_W, GDN_W), 1)
    head_ones = jnp.where((r384 >> CHUNK_SHIFT) == (c384 >> CHUNK_SHIFT), 1.0, 0.0).astype(BF16)
    r128 = lax.broadcasted_iota(jnp.int32, (LANE, GDN_W), 0)
    c128 = lax.broadcasted_iota(jnp.int32, (LANE, GDN_W), 1) >> CHUNK_SHIFT
    exp_a = jnp.where(r128 == c128 + BA_A, 1.0, 0.0).astype(BF16)
    exp_b = jnp.where(r128 == c128 + BA_B, 1.0, 0.0).astype(BF16)
    exp_a3 = jnp.concatenate([exp_a, exp_a, exp_a], axis=0)
    exp_b2 = jnp.concatenate([exp_b, exp_b], axis=0)
    head_sum = lambda x: jnp.dot(x.astype(BF16), head_ones, preferred_element_type=F32)
    rc = lax.broadcasted_iota(jnp.int32, (CHUNK, CHUNK), 0)
    cc = lax.broadcasted_iota(jnp.int32, (CHUNK, CHUNK), 1)
    ltri = jnp.where(cc <= rc, 1.0, 0.0).astype(BF16)
    row = lax.broadcasted_iota(jnp.int32, (CHUNK, LANE), 0)
    lane = lax.broadcasted_iota(jnp.int32, (CHUNK, LANE), 1)
    col = lane & (HEAD_DIM - 1)
    left = lane < HEAD_DIM
    m_incl = col <= row
    m_strict = col < row
    m_diag = col == row
    eye = jnp.where(m_diag, 1.0, 0.0)
    m_base = ((row >> 1) == (col >> 1)) & m_strict
    lvl_masks = [((row >> (sh + 1)) == (col >> (sh + 1))) & ((row >> sh) != (col >> sh)) & m_strict
                 for sh in range(1, CHUNK_SHIFT)]
    r2 = lax.broadcasted_iota(jnp.int32, (LANE, LANE), 0)
    c2 = lax.broadcasted_iota(jnp.int32, (LANE, LANE), 1)
    m_bd = (r2 >> CHUNK_SHIFT) == (c2 >> CHUNK_SHIFT)
    lane_ba = lax.broadcasted_iota(jnp.int32, (CHUNK, LANE), 1)
    is_a = (lane_ba >= BA_A) & (lane_ba < BA_A + GDN_HEADS)

    def bd(x):
        zero = jnp.zeros_like(x)
        return jnp.concatenate([jnp.where(left, x, zero), jnp.where(left, zero, x)], axis=0)

    units = [(c, p) for c in range(n_chunks) for p in range(n_pairs)]
    rsl = lambda c: slice(c * CHUNK, (c + 1) * CHUNK)
    psl = lambda p: slice(p * LANE, (p + 1) * LANE)

    wr = i & 1
    rd = 1 - wr
    qn_b, kn_b, kbeta_b, qg_b, kb_b, vb_b, kt_b = (pb_scr[rd, j] for j in range(7))
    gam = pf_scr[rd, 0]
    z_gate = pf_scr[rd, 1]
    ho_ref[:, GDN_W:MIX_W] = pc_scr[rd]

    ext_scr[pad:pad + lt, :] = h_ref[:, COL_QKV:COL_QKV + 3 * GDN_W]
    uext_scr[pad:pad + lt, :] = (h_ref[:, COL_SC + SC_W:COL_SC + 2 * SC_W]
                                 * h_ref[:, COL_SC + 2 * SC_W:COL_SC + 3 * SC_W])
    qkv_chunks = {}

    def front_conv(c):
        r0 = c * CHUNK
        conv = cw_ref[0:1, :] * ext_scr[pl.ds(r0 + pad - 3, CHUNK), :]
        for j in range(1, GDN_CONV):
            conv = conv + cw_ref[j:j + 1, :] * ext_scr[pl.ds(r0 + pad - 3 + j, CHUNK), :]
        qkv_chunks[c] = _silu(conv)
        y = scw_ref[0:1, :] * uext_scr[pl.ds(r0 + pad - 2, CHUNK), :]
        for j in range(1, SC_CONV):
            y = y + scw_ref[j:j + 1, :] * uext_scr[pl.ds(r0 + pad - 2 + j, CHUNK), :]
        pc_scr[wr, rsl(c), :] = jnp.concatenate(
            [h_ref[rsl(c), GDN_W:2 * GDN_W], h_ref[rsl(c), COL_SC:COL_SC + SC_W] * y], axis=1)
        pf_scr[wr, 1, rsl(c), :] = _silu(h_ref[rsl(c), COL_Z:COL_Z + GDN_W])

    def front_norm(c):
        qkv = qkv_chunks.pop(c)
        q = qkv[:, 0:GDN_W]
        k = qkv[:, GDN_W:2 * GDN_W]
        ss = head_sum(jnp.concatenate([q * q, k * k], axis=0))
        qn = q * lax.rsqrt(ss[0:CHUNK] + RMS_EPS) * HEAD_DIM ** -0.5
        kn = k * lax.rsqrt(ss[CHUNK:2 * CHUNK] + RMS_EPS)
        qkv_chunks[c] = (qn, kn, qkv[:, 2 * GDN_W:3 * GDN_W])

    def front_gates(c):
        qn, kn, v = qkv_chunks.pop(c)
        ba = h_ref[rsl(c), COL_BA:COL_BA + LANE]
        beta_blk = jax.nn.sigmoid(ba)
        sp_in = ba + gp_ref[1:2, :]
        softplus = jnp.maximum(sp_in, 0.0) + jnp.log(1.0 + jnp.exp(-jnp.abs(sp_in)))
        g_blk = jnp.where(is_a, -jnp.exp(gp_ref[0:1, :]) * softplus, 0.0)
        gam3 = jnp.dot(ltri, jnp.concatenate(_split3(g_blk), axis=1), preferred_element_type=F32)
        gam_blk = gam3[:, 0:LANE] + gam3[:, LANE:2 * LANE] + gam3[:, 2 * LANE:3 * LANE]
        gam_c = jnp.dot(jnp.concatenate(_split3(gam_blk), axis=1), exp_a3, preferred_element_type=F32)
        g_exp = jnp.exp(gam_c)
        k_fac = jnp.exp(gam_c[CHUNK - 1:CHUNK, :] - gam_c)
        beta = _sel_dot2(beta_blk, exp_b2)
        kbeta = kn * beta
        for j, val in enumerate((qn, kn, kbeta, qn * g_exp, kbeta * g_exp, v * beta, kn * k_fac)):
            pb_scr[wr, j, rsl(c), :] = val.astype(BF16)
        pf_scr[wr, 0, rsl(c), :] = gam_c

    pieces = [functools.partial(f, c) for c in range(n_chunks) for f in (front_conv, front_norm, front_gates)]

    def emit(n=1):
        for _ in range(n):
            if pieces:
                pieces.pop(0)()

    a_b, t_inv, qd_b = {}, {}, {}
    for c, p in units:
        gc = gam[rsl(c), psl(p)]
        gr = jnp.sum(jnp.where(m_diag, gc, 0.0), axis=0, keepdims=True)
        dec = jnp.exp(jnp.where(m_incl, gc - gr, NEG_BIG))
        kq = lax.dot_general(jnp.concatenate([qn_b[rsl(c), psl(p)], kbeta_b[rsl(c), psl(p)]], axis=0),
                             bd(kn_b[rsl(c), psl(p)]), (((1,), (1,)), ((), ())), preferred_element_type=F32)
        qd_b[c, p] = (kq[0:CHUNK] * dec).astype(BF16)
        a_mat = jnp.where(m_strict, dec * kq[CHUNK:2 * CHUNK], 0.0)
        a_b[c, p] = a_mat.astype(BF16)
        t_inv[c, p] = eye - jnp.where(m_base, a_mat, 0.0)
    zero_b = jnp.zeros((CHUNK, LANE), BF16)
    for lm in lvl_masks:
        w_lvl = {u_: jnp.dot(jnp.where(lm, a_b[u_], zero_b), bd(t_inv[u_].astype(BF16)),
                             preferred_element_type=F32) for u_ in units}
        emit()
        for u_ in units:
            t_inv[u_] = t_inv[u_] - jnp.dot(t_inv[u_].astype(BF16), bd(w_lvl[u_].astype(BF16)),
                                            preferred_element_type=F32)
        emit()
    u0, w_b, kw_b, q_add, x_sols = {}, {}, {}, {}, {}
    for c, p in units:
        rhs = jnp.concatenate([bd(vb_b[rsl(c), psl(p)]), bd(kb_b[rsl(c), psl(p)])], axis=1)
        x_sols[c, p] = jnp.dot(t_inv[c, p].astype(BF16), rhs, preferred_element_type=F32)
    emit()
    for c, p in units:
        x_sol = x_sols[c, p]
        u0[c, p] = x_sol[:, 0:LANE]
        w_b[c, p] = x_sol[:, LANE:2 * LANE].astype(BF16)
        ktx = lax.dot_general(kt_b[rsl(c), psl(p)], x_sol.astype(BF16), (((0,), (0,)), ((), ())),
                              preferred_element_type=F32)
        q_add[c, p] = jnp.where(m_bd, ktx[:, 0:LANE], 0.0)
        kw_b[c, p] = jnp.where(m_bd, ktx[:, LANE:2 * LANE], 0.0).astype(BF16)
    emit()

    s_start = {}
    s_cur = [s_scr[p] for p in range(n_pairs)]
    for c in range(n_chunks):
        for p in range(n_pairs):
            s_bf = s_cur[p].astype(BF16)
            s_start[c, p] = s_bf
            e_last = jnp.exp(gam[(c + 1) * CHUNK - 1:(c + 1) * CHUNK, psl(p)])
            s_cur[p] = s_cur[p] * e_last - jnp.dot(kw_b[c, p], s_bf, preferred_element_type=F32) + q_add[c, p]
        emit()
    for p in range(n_pairs):
        s_scr[p] = s_cur[p]

    wqs = {u_: jnp.dot(jnp.concatenate([w_b[u_], qg_b[rsl(u_[0]), psl(u_[1])]], axis=0), s_start[u_],
                       preferred_element_type=F32) for u_ in units}
    emit(len(pieces))
    for c, p in units:
        u = u0[c, p] - wqs[c, p][0:CHUNK]
        o_scr[rsl(c), psl(p)] = wqs[c, p][CHUNK:2 * CHUNK] + jnp.dot(qd_b[c, p], bd(u.astype(BF16)),
                                                                      preferred_element_type=F32)
    o = o_scr[...]
    o_ms = head_sum(o * o) * (1.0 / HEAD_DIM)
    ho_ref[:, 0:GDN_W] = o * lax.rsqrt(o_ms + RMS_EPS) * ng_ref[...] * z_gate

    cnew_ref[0] = ext_scr[pl.ds(lt + pad - 3, 3), :]
    ext_scr[0:pad, :] = ext_scr[lt:lt + pad, :]
    scnew_ref[0] = uext_scr[pl.ds(lt + pad - 2, 2), :]
    uext_scr[0:pad, :] = uext_scr[lt:lt + pad, :]

    def store_state():
        for p in range(n_pairs):
            snew_ref[0, 2 * p] = s_scr[p, 0:HEAD_DIM, 0:HEAD_DIM]
            snew_ref[0, 2 * p + 1] = s_scr[p, HEAD_DIM:LANE, HEAD_DIM:LANE]

    if one_tile_seqs:
        store_state()
    else:
        pl.when(i == pl.num_programs(1) - 1)(store_state)


def _gdn_sconv(h_all, row_off, nb, seq, cbuf, s0, sbuf, cw, scw, gp, ng):
    lt = _pick_tile(seq, (256, 128, 64))
    nt = seq // lt
    off = row_off // lt
    assert row_off % lt == 0
    one_tile_seqs = nt == 1
    if one_tile_seqs:
        grid = (1, nb + 1)
        f_seq = lambda b, i: jnp.minimum(i, nb - 1)
        b_seq = lambda b, i: jnp.maximum(i - 1, 0)
        f_tile = b_tile = lambda b, i: 0
    else:
        grid = (nb, nt + 1)
        f_seq = b_seq = lambda b, i: b
        f_tile = lambda b, i: jnp.minimum(i, nt - 1)
        b_tile = lambda b, i: jnp.maximum(i - 1, 0)
    tok_in = lambda b, i: (off + f_seq(b, i) * nt + f_tile(b, i), 0)
    tok_out = lambda b, i: (off + b_seq(b, i) * nt + b_tile(b, i), 0)
    return pl.pallas_call(
        functools.partial(_gdn_pipe_kernel, lt, one_tile_seqs),
        grid=grid,
        in_specs=[
            pl.BlockSpec((lt, P_PAD), tok_in),
            pl.BlockSpec((1, GDN_CONV - 1, 3 * GDN_W), lambda b, i: (f_seq(b, i), 0, 0)),
            pl.BlockSpec((1, GDN_HEADS, HEAD_DIM, HEAD_DIM), lambda b, i: (b_seq(b, i), 0, 0, 0)),
            pl.BlockSpec((1, SC_CONV - 1, SC_W), lambda b, i: (f_seq(b, i), 0, 0)),
            pl.BlockSpec((GDN_CONV, 3 * GDN_W), lambda b, i: (0, 0)),
            pl.BlockSpec((SC_CONV, SC_W), lambda b, i: (0, 0)),
            pl.BlockSpec((SUBLANE, LANE), lambda b, i: (0, 0)),
            pl.BlockSpec((1, GDN_W), lambda b, i: (0, 0)),
        ],
        out_specs=[
            pl.BlockSpec((lt, MIX_W), tok_out),
            pl.BlockSpec((1, GDN_CONV - 1, 3 * GDN_W), lambda b, i: (f_seq(b, i), 0, 0)),
            pl.BlockSpec((1, GDN_HEADS, HEAD_DIM, HEAD_DIM), lambda b, i: (b_seq(b, i), 0, 0, 0)),
            pl.BlockSpec((1, SC_CONV - 1, SC_W), lambda b, i: (f_seq(b, i), 0, 0)),
        ],
        out_shape=[
            jax.ShapeDtypeStruct(h_all.shape, F32),
            jax.ShapeDtypeStruct((nb, GDN_CONV - 1, 3 * GDN_W), F32),
            jax.ShapeDtypeStruct((nb, GDN_HEADS, HEAD_DIM, HEAD_DIM), F32),
            jax.ShapeDtypeStruct((nb, SC_CONV - 1, SC_W), F32),
        ],
        scratch_shapes=[
            pltpu.VMEM((lt + SUBLANE, 3 * GDN_W), F32),
            pltpu.VMEM((lt + SUBLANE, SC_W), F32),
            pltpu.VMEM((GDN_HEADS // 2, LANE, LANE), F32),
            pltpu.VMEM((lt, GDN_W), F32),
            pltpu.VMEM((2, 7, lt, GDN_W), BF16),
            pltpu.VMEM((2, 2, lt, GDN_W), F32),
            pltpu.VMEM((2, lt, MIX_W - GDN_W), F32),
        ],
        input_output_aliases={0: 0},
        compiler_params=_cparams("arbitrary", "arbitrary"),
        name="gdn_sconv",
    )(h_all, cbuf, s0, sbuf, cw, scw, gp, ng)


def _rope(x, cos, sin_signed):
    w = x.shape[1]
    nrep = w // LANE
    if nrep > 1:
        cos = jnp.concatenate([cos] * nrep, axis=1)
        sin_signed = jnp.concatenate([sin_signed] * nrep, axis=1)
    lane = lax.broadcasted_iota(jnp.int32, x.shape, 1) & (HEAD_DIM - 1)
    half = HEAD_DIM // 2
    swapped = jnp.where(lane < half, pltpu.roll(x, w - half, 1), pltpu.roll(x, half, 1))
    return x * cos + swapped * sin_signed


def _swa_kernel(lt, hist_valid, h_ref, hk_ref, hv_ref, cos_ref, sin_ref, sink_ref,
                o_ref, kr_ref, kall_scr, vall_scr):
    i = pl.program_id(1)

    @pl.when(i == 0)
    def _():
        kall_scr[0:WINDOW, :] = hk_ref[0]
        vall_scr[0:WINDOW, :] = hv_ref[0]

    cos = cos_ref[...]
    sin = sin_ref[...]
    q = _rope(h_ref[:, 0:SWA_W], cos, sin)
    k = _rope(h_ref[:, SWA_W:SWA_W + SWA_KV_W], cos, sin)
    kr_ref[...] = k
    kall_scr[WINDOW:WINDOW + lt, :] = k
    vall_scr[WINDOW:WINDOW + lt, :] = h_ref[:, SWA_W + SWA_KV_W:SWA_W + 2 * SWA_KV_W]
    q_b = q.astype(BF16)
    kall = kall_scr[...].astype(BF16)
    vall = vall_scr[...].astype(BF16)
    kall_sw = pltpu.roll(kall_scr[...], HEAD_DIM, 1).astype(BF16)
    vall_sw = pltpu.roll(vall_scr[...], HEAD_DIM, 1).astype(BF16)

    band = WINDOW + CHUNK
    half = 2 * LANE
    n_chunks = lt // CHUNK
    n_pairs = SWA_Q_HEADS // 2
    left = lax.broadcasted_iota(jnp.int32, (band, LANE), 1) < HEAD_DIM
    zero_band = jnp.zeros((band, LANE), BF16)
    zero_pad = jnp.zeros((half - band, LANE), BF16)
    k_src = ((kall, kall_sw), (kall, kall), (kall_sw, kall))
    v_src = ((vall, vall_sw), (vall, vall), (vall_sw, vall))

    def pair_bd(top, bot, ks):
        return jnp.concatenate([jnp.where(left, top[ks], zero_band), zero_pad,
                                jnp.where(left, zero_band, bot[ks]), zero_pad], axis=0)

    kb = lax.broadcasted_iota(jnp.int32, (1, half), 1)
    hist_ok = jnp.logical_or(hist_valid, i > 0)
    fill = [jnp.where(kb == band, sink_ref[0:1, hq:hq + 1], NEG_BIG) for hq in range(SWA_Q_HEADS)]
    r512 = lax.broadcasted_iota(jnp.int32, (2 * half, LANE), 0)
    l512 = lax.broadcasted_iota(jnp.int32, (2 * half, LANE), 1)
    ones_bd = jnp.where((r512 < half) == (l512 < HEAD_DIM), 1.0, 0.0).astype(BF16)
    units = [(c, p) for c in range(n_chunks) for p in range(n_pairs)]
    scores = {}
    for c, p in units:
        ks = slice(c * CHUNK, c * CHUNK + band)
        kbd = pair_bd(k_src[p][0], k_src[p][1], ks)
        scores[c, p] = lax.dot_general(q_b[c * CHUNK:(c + 1) * CHUNK, p * LANE:(p + 1) * LANE], kbd,
                                       (((1,), (1,)), ((), ())), preferred_element_type=F32)
    probs = {}
    for c, p in units:
        valid = (kb < band) & (hist_ok | ((kb >> CHUNK_SHIFT) + c >= WINDOW // CHUNK))
        halves = []
        for e_ in range(2):
            s = jnp.where(valid, scores[c, p][:, e_ * half:(e_ + 1) * half] * HEAD_DIM ** -0.5, fill[2 * p + e_])
            halves.append(jnp.exp(s - jnp.max(s, -1, keepdims=True)).astype(BF16))
        probs[c, p] = jnp.concatenate(halves, axis=1)
    for c, p in units:
        ks = slice(c * CHUNK, c * CHUNK + band)
        rhs = jnp.concatenate([pair_bd(v_src[p][0], v_src[p][1], ks), ones_bd], axis=1)
        od = jnp.dot(probs[c, p], rhs, preferred_element_type=F32)
        o_ref[c * CHUNK:(c + 1) * CHUNK, p * LANE:(p + 1) * LANE] = od[:, 0:LANE] / od[:, LANE:2 * LANE]

    if lt >= WINDOW:
        kall_scr[0:WINDOW, :] = kall_scr[lt:lt + WINDOW, :]
        vall_scr[0:WINDOW, :] = vall_scr[lt:lt + WINDOW, :]


def _swa(h_all, row_off, nb, seq, hist_k, hist_v, hist_valid, cos, sin, sinks):
    lt = _pick_tile(seq, (256, 128, 64))
    nt = seq // lt
    assert nt == 1 or lt >= WINDOW
    off = row_off // lt
    assert row_off % lt == 0
    swa_blk_w = P_PAD - COL_SWA
    return pl.pallas_call(
        functools.partial(_swa_kernel, lt, hist_valid),
        grid=(nb, nt),
        in_specs=[
            pl.BlockSpec((lt, swa_blk_w), lambda b, i: (off + b * nt + i, COL_SWA // swa_blk_w)),
            pl.BlockSpec((1, WINDOW, SWA_KV_W), lambda b, i: (b, 0, 0)),
            pl.BlockSpec((1, WINDOW, SWA_KV_W), lambda b, i: (b, 0, 0)),
            pl.BlockSpec((lt, LANE), lambda b, i: (i, 0)),
            pl.BlockSpec((lt, LANE), lambda b, i: (i, 0)),
            pl.BlockSpec((1, LANE), lambda b, i: (0, 0)),
        ],
        out_specs=[
            pl.BlockSpec((lt, SWA_W), lambda b, i: (off + b * nt + i, 1)),
            pl.BlockSpec((lt, SWA_KV_W), lambda b, i: (b * nt + i, 0)),
        ],
        out_shape=[
            jax.ShapeDtypeStruct(h_all.shape, F32),
            jax.ShapeDtypeStruct((nb * seq, SWA_KV_W), F32),
        ],
        scratch_shapes=[
            pltpu.VMEM((lt + WINDOW, SWA_KV_W), F32),
            pltpu.VMEM((lt + WINDOW, SWA_KV_W), F32),
        ],
        input_output_aliases={0: 0},
        compiler_params=_cparams("arbitrary", "arbitrary"),
        name="swa",
    )(h_all, hist_k, hist_v, cos, sin, sinks)


def _memattn_kernel(alpha, x_ref, xb_in_ref, gate_in_ref, ids_in_ref, mix_ref, wo_ref, g1_ref, b1_ref,
                    mk_ref, mv_ref, wcq_ref, wco_ref, g_ref, b_ref, wr_ref, br_ref,
                    x_out_ref, xb_ref, gate_ref, ids_ref, cnt_ref):
    del xb_in_ref, gate_in_ref, ids_in_ref
    x = _ln(alpha * x_ref[...] + _dot(mix_ref[...], wo_ref[...]), g1_ref[...], b1_ref[...])
    q = _dot(x, wcq_ref[...])
    mk = mk_ref[0].astype(BF16)
    mv = mv_ref[0].astype(BF16)
    n_mem = mk.shape[0]
    q_b = q.astype(BF16)
    left = lax.broadcasted_iota(jnp.int32, (n_mem, LANE), 1) < HEAD_DIM
    zero_kv = jnp.zeros((n_mem, LANE), BF16)

    def pair_bd(x):
        return jnp.concatenate([jnp.where(left, x, zero_kv), jnp.where(left, zero_kv, x)], axis=0)

    pairs = range(MEM_HEADS // 2)
    scores = [lax.dot_general(q_b[:, p * LANE:(p + 1) * LANE], pair_bd(mk[:, p * LANE:(p + 1) * LANE]),
                              (((1,), (1,)), ((), ())), preferred_element_type=F32) for p in pairs]
    probs = []
    for p in pairs:
        halves = []
        for e_ in range(2):
            s = scores[p][:, e_ * n_mem:(e_ + 1) * n_mem] * HEAD_DIM ** -0.5
            halves.append(jnp.exp(s - jnp.max(s, -1, keepdims=True)).astype(BF16))
        probs.append(jnp.concatenate(halves, axis=1))
    r2 = lax.broadcasted_iota(jnp.int32, (2 * n_mem, LANE), 0)
    l2 = lax.broadcasted_iota(jnp.int32, (2 * n_mem, LANE), 1)
    ones_bd = jnp.where((r2 < n_mem) == (l2 < HEAD_DIM), 1.0, 0.0).astype(BF16)
    outs = []
    for p in pairs:
        rhs = jnp.concatenate([pair_bd(mv[:, p * LANE:(p + 1) * LANE]), ones_bd], axis=1)
        od = jnp.dot(probs[p], rhs, preferred_element_type=F32)
        outs.append(od[:, 0:LANE] / od[:, LANE:2 * LANE])
    o = _dot(jnp.concatenate(outs, axis=1), wco_ref[...])
    x2 = _ln(alpha * x + o, g_ref[...], b_ref[...])
    x_out_ref[...] = x2
    packed = _pack_halves(x2)
    for p in range(xb_ref.shape[0]):
        xb_ref[p] = packed[:, p * LANE:(p + 1) * LANE]

    logits = _dot(x2, wr_ref[...]) + br_ref[...]
    lane = lax.broadcasted_iota(jnp.int32, logits.shape, 1).astype(F32)
    vals, idxs = [], []
    for _ in range(TOP_K):
        m = jnp.max(logits, -1, keepdims=True)
        idx = jnp.min(jnp.where(logits == m, lane, float(LANE)), -1, keepdims=True)
        vals.append(m)
        idxs.append(idx)
        logits = jnp.where(lane == idx, 2 * NEG_BIG, logits)
    es = [jnp.exp(val - vals[0]) for val in vals]
    den = es[0] + es[1] + es[2] + es[3]
    gate = jnp.zeros(logits.shape, F32)
    ids = jnp.zeros(logits.shape, F32)
    for kk in range(TOP_K):
        gate = jnp.where(lane == kk, es[kk] / den, gate)
        ids = jnp.where(lane == kk, idxs[kk], ids)
    gate_ref[...] = gate
    ids_ref[...] = ids.astype(jnp.int32)
    picked = jnp.sum(jnp.where(logits == 2 * NEG_BIG, 1.0, 0.0), 0, keepdims=True)
    cnt_ref[...] = jnp.broadcast_to(picked, cnt_ref.shape)


def _memattn_router(x_all, xb_all, gate_all, ids_all, h_all, w_o, g1, b1, row_off, nb, seq, mk, mv, wcq, wco,
                    g, b, wr, br, alpha):
    t, d = x_all.shape
    tl = _pick_tile(seq, (512, 256, 128, 64))
    nt = seq // tl
    off = row_off // tl
    assert row_off % tl == 0
    n_mem = mk.shape[1]
    tok = lambda bb, i: (off + bb * nt + i, 0)
    const = lambda bb, i: (0, 0)
    return pl.pallas_call(
        functools.partial(_memattn_kernel, alpha),
        grid=(nb, nt),
        in_specs=[
            pl.BlockSpec((tl, d), tok),
            pl.BlockSpec(memory_space=pl.ANY),
            pl.BlockSpec(memory_space=pl.ANY),
            pl.BlockSpec(memory_space=pl.ANY),
            pl.BlockSpec((tl, MIX_W), tok),
            pl.BlockSpec((MIX_W, d), const),
            pl.BlockSpec((1, d), const),
            pl.BlockSpec((1, d), const),
            pl.BlockSpec((1, n_mem, MEM_W), lambda bb, i: (bb, 0, 0)),
            pl.BlockSpec((1, n_mem, MEM_W), lambda bb, i: (bb, 0, 0)),
            pl.BlockSpec((d, MEM_W), const),
            pl.BlockSpec((MEM_W, d), const),
            pl.BlockSpec((1, d), const),
            pl.BlockSpec((1, d), const),
            pl.BlockSpec((d, LANE), const),
            pl.BlockSpec((1, LANE), const),
        ],
        out_specs=[
            pl.BlockSpec((tl, d), tok),
            pl.BlockSpec((d // (2 * LANE), tl, LANE), lambda bb, i: (0, off + bb * nt + i, 0)),
            pl.BlockSpec((tl, LANE), tok),
            pl.BlockSpec((tl, LANE), tok),
            pl.BlockSpec((SUBLANE, LANE), lambda bb, i: (bb * nt + i, 0)),
        ],
        out_shape=[
            jax.ShapeDtypeStruct((t, d), F32),
            jax.ShapeDtypeStruct((d // (2 * LANE), t, LANE), jnp.uint32),
            jax.ShapeDtypeStruct((t, LANE), F32),
            jax.ShapeDtypeStruct((t, LANE), jnp.int32),
            jax.ShapeDtypeStruct((nb * nt * SUBLANE, LANE), F32),
        ],
        input_output_aliases={0: 0, 1: 1, 2: 2, 3: 3},
        compiler_params=_cparams("arbitrary", "arbitrary"),
        name="memattn_router",
    )(x_all, xb_all, gate_all, ids_all, h_all, w_o, g1.reshape(1, d), b1.reshape(1, d), mk, mv, wcq, wco,
      g.reshape(1, d), b.reshape(1, d), wr, br)


def _expert_kernel(d_ff, be_ref, bv_ref, nu_ref, xs_ref, w1_ref, b1_ref, w2_ref, b2_ref, ys_ref, w1b_scr, w2b_scr):
    i = pl.program_id(0)
    prev = be_ref[jnp.maximum(i - 1, 0)]
    new_expert = (i == 0) | (be_ref[i] != prev)

    @pl.when(new_expert & (i < nu_ref[0]))
    def _():
        w1b_scr[...] = w1_ref[0, 0].astype(BF16)
        w2b_scr[...] = w2_ref[0, 0].astype(BF16)

    @pl.when(i < nu_ref[0])
    def _():
        row = lax.broadcasted_iota(jnp.int32, (xs_ref.shape[1], 1), 0)
        words = jnp.concatenate([xs_ref[p] for p in range(xs_ref.shape[0])], axis=1)
        xs = _unpack_halves(jnp.where(row < bv_ref[i], words, jnp.zeros_like(words))).astype(BF16)
        hh = jnp.dot(xs, w1b_scr[...], preferred_element_type=F32) + b1_ref[0, 0]
        glu = jnp.minimum(hh[:, 0:d_ff], SWIGLU_LIMIT)
        lin = jnp.clip(hh[:, d_ff:2 * d_ff], -SWIGLU_LIMIT, SWIGLU_LIMIT)
        act = glu * jax.nn.sigmoid(SWIGLU_ALPHA * glu) * (lin + 1.0)
        y = jnp.dot(act.astype(BF16), w2b_scr[...], preferred_element_type=F32) + b2_ref[0, 0]
        packed = _pack_halves(y)
        for p in range(ys_ref.shape[0]):
            ys_ref[p] = packed[:, p * LANE:(p + 1) * LANE]


def _experts(xs, blk_e, blk_valid, n_used, layer, w1, b1, w2, b2):
    n_planes, n_rows, _ = xs.shape
    n_l, n_e, d, two_ff = w1.shape
    d_ff = two_ff // 2
    n_blocks = n_rows // MOE_BM
    row_blk = lambda i, be, bv, nu: (0, jnp.minimum(i, nu[0] - 1), 0)
    exp_blk = lambda i, be, bv, nu: (layer, be[jnp.minimum(i, nu[0] - 1)], 0, 0)
    return pl.pallas_call(
        functools.partial(_expert_kernel, d_ff),
        grid_spec=pltpu.PrefetchScalarGridSpec(
            num_scalar_prefetch=3,
            grid=(n_blocks,),
            in_specs=[
                pl.BlockSpec((n_planes, MOE_BM, LANE), row_blk),
                pl.BlockSpec((1, 1, d, two_ff), exp_blk),
                pl.BlockSpec((1, 1, 1, two_ff), exp_blk),
                pl.BlockSpec((1, 1, d_ff, d), exp_blk),
                pl.BlockSpec((1, 1, 1, d), exp_blk),
            ],
            out_specs=pl.BlockSpec((n_planes, MOE_BM, LANE), row_blk),
            scratch_shapes=[pltpu.VMEM((d, two_ff), BF16), pltpu.VMEM((d_ff, d), BF16)],
        ),
        out_shape=jax.ShapeDtypeStruct((n_planes, n_rows, LANE), jnp.uint32),
        compiler_params=_cparams("arbitrary"),
        name="moe_experts",
    )(blk_e, blk_valid, n_used, xs, w1, b1.reshape(n_l, n_e, 1, two_ff), w2, b2.reshape(n_l, n_e, 1, d))


def _sc_mesh():
    return plsc.VectorSubcoreMesh(core_axis_name="c", subcore_axis_name="s")


def _sc_dispatch(xw, dest_km, n_rows):
    n_planes, t, _ = xw.shape
    n_src = n_planes * t
    nw = n_src // SC_WINDOW
    assert t % SC_WINDOW == 0

    @functools.partial(pl.kernel, out_type=jax.ShapeDtypeStruct((n_planes * n_rows, LANE), xw.dtype),
                       mesh=_sc_mesh(), scratch_types=[], name="moe_dispatch_sc")
    def scatter_rows(x_hbm, i0_hbm, i1_hbm, i2_hbm, i3_hbm, o_hbm):
        def body(x_vmem, i0, i1, i2, i3):
            for idx in (i0, i1, i2, i3):
                pltpu.sync_copy(x_vmem, o_hbm.at[idx.at[0]])

        pltpu.emit_pipeline(
            body,
            grid=(nw,),
            in_specs=[pl.BlockSpec((SC_WINDOW, LANE), lambda i: (i, 0))]
            + [pl.BlockSpec((1, SC_WINDOW), functools.partial(lambda k, i: (0, k * nw + i), k))
               for k in range(TOP_K)],
            out_specs=[],
            core_axis_name=("c", "s"),
            dimension_semantics=(pltpu.PARALLEL,),
        )(x_hbm, i0_hbm, i1_hbm, i2_hbm, i3_hbm)

    plane_off = (jnp.arange(n_planes, dtype=jnp.int32) * n_rows)[None, :, None]
    idx = (dest_km.reshape(TOP_K, 1, t) + plane_off).reshape(1, TOP_K * n_src)
    return scatter_rows(xw.reshape(n_src, LANE), idx, idx, idx, idx).reshape(n_planes, n_rows, LANE)


def _sc_gather(yw, idx_flat):
    n_planes, n_rows, _ = yw.shape
    n = idx_flat.shape[0]
    assert n % SC_WINDOW == 0

    @functools.partial(pl.kernel, out_type=jax.ShapeDtypeStruct((n_planes * n, LANE), yw.dtype),
                       mesh=_sc_mesh(), scratch_types=[], name="moe_gather_sc")
    def gather_rows(y_hbm, i_hbm, o_hbm):
        def body(i_vmem, o_vmem):
            pltpu.sync_copy(y_hbm.at[i_vmem.at[0]], o_vmem)

        pltpu.emit_pipeline(
            body,
            grid=(n_planes * n // SC_WINDOW,),
            in_specs=[pl.BlockSpec((1, SC_WINDOW), lambda i: (0, i))],
            out_specs=[pl.BlockSpec((SC_WINDOW, LANE), lambda i: (i, 0))],
            core_axis_name=("c", "s"),
            dimension_semantics=(pltpu.PARALLEL,),
        )(i_hbm, o_hbm)

    plane_off = (jnp.arange(n_planes, dtype=jnp.int32) * n_rows)[:, None]
    idx = (idx_flat[None, :] + plane_off).reshape(1, n_planes * n)
    return gather_rows(yw.reshape(n_planes * n_rows, LANE), idx).reshape(n_planes, n, LANE)


def _combine_kernel(alpha, y0_ref, y1_ref, y2_ref, y3_ref, gate_ref, x_ref, g_ref, b_ref, o_ref):
    gate = gate_ref[...]
    acc = alpha * x_ref[...]
    for k, y_ref in enumerate((y0_ref, y1_ref, y2_ref, y3_ref)):
        words = jnp.concatenate([y_ref[p] for p in range(y_ref.shape[0])], axis=1)
        acc = acc + gate[:, k:k + 1] * _unpack_halves(words)
    o_ref[...] = _ln(acc, g_ref[...], b_ref[...])


def _combine(yg, gate_all, x_all, g, b, alpha, row_off):
    t, d = x_all.shape
    tc = yg.shape[1] // TOP_K
    tm = _pick_tile(tc, (512, 256, 128, 64))
    nt = tc // tm
    off = row_off // tm
    assert row_off % tm == 0
    y_spec = lambda k: pl.BlockSpec((yg.shape[0], tm, LANE), lambda i: (0, k * nt + i, 0))
    return pl.pallas_call(
        functools.partial(_combine_kernel, alpha),
        grid=(nt,),
        in_specs=[
            y_spec(0), y_spec(1), y_spec(2), y_spec(3),
            pl.BlockSpec((tm, LANE), lambda i: (off + i, 0)),
            pl.BlockSpec((tm, d), lambda i: (off + i, 0)),
            pl.BlockSpec((1, d), lambda i: (0, 0)),
            pl.BlockSpec((1, d), lambda i: (0, 0)),
        ],
        out_specs=pl.BlockSpec((tm, d), lambda i: (off + i, 0)),
        out_shape=jax.ShapeDtypeStruct((t, d), F32),
        input_output_aliases={5: 0},
        compiler_params=_cparams("arbitrary"),
        name="moe_combine_ln3",
    )(yg, yg, yg, yg, gate_all, x_all, g.reshape(1, d), b.reshape(1, d))


def _dest_kernel(ids_ref, start_ref, dest_ref, run_scr):
    i = pl.program_id(0)

    @pl.when(i == 0)
    def _():
        run_scr[...] = jnp.zeros_like(run_scr)

    ids = ids_ref[...].astype(F32)
    tm = ids.shape[0]
    lane = lax.broadcasted_iota(jnp.int32, ids.shape, 1).astype(F32)
    one_hots = [jnp.where(lane == ids[:, k:k + 1], 1.0, 0.0) for k in range(TOP_K)]
    multi_hot = one_hots[0] + one_hots[1] + one_hots[2] + one_hots[3]
    r = lax.broadcasted_iota(jnp.int32, (tm, tm), 0)
    c = lax.broadcasted_iota(jnp.int32, (tm, tm), 1)
    before = jnp.dot(jnp.where(c < r, 1.0, 0.0).astype(BF16), multi_hot.astype(BF16),
                     preferred_element_type=F32) + (run_scr[0:1, :] + start_ref[...])
    dest = jnp.zeros(ids.shape, F32)
    for k in range(TOP_K):
        dest = jnp.where(lane == k, jnp.sum(one_hots[k] * before, -1, keepdims=True), dest)
    dest_ref[...] = dest.astype(jnp.int32)
    run_scr[0:1, :] = run_scr[0:1, :] + jnp.sum(multi_hot, 0, keepdims=True)


def _route(ids_all, counts, n_experts):
    t = ids_all.shape[0]
    tk = t * TOP_K
    n_blocks = -(-tk // MOE_BM) + n_experts
    n_rows = n_blocks * MOE_BM
    assert n_rows < 2 ** 24
    padded = (counts + MOE_BM - 1) // MOE_BM * MOE_BM
    pad_end = jnp.cumsum(padded)
    pad_start = pad_end - padded
    start_row = jnp.zeros((1, LANE), F32).at[0, 0:n_experts].set(pad_start.astype(F32))
    tm = _pick_tile(t, (512, 256, 128, 64))
    tok = pl.BlockSpec((tm, LANE), lambda i: (i, 0))
    dest = pl.pallas_call(
        _dest_kernel,
        grid=(t // tm,),
        in_specs=[tok, pl.BlockSpec((1, LANE), lambda i: (0, 0))],
        out_specs=tok,
        out_shape=jax.ShapeDtypeStruct((t, LANE), jnp.int32),
        scratch_shapes=[pltpu.VMEM((SUBLANE, LANE), F32)],
        compiler_params=_cparams("arbitrary"),
        name="moe_dest",
    )(ids_all, start_row)
    dest_km = dest[:, 0:TOP_K].T.reshape(-1)
    blk = jnp.arange(n_blocks, dtype=jnp.int32)
    blk_e = jnp.minimum(jnp.sum((blk[:, None] * MOE_BM >= pad_end[None, :]).astype(jnp.int32), axis=1),
                        n_experts - 1)
    blk_valid = jnp.clip(counts[blk_e] - (blk * MOE_BM - pad_start[blk_e]), 0, MOE_BM).astype(jnp.int32)
    n_used = (pad_end[-1] // MOE_BM).astype(jnp.int32).reshape(1)
    return dest_km, n_rows, blk_e, blk_valid, n_used


def _rope_tables(pos):
    half = HEAD_DIM // 2
    inv = ROPE_THETA ** (-jnp.arange(half, dtype=F32) / half)
    ang = pos.astype(F32)[:, None] * inv[None, :]
    cos = jnp.cos(ang)
    sin = jnp.sin(ang)
    cos = jnp.concatenate([cos, cos, cos, cos], axis=1)
    sin = jnp.concatenate([-sin, sin, -sin, sin], axis=1)
    return cos, sin


def _permute_w_in(w):
    d = w.shape[0]
    o_b = 4 * GDN_W
    o_q = o_b + 2 * GDN_HEADS
    o_sc = o_q + SWA_W + 2 * SWA_KV_W
    ba = jnp.concatenate([w[:, o_b:o_q], jnp.zeros((d, LANE - 2 * GDN_HEADS), w.dtype)], axis=1)
    return jnp.concatenate([w[:, 0:o_b], w[:, o_sc:], w[:, o_q:o_sc], ba], axis=1)


def kernel(x_prompt, x_sample, state_gdn_S, state_gdn_conv, cache_swa_k, cache_swa_v, state_sconv, cache_mem_k, cache_mem_v, mem_prompt, ln_in_g, ln_in_b, w_in, gdn_conv_w, gdn_a_log, gdn_dt_bias, gdn_norm_g, swa_sinks, sc_conv_w, w_o, ln1_g, ln1_b, w_cq, w_mk, w_mv, w_co, ln2_g, ln2_b, w_router, b_router, w1, b1, w2, b2, ln3_g, ln3_b):
    bp, lp, d = x_prompt.shape
    bs, ls, _ = x_sample.shape
    depth = w_in.shape[0]
    n_experts = w_router.shape[2]
    n_mem = mem_prompt.shape[1]
    tp, ts = bp * lp, bs * ls
    t_all = tp + ts
    alpha = (2 * depth) ** 0.25
    assert cache_swa_k.shape[2] == WINDOW and d == MIX_W

    x_all = _ln_in(x_prompt.reshape(tp, d), x_sample.reshape(ts, d), ln_in_g, ln_in_b)
    xb_all = jnp.zeros((d // (2 * LANE), t_all, LANE), jnp.uint32)
    gate_all = jnp.zeros((t_all, LANE), F32)
    ids_all = jnp.zeros((t_all, LANE), jnp.int32)

    cos_p, sin_p = _rope_tables(jnp.arange(lp))
    cos_s, sin_s = _rope_tables(PAST_LEN + jnp.arange(ls))
    mem_flat = mem_prompt.reshape(bp * n_mem, d)
    zeros_cbuf = jnp.zeros((bp, GDN_CONV - 1, 3 * GDN_W), F32)
    zeros_s = jnp.zeros((bp, GDN_HEADS, HEAD_DIM, HEAD_DIM), F32)
    zeros_sbuf = jnp.zeros((bp, SC_CONV - 1, SC_W), F32)
    zeros_hist = jnp.zeros((bp, WINDOW, SWA_KV_W), F32)

    outs = {k: [] for k in ("p_S", "p_conv", "p_k", "p_v", "p_sc", "p_mk", "p_mv",
                            "s_S", "s_conv", "s_k", "s_v", "s_sc")}
    for l in range(depth):
        w_in_l = _permute_w_in(w_in[l]).astype(BF16)
        gp = jnp.zeros((SUBLANE, LANE), F32)
        gp = gp.at[0, BA_A:BA_A + GDN_HEADS].set(gdn_a_log[l]).at[1, BA_A:BA_A + GDN_HEADS].set(gdn_dt_bias[l])
        ng = jnp.tile(gdn_norm_g[l].reshape(1, HEAD_DIM), (1, GDN_HEADS))
        sinks = jnp.zeros((1, LANE), F32).at[0, 0:SWA_Q_HEADS].set(swa_sinks[l])
        wr = jnp.concatenate([w_router[l], jnp.zeros((d, LANE - n_experts), F32)], axis=1).astype(BF16)
        br = jnp.concatenate([b_router[l], jnp.full((LANE - n_experts,), NEG_BIG, F32)]).reshape(1, LANE)

        mkv = _proj(mem_flat, jnp.concatenate([w_mk[l], w_mv[l]], axis=1).astype(BF16), "mem_kv")
        mk_p = mkv[:, 0:MEM_W].reshape(bp, n_mem, MEM_W)
        mv_p = mkv[:, MEM_W:2 * MEM_W].reshape(bp, n_mem, MEM_W)
        outs["p_mk"].append(mk_p.reshape(bp, n_mem, MEM_HEADS, HEAD_DIM))
        outs["p_mv"].append(mv_p.reshape(bp, n_mem, MEM_HEADS, HEAD_DIM))

        h_all = _proj(x_all, w_in_l, "w_in")

        v_p = h_all[0:tp, COL_SWA_V:COL_SWA_V + SWA_KV_W].reshape(bp, lp, SWA_KV_W)[:, lp - WINDOW:]
        v_s = h_all[tp:, COL_SWA_V:COL_SWA_V + SWA_KV_W]
        h_all, c_p, s_p, sc_p = _gdn_sconv(h_all, 0, bp, lp, zeros_cbuf, zeros_s, zeros_sbuf,
                                           gdn_conv_w[l], sc_conv_w[l], gp, ng)
        h_all, c_s, s_s, sc_s = _gdn_sconv(h_all, tp, bs, ls, state_gdn_conv[l], state_gdn_S[l], state_sconv[l],
                                           gdn_conv_w[l], sc_conv_w[l], gp, ng)
        h_all, kr_p = _swa(h_all, 0, bp, lp, zeros_hist, zeros_hist, False, cos_p, sin_p, sinks)
        h_all, kr_s = _swa(h_all, tp, bs, ls, cache_swa_k[l].reshape(bs, WINDOW, SWA_KV_W),
                           cache_swa_v[l].reshape(bs, WINDOW, SWA_KV_W), True, cos_s, sin_s, sinks)
        outs["p_S"].append(s_p); outs["p_conv"].append(c_p); outs["p_sc"].append(sc_p)
        outs["s_S"].append(s_s); outs["s_conv"].append(c_s); outs["s_sc"].append(sc_s)
        outs["p_k"].append(kr_p.reshape(bp, lp, SWA_KV_W)[:, lp - WINDOW:].reshape(bp, WINDOW, SWA_KV_HEADS, HEAD_DIM))
        outs["p_v"].append(v_p.reshape(bp, WINDOW, SWA_KV_HEADS, HEAD_DIM))
        outs["s_k"].append(kr_s.reshape(bs, ls, SWA_KV_HEADS, HEAD_DIM))
        outs["s_v"].append(v_s.reshape(bs, ls, SWA_KV_HEADS, HEAD_DIM))

        wo = w_o[l].astype(BF16)
        wcq = w_cq[l].astype(BF16)
        wco = w_co[l].astype(BF16)
        x_all, xb_all, gate_all, ids_all, cnt_p = _memattn_router(
            x_all, xb_all, gate_all, ids_all, h_all, wo, ln1_g[l], ln1_b[l], 0, bp, lp, mk_p, mv_p,
            wcq, wco, ln2_g[l], ln2_b[l], wr, br, alpha)
        x_all, xb_all, gate_all, ids_all, cnt_s = _memattn_router(
            x_all, xb_all, gate_all, ids_all, h_all, wo, ln1_g[l], ln1_b[l], tp, bs, ls,
            cache_mem_k[l].reshape(bs, n_mem, MEM_W), cache_mem_v[l].reshape(bs, n_mem, MEM_W),
            wcq, wco, ln2_g[l], ln2_b[l], wr, br, alpha)

        counts = sum(c_.reshape(-1, SUBLANE, LANE)[:, 0, 0:n_experts].sum(0) for c_ in (cnt_p, cnt_s))
        dest_km, n_rows, blk_e, blk_valid, n_used = _route(ids_all, counts.astype(jnp.int32), n_experts)
        xs = _sc_dispatch(xb_all, dest_km, n_rows)
        ys = _experts(xs, blk_e, blk_valid, n_used, l, w1, b1, w2, b2)
        dest_tk = dest_km.reshape(TOP_K, t_all)
        for part in range(MOE_PARTS):
            lo = part * (t_all // MOE_PARTS)
            yg = _sc_gather(ys, dest_tk[:, lo:lo + t_all // MOE_PARTS].reshape(-1))
            x_all = _combine(yg, gate_all, x_all, ln3_g[l], ln3_b[l], alpha, lo)

    st = lambda k: jnp.stack(outs[k])
    return (x_all[0:tp].reshape(bp, lp, d), x_all[tp:].reshape(bs, ls, d),
            st("p_S"), st("p_conv"), st("p_k"), st("p_v"), st("p_sc"), st("p_mk"), st("p_mv"),
            st("s_S"), st("s_conv"), st("s_k"), st("s_v"), st("s_sc"))
```

```python
import functools

import jax
import jax.numpy as jnp
from jax import lax
from jax.experimental import pallas as pl
from jax.experimental.pallas import tpu as pltpu
from jax.experimental.pallas import tpu_sc as plsc

F32 = jnp.float32
BF16 = jnp.bfloat16

LANE = 128
SUBLANE = 8
VMEM_LIMIT_BYTES = 56 * 1024 * 1024

CHUNK = 64
CHUNK_SHIFT = 6
HEAD_DIM = 64
GDN_HEADS = 6
GDN_W = GDN_HEADS * HEAD_DIM
GDN_CONV = 4
SWA_Q_HEADS = 6
SWA_KV_HEADS = 2
SWA_W = SWA_Q_HEADS * HEAD_DIM
SWA_KV_W = SWA_KV_HEADS * HEAD_DIM
WINDOW = 128
ROPE_THETA = 10000.0
SC_W = 256
SC_CONV = 3
MEM_HEADS = 4
MEM_W = MEM_HEADS * HEAD_DIM
TOP_K = 4
SWIGLU_LIMIT = 7.0
SWIGLU_ALPHA = 1.702
PAST_LEN = 2048
LN_EPS = 1e-5
RMS_EPS = 1e-6
NEG_BIG = -1e30

COL_QKV = 0
COL_Z = 1152
COL_SC = 1536
COL_SWA = 2304
COL_SWA_V = COL_SWA + SWA_W + SWA_KV_W
COL_BA = COL_SWA + SWA_W + 2 * SWA_KV_W
P_PAD = 3072
BA_B = 0
BA_A = GDN_HEADS
MIX_W = 1024

MOE_BM = 512
SC_WINDOW = 128
MOE_PARTS = 3


def _cparams(*sem):
    return pltpu.CompilerParams(dimension_semantics=sem, vmem_limit_bytes=VMEM_LIMIT_BYTES)


def _pick_tile(n, candidates):
    for c in candidates:
        if n % c == 0:
            return c
    raise ValueError(f"no tile in {candidates} divides {n}")


def _ln(x, g, b):
    mu = jnp.mean(x, -1, keepdims=True)
    xc = x - mu
    var = jnp.mean(xc * xc, -1, keepdims=True)
    return xc * lax.rsqrt(var + LN_EPS) * g + b


def _dot(a, b):
    return jnp.dot(a.astype(BF16), b.astype(BF16), preferred_element_type=F32)


def _silu(x):
    return x * jax.nn.sigmoid(x)


def _pack_halves(x):
    w = x.shape[1] // 2
    lo = lax.bitcast_convert_type(x[:, 0:w].astype(BF16).astype(F32), jnp.uint32)
    hi = lax.bitcast_convert_type(x[:, w:2 * w].astype(BF16).astype(F32), jnp.uint32)
    return (hi & jnp.uint32(0xFFFF0000)) | (lo >> 16)


def _unpack_halves(p):
    lo = lax.bitcast_convert_type(p << 16, F32)
    hi = lax.bitcast_convert_type(p & jnp.uint32(0xFFFF0000), F32)
    return jnp.concatenate([lo, hi], axis=1)


def _ln_in_kernel(n_p_tiles, xp_ref, xs_ref, g_ref, b_ref, o_ref):
    i = pl.program_id(0)

    @pl.when(i < n_p_tiles)
    def _():
        o_ref[...] = _ln(xp_ref[...], g_ref[...], b_ref[...])

    @pl.when(i >= n_p_tiles)
    def _():
        o_ref[...] = _ln(xs_ref[...], g_ref[...], b_ref[...])


def _ln_in(xp, xs, g, b):
    tp, d = xp.shape
    ts = xs.shape[0]
    tm = _pick_tile(ts, (512, 256, 128, 64))
    assert tp % tm == 0
    npt, nst = tp // tm, ts // tm
    return pl.pallas_call(
        functools.partial(_ln_in_kernel, npt),
        grid=(npt + nst,),
        in_specs=[
            pl.BlockSpec((tm, d), lambda i: (jnp.minimum(i, npt - 1), 0)),
            pl.BlockSpec((tm, d), lambda i: (jnp.maximum(i - npt, 0), 0)),
            pl.BlockSpec((1, d), lambda i: (0, 0)),
            pl.BlockSpec((1, d), lambda i: (0, 0)),
        ],
        out_specs=pl.BlockSpec((tm, d), lambda i: (i, 0)),
        out_shape=jax.ShapeDtypeStruct((tp + ts, d), F32),
        compiler_params=_cparams("arbitrary"),
        name="ln_in",
    )(xp, xs, g.reshape(1, d), b.reshape(1, d))


def _proj_kernel(x_ref, w_ref, o_ref):
    o_ref[...] = _dot(x_ref[...], w_ref[...])


def _proj(x, w, name):
    t, k = x.shape
    n = w.shape[1]
    tm = _pick_tile(t, (512, 256, 128, 64))
    return pl.pallas_call(
        _proj_kernel,
        grid=(t // tm,),
        in_specs=[pl.BlockSpec((tm, k), lambda i: (i, 0)),
                  pl.BlockSpec((k, n), lambda i: (0, 0))],
        out_specs=pl.BlockSpec((tm, n), lambda i: (i, 0)),
        out_shape=jax.ShapeDtypeStruct((t, n), F32),
        compiler_params=_cparams("arbitrary"),
        name=name,
    )(x, w)


def _split2(x):
    hi = x.astype(BF16)
    lo = (x - hi.astype(F32)).astype(BF16)
    return hi, lo


def _split3(x):
    hi = x.astype(BF16)
    r1 = x - hi.astype(F32)
    mid = r1.astype(BF16)
    lo = (r1 - mid.astype(F32)).astype(BF16)
    return hi, mid, lo


def _sel_dot2(x, sel2):
    return jnp.dot(jnp.concatenate(_split2(x), axis=1), sel2, preferred_element_type=F32)


def _gdn_pipe_kernel(lt, one_tile_seqs, h_ref, cbuf_ref, s0_ref, sbuf_ref, cw_ref, scw_ref, gp_ref, ng_ref,
                     ho_ref, cnew_ref, snew_ref, scnew_ref, ext_scr, uext_scr, s_scr, o_scr, pb_scr, pf_scr, pc_scr):
    i = pl.program_id(1)
    n_chunks = lt // CHUNK
    n_pairs = GDN_HEADS // 2
    pad = SUBLANE

    def load_seq_state():
        ext_scr[0:pad, :] = jnp.concatenate(
            [jnp.zeros((pad - (GDN_CONV - 1), 3 * GDN_W), F32), cbuf_ref[0]], axis=0)
        uext_scr[0:pad, :] = jnp.concatenate(
            [jnp.zeros((pad - (SC_CONV - 1), SC_W), F32), sbuf_ref[0]], axis=0)
        zero_blk = jnp.zeros((HEAD_DIM, HEAD_DIM), F32)
        for p in range(n_pairs):
            s_scr[p] = jnp.concatenate(
                [jnp.concatenate([s0_ref[0, 2 * p], zero_blk], axis=1),
                 jnp.concatenate([zero_blk, s0_ref[0, 2 * p + 1]], axis=1)], axis=0)

    @pl.when(i == 0)
    def _():
        pb_scr[...] = jnp.zeros_like(pb_scr)
        pf_scr[...] = jnp.zeros_like(pf_scr)
        pc_scr[...] = jnp.zeros_like(pc_scr)
        if not one_tile_seqs:
            load_seq_state()

    if one_tile_seqs:
        load_seq_state()

    r384 = lax.broadcasted_iota(jnp.int32, (GDN_W, GDN_W), 0)
    c384 = lax.broadcasted_iota(jnp.int32, (GDN_W, GDN_W), 1)
    head_ones = jnp.where((r384 >> CHUNK_SHIFT) == (c384 >> CHUNK_SHIFT), 1.0, 0.0).astype(BF16)
    r128 = lax.broadcasted_iota(jnp.int32, (LANE, GDN_W), 0)
    c128 = lax.broadcasted_iota(jnp.int32, (LANE, GDN_W), 1) >> CHUNK_SHIFT
    exp_a = jnp.where(r128 == c128 + BA_A, 1.0, 0.0).astype(BF16)
    exp_b = jnp.where(r128 == c128 + BA_B, 1.0, 0.0).astype(BF16)
    exp_a3 = jnp.concatenate([exp_a, exp_a, exp_a], axis=0)
    exp_b2 = jnp.concatenate([exp_b, exp_b], axis=0)
    head_sum = lambda x: jnp.dot(x.astype(BF16), head_ones, preferred_element_type=F32)
    rc = lax.broadcasted_iota(jnp.int32, (CHUNK, CHUNK), 0)
    cc = lax.broadcasted_iota(jnp.int32, (CHUNK, CHUNK), 1)
    ltri = jnp.where(cc <= rc, 1.0, 0.0).astype(BF16)
    row = lax.broadcasted_iota(jnp.int32, (CHUNK, LANE), 0)
    lane = lax.broadcasted_iota(jnp.int32, (CHUNK, LANE), 1)
    col = lane & (HEAD_DIM - 1)
    left = lane < HEAD_DIM
    m_incl = col <= row
    m_strict = col < row
    m_diag = col == row
    eye = jnp.where(m_diag, 1.0, 0.0)
    m_base = ((row >> 1) == (col >> 1)) & m_strict
    lvl_masks = [((row >> (sh + 1)) == (col >> (sh + 1))) & ((row >> sh) != (col >> sh)) & m_strict
                 for sh in range(1, CHUNK_SHIFT)]
    r2 = lax.broadcasted_iota(jnp.int32, (LANE, LANE), 0)
    c2 = lax.broadcasted_iota(jnp.int32, (LANE, LANE), 1)
    m_bd = (r2 >> CHUNK_SHIFT) == (c2 >> CHUNK_SHIFT)
    lane_ba = lax.broadcasted_iota(jnp.int32, (CHUNK, LANE), 1)
    is_a = (lane_ba >= BA_A) & (lane_ba < BA_A + GDN_HEADS)

    def bd(x):
        zero = jnp.zeros_like(x)
        return jnp.concatenate([jnp.where(left, x, zero), jnp.where(left, zero, x)], axis=0)

    units = [(c, p) for c in range(n_chunks) for p in range(n_pairs)]
    rsl = lambda c: slice(c * CHUNK, (c + 1) * CHUNK)
    psl = lambda p: slice(p * LANE, (p + 1) * LANE)

    wr = i & 1
    rd = 1 - wr
    qn_b, kn_b, kbeta_b, qg_b, kb_b, vb_b, kt_b = (pb_scr[rd, j] for j in range(7))
    gam = pf_scr[rd, 0]
    z_gate = pf_scr[rd, 1]
    ho_ref[:, GDN_W:MIX_W] = pc_scr[rd]

    ext_scr[pad:pad + lt, :] = h_ref[:, COL_QKV:COL_QKV + 3 * GDN_W]
    uext_scr[pad:pad + lt, :] = (h_ref[:, COL_SC + SC_W:COL_SC + 2 * SC_W]
                                 * h_ref[:, COL_SC + 2 * SC_W:COL_SC + 3 * SC_W])
    qkv_chunks = {}

    def front_conv(c):
        r0 = c * CHUNK
        conv = cw_ref[0:1, :] * ext_scr[pl.ds(r0 + pad - 3, CHUNK), :]
        for j in range(1, GDN_CONV):
            conv = conv + cw_ref[j:j + 1, :] * ext_scr[pl.ds(r0 + pad - 3 + j, CHUNK), :]
        qkv_chunks[c] = _silu(conv)
        y = scw_ref[0:1, :] * uext_scr[pl.ds(r0 + pad - 2, CHUNK), :]
        for j in range(1, SC_CONV):
            y = y + scw_ref[j:j + 1, :] * uext_scr[pl.ds(r0 + pad - 2 + j, CHUNK), :]
        pc_scr[wr, rsl(c), :] = jnp.concatenate(
            [h_ref[rsl(c), GDN_W:2 * GDN_W], h_ref[rsl(c), COL_SC:COL_SC + SC_W] * y], axis=1)
        pf_scr[wr, 1, rsl(c), :] = _silu(h_ref[rsl(c), COL_Z:COL_Z + GDN_W])

    def front_norm(c):
        qkv = qkv_chunks.pop(c)
        q = qkv[:, 0:GDN_W]
        k = qkv[:, GDN_W:2 * GDN_W]
        ss = head_sum(jnp.concatenate([q * q, k * k], axis=0))
        qn = q * lax.rsqrt(ss[0:CHUNK] + RMS_EPS) * HEAD_DIM ** -0.5
        kn = k * lax.rsqrt(ss[CHUNK:2 * CHUNK] + RMS_EPS)
        qkv_chunks[c] = (qn, kn, qkv[:, 2 * GDN_W:3 * GDN_W])

    def front_gates(c):
        qn, kn, v = qkv_chunks.pop(c)
        ba = h_ref[rsl(c), COL_BA:COL_BA + LANE]
        beta_blk = jax.nn.sigmoid(ba)
        sp_in = ba + gp_ref[1:2, :]
        softplus = jnp.maximum(sp_in, 0.0) + jnp.log(1.0 + jnp.exp(-jnp.abs(sp_in)))
        g_blk = jnp.where(is_a, -jnp.exp(gp_ref[0:1, :]) * softplus, 0.0)
        gam3 = jnp.dot(ltri, jnp.concatenate(_split3(g_blk), axis=1), preferred_element_type=F32)
        gam_blk = gam3[:, 0:LANE] + gam3[:, LANE:2 * LANE] + gam3[:, 2 * LANE:3 * LANE]
        gam_c = jnp.dot(jnp.concatenate(_split3(gam_blk), axis=1), exp_a3, preferred_element_type=F32)
        g_exp = jnp.exp(gam_c)
        k_fac = jnp.exp(gam_c[CHUNK - 1:CHUNK, :] - gam_c)
        beta = _sel_dot2(beta_blk, exp_b2)
        kbeta = kn * beta
        for j, val in enumerate((qn, kn, kbeta, qn * g_exp, kbeta * g_exp, v * beta, kn * k_fac)):
            pb_scr[wr, j, rsl(c), :] = val.astype(BF16)
        pf_scr[wr, 0, rsl(c), :] = gam_c

    pieces = [functools.partial(f, c) for c in range(n_chunks) for f in (front_conv, front_norm, front_gates)]

    def emit(n=1):
        for _ in range(n):
            if pieces:
                pieces.pop(0)()

    a_b, t_inv, qd_b = {}, {}, {}
    for c, p in units:
        gc = gam[rsl(c), psl(p)]
        gr = jnp.sum(jnp.where(m_diag, gc, 0.0), axis=0, keepdims=True)
        dec = jnp.exp(jnp.where(m_incl, gc - gr, NEG_BIG))
        kq = lax.dot_general(jnp.concatenate([qn_b[rsl(c), psl(p)], kbeta_b[rsl(c), psl(p)]], axis=0),
                             bd(kn_b[rsl(c), psl(p)]), (((1,), (1,)), ((), ())), preferred_element_type=F32)
        qd_b[c, p] = (kq[0:CHUNK] * dec).astype(BF16)
        a_mat = jnp.where(m_strict, dec * kq[CHUNK:2 * CHUNK], 0.0)
        a_b[c, p] = a_mat.astype(BF16)
        t_inv[c, p] = eye - jnp.where(m_base, a_mat, 0.0)
    zero_b = jnp.zeros((CHUNK, LANE), BF16)
    for lm in lvl_masks:
        w_lvl = {u_: jnp.dot(jnp.where(lm, a_b[u_], zero_b), bd(t_inv[u_].astype(BF16)),
                             preferred_element_type=F32) for u_ in units}
        emit()
        for u_ in units:
            t_inv[u_] = t_inv[u_] - jnp.dot(t_inv[u_].astype(BF16), bd(w_lvl[u_].astype(BF16)),
                                            preferred_element_type=F32)
        emit()
    u0, w_b, kw_b, q_add, x_sols = {}, {}, {}, {}, {}
    for c, p in units:
        rhs = jnp.concatenate([bd(vb_b[rsl(c), psl(p)]), bd(kb_b[rsl(c), psl(p)])], axis=1)
        x_sols[c, p] = jnp.dot(t_inv[c, p].astype(BF16), rhs, preferred_element_type=F32)
    emit()
    for c, p in units:
        x_sol = x_sols[c, p]
        u0[c, p] = x_sol[:, 0:LANE]
        w_b[c, p] = x_sol[:, LANE:2 * LANE].astype(BF16)
        ktx = lax.dot_general(kt_b[rsl(c), psl(p)], x_sol.astype(BF16), (((0,), (0,)), ((), ())),
                              preferred_element_type=F32)
        q_add[c, p] = jnp.where(m_bd, ktx[:, 0:LANE], 0.0)
        kw_b[c, p] = jnp.where(m_bd, ktx[:, LANE:2 * LANE], 0.0).astype(BF16)
    emit()

    s_start = {}
    s_cur = [s_scr[p] for p in range(n_pairs)]
    for c in range(n_chunks):
        for p in range(n_pairs):
            s_bf = s_cur[p].astype(BF16)
            s_start[c, p] = s_bf
            e_last = jnp.exp(gam[(c + 1) * CHUNK - 1:(c + 1) * CHUNK, psl(p)])
            s_cur[p] = s_cur[p] * e_last - jnp.dot(kw_b[c, p], s_bf, preferred_element_type=F32) + q_add[c, p]
        emit()
    for p in range(n_pairs):
        s_scr[p] = s_cur[p]

    wqs = {u_: jnp.dot(jnp.concatenate([w_b[u_], qg_b[rsl(u_[0]), psl(u_[1])]], axis=0), s_start[u_],
                       preferred_element_type=F32) for u_ in units}
    emit(len(pieces))
    for c, p in units:
        u = u0[c, p] - wqs[c, p][0:CHUNK]
        o_scr[rsl(c), psl(p)] = wqs[c, p][CHUNK:2 * CHUNK] + jnp.dot(qd_b[c, p], bd(u.astype(BF16)),
                                                                      preferred_element_type=F32)
    o = o_scr[...]
    o_ms = head_sum(o * o) * (1.0 / HEAD_DIM)
    ho_ref[:, 0:GDN_W] = o * lax.rsqrt(o_ms + RMS_EPS) * ng_ref[...] * z_gate

    cnew_ref[0] = ext_scr[pl.ds(lt + pad - 3, 3), :]
    ext_scr[0:pad, :] = ext_scr[lt:lt + pad, :]
    scnew_ref[0] = uext_scr[pl.ds(lt + pad - 2, 2), :]
    uext_scr[0:pad, :] = uext_scr[lt:lt + pad, :]

    def store_state():
        for p in range(n_pairs):
            snew_ref[0, 2 * p] = s_scr[p, 0:HEAD_DIM, 0:HEAD_DIM]
            snew_ref[0, 2 * p + 1] = s_scr[p, HEAD_DIM:LANE, HEAD_DIM:LANE]

    if one_tile_seqs:
        store_state()
    else:
        pl.when(i == pl.num_programs(1) - 1)(store_state)


def _gdn_sconv(h_all, row_off, nb, seq, cbuf, s0, sbuf, cw, scw, gp, ng):
    lt = _pick_tile(seq, (256, 128, 64))
    nt = seq // lt
    off = row_off // lt
    assert row_off % lt == 0
    one_tile_seqs = nt == 1
    if one_tile_seqs:
        grid = (1, nb + 1)
        f_seq = lambda b, i: jnp.minimum(i, nb - 1)
        b_seq = lambda b, i: jnp.maximum(i - 1, 0)
        f_tile = b_tile = lambda b, i: 0
    else:
        grid = (nb, nt + 1)
        f_seq = b_seq = lambda b, i: b
        f_tile = lambda b, i: jnp.minimum(i, nt - 1)
        b_tile = lambda b, i: jnp.maximum(i - 1, 0)
    tok_in = lambda b, i: (off + f_seq(b, i) * nt + f_tile(b, i), 0)
    tok_out = lambda b, i: (off + b_seq(b, i) * nt + b_tile(b, i), 0)
    return pl.pallas_call(
        functools.partial(_gdn_pipe_kernel, lt, one_tile_seqs),
        grid=grid,
        in_specs=[
            pl.BlockSpec((lt, P_PAD), tok_in),
            pl.BlockSpec((1, GDN_CONV - 1, 3 * GDN_W), lambda b, i: (f_seq(b, i), 0, 0)),
            pl.BlockSpec((1, GDN_HEADS, HEAD_DIM, HEAD_DIM), lambda b, i: (b_seq(b, i), 0, 0, 0)),
            pl.BlockSpec((1, SC_CONV - 1, SC_W), lambda b, i: (f_seq(b, i), 0, 0)),
            pl.BlockSpec((GDN_CONV, 3 * GDN_W), lambda b, i: (0, 0)),
            pl.BlockSpec((SC_CONV, SC_W), lambda b, i: (0, 0)),
            pl.BlockSpec((SUBLANE, LANE), lambda b, i: (0, 0)),
            pl.BlockSpec((1, GDN_W), lambda b, i: (0, 0)),
        ],
        out_specs=[
            pl.BlockSpec((lt, MIX_W), tok_out),
            pl.BlockSpec((1, GDN_CONV - 1, 3 * GDN_W), lambda b, i: (f_seq(b, i), 0, 0)),
            pl.BlockSpec((1, GDN_HEADS, HEAD_DIM, HEAD_DIM), lambda b, i: (b_seq(b, i), 0, 0, 0)),
            pl.BlockSpec((1, SC_CONV - 1, SC_W), lambda b, i: (f_seq(b, i), 0, 0)),
        ],
        out_shape=[
            jax.ShapeDtypeStruct(h_all.shape, F32),
            jax.ShapeDtypeStruct((nb, GDN_CONV - 1, 3 * GDN_W), F32),
            jax.ShapeDtypeStruct((nb, GDN_HEADS, HEAD_DIM, HEAD_DIM), F32),
            jax.ShapeDtypeStruct((nb, SC_CONV - 1, SC_W), F32),
        ],
        scratch_shapes=[
            pltpu.VMEM((lt + SUBLANE, 3 * GDN_W), F32),
            pltpu.VMEM((lt + SUBLANE, SC_W), F32),
            pltpu.VMEM((GDN_HEADS // 2, LANE, LANE), F32),
            pltpu.VMEM((lt, GDN_W), F32),
            pltpu.VMEM((2, 7, lt, GDN_W), BF16),
            pltpu.VMEM((2, 2, lt, GDN_W), F32),
            pltpu.VMEM((2, lt, MIX_W - GDN_W), F32),
        ],
        input_output_aliases={0: 0},
        compiler_params=_cparams("arbitrary", "arbitrary"),
        name="gdn_sconv",
    )(h_all, cbuf, s0, sbuf, cw, scw, gp, ng)


def _rope(x, cos, sin_signed):
    w = x.shape[1]
    nrep = w // LANE
    if nrep > 1:
        cos = jnp.concatenate([cos] * nrep, axis=1)
        sin_signed = jnp.concatenate([sin_signed] * nrep, axis=1)
    lane = lax.broadcasted_iota(jnp.int32, x.shape, 1) & (HEAD_DIM - 1)
    half = HEAD_DIM // 2
    swapped = jnp.where(lane < half, pltpu.roll(x, w - half, 1), pltpu.roll(x, half, 1))
    return x * cos + swapped * sin_signed


def _swa_kernel(lt, hist_valid, h_ref, hk_ref, hv_ref, cos_ref, sin_ref, sink_ref,
                o_ref, kr_ref, kall_scr, vall_scr):
    i = pl.program_id(1)

    @pl.when(i == 0)
    def _():
        kall_scr[0:WINDOW, :] = hk_ref[0]
        vall_scr[0:WINDOW, :] = hv_ref[0]

    cos = cos_ref[...]
    sin = sin_ref[...]
    q = _rope(h_ref[:, 0:SWA_W], cos, sin)
    k = _rope(h_ref[:, SWA_W:SWA_W + SWA_KV_W], cos, sin)
    kr_ref[...] = k
    kall_scr[WINDOW:WINDOW + lt, :] = k
    vall_scr[WINDOW:WINDOW + lt, :] = h_ref[:, SWA_W + SWA_KV_W:SWA_W + 2 * SWA_KV_W]
    q_b = q.astype(BF16)
    kall = kall_scr[...].astype(BF16)
    vall = vall_scr[...].astype(BF16)
    kall_sw = pltpu.roll(kall_scr[...], HEAD_DIM, 1).astype(BF16)
    vall_sw = pltpu.roll(vall_scr[...], HEAD_DIM, 1).astype(BF16)

    band = WINDOW + CHUNK
    half = 2 * LANE
    n_chunks = lt // CHUNK
    n_pairs = SWA_Q_HEADS // 2
    left = lax.broadcasted_iota(jnp.int32, (band, LANE), 1) < HEAD_DIM
    zero_band = jnp.zeros((band, LANE), BF16)
    zero_pad = jnp.zeros((half - band, LANE), BF16)
    k_src = ((kall, kall_sw), (kall, kall), (kall_sw, kall))
    v_src = ((vall, vall_sw), (vall, vall), (vall_sw, vall))

    def pair_bd(top, bot, ks):
        return jnp.concatenate([jnp.where(left, top[ks], zero_band), zero_pad,
                                jnp.where(left, zero_band, bot[ks]), zero_pad], axis=0)

    kb = lax.broadcasted_iota(jnp.int32, (1, half), 1)
    hist_ok = jnp.logical_or(hist_valid, i > 0)
    fill = [jnp.where(kb == band, sink_ref[0:1, hq:hq + 1], NEG_BIG) for hq in range(SWA_Q_HEADS)]
    r512 = lax.broadcasted_iota(jnp.int32, (2 * half, LANE), 0)
    l512 = lax.broadcasted_iota(jnp.int32, (2 * half, LANE), 1)
    ones_bd = jnp.where((r512 < half) == (l512 < HEAD_DIM), 1.0, 0.0).astype(BF16)
    units = [(c, p) for c in range(n_chunks) for p in range(n_pairs)]
    scores = {}
    for c, p in units:
        ks = slice(c * CHUNK, c * CHUNK + band)
        kbd = pair_bd(k_src[p][0], k_src[p][1], ks)
        scores[c, p] = lax.dot_general(q_b[c * CHUNK:(c + 1) * CHUNK, p * LANE:(p + 1) * LANE], kbd,
                                       (((1,), (1,)), ((), ())), preferred_element_type=F32)
    probs = {}
    for c, p in units:
        valid = (kb < band) & (hist_ok | ((kb >> CHUNK_SHIFT) + c >= WINDOW // CHUNK))
        halves = []
        for e_ in range(2):
            s = jnp.where(valid, scores[c, p][:, e_ * half:(e_ + 1) * half] * HEAD_DIM ** -0.5, fill[2 * p + e_])
            halves.append(jnp.exp(s - jnp.max(s, -1, keepdims=True)).astype(BF16))
        probs[c, p] = jnp.concatenate(halves, axis=1)
    for c, p in units:
        ks = slice(c * CHUNK, c * CHUNK + band)
        rhs = jnp.concatenate([pair_bd(v_src[p][0], v_src[p][1], ks), ones_bd], axis=1)
        od = jnp.dot(probs[c, p], rhs, preferred_element_type=F32)
        o_ref[c * CHUNK:(c + 1) * CHUNK, p * LANE:(p + 1) * LANE] = od[:, 0:LANE] / od[:, LANE:2 * LANE]

    if lt >= WINDOW:
        kall_scr[0:WINDOW, :] = kall_scr[lt:lt + WINDOW, :]
        vall_scr[0:WINDOW, :] = vall_scr[lt:lt + WINDOW, :]


def _swa(h_all, row_off, nb, seq, hist_k, hist_v, hist_valid, cos, sin, sinks):
    lt = _pick_tile(seq, (256, 128, 64))
    nt = seq // lt
    assert nt == 1 or lt >= WINDOW
    off = row_off // lt
    assert row_off % lt == 0
    swa_blk_w = P_PAD - COL_SWA
    return pl.pallas_call(
        functools.partial(_swa_kernel, lt, hist_valid),
        grid=(nb, nt),
        in_specs=[
            pl.BlockSpec((lt, swa_blk_w), lambda b, i: (off + b * nt + i, COL_SWA // swa_blk_w)),
            pl.BlockSpec((1, WINDOW, SWA_KV_W), lambda b, i: (b, 0, 0)),
            pl.BlockSpec((1, WINDOW, SWA_KV_W), lambda b, i: (b, 0, 0)),
            pl.BlockSpec((lt, LANE), lambda b, i: (i, 0)),
            pl.BlockSpec((lt, LANE), lambda b, i: (i, 0)),
            pl.BlockSpec((1, LANE), lambda b, i: (0, 0)),
        ],
        out_specs=[
            pl.BlockSpec((lt, SWA_W), lambda b, i: (off + b * nt + i, 1)),
            pl.BlockSpec((lt, SWA_KV_W), lambda b, i: (b * nt + i, 0)),
        ],
        out_shape=[
            jax.ShapeDtypeStruct(h_all.shape, F32),
            jax.ShapeDtypeStruct((nb * seq, SWA_KV_W), F32),
        ],
        scratch_shapes=[
            pltpu.VMEM((lt + WINDOW, SWA_KV_W), F32),
            pltpu.VMEM((lt + WINDOW, SWA_KV_W), F32),
        ],
        input_output_aliases={0: 0},
        compiler_params=_cparams("arbitrary", "arbitrary"),
        name="swa",
    )(h_all, hist_k, hist_v, cos, sin, sinks)


def _memattn_kernel(alpha, x_ref, xb_in_ref, gate_in_ref, ids_in_ref, mix_ref, wo_ref, g1_ref, b1_ref,
                    mk_ref, mv_ref, wcq_ref, wco_ref, g_ref, b_ref, wr_ref, br_ref,
                    x_out_ref, xb_ref, gate_ref, ids_ref, cnt_ref):
    del xb_in_ref, gate_in_ref, ids_in_ref
    x = _ln(alpha * x_ref[...] + _dot(mix_ref[...], wo_ref[...]), g1_ref[...], b1_ref[...])
    q = _dot(x, wcq_ref[...])
    mk = mk_ref[0].astype(BF16)
    mv = mv_ref[0].astype(BF16)
    n_mem = mk.shape[0]
    q_b = q.astype(BF16)
    left = lax.broadcasted_iota(jnp.int32, (n_mem, LANE), 1) < HEAD_DIM
    zero_kv = jnp.zeros((n_mem, LANE), BF16)

    def pair_bd(x):
        return jnp.concatenate([jnp.where(left, x, zero_kv), jnp.where(left, zero_kv, x)], axis=0)

    pairs = range(MEM_HEADS // 2)
    scores = [lax.dot_general(q_b[:, p * LANE:(p + 1) * LANE], pair_bd(mk[:, p * LANE:(p + 1) * LANE]),
                              (((1,), (1,)), ((), ())), preferred_element_type=F32) for p in pairs]
    probs = []
    for p in pairs:
        halves = []
        for e_ in range(2):
            s = scores[p][:, e_ * n_mem:(e_ + 1) * n_mem] * HEAD_DIM ** -0.5
            halves.append(jnp.exp(s - jnp.max(s, -1, keepdims=True)).astype(BF16))
        probs.append(jnp.concatenate(halves, axis=1))
    r2 = lax.broadcasted_iota(jnp.int32, (2 * n_mem, LANE), 0)
    l2 = lax.broadcasted_iota(jnp.int32, (2 * n_mem, LANE), 1)
    ones_bd = jnp.where((r2 < n_mem) == (l2 < HEAD_DIM), 1.0, 0.0).astype(BF16)
    outs = []
    for p in pairs:
        rhs = jnp.concatenate([pair_bd(mv[:, p * LANE:(p + 1) * LANE]), ones_bd], axis=1)
        od = jnp.dot(probs[p], rhs, preferred_element_type=F32)
        outs.append(od[:, 0:LANE] / od[:, LANE:2 * LANE])
    o = _dot(jnp.concatenate(outs, axis=1), wco_ref[...])
    x2 = _ln(alpha * x + o, g_ref[...], b_ref[...])
    x_out_ref[...] = x2
    packed = _pack_halves(x2)
    for p in range(xb_ref.shape[0]):
        xb_ref[p] = packed[:, p * LANE:(p + 1) * LANE]

    logits = _dot(x2, wr_ref[...]) + br_ref[...]
    lane = lax.broadcasted_iota(jnp.int32, logits.shape, 1).astype(F32)
    vals, idxs = [], []
    for _ in range(TOP_K):
        m = jnp.max(logits, -1, keepdims=True)
        idx = jnp.min(jnp.where(logits == m, lane, float(LANE)), -1, keepdims=True)
        vals.append(m)
        idxs.append(idx)
        logits = jnp.where(lane == idx, 2 * NEG_BIG, logits)
    es = [jnp.exp(val - vals[0]) for val in vals]
    den = es[0] + es[1] + es[2] + es[3]
    gate = jnp.zeros(logits.shape, F32)
    ids = jnp.zeros(logits.shape, F32)
    for kk in range(TOP_K):
        gate = jnp.where(lane == kk, es[kk] / den, gate)
        ids = jnp.where(lane == kk, idxs[kk], ids)
    gate_ref[...] = gate
    ids_ref[...] = ids.astype(jnp.int32)
    picked = jnp.sum(jnp.where(logits == 2 * NEG_BIG, 1.0, 0.0), 0, keepdims=True)
    cnt_ref[...] = jnp.broadcast_to(picked, cnt_ref.shape)


def _memattn_router(x_all, xb_all, gate_all, ids_all, h_all, w_o, g1, b1, row_off, nb, seq, mk, mv, wcq, wco,
                    g, b, wr, br, alpha):
    t, d = x_all.shape
    tl = _pick_tile(seq, (512, 256, 128, 64))
    nt = seq // tl
    off = row_off // tl
    assert row_off % tl == 0
    n_mem = mk.shape[1]
    tok = lambda bb, i: (off + bb * nt + i, 0)
    const = lambda bb, i: (0, 0)
    return pl.pallas_call(
        functools.partial(_memattn_kernel, alpha),
        grid=(nb, nt),
        in_specs=[
            pl.BlockSpec((tl, d), tok),
            pl.BlockSpec(memory_space=pl.ANY),
            pl.BlockSpec(memory_space=pl.ANY),
            pl.BlockSpec(memory_space=pl.ANY),
            pl.BlockSpec((tl, MIX_W), tok),
            pl.BlockSpec((MIX_W, d), const),
            pl.BlockSpec((1, d), const),
            pl.BlockSpec((1, d), const),
            pl.BlockSpec((1, n_mem, MEM_W), lambda bb, i: (bb, 0, 0)),
            pl.BlockSpec((1, n_mem, MEM_W), lambda bb, i: (bb, 0, 0)),
            pl.BlockSpec((d, MEM_W), const),
            pl.BlockSpec((MEM_W, d), const),
            pl.BlockSpec((1, d), const),
            pl.BlockSpec((1, d), const),
            pl.BlockSpec((d, LANE), const),
            pl.BlockSpec((1, LANE), const),
        ],
        out_specs=[
            pl.BlockSpec((tl, d), tok),
            pl.BlockSpec((d // (2 * LANE), tl, LANE), lambda bb, i: (0, off + bb * nt + i, 0)),
            pl.BlockSpec((tl, LANE), tok),
            pl.BlockSpec((tl, LANE), tok),
            pl.BlockSpec((SUBLANE, LANE), lambda bb, i: (bb * nt + i, 0)),
        ],
        out_shape=[
            jax.ShapeDtypeStruct((t, d), F32),
            jax.ShapeDtypeStruct((d // (2 * LANE), t, LANE), jnp.uint32),
            jax.ShapeDtypeStruct((t, LANE), F32),
            jax.ShapeDtypeStruct((t, LANE), jnp.int32),
            jax.ShapeDtypeStruct((nb * nt * SUBLANE, LANE), F32),
        ],
        input_output_aliases={0: 0, 1: 1, 2: 2, 3: 3},
        compiler_params=_cparams("arbitrary", "arbitrary"),
        name="memattn_router",
    )(x_all, xb_all, gate_all, ids_all, h_all, w_o, g1.reshape(1, d), b1.reshape(1, d), mk, mv, wcq, wco,
      g.reshape(1, d), b.reshape(1, d), wr, br)


def _expert_kernel(d_ff, be_ref, bv_ref, nu_ref, xs_ref, w1_ref, b1_ref, w2_ref, b2_ref, ys_ref, w1b_scr, w2b_scr):
    i = pl.program_id(0)
    prev = be_ref[jnp.maximum(i - 1, 0)]
    new_expert = (i == 0) | (be_ref[i] != prev)

    @pl.when(new_expert & (i < nu_ref[0]))
    def _():
        w1b_scr[...] = w1_ref[0, 0].astype(BF16)
        w2b_scr[...] = w2_ref[0, 0].astype(BF16)

    @pl.when(i < nu_ref[0])
    def _():
        row = lax.broadcasted_iota(jnp.int32, (xs_ref.shape[1], 1), 0)
        words = jnp.concatenate([xs_ref[p] for p in range(xs_ref.shape[0])], axis=1)
        xs = _unpack_halves(jnp.where(row < bv_ref[i], words, jnp.zeros_like(words))).astype(BF16)
        hh = jnp.dot(xs, w1b_scr[...], preferred_element_type=F32) + b1_ref[0, 0]
        glu = jnp.minimum(hh[:, 0:d_ff], SWIGLU_LIMIT)
        lin = jnp.clip(hh[:, d_ff:2 * d_ff], -SWIGLU_LIMIT, SWIGLU_LIMIT)
        act = glu * jax.nn.sigmoid(SWIGLU_ALPHA * glu) * (lin + 1.0)
        y = jnp.dot(act.astype(BF16), w2b_scr[...], preferred_element_type=F32) + b2_ref[0, 0]
        packed = _pack_halves(y)
        for p in range(ys_ref.shape[0]):
            ys_ref[p] = packed[:, p * LANE:(p + 1) * LANE]


def _experts(xs, blk_e, blk_valid, n_used, layer, w1, b1, w2, b2):
    n_planes, n_rows, _ = xs.shape
    n_l, n_e, d, two_ff = w1.shape
    d_ff = two_ff // 2
    n_blocks = n_rows // MOE_BM
    row_blk = lambda i, be, bv, nu: (0, jnp.minimum(i, nu[0] - 1), 0)
    exp_blk = lambda i, be, bv, nu: (layer, be[jnp.minimum(i, nu[0] - 1)], 0, 0)
    return pl.pallas_call(
        functools.partial(_expert_kernel, d_ff),
        grid_spec=pltpu.PrefetchScalarGridSpec(
            num_scalar_prefetch=3,
            grid=(n_blocks,),
            in_specs=[
                pl.BlockSpec((n_planes, MOE_BM, LANE), row_blk),
                pl.BlockSpec((1, 1, d, two_ff), exp_blk),
                pl.BlockSpec((1, 1, 1, two_ff), exp_blk),
                pl.BlockSpec((1, 1, d_ff, d), exp_blk),
                pl.BlockSpec((1, 1, 1, d), exp_blk),
            ],
            out_specs=pl.BlockSpec((n_planes, MOE_BM, LANE), row_blk),
            scratch_shapes=[pltpu.VMEM((d, two_ff), BF16), pltpu.VMEM((d_ff, d), BF16)],
        ),
        out_shape=jax.ShapeDtypeStruct((n_planes, n_rows, LANE), jnp.uint32),
        compiler_params=_cparams("arbitrary"),
        name="moe_experts",
    )(blk_e, blk_valid, n_used, xs, w1, b1.reshape(n_l, n_e, 1, two_ff), w2, b2.reshape(n_l, n_e, 1, d))


def _sc_mesh():
    return plsc.VectorSubcoreMesh(core_axis_name="c", subcore_axis_name="s")


def _sc_dispatch(xw, dest_km, n_rows):
    n_planes, t, _ = xw.shape
    n_src = n_planes * t
    nw = n_src // SC_WINDOW
    assert t % SC_WINDOW == 0

    @functools.partial(pl.kernel, out_type=jax.ShapeDtypeStruct((n_planes * n_rows, LANE), xw.dtype),
                       mesh=_sc_mesh(), scratch_types=[], name="moe_dispatch_sc")
    def scatter_rows(x_hbm, i0_hbm, i1_hbm, i2_hbm, i3_hbm, o_hbm):
        def body(x_vmem, i0, i1, i2, i3):
            for idx in (i0, i1, i2, i3):
                pltpu.sync_copy(x_vmem, o_hbm.at[idx.at[0]])

        pltpu.emit_pipeline(
            body,
            grid=(nw,),
            in_specs=[pl.BlockSpec((SC_WINDOW, LANE), lambda i: (i, 0))]
            + [pl.BlockSpec((1, SC_WINDOW), functools.partial(lambda k, i: (0, k * nw + i), k))
               for k in range(TOP_K)],
            out_specs=[],
            core_axis_name=("c", "s"),
            dimension_semantics=(pltpu.PARALLEL,),
        )(x_hbm, i0_hbm, i1_hbm, i2_hbm, i3_hbm)

    plane_off = (jnp.arange(n_planes, dtype=jnp.int32) * n_rows)[None, :, None]
    idx = (dest_km.reshape(TOP_K, 1, t) + plane_off).reshape(1, TOP_K * n_src)
    return scatter_rows(xw.reshape(n_src, LANE), idx, idx, idx, idx).reshape(n_planes, n_rows, LANE)


def _sc_gather(yw, idx_flat):
    n_planes, n_rows, _ = yw.shape
    n = idx_flat.shape[0]
    assert n % SC_WINDOW == 0

    @functools.partial(pl.kernel, out_type=jax.ShapeDtypeStruct((n_planes * n, LANE), yw.dtype),
                       mesh=_sc_mesh(), scratch_types=[], name="moe_gather_sc")
    def gather_rows(y_hbm, i_hbm, o_hbm):
        def body(i_vmem, o_vmem):
            pltpu.sync_copy(y_hbm.at[i_vmem.at[0]], o_vmem)

        pltpu.emit_pipeline(
            body,
            grid=(n_planes * n // SC_WINDOW,),
            in_specs=[pl.BlockSpec((1, SC_WINDOW), lambda i: (0, i))],
            out_specs=[pl.BlockSpec((SC_WINDOW, LANE), lambda i: (i, 0))],
            core_axis_name=("c", "s"),
            dimension_semantics=(pltpu.PARALLEL,),
        )(i_hbm, o_hbm)

    plane_off = (jnp.arange(n_planes, dtype=jnp.int32) * n_rows)[:, None]
    idx = (idx_flat[None, :] + plane_off).reshape(1, n_planes * n)
    return gather_rows(yw.reshape(n_planes * n_rows, LANE), idx).reshape(n_planes, n, LANE)


def _combine_kernel(alpha, y0_ref, y1_ref, y2_ref, y3_ref, gate_ref, x_ref, g_ref, b_ref, o_ref):
    gate = gate_ref[...]
    acc = alpha * x_ref[...]
    for k, y_ref in enumerate((y0_ref, y1_ref, y2_ref, y3_ref)):
        words = jnp.concatenate([y_ref[p] for p in range(y_ref.shape[0])], axis=1)
        acc = acc + gate[:, k:k + 1] * _unpack_halves(words)
    o_ref[...] = _ln(acc, g_ref[...], b_ref[...])


def _combine(yg, gate_all, x_all, g, b, alpha, row_off):
    t, d = x_all.shape
    tc = yg.shape[1] // TOP_K
    tm = _pick_tile(tc, (1024, 512, 256, 128, 64))
    nt = tc // tm
    off = row_off // tm
    assert row_off % tm == 0
    y_spec = lambda k: pl.BlockSpec((yg.shape[0], tm, LANE), lambda i: (0, k * nt + i, 0))
    return pl.pallas_call(
        functools.partial(_combine_kernel, alpha),
        grid=(nt,),
        in_specs=[
            y_spec(0), y_spec(1), y_spec(2), y_spec(3),
            pl.BlockSpec((tm, LANE), lambda i: (off + i, 0)),
            pl.BlockSpec((tm, d), lambda i: (off + i, 0)),
            pl.BlockSpec((1, d), lambda i: (0, 0)),
            pl.BlockSpec((1, d), lambda i: (0, 0)),
        ],
        out_specs=pl.BlockSpec((tm, d), lambda i: (off + i, 0)),
        out_shape=jax.ShapeDtypeStruct((t, d), F32),
        input_output_aliases={5: 0},
        compiler_params=_cparams("arbitrary"),
        name="moe_combine_ln3",
    )(yg, yg, yg, yg, gate_all, x_all, g.reshape(1, d), b.reshape(1, d))


def _dest_kernel(ids_ref, start_ref, dest_ref, run_scr):
    i = pl.program_id(0)

    @pl.when(i == 0)
    def _():
        run_scr[...] = jnp.zeros_like(run_scr)

    ids = ids_ref[...].astype(F32)
    tm = ids.shape[0]
    lane = lax.broadcasted_iota(jnp.int32, ids.shape, 1).astype(F32)
    one_hots = [jnp.where(lane == ids[:, k:k + 1], 1.0, 0.0) for k in range(TOP_K)]
    multi_hot = one_hots[0] + one_hots[1] + one_hots[2] + one_hots[3]
    r = lax.broadcasted_iota(jnp.int32, (tm, tm), 0)
    c = lax.broadcasted_iota(jnp.int32, (tm, tm), 1)
    before = jnp.dot(jnp.where(c < r, 1.0, 0.0).astype(BF16), multi_hot.astype(BF16),
                     preferred_element_type=F32) + (run_scr[0:1, :] + start_ref[...])
    dest = jnp.zeros(ids.shape, F32)
    for k in range(TOP_K):
        dest = jnp.where(lane == k, jnp.sum(one_hots[k] * before, -1, keepdims=True), dest)
    dest_ref[...] = dest.astype(jnp.int32)
    run_scr[0:1, :] = run_scr[0:1, :] + jnp.sum(multi_hot, 0, keepdims=True)


def _route(ids_all, counts, n_experts):
    t = ids_all.shape[0]
    tk = t * TOP_K
    n_blocks = -(-tk // MOE_BM) + n_experts
    n_rows = n_blocks * MOE_BM
    assert n_rows < 2 ** 24
    padded = (counts + MOE_BM - 1) // MOE_BM * MOE_BM
    pad_end = jnp.cumsum(padded)
    pad_start = pad_end - padded
    start_row = jnp.zeros((1, LANE), F32).at[0, 0:n_experts].set(pad_start.astype(F32))
    tm = _pick_tile(t, (512, 256, 128, 64))
    tok = pl.BlockSpec((tm, LANE), lambda i: (i, 0))
    dest = pl.pallas_call(
        _dest_kernel,
        grid=(t // tm,),
        in_specs=[tok, pl.BlockSpec((1, LANE), lambda i: (0, 0))],
        out_specs=tok,
        out_shape=jax.ShapeDtypeStruct((t, LANE), jnp.int32),
        scratch_shapes=[pltpu.VMEM((SUBLANE, LANE), F32)],
        compiler_params=_cparams("arbitrary"),
        name="moe_dest",
    )(ids_all, start_row)
    dest_km = dest[:, 0:TOP_K].T.reshape(-1)
    blk = jnp.arange(n_blocks, dtype=jnp.int32)
    blk_e = jnp.minimum(jnp.sum((blk[:, None] * MOE_BM >= pad_end[None, :]).astype(jnp.int32), axis=1),
                        n_experts - 1)
    blk_valid = jnp.clip(counts[blk_e] - (blk * MOE_BM - pad_start[blk_e]), 0, MOE_BM).astype(jnp.int32)
    n_used = (pad_end[-1] // MOE_BM).astype(jnp.int32).reshape(1)
    return dest_km, n_rows, blk_e, blk_valid, n_used


def _rope_tables(pos):
    half = HEAD_DIM // 2
    inv = ROPE_THETA ** (-jnp.arange(half, dtype=F32) / half)
    ang = pos.astype(F32)[:, None] * inv[None, :]
    cos = jnp.cos(ang)
    sin = jnp.sin(ang)
    cos = jnp.concatenate([cos, cos, cos, cos], axis=1)
    sin = jnp.concatenate([-sin, sin, -sin, sin], axis=1)
    return cos, sin


def _permute_w_in(w):
    d = w.shape[0]
    o_b = 4 * GDN_W
    o_q = o_b + 2 * GDN_HEADS
    o_sc = o_q + SWA_W + 2 * SWA_KV_W
    ba = jnp.concatenate([w[:, o_b:o_q], jnp.zeros((d, LANE - 2 * GDN_HEADS), w.dtype)], axis=1)
    return jnp.concatenate([w[:, 0:o_b], w[:, o_sc:], w[:, o_q:o_sc], ba], axis=1)


def kernel(x_prompt, x_sample, state_gdn_S, state_gdn_conv, cache_swa_k, cache_swa_v, state_sconv, cache_mem_k, cache_mem_v, mem_prompt, ln_in_g, ln_in_b, w_in, gdn_conv_w, gdn_a_log, gdn_dt_bias, gdn_norm_g, swa_sinks, sc_conv_w, w_o, ln1_g, ln1_b, w_cq, w_mk, w_mv, w_co, ln2_g, ln2_b, w_router, b_router, w1, b1, w2, b2, ln3_g, ln3_b):
    bp, lp, d = x_prompt.shape
    bs, ls, _ = x_sample.shape
    depth = w_in.shape[0]
    n_experts = w_router.shape[2]
    n_mem = mem_prompt.shape[1]
    tp, ts = bp * lp, bs * ls
    t_all = tp + ts
    alpha = (2 * depth) ** 0.25
    assert cache_swa_k.shape[2] == WINDOW and d == MIX_W

    x_all = _ln_in(x_prompt.reshape(tp, d), x_sample.reshape(ts, d), ln_in_g, ln_in_b)
    xb_all = jnp.zeros((d // (2 * LANE), t_all, LANE), jnp.uint32)
    gate_all = jnp.zeros((t_all, LANE), F32)
    ids_all = jnp.zeros((t_all, LANE), jnp.int32)

    cos_p, sin_p = _rope_tables(jnp.arange(lp))
    cos_s, sin_s = _rope_tables(PAST_LEN + jnp.arange(ls))
    mem_flat = mem_prompt.reshape(bp * n_mem, d)
    zeros_cbuf = jnp.zeros((bp, GDN_CONV - 1, 3 * GDN_W), F32)
    zeros_s = jnp.zeros((bp, GDN_HEADS, HEAD_DIM, HEAD_DIM), F32)
    zeros_sbuf = jnp.zeros((bp, SC_CONV - 1, SC_W), F32)
    zeros_hist = jnp.zeros((bp, WINDOW, SWA_KV_W), F32)

    outs = {k: [] for k in ("p_S", "p_conv", "p_k", "p_v", "p_sc", "p_mk", "p_mv",
                            "s_S", "s_conv", "s_k", "s_v", "s_sc")}
    for l in range(depth):
        w_in_l = _permute_w_in(w_in[l]).astype(BF16)
        gp = jnp.zeros((SUBLANE, LANE), F32)
        gp = gp.at[0, BA_A:BA_A + GDN_HEADS].set(gdn_a_log[l]).at[1, BA_A:BA_A + GDN_HEADS].set(gdn_dt_bias[l])
        ng = jnp.tile(gdn_norm_g[l].reshape(1, HEAD_DIM), (1, GDN_HEADS))
        sinks = jnp.zeros((1, LANE), F32).at[0, 0:SWA_Q_HEADS].set(swa_sinks[l])
        wr = jnp.concatenate([w_router[l], jnp.zeros((d, LANE - n_experts), F32)], axis=1).astype(BF16)
        br = jnp.concatenate([b_router[l], jnp.full((LANE - n_experts,), NEG_BIG, F32)]).reshape(1, LANE)

        mkv = _proj(mem_flat, jnp.concatenate([w_mk[l], w_mv[l]], axis=1).astype(BF16), "mem_kv")
        mk_p = mkv[:, 0:MEM_W].reshape(bp, n_mem, MEM_W)
        mv_p = mkv[:, MEM_W:2 * MEM_W].reshape(bp, n_mem, MEM_W)
        outs["p_mk"].append(mk_p.reshape(bp, n_mem, MEM_HEADS, HEAD_DIM))
        outs["p_mv"].append(mv_p.reshape(bp, n_mem, MEM_HEADS, HEAD_DIM))

        h_all = _proj(x_all, w_in_l, "w_in")

        v_p = h_all[0:tp, COL_SWA_V:COL_SWA_V + SWA_KV_W].reshape(bp, lp, SWA_KV_W)[:, lp - WINDOW:]
        v_s = h_all[tp:, COL_SWA_V:COL_SWA_V + SWA_KV_W]
        h_all, c_p, s_p, sc_p = _gdn_sconv(h_all, 0, bp, lp, zeros_cbuf, zeros_s, zeros_sbuf,
                                           gdn_conv_w[l], sc_conv_w[l], gp, ng)
        h_all, c_s, s_s, sc_s = _gdn_sconv(h_all, tp, bs, ls, state_gdn_conv[l], state_gdn_S[l], state_sconv[l],
                                           gdn_conv_w[l], sc_conv_w[l], gp, ng)
        h_all, kr_p = _swa(h_all, 0, bp, lp, zeros_hist, zeros_hist, False, cos_p, sin_p, sinks)
        h_all, kr_s = _swa(h_all, tp, bs, ls, cache_swa_k[l].reshape(bs, WINDOW, SWA_KV_W),
                           cache_swa_v[l].reshape(bs, WINDOW, SWA_KV_W), True, cos_s, sin_s, sinks)
        outs["p_S"].append(s_p); outs["p_conv"].append(c_p); outs["p_sc"].append(sc_p)
        outs["s_S"].append(s_s); outs["s_conv"].append(c_s); outs["s_sc"].append(sc_s)
        outs["p_k"].append(kr_p.reshape(bp, lp, SWA_KV_W)[:, lp - WINDOW:].reshape(bp, WINDOW, SWA_KV_HEADS, HEAD_DIM))
        outs["p_v"].append(v_p.reshape(bp, WINDOW, SWA_KV_HEADS, HEAD_DIM))
        outs["s_k"].append(kr_s.reshape(bs, ls, SWA_KV_HEADS, HEAD_DIM))
        outs["s_v"].append(v_s.reshape(bs, ls, SWA_KV_HEADS, HEAD_DIM))

        wo = w_o[l].astype(BF16)
        wcq = w_cq[l].astype(BF16)
        wco = w_co[l].astype(BF16)
        x_all, xb_all, gate_all, ids_all, cnt_p = _memattn_router(
            x_all, xb_all, gate_all, ids_all, h_all, wo, ln1_g[l], ln1_b[l], 0, bp, lp, mk_p, mv_p,
            wcq, wco, ln2_g[l], ln2_b[l], wr, br, alpha)
        x_all, xb_all, gate_all, ids_all, cnt_s = _memattn_router(
            x_all, xb_all, gate_all, ids_all, h_all, wo, ln1_g[l], ln1_b[l], tp, bs, ls,
            cache_mem_k[l].reshape(bs, n_mem, MEM_W), cache_mem_v[l].reshape(bs, n_mem, MEM_W),
            wcq, wco, ln2_g[l], ln2_b[l], wr, br, alpha)

        counts = sum(c_.reshape(-1, SUBLANE, LANE)[:, 0, 0:n_experts].sum(0) for c_ in (cnt_p, cnt_s))
        dest_km, n_rows, blk_e, blk_valid, n_used = _route(ids_all, counts.astype(jnp.int32), n_experts)
        xs = _sc_dispatch(xb_all, dest_km, n_rows)
        ys = _experts(xs, blk_e, blk_valid, n_used, l, w1, b1, w2, b2)
        dest_tk = dest_km.reshape(TOP_K, t_all)
        for part in range(MOE_PARTS):
            lo = part * (t_all // MOE_PARTS)
            yg = _sc_gather(ys, dest_tk[:, lo:lo + t_all // MOE_PARTS].reshape(-1))
            x_all = _combine(yg, gate_all, x_all, ln3_g[l], ln3_b[l], alpha, lo)

    st = lambda k: jnp.stack(outs[k])
    return (x_all[0:tp].reshape(bp, lp, d), x_all[tp:].reshape(bs, ls, d),
            st("p_S"), st("p_conv"), st("p_k"), st("p_v"), st("p_sc"), st("p_mk"), st("p_mv"),
            st("s_S"), st("s_conv"), st("s_k"), st("s_v"), st("s_sc"))
```
